```python
import jax, jax.numpy as jnp
from jax import lax
import numpy as np

D_MODEL = 1024
BATCH = 2
SEQ = 8192
DEPTH = 1

D_PLE = 256
EPS = 1e-6

GLA_HEADS = 4
GLA_DK = 64
GLA_DV = 128
GLA_LOWRANK = 16
GLA_TAU = 16.0
GLA_CHUNK = 64
GLA_QK = GLA_HEADS * GLA_DK
GLA_V = GLA_HEADS * GLA_DV

CONV_WIDTH = D_MODEL // 2
CONV_K = 3

D_MIX = GLA_V + CONV_WIDTH

IN_SPLIT_SIZES = (GLA_QK, GLA_QK, GLA_V, GLA_V, GLA_LOWRANK, CONV_WIDTH, CONV_WIDTH, CONV_WIDTH)
D_IN = sum(IN_SPLIT_SIZES)
IN_SPLIT_POINTS = tuple(int(c) for c in np.cumsum(IN_SPLIT_SIZES)[:-1])

N_GROUPS = 4
EXPERTS_PER_GROUP = 4
N_EXPERTS = N_GROUPS * EXPERTS_PER_GROUP
TOP_K = 2
D_EXPERT = D_MODEL // 2

kernel_name = "hymba_gla_shortconv_hiermoe_ple"


def rms_norm(x, g):
    xf = x.astype(jnp.float32)
    y = xf * lax.rsqrt(jnp.mean(xf * xf, axis=-1, keepdims=True) + EPS)
    return (y * g.astype(jnp.float32)).astype(x.dtype)


def gla_chunked(q, k, v, log_a):
    q, k, v, log_a = (t.astype(jnp.float32) for t in (q, k, v, log_a))
    b, h, s, dk = q.shape
    dv = v.shape[-1]
    n = s // GLA_CHUNK

    def to_chunks(t):
        return jnp.moveaxis(t.reshape(b, h, n, GLA_CHUNK, t.shape[-1]), 2, 0)

    causal = jnp.tril(jnp.ones((GLA_CHUNK, GLA_CHUNK), dtype=bool))[None, None, :, :, None]

    def step(state, inp):
        qc, kc, vc, ac = inp
        cum = jnp.cumsum(ac, axis=2)
        o_inter = jnp.einsum('bhld,bhde->bhle', qc * jnp.exp(cum), state)
        diff = cum[:, :, :, None, :] - cum[:, :, None, :, :]
        decay = jnp.exp(jnp.where(causal, diff, -jnp.inf))
        scores = jnp.einsum('bhtd,bhsd,bhtsd->bhts', qc, kc, decay)
        o_intra = jnp.einsum('bhts,bhse->bhte', scores, vc)
        last = cum[:, :, -1:, :]
        k_dec = kc * jnp.exp(last - cum)
        new_state = jnp.exp(last[:, :, 0, :])[..., None] * state + jnp.einsum('bhld,bhle->bhde', k_dec, vc)
        return new_state, o_inter + o_intra

    state0 = jnp.zeros((b, h, dk, dv), jnp.float32)
    _, out = lax.scan(step, state0, (to_chunks(q), to_chunks(k), to_chunks(v), to_chunks(log_a)))
    return jnp.moveaxis(out, 0, 2).reshape(b, h, s, dv)


def causal_depthwise_conv(u, w):
    return lax.conv_general_dilated(
        u, w[:, None, :].astype(u.dtype), window_strides=(1,),
        padding=[(CONV_K - 1, 0)], dimension_numbers=('NWC', 'WIO', 'NWC'),
        feature_group_count=u.shape[-1])


def hybrid_mixer(h, w_in, w_gla_gate, b_gla_gate, g_gla_out, w_conv, w_out):
    b, s, _ = h.shape
    proj = h @ w_in
    q, k, v, g, a_low, conv_b, conv_c, conv_u = jnp.split(proj, IN_SPLIT_POINTS, axis=-1)

    def heads(t, d):
        return t.reshape(b, s, GLA_HEADS, d).transpose(0, 2, 1, 3)

    log_a = jax.nn.log_sigmoid((a_low @ w_gla_gate + b_gla_gate).astype(jnp.float32)) / GLA_TAU
    o = gla_chunked(heads(q, GLA_DK) * (GLA_DK ** -0.5), heads(k, GLA_DK),
                    heads(v, GLA_DV), heads(log_a, GLA_DK))
    o = rms_norm(o, g_gla_out).astype(h.dtype)
    y_gla = o.transpose(0, 2, 1, 3).reshape(b, s, GLA_V) * jax.nn.silu(g)

    y_conv = conv_b * causal_depthwise_conv(conv_c * conv_u, w_conv)

    return jnp.concatenate([y_gla, y_conv], axis=-1) @ w_out


def hier_moe(h, w_group, b_group, w_router, b_router, w_gate, w_up, w_down):
    b, s, d = h.shape
    t = h.reshape(b * s, d)
    n_tok = t.shape[0]
    group_logits = (t @ w_group + b_group).astype(jnp.float32)
    group_prob = jax.nn.softmax(group_logits, axis=-1)
    p_grp, g_sel = lax.top_k(group_prob, 1)
    exp_logits = (t @ w_router + b_router).astype(jnp.float32).reshape(n_tok, N_GROUPS, EXPERTS_PER_GROUP)
    in_group = jnp.take_along_axis(exp_logits, g_sel[:, :, None], axis=1)[:, 0]
    top_p, top_i = lax.top_k(jax.nn.softmax(in_group, axis=-1), TOP_K)
    weights = p_grp * top_p / jnp.sum(top_p, axis=-1, keepdims=True)
    expert_id = g_sel * EXPERTS_PER_GROUP + top_i
    combine = jnp.einsum('tke,tk->te', jax.nn.one_hot(expert_id, N_EXPERTS, dtype=jnp.float32), weights).astype(t.dtype)
    y = jnp.zeros_like(t)
    for e in range(N_EXPERTS):
        hid = jax.nn.silu(t @ w_gate[e]) * (t @ w_up[e])
        y = y + combine[:, e:e + 1] * (hid @ w_down[e])
    return y.reshape(b, s, d)


def setup_inputs(seed: int = 0) -> dict:
    key = jax.random.key(seed)
    ks = jax.random.split(key, 24)
    f32 = jnp.float32

    def nrm(k, shape, scale):
        return jax.random.normal(k, shape, f32) * scale

    def gain(k, shape):
        return 1.0 + 0.02 * jax.random.normal(k, shape, f32)

    L = DEPTH
    return {
        "x": nrm(ks[0], (BATCH, SEQ, D_MODEL), 1.0),
        "p": nrm(ks[1], (DEPTH, BATCH, SEQ, D_PLE), 1.0),
        "g_mix": gain(ks[2], (L, D_MODEL)),
        "w_in": nrm(ks[3], (L, D_MODEL, D_IN), D_MODEL ** -0.5),
        "w_gla_gate": nrm(ks[4], (L, GLA_LOWRANK, GLA_QK), GLA_LOWRANK ** -0.5),
        "b_gla_gate": nrm(ks[5], (L, GLA_QK), 0.1),
        "g_gla_out": gain(ks[6], (L, GLA_DV)),
        "w_conv": nrm(ks[7], (L, CONV_K, CONV_WIDTH), CONV_K ** -0.5),
        "w_out": nrm(ks[8], (L, D_MIX, D_MODEL), D_MIX ** -0.5),
        "g_moe": gain(ks[9], (L, D_MODEL)),
        "w_group": nrm(ks[10], (L, D_MODEL, N_GROUPS), D_MODEL ** -0.5),
        "b_group": nrm(ks[11], (L, N_GROUPS), 0.01),
        "w_router": nrm(ks[12], (L, D_MODEL, N_EXPERTS), D_MODEL ** -0.5),
        "b_router": nrm(ks[13], (L, N_EXPERTS), 0.01),
        "w_exp_gate": nrm(ks[14], (L, N_EXPERTS, D_MODEL, D_EXPERT), D_MODEL ** -0.5),
        "w_exp_up": nrm(ks[15], (L, N_EXPERTS, D_MODEL, D_EXPERT), D_MODEL ** -0.5),
        "w_exp_down": nrm(ks[16], (L, N_EXPERTS, D_EXPERT, D_MODEL), D_EXPERT ** -0.5),
        "g_ple": gain(ks[17], (L, D_MODEL)),
        "w_ple_gate": nrm(ks[18], (L, D_MODEL, D_MODEL), D_MODEL ** -0.5),
        "w_ple_proj": nrm(ks[19], (L, D_PLE, D_MODEL), D_PLE ** -0.5),
        "g_final": gain(ks[20], (D_MODEL,)),
    }


def reference(x, p, g_mix, w_in, w_gla_gate, b_gla_gate, g_gla_out, w_conv, w_out,
              g_moe, w_group, b_group, w_router, b_router, w_exp_gate, w_exp_up, w_exp_down,
              g_ple, w_ple_gate, w_ple_proj, g_final):
    for i in range(DEPTH):
        h = rms_norm(x, g_mix[i])
        x = x + hybrid_mixer(h, w_in[i], w_gla_gate[i], b_gla_gate[i], g_gla_out[i], w_conv[i], w_out[i])
        h = rms_norm(x, g_moe[i])
        x = x + hier_moe(h, w_group[i], b_group[i], w_router[i], b_router[i],
                         w_exp_gate[i], w_exp_up[i], w_exp_down[i])
        gate = jax.nn.sigmoid(rms_norm(x, g_ple[i]) @ w_ple_gate[i])
        x = x + gate * (p[i] @ w_ple_proj[i])
    return rms_norm(x, g_final)
```

```python
import functools

import jax
import jax.numpy as jnp
from jax import lax
from jax.experimental import pallas as pl
from jax.experimental.pallas import tpu as pltpu

EPS = 1e-6
GLA_HEADS = 4
GLA_DK = 64
GLA_DV = 128
GLA_QK = GLA_HEADS * GLA_DK
GLA_V = GLA_HEADS * GLA_DV
GLA_LOWRANK = 16
GLA_TAU = 16.0
CONV_K = 3
N_GROUPS = 4
EXPERTS_PER_GROUP = 4
N_EXPERTS = N_GROUPS * EXPERTS_PER_GROUP

LANES = 128
MIX_TILE = 256
MIX_LEVELS = 8
ROUTE_ROWS = 32
ROUTE_EXPERT_ROW0 = 8
MOE_TILE = 1024
VMEM_LIMIT = 56 * 1024 * 1024

_NT = (((1,), (1,)), ((), ()))


def _rms(x, g):
    return x * lax.rsqrt(jnp.mean(x * x, axis=-1, keepdims=True) + EPS) * g


def _dot(a, b):
    return jnp.dot(a, b, preferred_element_type=jnp.float32)


def _dot_nt(a, b, precision=None):
    return lax.dot_general(a, b, _NT, precision=precision, preferred_element_type=jnp.float32)


def _shift_rows(x, shift):
    return pltpu.roll(x, shift % x.shape[0], axis=0)


def _block_end_rows(cum_ref, blk):
    n = MIX_TILE // blk
    width = cum_ref.shape[1]
    pieces = [jnp.broadcast_to(cum_ref[pl.ds(b * blk + blk - 1, 1), :], (blk, width)) for b in range(n)]
    return jnp.concatenate(pieces, axis=0)


def _mixer_kernel(x_ref, gmix_ref, wqkvg_ref, wa_ref, wc3_ref, wgate_ref, bgate_ref, ggla_ref,
                  wconv_ref, wout_ref, gmoe_ref, wrt_ref, brt_ref,
                  x1_ref, route_ref,
                  st_ref, carry_ref, cum_ref):
    f32, bf16 = jnp.float32, jnp.bfloat16
    T = MIX_TILE

    @pl.when(pl.program_id(1) == 0)
    def _():
        st_ref[...] = jnp.zeros_like(st_ref)
        carry_ref[...] = jnp.zeros_like(carry_ref)

    x = x_ref[0]
    hb = _rms(x, gmix_ref[...]).astype(bf16)
    qkvg = _dot(hb, wqkvg_ref[...])
    q = qkvg[:, :GLA_QK] * (GLA_DK ** -0.5)
    k = qkvg[:, GLA_QK:2 * GLA_QK]
    v = qkvg[:, 2 * GLA_QK:2 * GLA_QK + GLA_V]
    g = qkvg[:, 2 * GLA_QK + GLA_V:]
    a_low = _dot(hb, wa_ref[...])
    z = jnp.dot(a_low, wgate_ref[...], precision=lax.Precision.HIGHEST,
                preferred_element_type=f32) + bgate_ref[...]
    la = (jnp.minimum(z, 0.0) - jnp.log1p(jnp.exp(-jnp.abs(z)))) * (1.0 / GLA_TAU)

    row = lax.broadcasted_iota(jnp.int32, (T, GLA_QK), 0)
    col = lax.broadcasted_iota(jnp.int32, (T, GLA_QK), 1)
    lane_head = col // GLA_DK

    cum = la
    for j in range(MIX_LEVELS):
        sh = 1 << j
        cum = cum + jnp.where(row >= sh, _shift_rows(cum, sh), 0.0)
    cum_ref[...] = cum

    def level_exponents(l):
        if l == 0:
            return la, jnp.zeros_like(la)
        if l in (1, 2):
            blk = 1 << l
            r = row % blk
            d_q, d_k = la, jnp.zeros_like(la)
            for j in range(1, blk):
                d_q = d_q + jnp.where(r >= j, _shift_rows(la, j), 0.0)
                d_k = d_k + jnp.where(r <= blk - 1 - j, _shift_rows(la, -j), 0.0)
            return d_q, d_k
        blk = 1 << l
        cend = _block_end_rows(cum_ref, blk)
        d_q = cum - jnp.where(row >= blk, _shift_rows(cend, blk), 0.0)
        return d_q, cend - cum

    tt = lax.broadcasted_iota(jnp.int32, (T, T), 0)
    ss = lax.broadcasted_iota(jnp.int32, (T, T), 1)
    txs = jnp.bitwise_xor(tt, ss)
    pair_level = jnp.zeros((T, T), jnp.int32)
    for j in range(1, MIX_LEVELS):
        pair_level = pair_level + (txs >= (1 << j)).astype(jnp.int32)
    pair_level = jnp.where(tt > ss, pair_level, jnp.where(tt == ss, MIX_LEVELS, -1))

    scores = [jnp.zeros((T, T), f32) for _ in range(GLA_HEADS)]
    for l in range(MIX_LEVELS + 1):
        if l < MIX_LEVELS:
            d_q, d_k = level_exponents(l)
            ql = (q * jnp.exp(d_q)).astype(bf16)
            kl = (k * jnp.exp(d_k)).astype(bf16)
        else:
            ql, kl = q.astype(bf16), k.astype(bf16)
        sel = pair_level == l
        for h in range(GLA_HEADS):
            p = _dot_nt(jnp.where(lane_head == h, ql, jnp.zeros_like(ql)), kl)
            scores[h] = scores[h] + jnp.where(sel, p, 0.0)

    cum_last = cum_ref[pl.ds(T - 1, 1), :]
    q_in = (q * jnp.exp(cum)).astype(bf16)
    k_out = (k * jnp.exp(cum_last - cum)).astype(bf16)
    st = st_ref[...]
    st_b = st.astype(bf16)
    new_st = st * jnp.exp(cum_last)
    lane_head_st = lax.broadcasted_iota(jnp.int32, (GLA_DV, GLA_QK), 1) // GLA_DK
    ggla = ggla_ref[...]
    y_heads = []
    for h in range(GLA_HEADS):
        v_h = v[:, h * GLA_DV:(h + 1) * GLA_DV]
        o = _dot(scores[h].astype(bf16), v_h.astype(bf16))
        o = o + _dot_nt(jnp.where(lane_head == h, q_in, jnp.zeros_like(q_in)), st_b)
        upd = _dot(v_h.T.astype(bf16), k_out)
        new_st = new_st + jnp.where(lane_head_st == h, upd, 0.0)
        g_h = g[:, h * GLA_DV:(h + 1) * GLA_DV]
        y_heads.append(_rms(o, ggla) * (g_h * jax.nn.sigmoid(g_h)))
    st_ref[...] = new_st

    c3 = _dot(hb, wc3_ref[...])
    cw = c3.shape[1] // 3
    cb, cu = c3[:, :cw], c3[:, cw:2 * cw] * c3[:, 2 * cw:]
    crow = lax.broadcasted_iota(jnp.int32, (T, cw), 0)
    prev2, prev1 = carry_ref[0:1, :], carry_ref[1:2, :]
    m1 = jnp.where(crow == 0, prev1, _shift_rows(cu, 1))
    m2 = jnp.where(crow == 0, prev2, jnp.where(crow == 1, prev1, _shift_rows(cu, 2)))
    wconv = wconv_ref[...]
    y_conv = cb * (wconv[0:1, :] * m2 + wconv[1:2, :] * m1 + wconv[2:3, :] * cu)
    carry_ref[0:2, :] = cu[T - 2:, :]

    y = jnp.concatenate(y_heads + [y_conv], axis=1).astype(bf16)
    x1 = x + _dot(y, wout_ref[...])
    x1_ref[0] = x1

    h2 = _rms(x1, gmoe_ref[...])
    logits = _dot_nt(wrt_ref[...], h2, precision=lax.Precision.HIGHEST) + brt_ref[...]
    gl = [logits[i:i + 1, :] for i in range(N_GROUPS)]
    gmax = functools.reduce(jnp.maximum, gl)
    gsum = functools.reduce(lambda a, b: a + b, [jnp.exp(t - gmax) for t in gl])
    p_grp = 1.0 / gsum
    g_sel = jnp.full_like(gmax, N_GROUPS - 1).astype(jnp.int32)
    for i in reversed(range(N_GROUPS - 1)):
        g_sel = jnp.where(gl[i] == gmax, i, g_sel)
    ig = []
    for j in range(EXPERTS_PER_GROUP):
        acc = jnp.zeros_like(gmax)
        for gi in range(N_GROUPS):
            r0 = ROUTE_EXPERT_ROW0 + gi * EXPERTS_PER_GROUP + j
            acc = acc + jnp.where(g_sel == gi, logits[r0:r0 + 1, :], 0.0)
        ig.append(acc)

    def first_argmax(vals):
        m = functools.reduce(jnp.maximum, vals)
        idx = jnp.full_like(m, len(vals) - 1).astype(jnp.int32)
        for i in reversed(range(len(vals) - 1)):
            idx = jnp.where(vals[i] == m, i, idx)
        return m, idx

    m1_, i1 = first_argmax(ig)
    m2_, i2 = first_argmax([jnp.where(i1 == j, -jnp.inf, ig[j]) for j in range(EXPERTS_PER_GROUP)])
    e21 = jnp.exp(m2_ - m1_)
    w1 = p_grp / (1.0 + e21)
    w2 = p_grp * e21 / (1.0 + e21)
    e1 = g_sel * EXPERTS_PER_GROUP + i1
    e2 = g_sel * EXPERTS_PER_GROUP + i2
    rr = lax.broadcasted_iota(jnp.int32, (LANES, T), 0)
    comb_t = (jnp.where(rr == jnp.broadcast_to(e1, (LANES, T)), jnp.broadcast_to(w1, (LANES, T)), 0.0)
              + jnp.where(rr == jnp.broadcast_to(e2, (LANES, T)), jnp.broadcast_to(w2, (LANES, T)), 0.0))
    route_ref[...] = comb_t.T


def _moe_ple_kernel(x1_ref, comb_ref, p_ref, gmoe_ref, wg_ref, wu_ref, wd_ref,
                    gple_ref, wpg_ref, wpp_ref, gfin_ref, out_ref, h2_ref, acc_ref):
    bf16 = jnp.bfloat16
    e = pl.program_id(1)

    @pl.when(e == 0)
    def _():
        h2_ref[...] = _rms(x1_ref[...], gmoe_ref[...]).astype(bf16)
        acc_ref[...] = jnp.zeros_like(acc_ref)

    h2 = h2_ref[...]
    gate = _dot(h2, wg_ref[0])
    hid = (gate * jax.nn.sigmoid(gate)) * _dot(h2, wu_ref[0])
    comb = comb_ref[...]
    lane = lax.broadcasted_iota(jnp.int32, comb.shape, 1)
    c_e = jnp.sum(jnp.where(lane == e, comb, 0.0), axis=1, keepdims=True)
    acc_ref[...] += c_e * _dot(hid.astype(bf16), wd_ref[0])

    @pl.when(e == N_EXPERTS - 1)
    def _():
        x2 = x1_ref[...] + acc_ref[...]
        gate_p = jax.nn.sigmoid(_dot(_rms(x2, gple_ref[...]).astype(bf16), wpg_ref[...]))
        x3 = x2 + gate_p * _dot(p_ref[...].astype(bf16), wpp_ref[...])
        out_ref[...] = _rms(x3, gfin_ref[...])


def _const_spec(shape):
    return pl.BlockSpec(shape, lambda *_: (0,) * len(shape))


def _mixer(x, g_mix, w_in, w_gla_gate, b_gla_gate, g_gla_out, w_conv, w_out, g_moe,
           w_group, b_group, w_router, b_router):
    b, s, d = x.shape
    bf16 = jnp.bfloat16
    n_qkvg = 2 * GLA_QK + 2 * GLA_V
    w_qkvg = w_in[:, :n_qkvg].astype(bf16)
    w_a = w_in[:, n_qkvg:n_qkvg + GLA_LOWRANK].astype(bf16)
    w_c3 = w_in[:, n_qkvg + GLA_LOWRANK:].astype(bf16)
    cw = w_c3.shape[1] // 3
    wrt = jnp.zeros((ROUTE_ROWS, d), jnp.float32)
    wrt = wrt.at[:N_GROUPS].set(w_group.T).at[ROUTE_EXPERT_ROW0:ROUTE_EXPERT_ROW0 + N_EXPERTS].set(w_router.T)
    brt = jnp.zeros((ROUTE_ROWS, 1), jnp.float32)
    brt = brt.at[:N_GROUPS, 0].set(b_group).at[ROUTE_EXPERT_ROW0:ROUTE_EXPERT_ROW0 + N_EXPERTS, 0].set(b_router)
    args = (x, g_mix[None, :], w_qkvg, w_a, w_c3, w_gla_gate, b_gla_gate[None, :], g_gla_out[None, :],
            w_conv, w_out.astype(bf16), g_moe[None, :], wrt, brt)
    in_specs = [pl.BlockSpec((1, MIX_TILE, d), lambda i, j: (i, j, 0))]
    in_specs += [_const_spec(a.shape) for a in args[1:]]
    return pl.pallas_call(
        _mixer_kernel,
        grid=(b, s // MIX_TILE),
        in_specs=in_specs,
        out_specs=[pl.BlockSpec((1, MIX_TILE, d), lambda i, j: (i, j, 0)),
                   pl.BlockSpec((MIX_TILE, LANES), lambda i, j, n=s // MIX_TILE: (i * n + j, 0))],
        out_shape=[jax.ShapeDtypeStruct((b, s, d), jnp.float32),
                   jax.ShapeDtypeStruct((b * s, LANES), jnp.float32)],
        scratch_shapes=[pltpu.VMEM((GLA_DV, GLA_QK), jnp.float32),
                        pltpu.VMEM((8, cw), jnp.float32),
                        pltpu.VMEM((MIX_TILE, GLA_QK), jnp.float32)],
        compiler_params=pltpu.CompilerParams(dimension_semantics=("arbitrary", "arbitrary"),
                                             vmem_limit_bytes=VMEM_LIMIT),
        name="mixer",
    )(*args)


def _moe_ple(x1, comb, p, g_moe, w_gate, w_up, w_down, g_ple, w_ple_gate, w_ple_proj, g_final):
    t, d = x1.shape
    bf16 = jnp.bfloat16
    de = w_gate.shape[-1]
    dp = p.shape[-1]
    tile = lambda w: pl.BlockSpec((MOE_TILE, w), lambda i, e: (i, 0))
    return pl.pallas_call(
        _moe_ple_kernel,
        grid=(t // MOE_TILE, N_EXPERTS),
        in_specs=[tile(d), tile(LANES), tile(dp), _const_spec((1, d)),
                  pl.BlockSpec((1, d, de), lambda i, e: (e, 0, 0)),
                  pl.BlockSpec((1, d, de), lambda i, e: (e, 0, 0)),
                  pl.BlockSpec((1, de, d), lambda i, e: (e, 0, 0)),
                  _const_spec((1, d)), _const_spec((d, d)), _const_spec((dp, d)), _const_spec((1, d))],
        out_specs=tile(d),
        out_shape=jax.ShapeDtypeStruct((t, d), jnp.float32),
        scratch_shapes=[pltpu.VMEM((MOE_TILE, d), bf16), pltpu.VMEM((MOE_TILE, d), jnp.float32)],
        compiler_params=pltpu.CompilerParams(dimension_semantics=("arbitrary", "arbitrary"),
                                             vmem_limit_bytes=VMEM_LIMIT),
        name="moe_ple",
    )(x1, comb, p, g_moe[None, :], w_gate.astype(bf16), w_up.astype(bf16), w_down.astype(bf16),
      g_ple[None, :], w_ple_gate.astype(bf16), w_ple_proj.astype(bf16), g_final[None, :])


def kernel(x, p, g_mix, w_in, w_gla_gate, b_gla_gate, g_gla_out, w_conv, w_out, g_moe, w_group, b_group,
           w_router, b_router, w_exp_gate, w_exp_up, w_exp_down, g_ple, w_ple_gate, w_ple_proj, g_final):
    depth = w_in.shape[0]
    assert depth == 1, "the final norm is fused into the last (only) layer"
    b, s, d = x.shape
    assert s % MIX_TILE == 0 and (b * s) % MOE_TILE == 0
    x1, comb = _mixer(x, g_mix[0], w_in[0], w_gla_gate[0], b_gla_gate[0], g_gla_out[0], w_conv[0], w_out[0],
                      g_moe[0], w_group[0], b_group[0], w_router[0], b_router[0])
    out = _moe_ple(x1.reshape(b * s, d), comb, p[0].reshape(b * s, -1), g_moe[0],
                   w_exp_gate[0], w_exp_up[0], w_exp_down[0], g_ple[0], w_ple_gate[0], w_ple_proj[0], g_final)
    return out.reshape(b, s, d)
```

```python
import functools

import jax
import jax.numpy as jnp
from jax import lax
from jax.experimental import pallas as pl
from jax.experimental.pallas import tpu as pltpu

EPS = 1e-6
GLA_HEADS = 4
GLA_DK = 64
GLA_DV = 128
GLA_QK = GLA_HEADS * GLA_DK
GLA_V = GLA_HEADS * GLA_DV
GLA_LOWRANK = 16
GLA_TAU = 16.0
CONV_K = 3
N_GROUPS = 4
EXPERTS_PER_GROUP = 4
N_EXPERTS = N_GROUPS * EXPERTS_PER_GROUP

LANES = 128
MIX_TILE = 256
MIX_LEVELS = 8
ROUTE_ROWS = 32
ROUTE_EXPERT_ROW0 = 8
MOE_TILE = 256
VMEM_LIMIT = 56 * 1024 * 1024

PAIR_A = (0, 2, 2, 0, 0, 1)
PAIR_B = (1, 1, 3, 3, 2, 3)
PAIR_OF_KEY = {1: 0, 6: 1, 11: 2, 3: 3, 2: 4, 7: 5}
N_PAIRS = len(PAIR_A)
N_CLASSES = N_GROUPS * N_PAIRS
INFO_CLS, INFO_RANK, INFO_WA, INFO_WB = 0, 1, 2, 3
DMA_UNROLL = 8

_NT = (((1,), (1,)), ((), ()))


def _rms(x, g):
    return x * lax.rsqrt(jnp.mean(x * x, axis=-1, keepdims=True) + EPS) * g


def _dot(a, b):
    return jnp.dot(a, b, preferred_element_type=jnp.float32)


def _dot_nt(a, b, precision=None):
    return lax.dot_general(a, b, _NT, precision=precision, preferred_element_type=jnp.float32)


def _shift_rows(x, shift):
    return pltpu.roll(x, shift % x.shape[0], axis=0)


def _block_end_rows(cum_ref, blk):
    n = MIX_TILE // blk
    width = cum_ref.shape[1]
    pieces = [jnp.broadcast_to(cum_ref[pl.ds(b * blk + blk - 1, 1), :], (blk, width)) for b in range(n)]
    return jnp.concatenate(pieces, axis=0)


def _mixer_kernel(x_ref, gmix_ref, wqkvg_ref, wa_ref, wc3_ref, wgate_ref, bgate_ref, ggla_ref,
                  wconv_ref, wout_ref, gmoe_ref, wrt_ref, brt_ref,
                  rows_ref, info_ref, counts_ref,
                  st_ref, carry_ref, cum_ref, count_ref):
    f32, bf16 = jnp.float32, jnp.bfloat16
    T = MIX_TILE
    D = x_ref.shape[-1]

    @pl.when(pl.program_id(1) == 0)
    def _():
        st_ref[...] = jnp.zeros_like(st_ref)
        carry_ref[...] = jnp.zeros_like(carry_ref)

    @pl.when((pl.program_id(0) == 0) & (pl.program_id(1) == 0))
    def _():
        count_ref[...] = jnp.zeros_like(count_ref)

    x = x_ref[0]
    hb = _rms(x, gmix_ref[...]).astype(bf16)
    qkvg = _dot(hb, wqkvg_ref[...])
    q = qkvg[:, :GLA_QK] * (GLA_DK ** -0.5)
    k = qkvg[:, GLA_QK:2 * GLA_QK]
    v = qkvg[:, 2 * GLA_QK:2 * GLA_QK + GLA_V]
    g = qkvg[:, 2 * GLA_QK + GLA_V:]
    a_low = _dot(hb, wa_ref[...])
    z = jnp.dot(a_low, wgate_ref[...], precision=lax.Precision.HIGHEST,
                preferred_element_type=f32) + bgate_ref[...]
    la = (jnp.minimum(z, 0.0) - jnp.log1p(jnp.exp(-jnp.abs(z)))) * (1.0 / GLA_TAU)

    row = lax.broadcasted_iota(jnp.int32, (T, GLA_QK), 0)
    col = lax.broadcasted_iota(jnp.int32, (T, GLA_QK), 1)
    lane_head = col // GLA_DK

    cum = la
    for j in range(MIX_LEVELS):
        sh = 1 << j
        cum = cum + jnp.where(row >= sh, _shift_rows(cum, sh), 0.0)
    cum_ref[...] = cum

    def level_exponents(l):
        if l == 0:
            return la, jnp.zeros_like(la)
        if l in (1, 2):
            blk = 1 << l
            r = row % blk
            d_q, d_k = la, jnp.zeros_like(la)
            for j in range(1, blk):
                d_q = d_q + jnp.where(r >= j, _shift_rows(la, j), 0.0)
                d_k = d_k + jnp.where(r <= blk - 1 - j, _shift_rows(la, -j), 0.0)
            return d_q, d_k
        blk = 1 << l
        cend = _block_end_rows(cum_ref, blk)
        d_q = cum - jnp.where(row >= blk, _shift_rows(cend, blk), 0.0)
        return d_q, cend - cum

    tt = lax.broadcasted_iota(jnp.int32, (T, T), 0)
    ss = lax.broadcasted_iota(jnp.int32, (T, T), 1)
    txs = jnp.bitwise_xor(tt, ss)
    pair_level = jnp.zeros((T, T), jnp.int32)
    for j in range(1, MIX_LEVELS):
        pair_level = pair_level + (txs >= (1 << j)).astype(jnp.int32)
    pair_level = jnp.where(tt > ss, pair_level, jnp.where(tt == ss, MIX_LEVELS, -1))

    scores = [jnp.zeros((T, T), f32) for _ in range(GLA_HEADS)]
    for l in range(MIX_LEVELS + 1):
        if l < MIX_LEVELS:
            d_q, d_k = level_exponents(l)
            ql = (q * jnp.exp(d_q)).astype(bf16)
            kl = (k * jnp.exp(d_k)).astype(bf16)
        else:
            ql, kl = q.astype(bf16), k.astype(bf16)
        sel = pair_level == l
        for h in range(GLA_HEADS):
            p = _dot_nt(jnp.where(lane_head == h, ql, jnp.zeros_like(ql)), kl)
            scores[h] = scores[h] + jnp.where(sel, p, 0.0)

    cum_last = cum_ref[pl.ds(T - 1, 1), :]
    q_in = (q * jnp.exp(cum)).astype(bf16)
    k_out = (k * jnp.exp(cum_last - cum)).astype(bf16)
    st = st_ref[...]
    st_b = st.astype(bf16)
    new_st = st * jnp.exp(cum_last)
    lane_head_st = lax.broadcasted_iota(jnp.int32, (GLA_DV, GLA_QK), 1) // GLA_DK
    ggla = ggla_ref[...]
    y_heads = []
    for h in range(GLA_HEADS):
        v_h = v[:, h * GLA_DV:(h + 1) * GLA_DV]
        o = _dot(scores[h].astype(bf16), v_h.astype(bf16))
        o = o + _dot_nt(jnp.where(lane_head == h, q_in, jnp.zeros_like(q_in)), st_b)
        upd = _dot(v_h.T.astype(bf16), k_out)
        new_st = new_st + jnp.where(lane_head_st == h, upd, 0.0)
        g_h = g[:, h * GLA_DV:(h + 1) * GLA_DV]
        y_heads.append(_rms(o, ggla) * (g_h * jax.nn.sigmoid(g_h)))
    st_ref[...] = new_st

    c3 = _dot(hb, wc3_ref[...])
    cw = c3.shape[1] // 3
    cb, cu = c3[:, :cw], c3[:, cw:2 * cw] * c3[:, 2 * cw:]
    crow = lax.broadcasted_iota(jnp.int32, (T, cw), 0)
    prev2, prev1 = carry_ref[0:1, :], carry_ref[1:2, :]
    m1 = jnp.where(crow == 0, prev1, _shift_rows(cu, 1))
    m2 = jnp.where(crow == 0, prev2, jnp.where(crow == 1, prev1, _shift_rows(cu, 2)))
    wconv = wconv_ref[...]
    y_conv = cb * (wconv[0:1, :] * m2 + wconv[1:2, :] * m1 + wconv[2:3, :] * cu)
    carry_ref[0:2, :] = cu[T - 2:, :]

    y = jnp.concatenate(y_heads + [y_conv], axis=1).astype(bf16)
    x1 = x + _dot(y, wout_ref[...])
    rows_ref[:, :D] = x1

    h2 = _rms(x1, gmoe_ref[...])
    logits = _dot_nt(wrt_ref[...], h2, precision=lax.Precision.HIGHEST) + brt_ref[...]
    gl = [logits[i:i + 1, :] for i in range(N_GROUPS)]
    gmax = functools.reduce(jnp.maximum, gl)
    gsum = functools.reduce(lambda a, b: a + b, [jnp.exp(t - gmax) for t in gl])
    p_grp = 1.0 / gsum
    g_sel = jnp.full_like(gmax, N_GROUPS - 1).astype(jnp.int32)
    for i in reversed(range(N_GROUPS - 1)):
        g_sel = jnp.where(gl[i] == gmax, i, g_sel)
    ig = []
    for j in range(EXPERTS_PER_GROUP):
        acc = jnp.zeros_like(gmax)
        for gi in range(N_GROUPS):
            r0 = ROUTE_EXPERT_ROW0 + gi * EXPERTS_PER_GROUP + j
            acc = acc + jnp.where(g_sel == gi, logits[r0:r0 + 1, :], 0.0)
        ig.append(acc)

    def first_argmax(vals):
        m = functools.reduce(jnp.maximum, vals)
        idx = jnp.full_like(m, len(vals) - 1).astype(jnp.int32)
        for i in reversed(range(len(vals) - 1)):
            idx = jnp.where(vals[i] == m, i, idx)
        return m, idx

    m1_, i1 = first_argmax(ig)
    m2_, i2 = first_argmax([jnp.where(i1 == j, -jnp.inf, ig[j]) for j in range(EXPERTS_PER_GROUP)])
    e21 = jnp.exp(m2_ - m1_)
    w1 = p_grp / (1.0 + e21)
    w2 = p_grp * e21 / (1.0 + e21)
    key = jnp.minimum(i1, i2) * EXPERTS_PER_GROUP + jnp.maximum(i1, i2)
    pair = jnp.zeros_like(key)
    a_loc = jnp.zeros_like(key)
    for kk, pp in PAIR_OF_KEY.items():
        pair = jnp.where(key == kk, pp, pair)
        a_loc = jnp.where(key == kk, PAIR_A[pp], a_loc)
    w_a = jnp.where(i1 == a_loc, w1, w2)
    w_b = jnp.where(i1 == a_loc, w2, w1)
    cls = (g_sel * N_PAIRS + pair).astype(f32)
    rr = lax.broadcasted_iota(jnp.int32, (LANES, T), 0)
    rec_t = (jnp.where(rr == INFO_CLS, jnp.broadcast_to(cls, (LANES, T)), 0.0)
             + jnp.where(rr == INFO_WA, jnp.broadcast_to(w_a, (LANES, T)), 0.0)
             + jnp.where(rr == INFO_WB, jnp.broadcast_to(w_b, (LANES, T)), 0.0))
    rec = rec_t.T

    lane = lax.broadcasted_iota(jnp.int32, (T, LANES), 1)
    onehot = (lane == rec[:, INFO_CLS:INFO_CLS + 1].astype(jnp.int32)).astype(f32)
    tt2 = lax.broadcasted_iota(jnp.int32, (T, T), 0)
    ss2 = lax.broadcasted_iota(jnp.int32, (T, T), 1)
    earlier = (ss2 < tt2).astype(bf16)
    before = _dot(earlier, onehot.astype(bf16)) + count_ref[0:1, :]
    rank = jnp.sum(onehot * before, axis=1, keepdims=True)
    rec = jnp.where(lane == INFO_RANK, rank, rec)
    rows_ref[:, D:] = rec
    info_ref[...] = rec
    new_count = count_ref[0:1, :] + jnp.sum(onehot, axis=0, keepdims=True)
    count_ref[...] = jnp.broadcast_to(new_count, count_ref.shape)
    counts_ref[...] = jnp.broadcast_to(new_count, counts_ref.shape)


class _RowGather:
    def __init__(self, idx_ref, src_hbm, buf, sems):
        self.idx_ref, self.src_hbm, self.buf, self.sems = idx_ref, src_hbm, buf, sems

    def start(self, tile):
        slot = tile % 2

        def issue(it, carry):
            for u in range(DMA_UNROLL):
                r = it * DMA_UNROLL + u
                pltpu.make_async_copy(self.src_hbm.at[pl.ds(self.idx_ref[tile * MOE_TILE + r], 1), :],
                                      self.buf.at[slot, pl.ds(r, 1), :], self.sems.at[slot]).start()
            return carry

        lax.fori_loop(0, MOE_TILE // DMA_UNROLL, issue, 0)

    def wait(self, tile):
        slot = tile % 2
        pltpu.make_async_copy(self.src_hbm.at[pl.ds(0, MOE_TILE), :], self.buf.at[slot], self.sems.at[slot]).wait()
        return self.buf[slot]

    def pipeline_step(self, step, n_steps):
        @pl.when((step == 0) & (n_steps > 0))
        def _():
            self.start(step)

        @pl.when(step + 1 < n_steps)
        def _():
            self.start(step + 1)


def _expert_kernel(ea_ref, eb_ref, nused_ref, src_ref, rows_hbm, gmoe_ref, wga_ref, wua_ref, wda_ref,
                   wgb_ref, wub_ref, wdb_ref, y_ref, buf, sems):
    bf16 = jnp.bfloat16
    D = y_ref.shape[-1]
    step, n_used = pl.program_id(0), nused_ref[0]
    gather = _RowGather(src_ref, rows_hbm, buf, sems)
    gather.pipeline_step(step, n_used)

    @pl.when(step < n_used)
    def _():
        rows = gather.wait(step)
        rec = rows[:, D:]
        h2 = _rms(rows[:, :D], gmoe_ref[...]).astype(bf16)

        def expert(wg_ref, wu_ref, wd_ref):
            gate = _dot(h2, wg_ref[0])
            hid = (gate * jax.nn.sigmoid(gate)) * _dot(h2, wu_ref[0])
            return _dot(hid.astype(bf16), wd_ref[0])

        y = rec[:, INFO_WA:INFO_WA + 1] * expert(wga_ref, wua_ref, wda_ref)
        y_ref[...] = y + rec[:, INFO_WB:INFO_WB + 1] * expert(wgb_ref, wub_ref, wdb_ref)

    @pl.when(step >= n_used)
    def _():
        y_ref[...] = jnp.zeros_like(y_ref)


def _ple_final_kernel(slot_ref, x1_ref, p_ref, y_hbm, gple_ref, wpg_ref, wpp_ref, gfin_ref, out_ref, buf, sems):
    bf16 = jnp.bfloat16
    step, n_steps = pl.program_id(0), pl.num_programs(0)
    gather = _RowGather(slot_ref, y_hbm, buf, sems)
    gather.pipeline_step(step, n_steps)
    x2 = x1_ref[...] + gather.wait(step)
    gate_p = jax.nn.sigmoid(_dot(_rms(x2, gple_ref[...]).astype(bf16), wpg_ref[...]))
    x3 = x2 + gate_p * _dot(p_ref[...].astype(bf16), wpp_ref[...])
    out_ref[...] = _rms(x3, gfin_ref[...])


def _const_spec(shape):
    return pl.BlockSpec(shape, lambda *_: (0,) * len(shape))


def _mixer(x, g_mix, w_in, w_gla_gate, b_gla_gate, g_gla_out, w_conv, w_out, g_moe,
           w_group, b_group, w_router, b_router):
    b, s, d = x.shape
    n_s = s // MIX_TILE
    bf16 = jnp.bfloat16
    n_qkvg = 2 * GLA_QK + 2 * GLA_V
    w_qkvg = w_in[:, :n_qkvg].astype(bf16)
    w_a = w_in[:, n_qkvg:n_qkvg + GLA_LOWRANK].astype(bf16)
    w_c3 = w_in[:, n_qkvg + GLA_LOWRANK:].astype(bf16)
    cw = w_c3.shape[1] // 3
    wrt = jnp.zeros((ROUTE_ROWS, d), jnp.float32)
    wrt = wrt.at[:N_GROUPS].set(w_group.T).at[ROUTE_EXPERT_ROW0:ROUTE_EXPERT_ROW0 + N_EXPERTS].set(w_router.T)
    brt = jnp.zeros((ROUTE_ROWS, 1), jnp.float32)
    brt = brt.at[:N_GROUPS, 0].set(b_group).at[ROUTE_EXPERT_ROW0:ROUTE_EXPERT_ROW0 + N_EXPERTS, 0].set(b_router)
    args = (x, g_mix[None, :], w_qkvg, w_a, w_c3, w_gla_gate, b_gla_gate[None, :], g_gla_out[None, :],
            w_conv, w_out.astype(bf16), g_moe[None, :], wrt, brt)
    in_specs = [pl.BlockSpec((1, MIX_TILE, d), lambda i, j: (i, j, 0))]
    in_specs += [_const_spec(a.shape) for a in args[1:]]
    return pl.pallas_call(
        _mixer_kernel,
        grid=(b, n_s),
        in_specs=in_specs,
        out_specs=[pl.BlockSpec((MIX_TILE, d + LANES), lambda i, j: (i * n_s + j, 0)),
                   pl.BlockSpec((MIX_TILE, LANES), lambda i, j: (i * n_s + j, 0)),
                   _const_spec((8, LANES))],
        out_shape=[jax.ShapeDtypeStruct((b * s, d + LANES), jnp.float32),
                   jax.ShapeDtypeStruct((b * s, LANES), jnp.float32),
                   jax.ShapeDtypeStruct((8, LANES), jnp.float32)],
        scratch_shapes=[pltpu.VMEM((GLA_DV, GLA_QK), jnp.float32),
                        pltpu.VMEM((8, cw), jnp.float32),
                        pltpu.VMEM((MIX_TILE, GLA_QK), jnp.float32),
                        pltpu.VMEM((8, LANES), jnp.float32)],
        compiler_params=pltpu.CompilerParams(dimension_semantics=("arbitrary", "arbitrary"),
                                             vmem_limit_bytes=VMEM_LIMIT),
        name="mixer",
    )(*args)


def _sort_plan(info, counts, n_tok):
    i32 = jnp.int32
    n_tiles = n_tok // MOE_TILE + N_CLASSES
    cls = info[:, INFO_CLS].astype(i32)
    rank = info[:, INFO_RANK].astype(i32)
    cnt = counts[0, :N_CLASSES].astype(i32)
    tiles_per_cls = (cnt + MOE_TILE - 1) // MOE_TILE
    tile_end = jnp.cumsum(tiles_per_cls)
    tile_start = tile_end - tiles_per_cls
    n_used = tile_end[-1]
    slot = tile_start[cls] * MOE_TILE + rank
    src = jnp.zeros((n_tiles * MOE_TILE,), i32).at[slot].set(jnp.arange(n_tok, dtype=i32))
    tile_id = jnp.minimum(jnp.arange(n_tiles, dtype=i32), n_used - 1)
    tile_cls = jnp.searchsorted(tile_end, tile_id, side="right").astype(i32)
    grp, pair = tile_cls // N_PAIRS, tile_cls % N_PAIRS
    e_a = grp * EXPERTS_PER_GROUP + jnp.asarray(PAIR_A, i32)[pair]
    e_b = grp * EXPERTS_PER_GROUP + jnp.asarray(PAIR_B, i32)[pair]
    return slot, src, e_a, e_b, n_used.reshape(1), n_tiles


def _experts(plan, rows, g_moe, w_gate, w_up, w_down):
    _, src, e_a, e_b, n_used, n_tiles = plan
    bf16 = jnp.bfloat16
    d, de = w_gate.shape[-2:]
    w_a = lambda shape: pl.BlockSpec(shape, lambda i, ea, eb, nu, sr: (ea[i], 0, 0))
    w_b = lambda shape: pl.BlockSpec(shape, lambda i, ea, eb, nu, sr: (eb[i], 0, 0))
    wg, wu, wd = w_gate.astype(bf16), w_up.astype(bf16), w_down.astype(bf16)
    return pl.pallas_call(
        _expert_kernel,
        grid_spec=pltpu.PrefetchScalarGridSpec(
            num_scalar_prefetch=4, grid=(n_tiles,),
            in_specs=[pl.BlockSpec(memory_space=pl.ANY),
                      pl.BlockSpec((1, d), lambda i, *_: (0, 0)),
                      w_a((1, d, de)), w_a((1, d, de)), w_a((1, de, d)),
                      w_b((1, d, de)), w_b((1, d, de)), w_b((1, de, d))],
            out_specs=pl.BlockSpec((MOE_TILE, d), lambda i, *_: (i, 0)),
            scratch_shapes=[pltpu.VMEM((2, MOE_TILE, rows.shape[1]), jnp.float32),
                            pltpu.SemaphoreType.DMA((2,))]),
        out_shape=jax.ShapeDtypeStruct((n_tiles * MOE_TILE, d), jnp.float32),
        compiler_params=pltpu.CompilerParams(dimension_semantics=("arbitrary",),
                                             vmem_limit_bytes=VMEM_LIMIT),
        name="experts",
    )(e_a, e_b, n_used, src, rows, g_moe[None, :], wg, wu, wd, wg, wu, wd)


def _ple_final(slot, rows, p, y_sorted, g_ple, w_ple_gate, w_ple_proj, g_final):
    n_tok, dp = p.shape
    d = y_sorted.shape[1]
    bf16 = jnp.bfloat16
    const = lambda shape: pl.BlockSpec(shape, lambda i, sl: (0,) * len(shape))
    return pl.pallas_call(
        _ple_final_kernel,
        grid_spec=pltpu.PrefetchScalarGridSpec(
            num_scalar_prefetch=1, grid=(n_tok // MOE_TILE,),
            in_specs=[pl.BlockSpec((MOE_TILE, d), lambda i, sl: (i, 0)),
                      pl.BlockSpec((MOE_TILE, dp), lambda i, sl: (i, 0)),
                      pl.BlockSpec(memory_space=pl.ANY),
                      const((1, d)), const((d, d)), const((dp, d)), const((1, d))],
            out_specs=pl.BlockSpec((MOE_TILE, d), lambda i, sl: (i, 0)),
            scratch_shapes=[pltpu.VMEM((2, MOE_TILE, d), jnp.float32),
                            pltpu.SemaphoreType.DMA((2,))]),
        out_shape=jax.ShapeDtypeStruct((n_tok, d), jnp.float32),
        compiler_params=pltpu.CompilerParams(dimension_semantics=("arbitrary",),
                                             vmem_limit_bytes=VMEM_LIMIT),
        name="ple_final",
    )(slot, rows, p, y_sorted, g_ple[None, :], w_ple_gate.astype(bf16), w_ple_proj.astype(bf16), g_final[None, :])


def kernel(x, p, g_mix, w_in, w_gla_gate, b_gla_gate, g_gla_out, w_conv, w_out, g_moe, w_group, b_group,
           w_router, b_router, w_exp_gate, w_exp_up, w_exp_down, g_ple, w_ple_gate, w_ple_proj, g_final):
    depth = w_in.shape[0]
    assert depth == 1, "the final norm is fused into the last (only) layer"
    b, s, d = x.shape
    n_tok = b * s
    assert s % MIX_TILE == 0 and n_tok % MOE_TILE == 0 and MOE_TILE % DMA_UNROLL == 0
    rows, info, counts = _mixer(x, g_mix[0], w_in[0], w_gla_gate[0], b_gla_gate[0], g_gla_out[0], w_conv[0],
                                w_out[0], g_moe[0], w_group[0], b_group[0], w_router[0], b_router[0])
    plan = _sort_plan(info, counts, n_tok)
    y_sorted = _experts(plan, rows, g_moe[0], w_exp_gate[0], w_exp_up[0], w_exp_down[0])
    out = _ple_final(plan[0], rows, p[0].reshape(n_tok, -1), y_sorted, g_ple[0], w_ple_gate[0], w_ple_proj[0],
                     g_final)
    return out.reshape(b, s, d)
```

```python
import functools

import jax
import jax.numpy as jnp
from jax import lax
from jax.experimental import pallas as pl
from jax.experimental.pallas import tpu as pltpu

EPS = 1e-6
GLA_HEADS = 4
GLA_DK = 64
GLA_DV = 128
GLA_QK = GLA_HEADS * GLA_DK
GLA_V = GLA_HEADS * GLA_DV
GLA_LOWRANK = 16
GLA_TAU = 16.0
CONV_K = 3
N_GROUPS = 4
EXPERTS_PER_GROUP = 4
N_EXPERTS = N_GROUPS * EXPERTS_PER_GROUP

LANES = 128
MIX_TILE = 256
MIX_LEVELS = 8
ROUTE_ROWS = 32
ROUTE_EXPERT_ROW0 = 8
MOE_TILE = 256
VMEM_LIMIT = 56 * 1024 * 1024

PAIR_A = (0, 2, 2, 0, 0, 1)
PAIR_B = (1, 1, 3, 3, 2, 3)
PAIR_OF_KEY = {1: 0, 6: 1, 11: 2, 3: 3, 2: 4, 7: 5}
N_PAIRS = len(PAIR_A)
N_CLASSES = N_GROUPS * N_PAIRS
INFO_CLS, INFO_RANK, INFO_WA, INFO_WB = 0, 1, 2, 3
DMA_UNROLL = 8

_NT = (((1,), (1,)), ((), ()))


def _rms(x, g):
    return x * lax.rsqrt(jnp.mean(x * x, axis=-1, keepdims=True) + EPS) * g


def _dot(a, b):
    return jnp.dot(a, b, preferred_element_type=jnp.float32)


def _dot_nt(a, b, precision=None):
    return lax.dot_general(a, b, _NT, precision=precision, preferred_element_type=jnp.float32)


def _shift_rows(x, shift):
    return pltpu.roll(x, shift % x.shape[0], axis=0)


def _block_end_rows(cum_ref, blk):
    n = MIX_TILE // blk
    width = cum_ref.shape[1]
    pieces = [jnp.broadcast_to(cum_ref[pl.ds(b * blk + blk - 1, 1), :], (blk, width)) for b in range(n)]
    return jnp.concatenate(pieces, axis=0)


def _mixer_kernel(x_ref, gmix_ref, wqkvg_ref, wa_ref, wc3_ref, wgate_ref, bgate_ref, ggla_ref,
                  wconv_ref, wout_ref, gmoe_ref, wrt_ref, brt_ref,
                  rows_ref, meta_ref, counts_ref,
                  st_ref, carry_ref, cum_ref, count_ref):
    f32, bf16 = jnp.float32, jnp.bfloat16
    T = MIX_TILE
    D = x_ref.shape[-1]

    @pl.when(pl.program_id(1) == 0)
    def _():
        st_ref[...] = jnp.zeros_like(st_ref)
        carry_ref[...] = jnp.zeros_like(carry_ref)

    @pl.when((pl.program_id(0) == 0) & (pl.program_id(1) == 0))
    def _():
        count_ref[...] = jnp.zeros_like(count_ref)

    x = x_ref[0]
    hb = _rms(x, gmix_ref[...]).astype(bf16)
    qkvg = _dot(hb, wqkvg_ref[...])
    q = qkvg[:, :GLA_QK] * (GLA_DK ** -0.5)
    k = qkvg[:, GLA_QK:2 * GLA_QK]
    v = qkvg[:, 2 * GLA_QK:2 * GLA_QK + GLA_V]
    g = qkvg[:, 2 * GLA_QK + GLA_V:]
    a_low = _dot(hb, wa_ref[...])
    z = jnp.dot(a_low, wgate_ref[...], precision=lax.Precision.HIGHEST,
                preferred_element_type=f32) + bgate_ref[...]
    la = (jnp.minimum(z, 0.0) - jnp.log1p(jnp.exp(-jnp.abs(z)))) * (1.0 / GLA_TAU)

    row = lax.broadcasted_iota(jnp.int32, (T, GLA_QK), 0)
    col = lax.broadcasted_iota(jnp.int32, (T, GLA_QK), 1)
    lane_head = col // GLA_DK

    cum = la
    for j in range(MIX_LEVELS):
        sh = 1 << j
        cum = cum + jnp.where(row >= sh, _shift_rows(cum, sh), 0.0)
    cum_ref[...] = cum

    def level_exponents(l):
        if l == 0:
            return la, jnp.zeros_like(la)
        if l in (1, 2):
            blk = 1 << l
            r = row % blk
            d_q, d_k = la, jnp.zeros_like(la)
            for j in range(1, blk):
                d_q = d_q + jnp.where(r >= j, _shift_rows(la, j), 0.0)
                d_k = d_k + jnp.where(r <= blk - 1 - j, _shift_rows(la, -j), 0.0)
            return d_q, d_k
        blk = 1 << l
        cend = _block_end_rows(cum_ref, blk)
        d_q = cum - jnp.where(row >= blk, _shift_rows(cend, blk), 0.0)
        return d_q, cend - cum

    tt = lax.broadcasted_iota(jnp.int32, (T, T), 0)
    ss = lax.broadcasted_iota(jnp.int32, (T, T), 1)
    txs = jnp.bitwise_xor(tt, ss)
    pair_level = jnp.zeros((T, T), jnp.int32)
    for j in range(1, MIX_LEVELS):
        pair_level = pair_level + (txs >= (1 << j)).astype(jnp.int32)
    pair_level = jnp.where(tt > ss, pair_level, jnp.where(tt == ss, MIX_LEVELS, -1))

    scores = [jnp.zeros((T, T), f32) for _ in range(GLA_HEADS)]
    for l in range(MIX_LEVELS + 1):
        if l < MIX_LEVELS:
            d_q, d_k = level_exponents(l)
            ql = (q * jnp.exp(d_q)).astype(bf16)
            kl = (k * jnp.exp(d_k)).astype(bf16)
        else:
            ql, kl = q.astype(bf16), k.astype(bf16)
        sel = pair_level == l
        for h in range(GLA_HEADS):
            p = _dot_nt(jnp.where(lane_head == h, ql, jnp.zeros_like(ql)), kl)
            scores[h] = scores[h] + jnp.where(sel, p, 0.0)

    cum_last = cum_ref[pl.ds(T - 1, 1), :]
    q_in = (q * jnp.exp(cum)).astype(bf16)
    k_out = (k * jnp.exp(cum_last - cum)).astype(bf16)
    st = st_ref[...]
    st_b = st.astype(bf16)
    new_st = st * jnp.exp(cum_last)
    lane_head_st = lax.broadcasted_iota(jnp.int32, (GLA_DV, GLA_QK), 1) // GLA_DK
    ggla = ggla_ref[...]
    y_heads = []
    for h in range(GLA_HEADS):
        v_h = v[:, h * GLA_DV:(h + 1) * GLA_DV]
        o = _dot(scores[h].astype(bf16), v_h.astype(bf16))
        o = o + _dot_nt(jnp.where(lane_head == h, q_in, jnp.zeros_like(q_in)), st_b)
        upd = _dot(v_h.T.astype(bf16), k_out)
        new_st = new_st + jnp.where(lane_head_st == h, upd, 0.0)
        g_h = g[:, h * GLA_DV:(h + 1) * GLA_DV]
        y_heads.append(_rms(o, ggla) * (g_h * jax.nn.sigmoid(g_h)))
    st_ref[...] = new_st

    c3 = _dot(hb, wc3_ref[...])
    cw = c3.shape[1] // 3
    cb, cu = c3[:, :cw], c3[:, cw:2 * cw] * c3[:, 2 * cw:]
    crow = lax.broadcasted_iota(jnp.int32, (T, cw), 0)
    prev2, prev1 = carry_ref[0:1, :], carry_ref[1:2, :]
    m1 = jnp.where(crow == 0, prev1, _shift_rows(cu, 1))
    m2 = jnp.where(crow == 0, prev2, jnp.where(crow == 1, prev1, _shift_rows(cu, 2)))
    wconv = wconv_ref[...]
    y_conv = cb * (wconv[0:1, :] * m2 + wconv[1:2, :] * m1 + wconv[2:3, :] * cu)
    carry_ref[0:2, :] = cu[T - 2:, :]

    y = jnp.concatenate(y_heads + [y_conv], axis=1).astype(bf16)
    x1 = x + _dot(y, wout_ref[...])
    rows_ref[:, :D] = x1

    h2 = _rms(x1, gmoe_ref[...])
    logits = _dot_nt(wrt_ref[...], h2, precision=lax.Precision.HIGHEST) + brt_ref[...]
    gl = [logits[i:i + 1, :] for i in range(N_GROUPS)]
    gmax = functools.reduce(jnp.maximum, gl)
    gsum = functools.reduce(lambda a, b: a + b, [jnp.exp(t - gmax) for t in gl])
    p_grp = 1.0 / gsum
    g_sel = jnp.full_like(gmax, N_GROUPS - 1).astype(jnp.int32)
    for i in reversed(range(N_GROUPS - 1)):
        g_sel = jnp.where(gl[i] == gmax, i, g_sel)
    ig = []
    for j in range(EXPERTS_PER_GROUP):
        acc = jnp.zeros_like(gmax)
        for gi in range(N_GROUPS):
            r0 = ROUTE_EXPERT_ROW0 + gi * EXPERTS_PER_GROUP + j
            acc = acc + jnp.where(g_sel == gi, logits[r0:r0 + 1, :], 0.0)
        ig.append(acc)

    def first_argmax(vals):
        m = functools.reduce(jnp.maximum, vals)
        idx = jnp.full_like(m, len(vals) - 1).astype(jnp.int32)
        for i in reversed(range(len(vals) - 1)):
            idx = jnp.where(vals[i] == m, i, idx)
        return m, idx

    m1_, i1 = first_argmax(ig)
    m2_, i2 = first_argmax([jnp.where(i1 == j, -jnp.inf, ig[j]) for j in range(EXPERTS_PER_GROUP)])
    e21 = jnp.exp(m2_ - m1_)
    w1 = p_grp / (1.0 + e21)
    w2 = p_grp * e21 / (1.0 + e21)
    key = jnp.minimum(i1, i2) * EXPERTS_PER_GROUP + jnp.maximum(i1, i2)
    pair = jnp.zeros_like(key)
    a_loc = jnp.zeros_like(key)
    for kk, pp in PAIR_OF_KEY.items():
        pair = jnp.where(key == kk, pp, pair)
        a_loc = jnp.where(key == kk, PAIR_A[pp], a_loc)
    w_a = jnp.where(i1 == a_loc, w1, w2)
    w_b = jnp.where(i1 == a_loc, w2, w1)
    cls = g_sel * N_PAIRS + pair
    rr = lax.broadcasted_iota(jnp.int32, (LANES, T), 0)
    rec_t = (jnp.where(rr == INFO_WA, jnp.broadcast_to(w_a, (LANES, T)), 0.0)
             + jnp.where(rr == INFO_WB, jnp.broadcast_to(w_b, (LANES, T)), 0.0))
    rows_ref[:, D:] = rec_t.T

    onehot = (rr == jnp.broadcast_to(cls, (LANES, T))).astype(f32)
    earlier = (tt < ss).astype(bf16)
    count = count_ref[:, 0:1]
    before = _dot(onehot.astype(bf16), earlier) + count
    rank = jnp.sum(onehot * before, axis=0, keepdims=True).astype(jnp.int32)
    r8 = lax.broadcasted_iota(jnp.int32, (8, T), 0)
    meta_ref[0] = jnp.where(r8 == INFO_CLS, jnp.broadcast_to(cls, (8, T)),
                            jnp.where(r8 == INFO_RANK, jnp.broadcast_to(rank, (8, T)), 0))
    new_count = jnp.broadcast_to(count + jnp.sum(onehot, axis=1, keepdims=True), count_ref.shape)
    count_ref[...] = new_count
    counts_ref[...] = new_count.astype(jnp.int32)


class _RowGather:
    def __init__(self, index_of, src_hbm, buf, sems):
        self.index_of, self.src_hbm, self.buf, self.sems = index_of, src_hbm, buf, sems

    def start(self, tile, slot):
        for r in range(MOE_TILE):
            pltpu.make_async_copy(self.src_hbm.at[pl.ds(self.index_of(tile * MOE_TILE + r), 1), :],
                                  self.buf.at[slot, pl.ds(r, 1), :], self.sems.at[slot]).start()

    def wait(self, slot):
        pltpu.make_async_copy(self.src_hbm.at[pl.ds(0, MOE_TILE), :], self.buf.at[slot], self.sems.at[slot]).wait()


def _sorted_slot(tstart_ref, cls_ref, rank_ref, t):
    return tstart_ref[cls_ref[t]] * MOE_TILE + rank_ref[t]


def _expert_kernel(ea_ref, eb_ref, nused_ref, tstart_ref, cnt_ref, cls_ref, rank_ref,
                   rows_hbm, gmoe_ref, wga_ref, wua_ref, wda_ref, wgb_ref, wub_ref, wdb_ref,
                   y_ref, buf, sems, src_ref):
    bf16 = jnp.bfloat16
    D = y_ref.shape[-1]
    n_tok = cls_ref.shape[0]
    step, n_used = pl.program_id(0), nused_ref[0]
    gather = _RowGather(lambda i: src_ref[i], rows_hbm, buf, sems)

    @pl.when(step == 0)
    def _():
        for c in range(N_CLASSES):
            base = tstart_ref[c] * MOE_TILE

            def pad(r, carry, base=base):
                src_ref[base + r] = 0
                return carry

            lax.fori_loop(cnt_ref[c], (tstart_ref[c + 1] - tstart_ref[c]) * MOE_TILE, pad, 0)

        def place(it, carry):
            for u in range(DMA_UNROLL):
                t = it * DMA_UNROLL + u
                src_ref[_sorted_slot(tstart_ref, cls_ref, rank_ref, t)] = t
            return carry

        lax.fori_loop(0, n_tok // DMA_UNROLL, place, 0)
        gather.start(0, 0)

    @pl.when(step < n_used)
    def _():
        slot = step % 2
        gather.wait(slot)
        rows = buf[slot]
        rec = rows[:, D:]
        h2 = _rms(rows[:, :D], gmoe_ref[...]).astype(bf16)
        gather.start(jnp.minimum(step + 1, n_used - 1), 1 - slot)

        def expert(wg_ref, wu_ref, wd_ref):
            gate = _dot(h2, wg_ref[0])
            hid = (gate * jax.nn.sigmoid(gate)) * _dot(h2, wu_ref[0])
            return _dot(hid.astype(bf16), wd_ref[0])

        y = rec[:, INFO_WA:INFO_WA + 1] * expert(wga_ref, wua_ref, wda_ref)
        y_ref[...] = y + rec[:, INFO_WB:INFO_WB + 1] * expert(wgb_ref, wub_ref, wdb_ref)

        @pl.when(step == n_used - 1)
        def _():
            gather.wait(1 - slot)

    @pl.when(step >= n_used)
    def _():
        y_ref[...] = jnp.zeros_like(y_ref)


def _ple_final_kernel(tstart_ref, cls_ref, rank_ref, x1_ref, p_ref, y_hbm, gple_ref, wpg_ref, wpp_ref, gfin_ref,
                      out_ref, buf, sems):
    bf16 = jnp.bfloat16
    step, n_steps = pl.program_id(0), pl.num_programs(0)
    gather = _RowGather(lambda t: _sorted_slot(tstart_ref, cls_ref, rank_ref, t), y_hbm, buf, sems)

    @pl.when(step == 0)
    def _():
        gather.start(0, 0)

    slot = step % 2
    gather.wait(slot)
    x2 = x1_ref[...] + buf[slot]
    gather.start(jnp.minimum(step + 1, n_steps - 1), 1 - slot)
    gate_p = jax.nn.sigmoid(_dot(_rms(x2, gple_ref[...]).astype(bf16), wpg_ref[...]))
    x3 = x2 + gate_p * _dot(p_ref[...].astype(bf16), wpp_ref[...])
    out_ref[...] = _rms(x3, gfin_ref[...])

    @pl.when(step == n_steps - 1)
    def _():
        gather.wait(1 - slot)


def _const_spec(shape):
    return pl.BlockSpec(shape, lambda *_: (0,) * len(shape))


def _mixer(x, g_mix, w_in, w_gla_gate, b_gla_gate, g_gla_out, w_conv, w_out, g_moe,
           w_group, b_group, w_router, b_router):
    b, s, d = x.shape
    n_s = s // MIX_TILE
    bf16 = jnp.bfloat16
    n_qkvg = 2 * GLA_QK + 2 * GLA_V
    w_qkvg = w_in[:, :n_qkvg].astype(bf16)
    w_a = w_in[:, n_qkvg:n_qkvg + GLA_LOWRANK].astype(bf16)
    w_c3 = w_in[:, n_qkvg + GLA_LOWRANK:].astype(bf16)
    cw = w_c3.shape[1] // 3
    wrt = jnp.zeros((ROUTE_ROWS, d), jnp.float32)
    wrt = wrt.at[:N_GROUPS].set(w_group.T).at[ROUTE_EXPERT_ROW0:ROUTE_EXPERT_ROW0 + N_EXPERTS].set(w_router.T)
    brt = jnp.zeros((ROUTE_ROWS, 1), jnp.float32)
    brt = brt.at[:N_GROUPS, 0].set(b_group).at[ROUTE_EXPERT_ROW0:ROUTE_EXPERT_ROW0 + N_EXPERTS, 0].set(b_router)
    args = (x, g_mix[None, :], w_qkvg, w_a, w_c3, w_gla_gate, b_gla_gate[None, :], g_gla_out[None, :],
            w_conv, w_out.astype(bf16), g_moe[None, :], wrt, brt)
    in_specs = [pl.BlockSpec((1, MIX_TILE, d), lambda i, j: (i, j, 0))]
    in_specs += [_const_spec(a.shape) for a in args[1:]]
    return pl.pallas_call(
        _mixer_kernel,
        grid=(b, n_s),
        in_specs=in_specs,
        out_specs=[pl.BlockSpec((MIX_TILE, d + LANES), lambda i, j: (i * n_s + j, 0)),
                   pl.BlockSpec((1, 8, MIX_TILE), lambda i, j: (i * n_s + j, 0, 0)),
                   _const_spec((LANES, LANES))],
        out_shape=[jax.ShapeDtypeStruct((b * s, d + LANES), jnp.float32),
                   jax.ShapeDtypeStruct((b * n_s, 8, MIX_TILE), jnp.int32),
                   jax.ShapeDtypeStruct((LANES, LANES), jnp.int32)],
        scratch_shapes=[pltpu.VMEM((GLA_DV, GLA_QK), jnp.float32),
                        pltpu.VMEM((8, cw), jnp.float32),
                        pltpu.VMEM((MIX_TILE, GLA_QK), jnp.float32),
                        pltpu.VMEM((LANES, LANES), jnp.float32)],
        compiler_params=pltpu.CompilerParams(dimension_semantics=("arbitrary", "arbitrary"),
                                             vmem_limit_bytes=VMEM_LIMIT),
        name="mixer",
    )(*args)


def _sort_plan(meta, counts, n_tok):
    i32 = jnp.int32
    n_tiles = n_tok // MOE_TILE + N_CLASSES
    cls = meta[:, INFO_CLS, :].reshape(n_tok)
    rank = meta[:, INFO_RANK, :].reshape(n_tok)
    cnt = counts[:N_CLASSES, 0]
    tiles_per_cls = (cnt + MOE_TILE - 1) // MOE_TILE
    tile_end = jnp.cumsum(tiles_per_cls)
    n_used = tile_end[-1:]
    tstart = jnp.concatenate([tile_end - tiles_per_cls, n_used])
    tile_id = jnp.minimum(jnp.arange(n_tiles, dtype=i32), n_used - 1)
    tile_cls = jnp.sum((tile_id[:, None] >= tile_end[None, :]).astype(i32), axis=1)
    grp, pair = tile_cls // N_PAIRS, tile_cls % N_PAIRS
    e_a, e_b = grp * EXPERTS_PER_GROUP, grp * EXPERTS_PER_GROUP
    for pp in range(N_PAIRS):
        e_a = e_a + jnp.where(pair == pp, PAIR_A[pp], 0)
        e_b = e_b + jnp.where(pair == pp, PAIR_B[pp], 0)
    return dict(cls=cls, rank=rank, cnt=cnt, tstart=tstart, e_a=e_a, e_b=e_b, n_used=n_used, n_tiles=n_tiles)


def _experts(plan, rows, g_moe, w_gate, w_up, w_down):
    n_tiles = plan["n_tiles"]
    bf16 = jnp.bfloat16
    d, de = w_gate.shape[-2:]
    w_a = lambda shape: pl.BlockSpec(shape, lambda i, ea, eb, *_: (ea[i], 0, 0))
    w_b = lambda shape: pl.BlockSpec(shape, lambda i, ea, eb, *_: (eb[i], 0, 0))
    wg, wu, wd = w_gate.astype(bf16), w_up.astype(bf16), w_down.astype(bf16)
    return pl.pallas_call(
        _expert_kernel,
        grid_spec=pltpu.PrefetchScalarGridSpec(
            num_scalar_prefetch=7, grid=(n_tiles,),
            in_specs=[pl.BlockSpec(memory_space=pl.ANY),
                      pl.BlockSpec((1, d), lambda i, *_: (0, 0)),
                      w_a((1, d, de)), w_a((1, d, de)), w_a((1, de, d)),
                      w_b((1, d, de)), w_b((1, d, de)), w_b((1, de, d))],
            out_specs=pl.BlockSpec((MOE_TILE, d), lambda i, *_: (i, 0)),
            scratch_shapes=[pltpu.VMEM((2, MOE_TILE, rows.shape[1]), jnp.float32),
                            pltpu.SemaphoreType.DMA((2,)),
                            pltpu.SMEM((n_tiles * MOE_TILE,), jnp.int32)]),
        out_shape=jax.ShapeDtypeStruct((n_tiles * MOE_TILE, d), jnp.float32),
        compiler_params=pltpu.CompilerParams(dimension_semantics=("arbitrary",),
                                             vmem_limit_bytes=VMEM_LIMIT),
        name="experts",
    )(plan["e_a"], plan["e_b"], plan["n_used"], plan["tstart"], plan["cnt"], plan["cls"], plan["rank"],
      rows, g_moe[None, :], wg, wu, wd, wg, wu, wd)


def _ple_final(plan, rows, p, y_sorted, g_ple, w_ple_gate, w_ple_proj, g_final):
    n_tok, dp = p.shape
    d = y_sorted.shape[1]
    bf16 = jnp.bfloat16
    const = lambda shape: pl.BlockSpec(shape, lambda i, *_: (0,) * len(shape))
    tile = lambda width: pl.BlockSpec((MOE_TILE, width), lambda i, *_: (i, 0))
    return pl.pallas_call(
        _ple_final_kernel,
        grid_spec=pltpu.PrefetchScalarGridSpec(
            num_scalar_prefetch=3, grid=(n_tok // MOE_TILE,),
            in_specs=[tile(d),
                      tile(dp),
                      pl.BlockSpec(memory_space=pl.ANY),
                      const((1, d)), const((d, d)), const((dp, d)), const((1, d))],
            out_specs=tile(d),
            scratch_shapes=[pltpu.VMEM((2, MOE_TILE, d), jnp.float32),
                            pltpu.SemaphoreType.DMA((2,))]),
        out_shape=jax.ShapeDtypeStruct((n_tok, d), jnp.float32),
        compiler_params=pltpu.CompilerParams(dimension_semantics=("arbitrary",),
                                             vmem_limit_bytes=VMEM_LIMIT),
        name="ple_final",
    )(plan["tstart"], plan["cls"], plan["rank"], rows, p, y_sorted,
      g_ple[None, :], w_ple_gate.astype(bf16), w_ple_proj.astype(bf16), g_final[None, :])


def kernel(x, p, g_mix, w_in, w_gla_gate, b_gla_gate, g_gla_out, w_conv, w_out, g_moe, w_group, b_group,
           w_router, b_router, w_exp_gate, w_exp_up, w_exp_down, g_ple, w_ple_gate, w_ple_proj, g_final):
    depth = w_in.shape[0]
    assert depth == 1, "the final norm is fused into the last (only) layer"
    b, s, d = x.shape
    n_tok = b * s
    assert s % MIX_TILE == 0 and n_tok % MOE_TILE == 0 and MOE_TILE % DMA_UNROLL == 0
    rows, meta, counts = _mixer(x, g_mix[0], w_in[0], w_gla_gate[0], b_gla_gate[0], g_gla_out[0], w_conv[0],
                                w_out[0], g_moe[0], w_group[0], b_group[0], w_router[0], b_router[0])
    plan = _sort_plan(meta, counts, n_tok)
    y_sorted = _experts(plan, rows, g_moe[0], w_exp_gate[0], w_exp_up[0], w_exp_down[0])
    out = _ple_final(plan, rows, p[0].reshape(n_tok, -1), y_sorted, g_ple[0], w_ple_gate[0], w_ple_proj[0],
                     g_final)
    return out.reshape(b, s, d)
```

```python
import functools

import jax
import jax.numpy as jnp
from jax import lax
from jax.experimental import pallas as pl
from jax.experimental.pallas import tpu as pltpu

EPS = 1e-6
GLA_HEADS = 4
GLA_DK = 64
GLA_DV = 128
GLA_QK = GLA_HEADS * GLA_DK
GLA_V = GLA_HEADS * GLA_DV
GLA_LOWRANK = 16
GLA_TAU = 16.0
CONV_K = 3
N_GROUPS = 4
EXPERTS_PER_GROUP = 4
N_EXPERTS = N_GROUPS * EXPERTS_PER_GROUP

LANES = 128
MIX_TILE = 256
MIX_LEVELS = 8
ROUTE_ROWS = 32
ROUTE_EXPERT_ROW0 = 8
MOE_TILE = 256
VMEM_LIMIT = 56 * 1024 * 1024

PAIR_A = (0, 2, 2, 0, 0, 1)
PAIR_B = (1, 1, 3, 3, 2, 3)
PAIR_OF_KEY = {1: 0, 6: 1, 11: 2, 3: 3, 2: 4, 7: 5}
N_PAIRS = len(PAIR_A)
N_CLASSES = N_GROUPS * N_PAIRS
INFO_CLS, INFO_RANK, INFO_WA, INFO_WB = 0, 1, 2, 3
DMA_UNROLL = 8

_NT = (((1,), (1,)), ((), ()))


def _rms(x, g):
    return x * lax.rsqrt(jnp.mean(x * x, axis=-1, keepdims=True) + EPS) * g


def _dot(a, b):
    return jnp.dot(a, b, preferred_element_type=jnp.float32)


def _dot_nt(a, b):
    return lax.dot_general(a, b, _NT, preferred_element_type=jnp.float32)


def _split_bf16(a):
    hi = a.astype(jnp.bfloat16)
    return hi, (a - hi.astype(jnp.float32)).astype(jnp.bfloat16)


def _shift_rows(x, shift):
    return pltpu.roll(x, shift % x.shape[0], axis=0)


def _block_end_rows(cum_ref, blk):
    n = MIX_TILE // blk
    width = cum_ref.shape[1]
    pieces = [jnp.broadcast_to(cum_ref[pl.ds(b * blk + blk - 1, 1), :], (blk, width)) for b in range(n)]
    return jnp.concatenate(pieces, axis=0)


def _mixer_kernel(x_ref, gmix_ref, wqkvg_ref, wa_ref, wc3_ref, wgate_ref, bgate_ref, ggla_ref,
                  wconv_ref, wout_ref, gmoe_ref, wrt_ref, brt_ref,
                  rows_ref, meta_ref, counts_ref,
                  st_ref, carry_ref, cum_ref, count_ref):
    f32, bf16 = jnp.float32, jnp.bfloat16
    T = MIX_TILE
    D = x_ref.shape[-1]

    @pl.when(pl.program_id(1) == 0)
    def _():
        st_ref[...] = jnp.zeros_like(st_ref)
        carry_ref[...] = jnp.zeros_like(carry_ref)

    @pl.when((pl.program_id(0) == 0) & (pl.program_id(1) == 0))
    def _():
        count_ref[...] = jnp.zeros_like(count_ref)

    x = x_ref[0]
    hb = _rms(x, gmix_ref[...]).astype(bf16)
    qkvg = _dot(hb, wqkvg_ref[...])
    q = qkvg[:, :GLA_QK] * (GLA_DK ** -0.5)
    k = qkvg[:, GLA_QK:2 * GLA_QK]
    v = qkvg[:, 2 * GLA_QK:2 * GLA_QK + GLA_V]
    g = qkvg[:, 2 * GLA_QK + GLA_V:]
    a_low = _dot(hb, wa_ref[...])
    a_hi, a_lo = _split_bf16(a_low)
    z = _dot(jnp.concatenate([a_hi, a_lo, a_hi], axis=1), wgate_ref[...]) + bgate_ref[...]
    la = (jnp.minimum(z, 0.0) - jnp.log1p(jnp.exp(-jnp.abs(z)))) * (1.0 / GLA_TAU)

    row = lax.broadcasted_iota(jnp.int32, (T, GLA_QK), 0)
    col = lax.broadcasted_iota(jnp.int32, (T, GLA_QK), 1)
    lane_head = col // GLA_DK

    cum = la
    for j in range(MIX_LEVELS):
        sh = 1 << j
        cum = cum + jnp.where(row >= sh, _shift_rows(cum, sh), 0.0)
    cum_ref[...] = cum

    def level_exponents(l):
        if l == 0:
            return la, jnp.zeros_like(la)
        if l in (1, 2):
            blk = 1 << l
            r = row % blk
            d_q, d_k = la, jnp.zeros_like(la)
            for j in range(1, blk):
                d_q = d_q + jnp.where(r >= j, _shift_rows(la, j), 0.0)
                d_k = d_k + jnp.where(r <= blk - 1 - j, _shift_rows(la, -j), 0.0)
            return d_q, d_k
        blk = 1 << l
        cend = _block_end_rows(cum_ref, blk)
        d_q = cum - jnp.where(row >= blk, _shift_rows(cend, blk), 0.0)
        return d_q, cend - cum

    tt = lax.broadcasted_iota(jnp.int32, (T, T), 0)
    ss = lax.broadcasted_iota(jnp.int32, (T, T), 1)
    txs = jnp.bitwise_xor(tt, ss)
    pair_level = jnp.zeros((T, T), jnp.int32)
    for j in range(1, MIX_LEVELS):
        pair_level = pair_level + (txs >= (1 << j)).astype(jnp.int32)
    pair_level = jnp.where(tt > ss, pair_level, jnp.where(tt == ss, MIX_LEVELS, -1))

    lane_head_st = lax.broadcasted_iota(jnp.int32, (GLA_DV, GLA_QK), 1) // GLA_DK

    def per_head_rows(a, heads_of_lane):
        return jnp.concatenate([jnp.where(heads_of_lane == h, a, jnp.zeros_like(a)) for h in range(GLA_HEADS)],
                               axis=0)

    scores = [jnp.zeros((T, T), f32) for _ in range(GLA_HEADS)]
    for l in range(MIX_LEVELS + 1):
        if l < MIX_LEVELS:
            d_q, d_k = level_exponents(l)
            ql = (q * jnp.exp(d_q)).astype(bf16)
            kl = (k * jnp.exp(d_k)).astype(bf16)
        else:
            ql, kl = q.astype(bf16), k.astype(bf16)
        sel = pair_level == l
        p = _dot_nt(ql, per_head_rows(kl, lane_head))
        for h in range(GLA_HEADS):
            scores[h] = jnp.where(sel, p[:, h * T:(h + 1) * T], scores[h])

    cum_last = cum_ref[pl.ds(T - 1, 1), :]
    q_in = (q * jnp.exp(cum)).astype(bf16)
    k_out = (k * jnp.exp(cum_last - cum)).astype(bf16)
    st = st_ref[...]
    o_state = _dot_nt(q_in, per_head_rows(st.astype(bf16), lane_head_st))
    upd = _dot(v.T.astype(bf16), k_out)
    new_st = st * jnp.exp(cum_last)
    ggla = ggla_ref[...]
    y_heads = []
    for h in range(GLA_HEADS):
        v_h = v[:, h * GLA_DV:(h + 1) * GLA_DV]
        o = _dot(scores[h].astype(bf16), v_h.astype(bf16)) + o_state[:, h * GLA_DV:(h + 1) * GLA_DV]
        new_st = new_st + jnp.where(lane_head_st == h, upd[h * GLA_DV:(h + 1) * GLA_DV], 0.0)
        g_h = g[:, h * GLA_DV:(h + 1) * GLA_DV]
        y_heads.append(_rms(o, ggla) * (g_h * jax.nn.sigmoid(g_h)))
    st_ref[...] = new_st

    c3 = _dot(hb, wc3_ref[...])
    cw = c3.shape[1] // 3
    cb, cu = c3[:, :cw], c3[:, cw:2 * cw] * c3[:, 2 * cw:]
    crow = lax.broadcasted_iota(jnp.int32, (T, cw), 0)
    prev2, prev1 = carry_ref[0:1, :], carry_ref[1:2, :]
    m1 = jnp.where(crow == 0, prev1, _shift_rows(cu, 1))
    m2 = jnp.where(crow == 0, prev2, jnp.where(crow == 1, prev1, _shift_rows(cu, 2)))
    wconv = wconv_ref[...]
    y_conv = cb * (wconv[0:1, :] * m2 + wconv[1:2, :] * m1 + wconv[2:3, :] * cu)
    carry_ref[0:2, :] = cu[T - 2:, :]

    y = jnp.concatenate(y_heads + [y_conv], axis=1).astype(bf16)
    x1 = x + _dot(y, wout_ref[...])
    rows_ref[:, :D] = x1

    h2 = _rms(x1, gmoe_ref[...])
    h2_hi, h2_lo = _split_bf16(h2)
    part = _dot_nt(wrt_ref[...], h2_hi)
    logits = (part[:ROUTE_ROWS] + part[ROUTE_ROWS:] + _dot_nt(wrt_ref[:ROUTE_ROWS, :], h2_lo)) + brt_ref[...]
    gl = [logits[i:i + 1, :] for i in range(N_GROUPS)]
    gmax = functools.reduce(jnp.maximum, gl)
    gsum = functools.reduce(lambda a, b: a + b, [jnp.exp(t - gmax) for t in gl])
    p_grp = 1.0 / gsum
    g_sel = jnp.full_like(gmax, N_GROUPS - 1).astype(jnp.int32)
    for i in reversed(range(N_GROUPS - 1)):
        g_sel = jnp.where(gl[i] == gmax, i, g_sel)
    ig = []
    for j in range(EXPERTS_PER_GROUP):
        acc = jnp.zeros_like(gmax)
        for gi in range(N_GROUPS):
            r0 = ROUTE_EXPERT_ROW0 + gi * EXPERTS_PER_GROUP + j
            acc = acc + jnp.where(g_sel == gi, logits[r0:r0 + 1, :], 0.0)
        ig.append(acc)

    def first_argmax(vals):
        m = functools.reduce(jnp.maximum, vals)
        idx = jnp.full_like(m, len(vals) - 1).astype(jnp.int32)
        for i in reversed(range(len(vals) - 1)):
            idx = jnp.where(vals[i] == m, i, idx)
        return m, idx

    m1_, i1 = first_argmax(ig)
    m2_, i2 = first_argmax([jnp.where(i1 == j, -jnp.inf, ig[j]) for j in range(EXPERTS_PER_GROUP)])
    e21 = jnp.exp(m2_ - m1_)
    w1 = p_grp / (1.0 + e21)
    w2 = p_grp * e21 / (1.0 + e21)
    key = jnp.minimum(i1, i2) * EXPERTS_PER_GROUP + jnp.maximum(i1, i2)
    pair = jnp.zeros_like(key)
    a_loc = jnp.zeros_like(key)
    for kk, pp in PAIR_OF_KEY.items():
        pair = jnp.where(key == kk, pp, pair)
        a_loc = jnp.where(key == kk, PAIR_A[pp], a_loc)
    w_a = jnp.where(i1 == a_loc, w1, w2)
    w_b = jnp.where(i1 == a_loc, w2, w1)
    cls = g_sel * N_PAIRS + pair
    rr = lax.broadcasted_iota(jnp.int32, (LANES, T), 0)
    rec_t = (jnp.where(rr == INFO_WA, jnp.broadcast_to(w_a, (LANES, T)), 0.0)
             + jnp.where(rr == INFO_WB, jnp.broadcast_to(w_b, (LANES, T)), 0.0))
    rows_ref[:, D:] = rec_t.T

    onehot = (rr == jnp.broadcast_to(cls, (LANES, T))).astype(f32)
    earlier = (tt < ss).astype(bf16)
    count = count_ref[:, 0:1]
    before = _dot(onehot.astype(bf16), earlier) + count
    rank = jnp.sum(onehot * before, axis=0, keepdims=True).astype(jnp.int32)
    r8 = lax.broadcasted_iota(jnp.int32, (8, T), 0)
    meta_ref[0] = jnp.where(r8 == INFO_CLS, jnp.broadcast_to(cls, (8, T)),
                            jnp.where(r8 == INFO_RANK, jnp.broadcast_to(rank, (8, T)), 0))
    new_count = jnp.broadcast_to(count + jnp.sum(onehot, axis=1, keepdims=True), count_ref.shape)
    count_ref[...] = new_count
    counts_ref[...] = new_count.astype(jnp.int32)


class _RowGather:
    def __init__(self, index_of, src_hbm, buf, sems):
        self.index_of, self.src_hbm, self.buf, self.sems = index_of, src_hbm, buf, sems

    def start(self, tile, slot):
        for r in range(MOE_TILE):
            pltpu.make_async_copy(self.src_hbm.at[pl.ds(self.index_of(tile * MOE_TILE + r), 1), :],
                                  self.buf.at[slot, pl.ds(r, 1), :], self.sems.at[slot]).start()

    def wait(self, slot):
        pltpu.make_async_copy(self.src_hbm.at[pl.ds(0, MOE_TILE), :], self.buf.at[slot], self.sems.at[slot]).wait()


def _sorted_slot(tstart_ref, cls_ref, rank_ref, t):
    return tstart_ref[cls_ref[t]] * MOE_TILE + rank_ref[t]


def _expert_kernel(ea_ref, eb_ref, nused_ref, tstart_ref, cnt_ref, cls_ref, rank_ref,
                   rows_hbm, gmoe_ref, wga_ref, wua_ref, wda_ref, wgb_ref, wub_ref, wdb_ref,
                   y_ref, buf, sems, src_ref):
    bf16 = jnp.bfloat16
    D = y_ref.shape[-1]
    n_tok = cls_ref.shape[0]
    step, n_used = pl.program_id(0), nused_ref[0]
    gather = _RowGather(lambda i: src_ref[i], rows_hbm, buf, sems)

    @pl.when(step == 0)
    def _():
        for c in range(N_CLASSES):
            base = tstart_ref[c] * MOE_TILE

            def pad(r, carry, base=base):
                src_ref[base + r] = 0
                return carry

            lax.fori_loop(cnt_ref[c], (tstart_ref[c + 1] - tstart_ref[c]) * MOE_TILE, pad, 0)

        def place(it, carry):
            for u in range(DMA_UNROLL):
                t = it * DMA_UNROLL + u
                src_ref[_sorted_slot(tstart_ref, cls_ref, rank_ref, t)] = t
            return carry

        lax.fori_loop(0, n_tok // DMA_UNROLL, place, 0)
        gather.start(0, 0)

    slot = step % 2

    @pl.when(step + 1 < n_used)
    def _():
        gather.start(step + 1, 1 - slot)

    @pl.when(step < n_used)
    def _():
        gather.wait(slot)
        rows = buf[slot]
        rec = rows[:, D:]
        h2 = _rms(rows[:, :D], gmoe_ref[...]).astype(bf16)

        def expert(wg_ref, wu_ref, wd_ref):
            gate = _dot(h2, wg_ref[0])
            hid = (gate * jax.nn.sigmoid(gate)) * _dot(h2, wu_ref[0])
            return _dot(hid.astype(bf16), wd_ref[0])

        y = rec[:, INFO_WA:INFO_WA + 1] * expert(wga_ref, wua_ref, wda_ref)
        y_ref[...] = y + rec[:, INFO_WB:INFO_WB + 1] * expert(wgb_ref, wub_ref, wdb_ref)

    @pl.when(step >= n_used)
    def _():
        y_ref[...] = jnp.zeros_like(y_ref)


def _ple_final_kernel(tstart_ref, cls_ref, rank_ref, x1_ref, p_ref, y_hbm, gple_ref, wpg_ref, wpp_ref, gfin_ref,
                      out_ref, buf, sems):
    bf16 = jnp.bfloat16
    step, n_steps = pl.program_id(0), pl.num_programs(0)
    gather = _RowGather(lambda t: _sorted_slot(tstart_ref, cls_ref, rank_ref, t), y_hbm, buf, sems)

    @pl.when(step == 0)
    def _():
        gather.start(0, 0)

    slot = step % 2

    @pl.when(step + 1 < n_steps)
    def _():
        gather.start(step + 1, 1 - slot)

    gather.wait(slot)
    x2 = x1_ref[...] + buf[slot]
    gate_p = jax.nn.sigmoid(_dot(_rms(x2, gple_ref[...]).astype(bf16), wpg_ref[...]))
    x3 = x2 + gate_p * _dot(p_ref[...].astype(bf16), wpp_ref[...])
    out_ref[...] = _rms(x3, gfin_ref[...])


def _const_spec(shape):
    return pl.BlockSpec(shape, lambda *_: (0,) * len(shape))


def _mixer(x, g_mix, w_in, w_gla_gate, b_gla_gate, g_gla_out, w_conv, w_out, g_moe,
           w_group, b_group, w_router, b_router):
    b, s, d = x.shape
    n_s = s // MIX_TILE
    bf16 = jnp.bfloat16
    n_qkvg = 2 * GLA_QK + 2 * GLA_V
    w_qkvg = w_in[:, :n_qkvg].astype(bf16)
    w_a = w_in[:, n_qkvg:n_qkvg + GLA_LOWRANK].astype(bf16)
    w_c3 = w_in[:, n_qkvg + GLA_LOWRANK:].astype(bf16)
    cw = w_c3.shape[1] // 3
    wrt = jnp.zeros((ROUTE_ROWS, d), jnp.float32)
    wrt = wrt.at[:N_GROUPS].set(w_group.T).at[ROUTE_EXPERT_ROW0:ROUTE_EXPERT_ROW0 + N_EXPERTS].set(w_router.T)
    brt = jnp.zeros((ROUTE_ROWS, 1), jnp.float32)
    brt = brt.at[:N_GROUPS, 0].set(b_group).at[ROUTE_EXPERT_ROW0:ROUTE_EXPERT_ROW0 + N_EXPERTS, 0].set(b_router)
    wrt_split = jnp.concatenate(_split_bf16(wrt), axis=0)
    gate_hi, gate_lo = _split_bf16(w_gla_gate)
    w_gate_split = jnp.concatenate([gate_hi, gate_hi, gate_lo], axis=0)
    args = (x, g_mix[None, :], w_qkvg, w_a, w_c3, w_gate_split, b_gla_gate[None, :], g_gla_out[None, :],
            w_conv, w_out.astype(bf16), g_moe[None, :], wrt_split, brt)
    in_specs = [pl.BlockSpec((1, MIX_TILE, d), lambda i, j: (i, j, 0))]
    in_specs += [_const_spec(a.shape) for a in args[1:]]
    return pl.pallas_call(
        _mixer_kernel,
        grid=(b, n_s),
        in_specs=in_specs,
        out_specs=[pl.BlockSpec((MIX_TILE, d + LANES), lambda i, j: (i * n_s + j, 0)),
                   pl.BlockSpec((1, 8, MIX_TILE), lambda i, j: (i * n_s + j, 0, 0)),
                   _const_spec((LANES, LANES))],
        out_shape=[jax.ShapeDtypeStruct((b * s, d + LANES), jnp.float32),
                   jax.ShapeDtypeStruct((b * n_s, 8, MIX_TILE), jnp.int32),
                   jax.ShapeDtypeStruct((LANES, LANES), jnp.int32)],
        scratch_shapes=[pltpu.VMEM((GLA_DV, GLA_QK), jnp.float32),
                        pltpu.VMEM((8, cw), jnp.float32),
                        pltpu.VMEM((MIX_TILE, GLA_QK), jnp.float32),
                        pltpu.VMEM((LANES, LANES), jnp.float32)],
        compiler_params=pltpu.CompilerParams(dimension_semantics=("arbitrary", "arbitrary"),
                                             vmem_limit_bytes=VMEM_LIMIT),
        name="mixer",
    )(*args)


def _sort_plan(meta, counts, n_tok):
    i32 = jnp.int32
    n_tiles = n_tok // MOE_TILE + N_CLASSES
    cls = meta[:, INFO_CLS, :].reshape(n_tok)
    rank = meta[:, INFO_RANK, :].reshape(n_tok)
    cnt = counts[:N_CLASSES, 0]
    tiles_per_cls = (cnt + MOE_TILE - 1) // MOE_TILE
    tile_end = jnp.cumsum(tiles_per_cls)
    n_used = tile_end[-1:]
    tstart = jnp.concatenate([tile_end - tiles_per_cls, n_used])
    tile_id = jnp.minimum(jnp.arange(n_tiles, dtype=i32), n_used - 1)
    tile_cls = jnp.sum((tile_id[:, None] >= tile_end[None, :]).astype(i32), axis=1)
    grp, pair = tile_cls // N_PAIRS, tile_cls % N_PAIRS
    e_a, e_b = grp * EXPERTS_PER_GROUP, grp * EXPERTS_PER_GROUP
    for pp in range(N_PAIRS):
        e_a = e_a + jnp.where(pair == pp, PAIR_A[pp], 0)
        e_b = e_b + jnp.where(pair == pp, PAIR_B[pp], 0)
    return dict(cls=cls, rank=rank, cnt=cnt, tstart=tstart, e_a=e_a, e_b=e_b, n_used=n_used, n_tiles=n_tiles)


def _experts(plan, rows, g_moe, w_gate, w_up, w_down):
    n_tiles = plan["n_tiles"]
    bf16 = jnp.bfloat16
    d, de = w_gate.shape[-2:]
    w_a = lambda shape: pl.BlockSpec(shape, lambda i, ea, eb, *_: (ea[i], 0, 0))
    w_b = lambda shape: pl.BlockSpec(shape, lambda i, ea, eb, *_: (eb[i], 0, 0))
    wg, wu, wd = w_gate.astype(bf16), w_up.astype(bf16), w_down.astype(bf16)
    return pl.pallas_call(
        _expert_kernel,
        grid_spec=pltpu.PrefetchScalarGridSpec(
            num_scalar_prefetch=7, grid=(n_tiles,),
            in_specs=[pl.BlockSpec(memory_space=pl.ANY),
                      pl.BlockSpec((1, d), lambda i, *_: (0, 0)),
                      w_a((1, d, de)), w_a((1, d, de)), w_a((1, de, d)),
                      w_b((1, d, de)), w_b((1, d, de)), w_b((1, de, d))],
            out_specs=pl.BlockSpec((MOE_TILE, d), lambda i, *_: (i, 0)),
            scratch_shapes=[pltpu.VMEM((2, MOE_TILE, rows.shape[1]), jnp.float32),
                            pltpu.SemaphoreType.DMA((2,)),
                            pltpu.SMEM((n_tiles * MOE_TILE,), jnp.int32)]),
        out_shape=jax.ShapeDtypeStruct((n_tiles * MOE_TILE, d), jnp.float32),
        compiler_params=pltpu.CompilerParams(dimension_semantics=("arbitrary",),
                                             vmem_limit_bytes=VMEM_LIMIT),
        name="experts",
    )(plan["e_a"], plan["e_b"], plan["n_used"], plan["tstart"], plan["cnt"], plan["cls"], plan["rank"],
      rows, g_moe[None, :], wg, wu, wd, wg, wu, wd)


def _ple_final(plan, rows, p, y_sorted, g_ple, w_ple_gate, w_ple_proj, g_final):
    n_tok, dp = p.shape
    d = y_sorted.shape[1]
    bf16 = jnp.bfloat16
    const = lambda shape: pl.BlockSpec(shape, lambda i, *_: (0,) * len(shape))
    tile = lambda width: pl.BlockSpec((MOE_TILE, width), lambda i, *_: (i, 0))
    return pl.pallas_call(
        _ple_final_kernel,
        grid_spec=pltpu.PrefetchScalarGridSpec(
            num_scalar_prefetch=3, grid=(n_tok // MOE_TILE,),
            in_specs=[tile(d),
                      tile(dp),
                      pl.BlockSpec(memory_space=pl.ANY),
                      const((1, d)), const((d, d)), const((dp, d)), const((1, d))],
            out_specs=tile(d),
            scratch_shapes=[pltpu.VMEM((2, MOE_TILE, d), jnp.float32),
                            pltpu.SemaphoreType.DMA((2,))]),
        out_shape=jax.ShapeDtypeStruct((n_tok, d), jnp.float32),
        compiler_params=pltpu.CompilerParams(dimension_semantics=("arbitrary",),
                                             vmem_limit_bytes=VMEM_LIMIT),
        name="ple_final",
    )(plan["tstart"], plan["cls"], plan["rank"], rows, p, y_sorted,
      g_ple[None, :], w_ple_gate.astype(bf16), w_ple_proj.astype(bf16), g_final[None, :])


def kernel(x, p, g_mix, w_in, w_gla_gate, b_gla_gate, g_gla_out, w_conv, w_out, g_moe, w_group, b_group,
           w_router, b_router, w_exp_gate, w_exp_up, w_exp_down, g_ple, w_ple_gate, w_ple_proj, g_final):
    depth = w_in.shape[0]
    assert depth == 1, "the final norm is fused into the last (only) layer"
    b, s, d = x.shape
    n_tok = b * s
    assert s % MIX_TILE == 0 and n_tok % MOE_TILE == 0 and MOE_TILE % DMA_UNROLL == 0
    rows, meta, counts = _mixer(x, g_mix[0], w_in[0], w_gla_gate[0], b_gla_gate[0], g_gla_out[0], w_conv[0],
                                w_out[0], g_moe[0], w_group[0], b_group[0], w_router[0], b_router[0])
    plan = _sort_plan(meta, counts, n_tok)
    y_sorted = _experts(plan, rows, g_moe[0], w_exp_gate[0], w_exp_up[0], w_exp_down[0])
    out = _ple_final(plan, rows, p[0].reshape(n_tok, -1), y_sorted, g_ple[0], w_ple_gate[0], w_ple_proj[0],
                     g_final)
    return out.reshape(b, s, d)
```

```python
import functools

import jax
import jax.numpy as jnp
from jax import lax
from jax.experimental import pallas as pl
from jax.experimental.pallas import tpu as pltpu

EPS = 1e-6
GLA_HEADS = 4
GLA_DK = 64
GLA_DV = 128
GLA_QK = GLA_HEADS * GLA_DK
GLA_V = GLA_HEADS * GLA_DV
GLA_LOWRANK = 16
GLA_TAU = 16.0
CONV_K = 3
N_GROUPS = 4
EXPERTS_PER_GROUP = 4
N_EXPERTS = N_GROUPS * EXPERTS_PER_GROUP

LANES = 128
MIX_TILE = 256
MIX_LEVELS = 8
ROUTE_ROWS = 32
ROUTE_EXPERT_ROW0 = 8
MOE_TILE = 256
VMEM_LIMIT = 56 * 1024 * 1024

PAIR_A = (0, 2, 2, 0, 0, 1)
PAIR_B = (1, 1, 3, 3, 2, 3)
PAIR_OF_KEY = {1: 0, 6: 1, 11: 2, 3: 3, 2: 4, 7: 5}
N_PAIRS = len(PAIR_A)
N_CLASSES = N_GROUPS * N_PAIRS
INFO_CLS, INFO_RANK, INFO_WA, INFO_WB = 0, 1, 2, 3
DMA_UNROLL = 8
EXPERT_CHUNK = 256
ROW_RECORD = 9
Y_RECORD = 8

_NT = (((1,), (1,)), ((), ()))


def _rms(x, g):
    return x * lax.rsqrt(jnp.mean(x * x, axis=-1, keepdims=True) + EPS) * g


def _dot(a, b):
    return jnp.dot(a, b, preferred_element_type=jnp.float32)


def _dot_nt(a, b):
    return lax.dot_general(a, b, _NT, preferred_element_type=jnp.float32)


def _split_bf16(a):
    hi = a.astype(jnp.bfloat16)
    return hi, (a - hi.astype(jnp.float32)).astype(jnp.bfloat16)


def _shift_rows(x, shift):
    return pltpu.roll(x, shift % x.shape[0], axis=0)


def _block_end_rows(cum_ref, blk):
    n = MIX_TILE // blk
    width = cum_ref.shape[1]
    pieces = [jnp.broadcast_to(cum_ref[pl.ds(b * blk + blk - 1, 1), :], (blk, width)) for b in range(n)]
    return jnp.concatenate(pieces, axis=0)


def _mixer_kernel(x_ref, gmix_ref, wqkvg_ref, wa_ref, wc3_ref, wgate_ref, bgate_ref, ggla_ref,
                  wconv_ref, wout_ref, gmoe_ref, wrt_ref, brt_ref,
                  rows_ref, meta_ref, counts_ref,
                  st_ref, carry_ref, cum_ref, count_ref):
    f32, bf16 = jnp.float32, jnp.bfloat16
    T = MIX_TILE
    D = x_ref.shape[-1]

    @pl.when(pl.program_id(1) == 0)
    def _():
        st_ref[...] = jnp.zeros_like(st_ref)
        carry_ref[...] = jnp.zeros_like(carry_ref)

    @pl.when((pl.program_id(0) == 0) & (pl.program_id(1) == 0))
    def _():
        count_ref[...] = jnp.zeros_like(count_ref)

    x = x_ref[0]
    hb = _rms(x, gmix_ref[...]).astype(bf16)
    qkvg = _dot(hb, wqkvg_ref[...])
    q = qkvg[:, :GLA_QK] * (GLA_DK ** -0.5)
    k = qkvg[:, GLA_QK:2 * GLA_QK]
    v = qkvg[:, 2 * GLA_QK:2 * GLA_QK + GLA_V]
    g = qkvg[:, 2 * GLA_QK + GLA_V:]
    a_low = _dot(hb, wa_ref[...])
    a_hi, a_lo = _split_bf16(a_low)
    z = _dot(jnp.concatenate([a_hi, a_lo, a_hi], axis=1), wgate_ref[...]) + bgate_ref[...]
    la = (jnp.minimum(z, 0.0) - jnp.log1p(jnp.exp(-jnp.abs(z)))) * (1.0 / GLA_TAU)

    row = lax.broadcasted_iota(jnp.int32, (T, GLA_QK), 0)
    col = lax.broadcasted_iota(jnp.int32, (T, GLA_QK), 1)
    lane_head = col // GLA_DK

    cum = la
    for j in range(MIX_LEVELS):
        sh = 1 << j
        cum = cum + jnp.where(row >= sh, _shift_rows(cum, sh), 0.0)
    cum_ref[...] = cum

    def level_exponents(l):
        if l == 0:
            return la, jnp.zeros_like(la)
        if l in (1, 2):
            blk = 1 << l
            r = row % blk
            d_q, d_k = la, jnp.zeros_like(la)
            for j in range(1, blk):
                d_q = d_q + jnp.where(r >= j, _shift_rows(la, j), 0.0)
                d_k = d_k + jnp.where(r <= blk - 1 - j, _shift_rows(la, -j), 0.0)
            return d_q, d_k
        blk = 1 << l
        cend = _block_end_rows(cum_ref, blk)
        d_q = cum - jnp.where(row >= blk, _shift_rows(cend, blk), 0.0)
        return d_q, cend - cum

    tt = lax.broadcasted_iota(jnp.int32, (T, T), 0)
    ss = lax.broadcasted_iota(jnp.int32, (T, T), 1)
    txs = jnp.bitwise_xor(tt, ss)
    pair_level = jnp.zeros((T, T), jnp.int32)
    for j in range(1, MIX_LEVELS):
        pair_level = pair_level + (txs >= (1 << j)).astype(jnp.int32)
    pair_level = jnp.where(tt > ss, pair_level, jnp.where(tt == ss, MIX_LEVELS, -1))

    lane_head_st = lax.broadcasted_iota(jnp.int32, (GLA_DV, GLA_QK), 1) // GLA_DK

    def per_head_rows(a, heads_of_lane):
        return jnp.concatenate([jnp.where(heads_of_lane == h, a, jnp.zeros_like(a)) for h in range(GLA_HEADS)],
                               axis=0)

    scores = [jnp.zeros((T, T), f32) for _ in range(GLA_HEADS)]
    for l in range(MIX_LEVELS + 1):
        if l < MIX_LEVELS:
            d_q, d_k = level_exponents(l)
            ql = (q * jnp.exp(d_q)).astype(bf16)
            kl = (k * jnp.exp(d_k)).astype(bf16)
        else:
            ql, kl = q.astype(bf16), k.astype(bf16)
        sel = pair_level == l
        p = _dot_nt(ql, per_head_rows(kl, lane_head))
        for h in range(GLA_HEADS):
            scores[h] = jnp.where(sel, p[:, h * T:(h + 1) * T], scores[h])

    cum_last = cum_ref[pl.ds(T - 1, 1), :]
    q_in = (q * jnp.exp(cum)).astype(bf16)
    k_out = (k * jnp.exp(cum_last - cum)).astype(bf16)
    st = st_ref[...]
    o_state = _dot_nt(q_in, per_head_rows(st.astype(bf16), lane_head_st))
    upd = _dot(v.T.astype(bf16), k_out)
    new_st = st * jnp.exp(cum_last)
    ggla = ggla_ref[...]
    y_heads = []
    for h in range(GLA_HEADS):
        v_h = v[:, h * GLA_DV:(h + 1) * GLA_DV]
        o = _dot(scores[h].astype(bf16), v_h.astype(bf16)) + o_state[:, h * GLA_DV:(h + 1) * GLA_DV]
        new_st = new_st + jnp.where(lane_head_st == h, upd[h * GLA_DV:(h + 1) * GLA_DV], 0.0)
        g_h = g[:, h * GLA_DV:(h + 1) * GLA_DV]
        y_heads.append(_rms(o, ggla) * (g_h * jax.nn.sigmoid(g_h)))
    st_ref[...] = new_st

    c3 = _dot(hb, wc3_ref[...])
    cw = c3.shape[1] // 3
    cb, cu = c3[:, :cw], c3[:, cw:2 * cw] * c3[:, 2 * cw:]
    crow = lax.broadcasted_iota(jnp.int32, (T, cw), 0)
    prev2, prev1 = carry_ref[0:1, :], carry_ref[1:2, :]
    m1 = jnp.where(crow == 0, prev1, _shift_rows(cu, 1))
    m2 = jnp.where(crow == 0, prev2, jnp.where(crow == 1, prev1, _shift_rows(cu, 2)))
    wconv = wconv_ref[...]
    y_conv = cb * (wconv[0:1, :] * m2 + wconv[1:2, :] * m1 + wconv[2:3, :] * cu)
    carry_ref[0:2, :] = cu[T - 2:, :]

    y = jnp.concatenate(y_heads + [y_conv], axis=1).astype(bf16)
    x1 = x + _dot(y, wout_ref[...])

    h2 = _rms(x1, gmoe_ref[...])
    h2_hi, h2_lo = _split_bf16(h2)
    part = _dot_nt(wrt_ref[...], h2_hi)
    logits = (part[:ROUTE_ROWS] + part[ROUTE_ROWS:] + _dot_nt(wrt_ref[:ROUTE_ROWS, :], h2_lo)) + brt_ref[...]
    gl = [logits[i:i + 1, :] for i in range(N_GROUPS)]
    gmax = functools.reduce(jnp.maximum, gl)
    gsum = functools.reduce(lambda a, b: a + b, [jnp.exp(t - gmax) for t in gl])
    p_grp = 1.0 / gsum
    g_sel = jnp.full_like(gmax, N_GROUPS - 1).astype(jnp.int32)
    for i in reversed(range(N_GROUPS - 1)):
        g_sel = jnp.where(gl[i] == gmax, i, g_sel)
    ig = []
    for j in range(EXPERTS_PER_GROUP):
        acc = jnp.zeros_like(gmax)
        for gi in range(N_GROUPS):
            r0 = ROUTE_EXPERT_ROW0 + gi * EXPERTS_PER_GROUP + j
            acc = acc + jnp.where(g_sel == gi, logits[r0:r0 + 1, :], 0.0)
        ig.append(acc)

    def first_argmax(vals):
        m = functools.reduce(jnp.maximum, vals)
        idx = jnp.full_like(m, len(vals) - 1).astype(jnp.int32)
        for i in reversed(range(len(vals) - 1)):
            idx = jnp.where(vals[i] == m, i, idx)
        return m, idx

    m1_, i1 = first_argmax(ig)
    m2_, i2 = first_argmax([jnp.where(i1 == j, -jnp.inf, ig[j]) for j in range(EXPERTS_PER_GROUP)])
    e21 = jnp.exp(m2_ - m1_)
    w1 = p_grp / (1.0 + e21)
    w2 = p_grp * e21 / (1.0 + e21)
    key = jnp.minimum(i1, i2) * EXPERTS_PER_GROUP + jnp.maximum(i1, i2)
    pair = jnp.zeros_like(key)
    a_loc = jnp.zeros_like(key)
    for kk, pp in PAIR_OF_KEY.items():
        pair = jnp.where(key == kk, pp, pair)
        a_loc = jnp.where(key == kk, PAIR_A[pp], a_loc)
    w_a = jnp.where(i1 == a_loc, w1, w2)
    w_b = jnp.where(i1 == a_loc, w2, w1)
    cls = g_sel * N_PAIRS + pair
    rr = lax.broadcasted_iota(jnp.int32, (LANES, T), 0)
    rec_t = (jnp.where(rr == INFO_WA, jnp.broadcast_to(w_a, (LANES, T)), 0.0)
             + jnp.where(rr == INFO_WB, jnp.broadcast_to(w_b, (LANES, T)), 0.0))
    pieces = [x1[:, c * LANES:(c + 1) * LANES] for c in range(D // LANES)] + [rec_t.T]
    assert len(pieces) == ROW_RECORD
    _store_records(rows_ref, pieces, ROW_RECORD)

    onehot = (rr == jnp.broadcast_to(cls, (LANES, T))).astype(f32)
    earlier = (tt < ss).astype(bf16)
    count = count_ref[:, 0:1]
    before = _dot(onehot.astype(bf16), earlier) + count
    rank = jnp.sum(onehot * before, axis=0, keepdims=True).astype(jnp.int32)
    r8 = lax.broadcasted_iota(jnp.int32, (8, T), 0)
    meta_ref[0] = jnp.where(r8 == INFO_CLS, jnp.broadcast_to(cls, (8, T)),
                            jnp.where(r8 == INFO_RANK, jnp.broadcast_to(rank, (8, T)), 0))
    new_count = jnp.broadcast_to(count + jnp.sum(onehot, axis=1, keepdims=True), count_ref.shape)
    count_ref[...] = new_count
    counts_ref[...] = new_count.astype(jnp.int32)


def _store_records(ref, pieces, record_rows):
    n = ref.shape[0] // record_rows
    for c, piece in enumerate(pieces):
        ref[pl.ds(c, n, stride=record_rows), :] = piece


def _load_records(ref, first, count, record_rows, lead=()):
    n = ref.shape[-2] // record_rows
    return jnp.concatenate([ref[lead + (pl.ds(first + c, n, stride=record_rows), slice(None))]
                            for c in range(count)], axis=1)


class _RowGather:
    def __init__(self, index_of, src_hbm, buf, sems, record_rows):
        self.index_of, self.src_hbm, self.buf, self.sems, self.rr = index_of, src_hbm, buf, sems, record_rows

    def start(self, tile, slot, part=0, n_parts=1):
        rr, per_part = self.rr, MOE_TILE // n_parts
        for i in range(per_part):
            r = part * per_part + i
            first = pl.multiple_of(self.index_of(tile * MOE_TILE + r) * rr, rr)
            pltpu.make_async_copy(self.src_hbm.at[pl.ds(first, rr), :],
                                  self.buf.at[slot, pl.ds(r * rr, rr), :],
                                  self.sems.at[slot]).start()

    def wait(self, slot):
        pltpu.make_async_copy(self.src_hbm.at[pl.ds(0, MOE_TILE * self.rr), :], self.buf.at[slot],
                              self.sems.at[slot]).wait()


def _sorted_slot(tstart_ref, cls_ref, rank_ref, t):
    return tstart_ref[cls_ref[t]] * MOE_TILE + rank_ref[t]


def _expert_kernel(ea_ref, eb_ref, nused_ref, tstart_ref, cnt_ref, cls_ref, rank_ref,
                   rows_hbm, gmoe_ref, wga_ref, wua_ref, wda_ref, wgb_ref, wub_ref, wdb_ref,
                   y_ref, buf, sems, src_ref, h2_ref, acc_ref):
    bf16 = jnp.bfloat16
    n_x = gmoe_ref.shape[-1] // LANES
    n_tok = cls_ref.shape[0]
    step, n_used = pl.program_id(0), nused_ref[0]
    gather = _RowGather(lambda i: src_ref[i], rows_hbm, buf, sems, ROW_RECORD)

    @pl.when(step == 0)
    def _():
        for c in range(N_CLASSES):
            base = tstart_ref[c] * MOE_TILE

            def pad(r, carry, base=base):
                src_ref[base + r] = 0
                return carry

            lax.fori_loop(cnt_ref[c], (tstart_ref[c + 1] - tstart_ref[c]) * MOE_TILE, pad, 0)

        def place(it, carry):
            for u in range(DMA_UNROLL):
                t = it * DMA_UNROLL + u
                src_ref[_sorted_slot(tstart_ref, cls_ref, rank_ref, t)] = t
            return carry

        lax.fori_loop(0, n_tok // DMA_UNROLL, place, 0)
        gather.start(0, 0)

    slot = step % 2
    n_chunks = wga_ref.shape[1]
    n_parts = 2 * n_chunks

    @pl.when(step < n_used)
    def _():
        gather.wait(slot)
        rec = _load_records(buf, n_x, 1, ROW_RECORD, lead=(slot,))
        h2_ref[...] = _rms(_load_records(buf, 0, n_x, ROW_RECORD, lead=(slot,)), gmoe_ref[...]).astype(bf16)
        acc_ref[...] = jnp.zeros_like(acc_ref)

        for e, (wg_ref, wu_ref, wd_ref, lane) in enumerate(((wga_ref, wua_ref, wda_ref, INFO_WA),
                                                            (wgb_ref, wub_ref, wdb_ref, INFO_WB))):
            weight = rec[:, lane:lane + 1]

            def chunk(j, carry, e=e, wg_ref=wg_ref, wu_ref=wu_ref, wd_ref=wd_ref, weight=weight):
                @pl.when(step + 1 < n_used)
                def _():
                    gather.start(step + 1, 1 - slot, part=e * n_chunks + j, n_parts=n_parts)

                h2 = h2_ref[...]
                gate = _dot(h2, wg_ref[0, j])
                hid = (gate * jax.nn.sigmoid(gate)) * _dot(h2, wu_ref[0, j])
                acc_ref[...] += weight * _dot(hid.astype(bf16), wd_ref[0, j])
                return carry

            lax.fori_loop(0, n_chunks, chunk, 0)

        y = acc_ref[...]
        _store_records(y_ref, [y[:, c * LANES:(c + 1) * LANES] for c in range(n_x)], Y_RECORD)

    @pl.when(step >= n_used)
    def _():
        y_ref[...] = jnp.zeros_like(y_ref)


def _ple_final_kernel(tstart_ref, cls_ref, rank_ref, x1_ref, p_ref, y_hbm, gple_ref, wpg_ref, wpp_ref, gfin_ref,
                      out_ref, buf, sems):
    bf16 = jnp.bfloat16
    step, n_steps = pl.program_id(0), pl.num_programs(0)
    n_x = gple_ref.shape[-1] // LANES
    gather = _RowGather(lambda t: _sorted_slot(tstart_ref, cls_ref, rank_ref, t), y_hbm, buf, sems, Y_RECORD)

    @pl.when(step == 0)
    def _():
        gather.start(0, 0)

    slot = step % 2

    @pl.when(step + 1 < n_steps)
    def _():
        gather.start(step + 1, 1 - slot)

    gather.wait(slot)
    x2 = _load_records(x1_ref, 0, n_x, ROW_RECORD) + _load_records(buf, 0, n_x, Y_RECORD, lead=(slot,))
    gate_p = jax.nn.sigmoid(_dot(_rms(x2, gple_ref[...]).astype(bf16), wpg_ref[...]))
    x3 = x2 + gate_p * _dot(p_ref[...].astype(bf16), wpp_ref[...])
    out_ref[...] = _rms(x3, gfin_ref[...])


def _const_spec(shape):
    return pl.BlockSpec(shape, lambda *_: (0,) * len(shape))


def _mixer(x, g_mix, w_in, w_gla_gate, b_gla_gate, g_gla_out, w_conv, w_out, g_moe,
           w_group, b_group, w_router, b_router):
    b, s, d = x.shape
    assert d == (ROW_RECORD - 1) * LANES
    n_s = s // MIX_TILE
    bf16 = jnp.bfloat16
    n_qkvg = 2 * GLA_QK + 2 * GLA_V
    w_qkvg = w_in[:, :n_qkvg].astype(bf16)
    w_a = w_in[:, n_qkvg:n_qkvg + GLA_LOWRANK].astype(bf16)
    w_c3 = w_in[:, n_qkvg + GLA_LOWRANK:].astype(bf16)
    cw = w_c3.shape[1] // 3
    wrt = jnp.zeros((ROUTE_ROWS, d), jnp.float32)
    wrt = wrt.at[:N_GROUPS].set(w_group.T).at[ROUTE_EXPERT_ROW0:ROUTE_EXPERT_ROW0 + N_EXPERTS].set(w_router.T)
    brt = jnp.zeros((ROUTE_ROWS, 1), jnp.float32)
    brt = brt.at[:N_GROUPS, 0].set(b_group).at[ROUTE_EXPERT_ROW0:ROUTE_EXPERT_ROW0 + N_EXPERTS, 0].set(b_router)
    wrt_split = jnp.concatenate(_split_bf16(wrt), axis=0)
    gate_hi, gate_lo = _split_bf16(w_gla_gate)
    w_gate_split = jnp.concatenate([gate_hi, gate_hi, gate_lo], axis=0)
    args = (x, g_mix[None, :], w_qkvg, w_a, w_c3, w_gate_split, b_gla_gate[None, :], g_gla_out[None, :],
            w_conv, w_out.astype(bf16), g_moe[None, :], wrt_split, brt)
    in_specs = [pl.BlockSpec((1, MIX_TILE, d), lambda i, j: (i, j, 0))]
    in_specs += [_const_spec(a.shape) for a in args[1:]]
    return pl.pallas_call(
        _mixer_kernel,
        grid=(b, n_s),
        in_specs=in_specs,
        out_specs=[pl.BlockSpec((MIX_TILE * ROW_RECORD, LANES), lambda i, j: (i * n_s + j, 0)),
                   pl.BlockSpec((1, 8, MIX_TILE), lambda i, j: (i * n_s + j, 0, 0)),
                   _const_spec((LANES, LANES))],
        out_shape=[jax.ShapeDtypeStruct((b * s * ROW_RECORD, LANES), jnp.float32),
                   jax.ShapeDtypeStruct((b * n_s, 8, MIX_TILE), jnp.int32),
                   jax.ShapeDtypeStruct((LANES, LANES), jnp.int32)],
        scratch_shapes=[pltpu.VMEM((GLA_DV, GLA_QK), jnp.float32),
                        pltpu.VMEM((8, cw), jnp.float32),
                        pltpu.VMEM((MIX_TILE, GLA_QK), jnp.float32),
                        pltpu.VMEM((LANES, LANES), jnp.float32)],
        compiler_params=pltpu.CompilerParams(dimension_semantics=("arbitrary", "arbitrary"),
                                             vmem_limit_bytes=VMEM_LIMIT),
        name="mixer",
    )(*args)


def _sort_plan(meta, counts, n_tok):
    i32 = jnp.int32
    n_tiles = n_tok // MOE_TILE + N_CLASSES
    cls = meta[:, INFO_CLS, :].reshape(n_tok)
    rank = meta[:, INFO_RANK, :].reshape(n_tok)
    cnt = counts[:N_CLASSES, 0]
    tiles_per_cls = (cnt + MOE_TILE - 1) // MOE_TILE
    tile_end = jnp.cumsum(tiles_per_cls)
    n_used = tile_end[-1:]
    tstart = jnp.concatenate([tile_end - tiles_per_cls, n_used])
    tile_id = jnp.minimum(jnp.arange(n_tiles, dtype=i32), n_used - 1)
    tile_cls = jnp.sum((tile_id[:, None] >= tile_end[None, :]).astype(i32), axis=1)
    grp, pair = tile_cls // N_PAIRS, tile_cls % N_PAIRS
    e_a, e_b = grp * EXPERTS_PER_GROUP, grp * EXPERTS_PER_GROUP
    for pp in range(N_PAIRS):
        e_a = e_a + jnp.where(pair == pp, PAIR_A[pp], 0)
        e_b = e_b + jnp.where(pair == pp, PAIR_B[pp], 0)
    return dict(cls=cls, rank=rank, cnt=cnt, tstart=tstart, e_a=e_a, e_b=e_b, n_used=n_used, n_tiles=n_tiles)


def _experts(plan, rows, g_moe, w_gate, w_up, w_down):
    n_tiles = plan["n_tiles"]
    assert w_gate.shape[-2] == Y_RECORD * LANES
    bf16 = jnp.bfloat16
    n_exp, d, de = w_gate.shape
    n_chunks = de // EXPERT_CHUNK
    wg, wu = (w.reshape(n_exp, d, n_chunks, EXPERT_CHUNK).transpose(0, 2, 1, 3).astype(bf16) for w in (w_gate, w_up))
    wd = w_down.reshape(n_exp, n_chunks, EXPERT_CHUNK, d).astype(bf16)
    w_a = lambda shape: pl.BlockSpec(shape, lambda i, ea, eb, *_: (ea[i], 0, 0, 0))
    w_b = lambda shape: pl.BlockSpec(shape, lambda i, ea, eb, *_: (eb[i], 0, 0, 0))
    up_shape, down_shape = (1, n_chunks, d, EXPERT_CHUNK), (1, n_chunks, EXPERT_CHUNK, d)
    return pl.pallas_call(
        _expert_kernel,
        grid_spec=pltpu.PrefetchScalarGridSpec(
            num_scalar_prefetch=7, grid=(n_tiles,),
            in_specs=[pl.BlockSpec(memory_space=pl.ANY),
                      pl.BlockSpec((1, d), lambda i, *_: (0, 0)),
                      w_a(up_shape), w_a(up_shape), w_a(down_shape),
                      w_b(up_shape), w_b(up_shape), w_b(down_shape)],
            out_specs=pl.BlockSpec((MOE_TILE * Y_RECORD, LANES), lambda i, *_: (i, 0)),
            scratch_shapes=[pltpu.VMEM((2, MOE_TILE * ROW_RECORD, LANES), jnp.float32),
                            pltpu.SemaphoreType.DMA((2,)),
                            pltpu.SMEM((n_tiles * MOE_TILE,), jnp.int32),
                            pltpu.VMEM((MOE_TILE, d), bf16),
                            pltpu.VMEM((MOE_TILE, d), jnp.float32)]),
        out_shape=jax.ShapeDtypeStruct((n_tiles * MOE_TILE * Y_RECORD, LANES), jnp.float32),
        compiler_params=pltpu.CompilerParams(dimension_semantics=("arbitrary",),
                                             vmem_limit_bytes=VMEM_LIMIT),
        name="experts",
    )(plan["e_a"], plan["e_b"], plan["n_used"], plan["tstart"], plan["cnt"], plan["cls"], plan["rank"],
      rows, g_moe[None, :], wg, wu, wd, wg, wu, wd)


def _ple_final(plan, rows, p, y_sorted, g_ple, w_ple_gate, w_ple_proj, g_final):
    n_tok, dp = p.shape
    d = w_ple_gate.shape[0]
    bf16 = jnp.bfloat16
    const = lambda shape: pl.BlockSpec(shape, lambda i, *_: (0,) * len(shape))
    tile = lambda rows_, width: pl.BlockSpec((rows_, width), lambda i, *_: (i, 0))
    return pl.pallas_call(
        _ple_final_kernel,
        grid_spec=pltpu.PrefetchScalarGridSpec(
            num_scalar_prefetch=3, grid=(n_tok // MOE_TILE,),
            in_specs=[tile(MOE_TILE * ROW_RECORD, LANES),
                      tile(MOE_TILE, dp),
                      pl.BlockSpec(memory_space=pl.ANY),
                      const((1, d)), const((d, d)), const((dp, d)), const((1, d))],
            out_specs=tile(MOE_TILE, d),
            scratch_shapes=[pltpu.VMEM((2, MOE_TILE * Y_RECORD, LANES), jnp.float32),
                            pltpu.SemaphoreType.DMA((2,))]),
        out_shape=jax.ShapeDtypeStruct((n_tok, d), jnp.float32),
        compiler_params=pltpu.CompilerParams(dimension_semantics=("arbitrary",),
                                             vmem_limit_bytes=VMEM_LIMIT),
        name="ple_final",
    )(plan["tstart"], plan["cls"], plan["rank"], rows, p, y_sorted,
      g_ple[None, :], w_ple_gate.astype(bf16), w_ple_proj.astype(bf16), g_final[None, :])


def kernel(x, p, g_mix, w_in, w_gla_gate, b_gla_gate, g_gla_out, w_conv, w_out, g_moe, w_group, b_group,
           w_router, b_router, w_exp_gate, w_exp_up, w_exp_down, g_ple, w_ple_gate, w_ple_proj, g_final):
    depth = w_in.shape[0]
    assert depth == 1, "the final norm is fused into the last (only) layer"
    b, s, d = x.shape
    n_tok = b * s
    assert s % MIX_TILE == 0 and n_tok % MOE_TILE == 0 and MOE_TILE % DMA_UNROLL == 0
    rows, meta, counts = _mixer(x, g_mix[0], w_in[0], w_gla_gate[0], b_gla_gate[0], g_gla_out[0], w_conv[0],
                                w_out[0], g_moe[0], w_group[0], b_group[0], w_router[0], b_router[0])
    plan = _sort_plan(meta, counts, n_tok)
    y_sorted = _experts(plan, rows, g_moe[0], w_exp_gate[0], w_exp_up[0], w_exp_down[0])
    out = _ple_final(plan, rows, p[0].reshape(n_tok, -1), y_sorted, g_ple[0], w_ple_gate[0], w_ple_proj[0],
                     g_final)
    return out.reshape(b, s, d)
```

```python
import functools

import jax
import jax.numpy as jnp
from jax import lax
from jax.experimental import pallas as pl
from jax.experimental.pallas import tpu as pltpu

EPS = 1e-6
GLA_HEADS = 4
GLA_DK = 64
GLA_DV = 128
GLA_QK = GLA_HEADS * GLA_DK
GLA_V = GLA_HEADS * GLA_DV
GLA_LOWRANK = 16
GLA_TAU = 16.0
CONV_K = 3
N_GROUPS = 4
EXPERTS_PER_GROUP = 4
N_EXPERTS = N_GROUPS * EXPERTS_PER_GROUP

LANES = 128
MIX_TILE = 256
MIX_LEVELS = 8
ROUTE_ROWS = 32
ROUTE_EXPERT_ROW0 = 8
MOE_TILE = 256
VMEM_LIMIT = 56 * 1024 * 1024

PAIR_A = (0, 2, 2, 0, 0, 1)
PAIR_B = (1, 1, 3, 3, 2, 3)
PAIR_OF_KEY = {1: 0, 6: 1, 11: 2, 3: 3, 2: 4, 7: 5}
N_PAIRS = len(PAIR_A)
N_CLASSES = N_GROUPS * N_PAIRS
INFO_CLS, INFO_RANK, INFO_WA, INFO_WB = 0, 1, 2, 3
PLACE_UNROLL = 8

_NT = (((1,), (1,)), ((), ()))


def _rms(x, g):
    return x * lax.rsqrt(jnp.mean(x * x, axis=-1, keepdims=True) + EPS) * g


def _dot(a, b):
    return jnp.dot(a, b, preferred_element_type=jnp.float32)


def _dot_nt(a, b):
    return lax.dot_general(a, b, _NT, preferred_element_type=jnp.float32)


def _split_bf16(a):
    hi = a.astype(jnp.bfloat16)
    return hi, (a - hi.astype(jnp.float32)).astype(jnp.bfloat16)


def _shift_rows(x, shift):
    return pltpu.roll(x, shift % x.shape[0], axis=0)


def _mixer_kernel(x_ref, gmix_ref, wqkvg_ref, wa_ref, wc3_ref, wgate_ref, bgate_ref, ggla_ref,
                  wconv_ref, wout_ref, gmoe_ref, wrt_ref, brt_ref, wge_ref, wue_ref, wde_ref,
                  rows_ref, meta_ref, counts_ref, wge16_ref, wue16_ref, wde16_ref,
                  st_ref, carry_ref, count_ref, level_ref, tril_ref):
    f32, bf16 = jnp.float32, jnp.bfloat16
    T = MIX_TILE
    D = x_ref.shape[-1]

    wge16_ref[...] = wge_ref[...].astype(bf16)
    wue16_ref[...] = wue_ref[...].astype(bf16)
    wde16_ref[...] = wde_ref[...].astype(bf16)

    @pl.when(pl.program_id(1) == 0)
    def _():
        st_ref[...] = jnp.zeros_like(st_ref)
        carry_ref[...] = jnp.zeros_like(carry_ref)

    @pl.when((pl.program_id(0) == 0) & (pl.program_id(1) == 0))
    def _():
        count_ref[...] = jnp.zeros_like(count_ref)
        tt = lax.broadcasted_iota(jnp.int32, (T, T), 0)
        ss = lax.broadcasted_iota(jnp.int32, (T, T), 1)
        txs = jnp.bitwise_xor(tt, ss)
        level = jnp.zeros((T, T), jnp.int32)
        for j in range(1, MIX_LEVELS):
            level = level + (txs >= (1 << j)).astype(jnp.int32)
        level_ref[...] = jnp.where(tt > ss, level, jnp.where(tt == ss, MIX_LEVELS, -1))
        tril_ref[...] = (ss <= tt).astype(bf16)

    x = x_ref[0]
    hb = _rms(x, gmix_ref[...]).astype(bf16)
    qkvg = _dot(hb, wqkvg_ref[...])
    q = qkvg[:, :GLA_QK] * (GLA_DK ** -0.5)
    k = qkvg[:, GLA_QK:2 * GLA_QK]
    v = qkvg[:, 2 * GLA_QK:2 * GLA_QK + GLA_V]
    g = qkvg[:, 2 * GLA_QK + GLA_V:]
    a_low = _dot(hb, wa_ref[...])
    a_hi, a_lo = _split_bf16(a_low)
    z = _dot(jnp.concatenate([a_hi, a_lo, a_hi], axis=1), wgate_ref[...]) + bgate_ref[...]
    la = (jnp.minimum(z, 0.0) - jnp.log(1.0 + jnp.exp(-jnp.abs(z)))) * (1.0 / GLA_TAU)

    row = lax.broadcasted_iota(jnp.int32, (T, GLA_QK), 0)

    def next_level(l, q_l, k_l, block):
        upper = ((row >> l) & 1) == 1
        below = _shift_rows(block, 1 << l)
        above = _shift_rows(block, -(1 << l))
        return (q_l * jnp.where(upper, below, 1.0), k_l * jnp.where(upper, 1.0, above),
                block * jnp.where(upper, below, above))

    decay = jnp.exp(la)
    H = T // 2
    assert GLA_DV == H
    half_level = level_ref[0:H, 0:H]
    lane_head_st = lax.broadcasted_iota(jnp.int32, (H, GLA_QK), 1) // GLA_DK

    def per_head_rows(a):
        return jnp.concatenate([jnp.where(lane_head_st == h, a, jnp.zeros_like(a)) for h in range(GLA_HEADS)],
                               axis=0)

    def head_blocks(p):
        return [p[:, h * H:(h + 1) * H] for h in range(GLA_HEADS)]

    diag0 = [jnp.zeros((H, H), f32) for _ in range(GLA_HEADS)]
    diag1 = [jnp.zeros((H, H), f32) for _ in range(GLA_HEADS)]

    def add_level(l, q_l, k_l):
        sel = half_level == l
        ql, kl = q_l.astype(bf16), k_l.astype(bf16)
        p0 = head_blocks(_dot_nt(ql[:H], per_head_rows(kl[:H])))
        p1 = head_blocks(_dot_nt(ql[H:], per_head_rows(kl[H:])))
        for h in range(GLA_HEADS):
            diag0[h] = jnp.where(sel, p0[h], diag0[h])
            diag1[h] = jnp.where(sel, p1[h], diag1[h])

    add_level(MIX_LEVELS, q, k)
    q_l, k_l, block = q * decay, k, decay
    for l in range(MIX_LEVELS - 1):
        add_level(l, q_l, k_l)
        q_l, k_l, block = next_level(l, q_l, k_l, block)
    low = head_blocks(_dot_nt(q_l[H:].astype(bf16), per_head_rows(k_l[:H].astype(bf16))))
    zero_block = jnp.zeros((H, H), f32)
    scores = [jnp.concatenate([jnp.concatenate([diag0[h], zero_block], axis=1),
                               jnp.concatenate([low[h], diag1[h]], axis=1)], axis=0) for h in range(GLA_HEADS)]

    q_in, k_out, tile_decay = next_level(MIX_LEVELS - 1, q_l, k_l, block)
    st = st_ref[...]
    o_state = _dot_nt(q_in.astype(bf16), per_head_rows(st.astype(bf16)))
    upd = _dot(v.T.astype(bf16), k_out.astype(bf16))
    new_st = st * tile_decay[:H]
    ggla = ggla_ref[...]
    y_heads = []
    for h in range(GLA_HEADS):
        v_h = v[:, h * GLA_DV:(h + 1) * GLA_DV]
        o = _dot(scores[h].astype(bf16), v_h.astype(bf16)) + o_state[:, h * GLA_DV:(h + 1) * GLA_DV]
        new_st = new_st + jnp.where(lane_head_st == h, upd[h * GLA_DV:(h + 1) * GLA_DV], 0.0)
        g_h = g[:, h * GLA_DV:(h + 1) * GLA_DV]
        y_heads.append(_rms(o, ggla) * (g_h * jax.nn.sigmoid(g_h)))
    st_ref[...] = new_st

    c3 = _dot(hb, wc3_ref[...])
    cw = c3.shape[1] // 3
    cb, cu = c3[:, :cw], c3[:, cw:2 * cw] * c3[:, 2 * cw:]
    crow = lax.broadcasted_iota(jnp.int32, (T, cw), 0)
    prev2, prev1 = carry_ref[0:1, :], carry_ref[1:2, :]
    m1 = jnp.where(crow == 0, prev1, _shift_rows(cu, 1))
    m2 = jnp.where(crow == 0, prev2, jnp.where(crow == 1, prev1, _shift_rows(cu, 2)))
    wconv = wconv_ref[...]
    y_conv = cb * (wconv[0:1, :] * m2 + wconv[1:2, :] * m1 + wconv[2:3, :] * cu)
    carry_ref[0:2, :] = cu[T - 2:, :]

    y = jnp.concatenate(y_heads + [y_conv], axis=1).astype(bf16)
    x1 = x + _dot(y, wout_ref[...])
    rows_ref[:, :D] = x1

    h2 = _rms(x1, gmoe_ref[...])
    h2_hi, h2_lo = _split_bf16(h2)
    part = _dot_nt(wrt_ref[...], h2_hi)
    logits = (part[:ROUTE_ROWS] + part[ROUTE_ROWS:] + _dot_nt(wrt_ref[:ROUTE_ROWS, :], h2_lo)) + brt_ref[...]
    gl = [logits[i:i + 1, :] for i in range(N_GROUPS)]
    gmax = functools.reduce(jnp.maximum, gl)
    gsum = functools.reduce(lambda a, b: a + b, [jnp.exp(t - gmax) for t in gl])
    p_grp = 1.0 / gsum
    g_sel = jnp.full_like(gmax, N_GROUPS - 1).astype(jnp.int32)
    for i in reversed(range(N_GROUPS - 1)):
        g_sel = jnp.where(gl[i] == gmax, i, g_sel)
    ig = []
    for j in range(EXPERTS_PER_GROUP):
        acc = jnp.zeros_like(gmax)
        for gi in range(N_GROUPS):
            r0 = ROUTE_EXPERT_ROW0 + gi * EXPERTS_PER_GROUP + j
            acc = acc + jnp.where(g_sel == gi, logits[r0:r0 + 1, :], 0.0)
        ig.append(acc)

    def first_argmax(vals):
        m = functools.reduce(jnp.maximum, vals)
        idx = jnp.full_like(m, len(vals) - 1).astype(jnp.int32)
        for i in reversed(range(len(vals) - 1)):
            idx = jnp.where(vals[i] == m, i, idx)
        return m, idx

    m1_, i1 = first_argmax(ig)
    m2_, i2 = first_argmax([jnp.where(i1 == j, -jnp.inf, ig[j]) for j in range(EXPERTS_PER_GROUP)])
    e21 = jnp.exp(m2_ - m1_)
    w1 = p_grp / (1.0 + e21)
    w2 = p_grp * e21 / (1.0 + e21)
    key = jnp.minimum(i1, i2) * EXPERTS_PER_GROUP + jnp.maximum(i1, i2)
    pair = jnp.zeros_like(key)
    a_loc = jnp.zeros_like(key)
    for kk, pp in PAIR_OF_KEY.items():
        pair = jnp.where(key == kk, pp, pair)
        a_loc = jnp.where(key == kk, PAIR_A[pp], a_loc)
    w_a = jnp.where(i1 == a_loc, w1, w2)
    w_b = jnp.where(i1 == a_loc, w2, w1)
    cls = g_sel * N_PAIRS + pair
    rr = lax.broadcasted_iota(jnp.int32, (LANES, T), 0)
    rec_t = (jnp.where(rr == INFO_WA, jnp.broadcast_to(w_a, (LANES, T)), 0.0)
             + jnp.where(rr == INFO_WB, jnp.broadcast_to(w_b, (LANES, T)), 0.0))
    rows_ref[:, D:] = rec_t.T

    onehot = (rr == jnp.broadcast_to(cls, (LANES, T))).astype(f32)
    count = count_ref[:, 0:1]
    before = _dot_nt(onehot.astype(bf16), tril_ref[...]) - onehot + count
    rank = jnp.sum(onehot * before, axis=0, keepdims=True).astype(jnp.int32)
    r8 = lax.broadcasted_iota(jnp.int32, (8, T), 0)
    meta_ref[0] = jnp.where(r8 == INFO_CLS, jnp.broadcast_to(cls, (8, T)),
                            jnp.where(r8 == INFO_RANK, jnp.broadcast_to(rank, (8, T)), 0))
    new_count = jnp.broadcast_to(count + jnp.sum(onehot, axis=1, keepdims=True), count_ref.shape)
    count_ref[...] = new_count
    counts_ref[...] = new_count.astype(jnp.int32)


class _RowGather:
    def __init__(self, index_of, src_hbm, buf, sems):
        self.index_of, self.src_hbm, self.buf, self.sems = index_of, src_hbm, buf, sems

    def start(self, tile, slot):
        for r in range(MOE_TILE):
            pltpu.make_async_copy(self.src_hbm.at[pl.ds(self.index_of(tile * MOE_TILE + r), 1), :],
                                  self.buf.at[slot, pl.ds(r, 1), :], self.sems.at[slot]).start()

    def wait(self, slot):
        pltpu.make_async_copy(self.src_hbm.at[pl.ds(0, MOE_TILE), :], self.buf.at[slot], self.sems.at[slot]).wait()


def _sorted_slot(tstart_ref, cls_ref, rank_ref, t):
    return tstart_ref[cls_ref[t]] * MOE_TILE + rank_ref[t]


def _expert_kernel(ea_ref, eb_ref, nused_ref, tstart_ref, cnt_ref, cls_ref, rank_ref,
                   rows_hbm, gmoe_ref, wga_ref, wua_ref, wda_ref, wgb_ref, wub_ref, wdb_ref,
                   y_ref, buf, sems, src_ref):
    bf16 = jnp.bfloat16
    D = y_ref.shape[-1]
    n_tok = cls_ref.shape[0]
    step, n_used = pl.program_id(0), nused_ref[0]
    gather = _RowGather(lambda i: src_ref[i], rows_hbm, buf, sems)

    @pl.when(step == 0)
    def _():
        for c in range(N_CLASSES):
            base = tstart_ref[c] * MOE_TILE

            def pad(r, carry, base=base):
                src_ref[base + r] = 0
                return carry

            lax.fori_loop(cnt_ref[c], (tstart_ref[c + 1] - tstart_ref[c]) * MOE_TILE, pad, 0)

        def place(it, carry):
            for u in range(PLACE_UNROLL):
                t = it * PLACE_UNROLL + u
                src_ref[_sorted_slot(tstart_ref, cls_ref, rank_ref, t)] = t
            return carry

        lax.fori_loop(0, n_tok // PLACE_UNROLL, place, 0)
        gather.start(0, 0)

    slot = step % 2

    @pl.when(step + 1 < n_used)
    def _():
        gather.start(step + 1, 1 - slot)

    @pl.when(step < n_used)
    def _():
        gather.wait(slot)
        rows = buf[slot]
        rec = rows[:, D:]
        h2 = _rms(rows[:, :D], gmoe_ref[...]).astype(bf16)

        def expert(wg_ref, wu_ref, wd_ref):
            gate = _dot(h2, wg_ref[0])
            hid = (gate * jax.nn.sigmoid(gate)) * _dot(h2, wu_ref[0])
            return _dot(hid.astype(bf16), wd_ref[0])

        y = rec[:, INFO_WA:INFO_WA + 1] * expert(wga_ref, wua_ref, wda_ref)
        y_ref[...] = y + rec[:, INFO_WB:INFO_WB + 1] * expert(wgb_ref, wub_ref, wdb_ref)

    @pl.when(step >= n_used)
    def _():
        y_ref[...] = jnp.zeros_like(y_ref)


def _ple_final_kernel(tstart_ref, cls_ref, rank_ref, x1_ref, p_ref, y_hbm, gple_ref, wpg_ref, wpp_ref, gfin_ref,
                      out_ref, buf, sems):
    bf16 = jnp.bfloat16
    step, n_steps = pl.program_id(0), pl.num_programs(0)
    gather = _RowGather(lambda t: _sorted_slot(tstart_ref, cls_ref, rank_ref, t), y_hbm, buf, sems)

    @pl.when(step == 0)
    def _():
        gather.start(0, 0)

    slot = step % 2

    @pl.when(step + 1 < n_steps)
    def _():
        gather.start(step + 1, 1 - slot)

    gather.wait(slot)
    x2 = x1_ref[...] + buf[slot]
    gate_p = jax.nn.sigmoid(_dot(_rms(x2, gple_ref[...]).astype(bf16), wpg_ref[...]))
    x3 = x2 + gate_p * _dot(p_ref[...].astype(bf16), wpp_ref[...])
    out_ref[...] = _rms(x3, gfin_ref[...])


def _const_spec(shape):
    return pl.BlockSpec(shape, lambda *_: (0,) * len(shape))


def _mixer(x, g_mix, w_in, w_gla_gate, b_gla_gate, g_gla_out, w_conv, w_out, g_moe,
           w_group, b_group, w_router, b_router, w_exp_gate, w_exp_up, w_exp_down):
    b, s, d = x.shape
    n_s = s // MIX_TILE
    n_steps = b * n_s
    n_exp, _, de = w_exp_gate.shape
    assert (n_exp * d) % (16 * n_steps) == 0 and (n_exp * de) % (16 * n_steps) == 0
    up_rows, down_rows = n_exp * d // n_steps, n_exp * de // n_steps
    bf16 = jnp.bfloat16
    n_qkvg = 2 * GLA_QK + 2 * GLA_V
    w_qkvg = w_in[:, :n_qkvg].astype(bf16)
    w_a = w_in[:, n_qkvg:n_qkvg + GLA_LOWRANK].astype(bf16)
    w_c3 = w_in[:, n_qkvg + GLA_LOWRANK:].astype(bf16)
    cw = w_c3.shape[1] // 3
    wrt = jnp.zeros((ROUTE_ROWS, d), jnp.float32)
    wrt = wrt.at[:N_GROUPS].set(w_group.T).at[ROUTE_EXPERT_ROW0:ROUTE_EXPERT_ROW0 + N_EXPERTS].set(w_router.T)
    brt = jnp.zeros((ROUTE_ROWS, 1), jnp.float32)
    brt = brt.at[:N_GROUPS, 0].set(b_group).at[ROUTE_EXPERT_ROW0:ROUTE_EXPERT_ROW0 + N_EXPERTS, 0].set(b_router)
    wrt_split = jnp.concatenate(_split_bf16(wrt), axis=0)
    gate_hi, gate_lo = _split_bf16(w_gla_gate)
    w_gate_split = jnp.concatenate([gate_hi, gate_hi, gate_lo], axis=0)
    args = (x, g_mix[None, :], w_qkvg, w_a, w_c3, w_gate_split, b_gla_gate[None, :], g_gla_out[None, :],
            w_conv, w_out.astype(bf16), g_moe[None, :], wrt_split, brt)
    slabs = (w_exp_gate.reshape(n_exp * d, de), w_exp_up.reshape(n_exp * d, de), w_exp_down.reshape(n_exp * de, d))
    step_block = lambda rows_, width: pl.BlockSpec((rows_, width), lambda i, j: (i * n_s + j, 0))
    slab_specs = [step_block(up_rows, de), step_block(up_rows, de), step_block(down_rows, d)]
    in_specs = [pl.BlockSpec((1, MIX_TILE, d), lambda i, j: (i, j, 0))]
    in_specs += [_const_spec(a.shape) for a in args[1:]] + slab_specs
    rows, meta, counts, wg16, wu16, wd16 = pl.pallas_call(
        _mixer_kernel,
        grid=(b, n_s),
        in_specs=in_specs,
        out_specs=[step_block(MIX_TILE, d + LANES),
                   pl.BlockSpec((1, 8, MIX_TILE), lambda i, j: (i * n_s + j, 0, 0)),
                   _const_spec((LANES, LANES))] + slab_specs,
        out_shape=[jax.ShapeDtypeStruct((b * s, d + LANES), jnp.float32),
                   jax.ShapeDtypeStruct((b * n_s, 8, MIX_TILE), jnp.int32),
                   jax.ShapeDtypeStruct((LANES, LANES), jnp.int32)]
                  + [jax.ShapeDtypeStruct(w.shape, bf16) for w in slabs],
        scratch_shapes=[pltpu.VMEM((GLA_DV, GLA_QK), jnp.float32),
                        pltpu.VMEM((8, cw), jnp.float32),
                        pltpu.VMEM((LANES, LANES), jnp.float32),
                        pltpu.VMEM((MIX_TILE, MIX_TILE), jnp.int32),
                        pltpu.VMEM((MIX_TILE, MIX_TILE), bf16)],
        compiler_params=pltpu.CompilerParams(dimension_semantics=("arbitrary", "arbitrary"),
                                             vmem_limit_bytes=VMEM_LIMIT),
        name="mixer",
    )(*args, *slabs)
    return (rows, meta, counts, wg16.reshape(n_exp, d, de), wu16.reshape(n_exp, d, de), wd16.reshape(n_exp, de, d))


def _sort_plan(meta, counts, n_tok):
    i32 = jnp.int32
    n_tiles = n_tok // MOE_TILE + N_CLASSES
    cls = meta[:, INFO_CLS, :].reshape(n_tok)
    rank = meta[:, INFO_RANK, :].reshape(n_tok)
    cnt = counts[:N_CLASSES, 0]
    tiles_per_cls = (cnt + MOE_TILE - 1) // MOE_TILE
    tile_end = jnp.cumsum(tiles_per_cls)
    n_used = tile_end[-1:]
    tstart = jnp.concatenate([tile_end - tiles_per_cls, n_used])
    tile_id = jnp.minimum(jnp.arange(n_tiles, dtype=i32), n_used - 1)
    tile_cls = jnp.sum((tile_id[:, None] >= tile_end[None, :]).astype(i32), axis=1)
    grp, pair = tile_cls // N_PAIRS, tile_cls % N_PAIRS
    e_a, e_b = grp * EXPERTS_PER_GROUP, grp * EXPERTS_PER_GROUP
    for pp in range(N_PAIRS):
        e_a = e_a + jnp.where(pair == pp, PAIR_A[pp], 0)
        e_b = e_b + jnp.where(pair == pp, PAIR_B[pp], 0)
    return dict(cls=cls, rank=rank, cnt=cnt, tstart=tstart, e_a=e_a, e_b=e_b, n_used=n_used, n_tiles=n_tiles)


def _experts(plan, rows, g_moe, wg, wu, wd):
    n_tiles = plan["n_tiles"]
    d, de = wg.shape[-2:]
    w_a = lambda shape: pl.BlockSpec(shape, lambda i, ea, eb, *_: (ea[i], 0, 0))
    w_b = lambda shape: pl.BlockSpec(shape, lambda i, ea, eb, *_: (eb[i], 0, 0))
    return pl.pallas_call(
        _expert_kernel,
        grid_spec=pltpu.PrefetchScalarGridSpec(
            num_scalar_prefetch=7, grid=(n_tiles,),
            in_specs=[pl.BlockSpec(memory_space=pl.ANY),
                      pl.BlockSpec((1, d), lambda i, *_: (0, 0)),
                      w_a((1, d, de)), w_a((1, d, de)), w_a((1, de, d)),
                      w_b((1, d, de)), w_b((1, d, de)), w_b((1, de, d))],
            out_specs=pl.BlockSpec((MOE_TILE, d), lambda i, *_: (i, 0)),
            scratch_shapes=[pltpu.VMEM((2, MOE_TILE, rows.shape[1]), jnp.float32),
                            pltpu.SemaphoreType.DMA((2,)),
                            pltpu.SMEM((n_tiles * MOE_TILE,), jnp.int32)]),
        out_shape=jax.ShapeDtypeStruct((n_tiles * MOE_TILE, d), jnp.float32),
        compiler_params=pltpu.CompilerParams(dimension_semantics=("arbitrary",),
                                             vmem_limit_bytes=VMEM_LIMIT),
        name="experts",
    )(plan["e_a"], plan["e_b"], plan["n_used"], plan["tstart"], plan["cnt"], plan["cls"], plan["rank"],
      rows, g_moe[None, :], wg, wu, wd, wg, wu, wd)


def _ple_final(plan, rows, p, y_sorted, g_ple, w_ple_gate, w_ple_proj, g_final):
    n_tok, dp = p.shape
    d = y_sorted.shape[1]
    bf16 = jnp.bfloat16
    const = lambda shape: pl.BlockSpec(shape, lambda i, *_: (0,) * len(shape))
    tile = lambda width: pl.BlockSpec((MOE_TILE, width), lambda i, *_: (i, 0))
    return pl.pallas_call(
        _ple_final_kernel,
        grid_spec=pltpu.PrefetchScalarGridSpec(
            num_scalar_prefetch=3, grid=(n_tok // MOE_TILE,),
            in_specs=[tile(d),
                      tile(dp),
                      pl.BlockSpec(memory_space=pl.ANY),
                      const((1, d)), const((d, d)), const((dp, d)), const((1, d))],
            out_specs=tile(d),
            scratch_shapes=[pltpu.VMEM((2, MOE_TILE, d), jnp.float32),
                            pltpu.SemaphoreType.DMA((2,))]),
        out_shape=jax.ShapeDtypeStruct((n_tok, d), jnp.float32),
        compiler_params=pltpu.CompilerParams(dimension_semantics=("arbitrary",),
                                             vmem_limit_bytes=VMEM_LIMIT),
        name="ple_final",
    )(plan["tstart"], plan["cls"], plan["rank"], rows, p, y_sorted,
      g_ple[None, :], w_ple_gate.astype(bf16), w_ple_proj.astype(bf16), g_final[None, :])


def kernel(x, p, g_mix, w_in, w_gla_gate, b_gla_gate, g_gla_out, w_conv, w_out, g_moe, w_group, b_group,
           w_router, b_router, w_exp_gate, w_exp_up, w_exp_down, g_ple, w_ple_gate, w_ple_proj, g_final):
    depth = w_in.shape[0]
    assert depth == 1, "the final norm is fused into the last (only) layer"
    b, s, d = x.shape
    n_tok = b * s
    assert s % MIX_TILE == 0 and n_tok % MOE_TILE == 0 and n_tok % PLACE_UNROLL == 0
    rows, meta, counts, wg16, wu16, wd16 = _mixer(
        x, g_mix[0], w_in[0], w_gla_gate[0], b_gla_gate[0], g_gla_out[0], w_conv[0], w_out[0], g_moe[0],
        w_group[0], b_group[0], w_router[0], b_router[0], w_exp_gate[0], w_exp_up[0], w_exp_down[0])
    plan = _sort_plan(meta, counts, n_tok)
    y_sorted = _experts(plan, rows, g_moe[0], wg16, wu16, wd16)
    out = _ple_final(plan, rows, p[0].reshape(n_tok, -1), y_sorted, g_ple[0], w_ple_gate[0], w_ple_proj[0],
                     g_final)
    return out.reshape(b, s, d)
```

```python
import functools

import jax
import jax.numpy as jnp
from jax import lax
from jax.experimental import pallas as pl
from jax.experimental.pallas import tpu as pltpu

EPS = 1e-6
GLA_HEADS = 4
GLA_DK = 64
GLA_DV = 128
GLA_QK = GLA_HEADS * GLA_DK
GLA_V = GLA_HEADS * GLA_DV
GLA_LOWRANK = 16
GLA_TAU = 16.0
CONV_K = 3
N_GROUPS = 4
EXPERTS_PER_GROUP = 4
N_EXPERTS = N_GROUPS * EXPERTS_PER_GROUP

LANES = 128
MIX_TILE = 256
MIX_LEVELS = 8
ROUTE_ROWS = 32
ROUTE_EXPERT_ROW0 = 8
MOE_TILE = 256
VMEM_LIMIT = 56 * 1024 * 1024

PAIR_A = (0, 2, 2, 0, 0, 1)
PAIR_B = (1, 1, 3, 3, 2, 3)
PAIR_OF_KEY = {1: 0, 6: 1, 11: 2, 3: 3, 2: 4, 7: 5}
N_PAIRS = len(PAIR_A)
N_CLASSES = N_GROUPS * N_PAIRS
INFO_CLS, INFO_RANK, INFO_WA, INFO_WB = 0, 1, 2, 3
PLACE_UNROLL = 8

_NT = (((1,), (1,)), ((), ()))
_DONE = object()


def _rms(x, g):
    return x * lax.rsqrt(jnp.mean(x * x, axis=-1, keepdims=True) + EPS) * g


def _dot(a, b):
    return jnp.dot(a, b, preferred_element_type=jnp.float32)


def _dot_nt(a, b):
    return lax.dot_general(a, b, _NT, preferred_element_type=jnp.float32)


def _split_bf16(a):
    hi = a.astype(jnp.bfloat16)
    return hi, (a - hi.astype(jnp.float32)).astype(jnp.bfloat16)


def _shift_rows(x, shift):
    return pltpu.roll(x, shift % x.shape[0], axis=0)


def _mixer_kernel(x_ref, gmix_ref, wqkvg_ref, wa_ref, wc3_ref, wgate_ref, bgate_ref, ggla_ref,
                  wconv_ref, wout_ref, gmoe_ref, wrt_ref, brt_ref, wge_ref, wue_ref, wde_ref,
                  rows_ref, meta_ref, counts_ref, wge16_ref, wue16_ref, wde16_ref,
                  st_ref, carry_ref, count_ref, level_ref, tril_ref):
    bf16 = jnp.bfloat16
    T = MIX_TILE

    wge16_ref[...] = wge_ref[...].astype(bf16)
    wue16_ref[...] = wue_ref[...].astype(bf16)
    wde16_ref[...] = wde_ref[...].astype(bf16)

    @pl.when(pl.program_id(0) == 0)
    def _():
        st_ref[...] = jnp.zeros_like(st_ref)
        carry_ref[...] = jnp.zeros_like(carry_ref)
        count_ref[...] = jnp.zeros_like(count_ref)
        tt = lax.broadcasted_iota(jnp.int32, (T, T), 0)
        ss = lax.broadcasted_iota(jnp.int32, (T, T), 1)
        txs = jnp.bitwise_xor(tt, ss)
        level = jnp.zeros((T, T), jnp.int32)
        for j in range(1, MIX_LEVELS):
            level = level + (txs >= (1 << j)).astype(jnp.int32)
        level_ref[...] = jnp.where(tt > ss, level, jnp.where(tt == ss, MIX_LEVELS, -1))
        tril_ref[...] = (ss <= tt).astype(bf16)

    tiles = [_mixer_tile(x_ref.at[pl.ds(b, 1)], gmix_ref, wqkvg_ref, wa_ref, wc3_ref, wgate_ref, bgate_ref,
                         ggla_ref, wconv_ref, wout_ref, gmoe_ref, wrt_ref, brt_ref,
                         rows_ref.at[b], meta_ref.at[b], counts_ref,
                         st_ref.at[b], carry_ref.at[b], count_ref, level_ref, tril_ref)
             for b in range(x_ref.shape[0])]
    while tiles:
        tiles = [t for t in tiles if next(t, _DONE) is not _DONE]


def _mixer_tile(x_ref, gmix_ref, wqkvg_ref, wa_ref, wc3_ref, wgate_ref, bgate_ref, ggla_ref,
                wconv_ref, wout_ref, gmoe_ref, wrt_ref, brt_ref,
                rows_ref, meta_ref, counts_ref,
                st_ref, carry_ref, count_ref, level_ref, tril_ref):
    f32, bf16 = jnp.float32, jnp.bfloat16
    T = MIX_TILE
    D = x_ref.shape[-1]

    x = x_ref[0]
    hb = _rms(x, gmix_ref[...]).astype(bf16)
    qkvg = _dot(hb, wqkvg_ref[...])
    q = qkvg[:, :GLA_QK] * (GLA_DK ** -0.5)
    k = qkvg[:, GLA_QK:2 * GLA_QK]
    v = qkvg[:, 2 * GLA_QK:2 * GLA_QK + GLA_V]
    g = qkvg[:, 2 * GLA_QK + GLA_V:]
    a_low = _dot(hb, wa_ref[...])
    a_hi, a_lo = _split_bf16(a_low)
    z = _dot(jnp.concatenate([a_hi, a_lo, a_hi], axis=1), wgate_ref[...]) + bgate_ref[...]
    la = (jnp.minimum(z, 0.0) - jnp.log(1.0 + jnp.exp(-jnp.abs(z)))) * (1.0 / GLA_TAU)
    yield

    row = lax.broadcasted_iota(jnp.int32, (T, GLA_QK), 0)

    def next_level(l, q_l, k_l, block):
        upper = ((row >> l) & 1) == 1
        below = _shift_rows(block, 1 << l)
        above = _shift_rows(block, -(1 << l))
        return (q_l * jnp.where(upper, below, 1.0), k_l * jnp.where(upper, 1.0, above),
                block * jnp.where(upper, below, above))

    decay = jnp.exp(la)
    H = T // 2
    assert GLA_DV == H
    half_level = level_ref[0:H, 0:H]
    lane_head_st = lax.broadcasted_iota(jnp.int32, (H, GLA_QK), 1) // GLA_DK

    def per_head_rows(a):
        return jnp.concatenate([jnp.where(lane_head_st == h, a, jnp.zeros_like(a)) for h in range(GLA_HEADS)],
                               axis=0)

    def head_blocks(p):
        return [p[:, h * H:(h + 1) * H] for h in range(GLA_HEADS)]

    diag0 = [jnp.zeros((H, H), f32) for _ in range(GLA_HEADS)]
    diag1 = [jnp.zeros((H, H), f32) for _ in range(GLA_HEADS)]

    def add_level(l, q_l, k_l):
        sel = half_level == l
        ql, kl = q_l.astype(bf16), k_l.astype(bf16)
        p0 = head_blocks(_dot_nt(ql[:H], per_head_rows(kl[:H])))
        p1 = head_blocks(_dot_nt(ql[H:], per_head_rows(kl[H:])))
        for h in range(GLA_HEADS):
            diag0[h] = jnp.where(sel, p0[h], diag0[h])
            diag1[h] = jnp.where(sel, p1[h], diag1[h])

    add_level(MIX_LEVELS, q, k)
    q_l, k_l, block = q * decay, k, decay
    for l in range(MIX_LEVELS - 1):
        add_level(l, q_l, k_l)
        q_l, k_l, block = next_level(l, q_l, k_l, block)
        if l % 2 == 1:
            yield
    low = head_blocks(_dot_nt(q_l[H:].astype(bf16), per_head_rows(k_l[:H].astype(bf16))))
    zero_block = jnp.zeros((H, H), f32)
    scores = [jnp.concatenate([jnp.concatenate([diag0[h], zero_block], axis=1),
                               jnp.concatenate([low[h], diag1[h]], axis=1)], axis=0) for h in range(GLA_HEADS)]

    yield

    q_in, k_out, tile_decay = next_level(MIX_LEVELS - 1, q_l, k_l, block)
    st = st_ref[...]
    o_state = _dot_nt(q_in.astype(bf16), per_head_rows(st.astype(bf16)))
    upd = _dot(v.T.astype(bf16), k_out.astype(bf16))
    new_st = st * tile_decay[:H]
    ggla = ggla_ref[...]
    y_heads = []
    for h in range(GLA_HEADS):
        v_h = v[:, h * GLA_DV:(h + 1) * GLA_DV]
        o = _dot(scores[h].astype(bf16), v_h.astype(bf16)) + o_state[:, h * GLA_DV:(h + 1) * GLA_DV]
        new_st = new_st + jnp.where(lane_head_st == h, upd[h * GLA_DV:(h + 1) * GLA_DV], 0.0)
        g_h = g[:, h * GLA_DV:(h + 1) * GLA_DV]
        y_heads.append(_rms(o, ggla) * (g_h * jax.nn.sigmoid(g_h)))
        if h % 2 == 1:
            yield
    st_ref[...] = new_st

    yield

    c3 = _dot(hb, wc3_ref[...])
    cw = c3.shape[1] // 3
    cb, cu = c3[:, :cw], c3[:, cw:2 * cw] * c3[:, 2 * cw:]
    crow = lax.broadcasted_iota(jnp.int32, (T, cw), 0)
    prev2, prev1 = carry_ref[0:1, :], carry_ref[1:2, :]
    m1 = jnp.where(crow == 0, prev1, _shift_rows(cu, 1))
    m2 = jnp.where(crow == 0, prev2, jnp.where(crow == 1, prev1, _shift_rows(cu, 2)))
    wconv = wconv_ref[...]
    y_conv = cb * (wconv[0:1, :] * m2 + wconv[1:2, :] * m1 + wconv[2:3, :] * cu)
    carry_ref[0:2, :] = cu[T - 2:, :]

    y = jnp.concatenate(y_heads + [y_conv], axis=1).astype(bf16)
    x1 = x + _dot(y, wout_ref[...])
    rows_ref[:, :D] = x1

    yield

    h2 = _rms(x1, gmoe_ref[...])
    h2_hi, h2_lo = _split_bf16(h2)
    part = _dot_nt(wrt_ref[...], h2_hi)
    logits = (part[:ROUTE_ROWS] + part[ROUTE_ROWS:] + _dot_nt(wrt_ref[:ROUTE_ROWS, :], h2_lo)) + brt_ref[...]
    gl = [logits[i:i + 1, :] for i in range(N_GROUPS)]
    gmax = functools.reduce(jnp.maximum, gl)
    gsum = functools.reduce(lambda a, b: a + b, [jnp.exp(t - gmax) for t in gl])
    p_grp = 1.0 / gsum
    g_sel = jnp.full_like(gmax, N_GROUPS - 1).astype(jnp.int32)
    for i in reversed(range(N_GROUPS - 1)):
        g_sel = jnp.where(gl[i] == gmax, i, g_sel)
    ig = []
    for j in range(EXPERTS_PER_GROUP):
        acc = jnp.zeros_like(gmax)
        for gi in range(N_GROUPS):
            r0 = ROUTE_EXPERT_ROW0 + gi * EXPERTS_PER_GROUP + j
            acc = acc + jnp.where(g_sel == gi, logits[r0:r0 + 1, :], 0.0)
        ig.append(acc)

    def first_argmax(vals):
        m = functools.reduce(jnp.maximum, vals)
        idx = jnp.full_like(m, len(vals) - 1).astype(jnp.int32)
        for i in reversed(range(len(vals) - 1)):
            idx = jnp.where(vals[i] == m, i, idx)
        return m, idx

    m1_, i1 = first_argmax(ig)
    m2_, i2 = first_argmax([jnp.where(i1 == j, -jnp.inf, ig[j]) for j in range(EXPERTS_PER_GROUP)])
    e21 = jnp.exp(m2_ - m1_)
    w1 = p_grp / (1.0 + e21)
    w2 = p_grp * e21 / (1.0 + e21)
    key = jnp.minimum(i1, i2) * EXPERTS_PER_GROUP + jnp.maximum(i1, i2)
    pair = jnp.zeros_like(key)
    a_loc = jnp.zeros_like(key)
    for kk, pp in PAIR_OF_KEY.items():
        pair = jnp.where(key == kk, pp, pair)
        a_loc = jnp.where(key == kk, PAIR_A[pp], a_loc)
    w_a = jnp.where(i1 == a_loc, w1, w2)
    w_b = jnp.where(i1 == a_loc, w2, w1)
    cls = g_sel * N_PAIRS + pair
    rr = lax.broadcasted_iota(jnp.int32, (LANES, T), 0)
    rec_t = (jnp.where(rr == INFO_WA, jnp.broadcast_to(w_a, (LANES, T)), 0.0)
             + jnp.where(rr == INFO_WB, jnp.broadcast_to(w_b, (LANES, T)), 0.0))
    rows_ref[:, D:] = rec_t.T

    onehot = (rr == jnp.broadcast_to(cls, (LANES, T))).astype(f32)
    count = count_ref[:, 0:1]
    before = _dot_nt(onehot.astype(bf16), tril_ref[...]) - onehot + count
    rank = jnp.sum(onehot * before, axis=0, keepdims=True).astype(jnp.int32)
    r8 = lax.broadcasted_iota(jnp.int32, (8, T), 0)
    meta_ref[0] = jnp.where(r8 == INFO_CLS, jnp.broadcast_to(cls, (8, T)),
                            jnp.where(r8 == INFO_RANK, jnp.broadcast_to(rank, (8, T)), 0))
    new_count = jnp.broadcast_to(count + jnp.sum(onehot, axis=1, keepdims=True), count_ref.shape)
    count_ref[...] = new_count
    counts_ref[...] = new_count.astype(jnp.int32)


class _RowGather:
    def __init__(self, index_of, src_hbm, buf, sems):
        self.index_of, self.src_hbm, self.buf, self.sems = index_of, src_hbm, buf, sems

    def start(self, tile, slot):
        for r in range(MOE_TILE):
            pltpu.make_async_copy(self.src_hbm.at[pl.ds(self.index_of(tile * MOE_TILE + r), 1), :],
                                  self.buf.at[slot, pl.ds(r, 1), :], self.sems.at[slot]).start()

    def wait(self, slot):
        pltpu.make_async_copy(self.src_hbm.at[pl.ds(0, MOE_TILE), :], self.buf.at[slot], self.sems.at[slot]).wait()


def _sorted_slot(tstart_ref, cls_ref, rank_ref, t):
    return tstart_ref[cls_ref[t]] * MOE_TILE + rank_ref[t]


def _expert_kernel(ea_ref, eb_ref, nused_ref, tstart_ref, cnt_ref, cls_ref, rank_ref,
                   rows_hbm, gmoe_ref, wga_ref, wua_ref, wda_ref, wgb_ref, wub_ref, wdb_ref,
                   y_ref, buf, sems, src_ref):
    bf16 = jnp.bfloat16
    D = y_ref.shape[-1]
    n_tok = cls_ref.shape[0]
    step, n_used = pl.program_id(0), nused_ref[0]
    gather = _RowGather(lambda i: src_ref[i], rows_hbm, buf, sems)

    @pl.when(step == 0)
    def _():
        for c in range(N_CLASSES):
            base = tstart_ref[c] * MOE_TILE

            def pad(r, carry, base=base):
                src_ref[base + r] = 0
                return carry

            lax.fori_loop(cnt_ref[c], (tstart_ref[c + 1] - tstart_ref[c]) * MOE_TILE, pad, 0)

        def place(it, carry):
            for u in range(PLACE_UNROLL):
                t = it * PLACE_UNROLL + u
                src_ref[_sorted_slot(tstart_ref, cls_ref, rank_ref, t)] = t
            return carry

        lax.fori_loop(0, n_tok // PLACE_UNROLL, place, 0)
        gather.start(0, 0)

    slot = step % 2

    @pl.when(step + 1 < n_used)
    def _():
        gather.start(step + 1, 1 - slot)

    @pl.when(step < n_used)
    def _():
        gather.wait(slot)
        rows = buf[slot]
        rec = rows[:, D:]
        h2 = _rms(rows[:, :D], gmoe_ref[...]).astype(bf16)

        def expert(wg_ref, wu_ref, wd_ref):
            gate = _dot(h2, wg_ref[0])
            hid = (gate * jax.nn.sigmoid(gate)) * _dot(h2, wu_ref[0])
            return _dot(hid.astype(bf16), wd_ref[0])

        y = rec[:, INFO_WA:INFO_WA + 1] * expert(wga_ref, wua_ref, wda_ref)
        y_ref[...] = y + rec[:, INFO_WB:INFO_WB + 1] * expert(wgb_ref, wub_ref, wdb_ref)

    @pl.when(step >= n_used)
    def _():
        y_ref[...] = jnp.zeros_like(y_ref)


def _ple_final_kernel(tstart_ref, cls_ref, rank_ref, x1_ref, p_ref, y_hbm, gple_ref, wpg_ref, wpp_ref, gfin_ref,
                      out_ref, buf, sems):
    bf16 = jnp.bfloat16
    step, n_steps = pl.program_id(0), pl.num_programs(0)
    gather = _RowGather(lambda t: _sorted_slot(tstart_ref, cls_ref, rank_ref, t), y_hbm, buf, sems)

    @pl.when(step == 0)
    def _():
        gather.start(0, 0)

    slot = step % 2

    @pl.when(step + 1 < n_steps)
    def _():
        gather.start(step + 1, 1 - slot)

    gather.wait(slot)
    x2 = x1_ref[...] + buf[slot]
    gate_p = jax.nn.sigmoid(_dot(_rms(x2, gple_ref[...]).astype(bf16), wpg_ref[...]))
    x3 = x2 + gate_p * _dot(p_ref[...].astype(bf16), wpp_ref[...])
    out_ref[...] = _rms(x3, gfin_ref[...])


def _const_spec(shape):
    return pl.BlockSpec(shape, lambda *_: (0,) * len(shape))


def _mixer(x, g_mix, w_in, w_gla_gate, b_gla_gate, g_gla_out, w_conv, w_out, g_moe,
           w_group, b_group, w_router, b_router, w_exp_gate, w_exp_up, w_exp_down):
    b, s, d = x.shape
    n_steps = s // MIX_TILE
    n_exp, _, de = w_exp_gate.shape
    assert (n_exp * d) % (16 * n_steps) == 0 and (n_exp * de) % (16 * n_steps) == 0
    up_rows, down_rows = n_exp * d // n_steps, n_exp * de // n_steps
    bf16 = jnp.bfloat16
    n_qkvg = 2 * GLA_QK + 2 * GLA_V
    w_qkvg = w_in[:, :n_qkvg].astype(bf16)
    w_a = w_in[:, n_qkvg:n_qkvg + GLA_LOWRANK].astype(bf16)
    w_c3 = w_in[:, n_qkvg + GLA_LOWRANK:].astype(bf16)
    cw = w_c3.shape[1] // 3
    wrt = jnp.zeros((ROUTE_ROWS, d), jnp.float32)
    wrt = wrt.at[:N_GROUPS].set(w_group.T).at[ROUTE_EXPERT_ROW0:ROUTE_EXPERT_ROW0 + N_EXPERTS].set(w_router.T)
    brt = jnp.zeros((ROUTE_ROWS, 1), jnp.float32)
    brt = brt.at[:N_GROUPS, 0].set(b_group).at[ROUTE_EXPERT_ROW0:ROUTE_EXPERT_ROW0 + N_EXPERTS, 0].set(b_router)
    wrt_split = jnp.concatenate(_split_bf16(wrt), axis=0)
    gate_hi, gate_lo = _split_bf16(w_gla_gate)
    w_gate_split = jnp.concatenate([gate_hi, gate_hi, gate_lo], axis=0)
    args = (x, g_mix[None, :], w_qkvg, w_a, w_c3, w_gate_split, b_gla_gate[None, :], g_gla_out[None, :],
            w_conv, w_out.astype(bf16), g_moe[None, :], wrt_split, brt)
    slabs = (w_exp_gate.reshape(n_exp * d, de), w_exp_up.reshape(n_exp * d, de), w_exp_down.reshape(n_exp * de, d))
    slab_specs = [pl.BlockSpec((rows_, width), lambda j: (j, 0))
                  for rows_, width in ((up_rows, de), (up_rows, de), (down_rows, d))]
    in_specs = [pl.BlockSpec((b, MIX_TILE, d), lambda j: (0, j, 0))]
    in_specs += [_const_spec(a.shape) for a in args[1:]] + slab_specs
    rows, meta, counts, wg16, wu16, wd16 = pl.pallas_call(
        _mixer_kernel,
        grid=(n_steps,),
        in_specs=in_specs,
        out_specs=[pl.BlockSpec((b, MIX_TILE, d + LANES), lambda j: (0, j, 0)),
                   pl.BlockSpec((b, 1, 8, MIX_TILE), lambda j: (0, j, 0, 0)),
                   _const_spec((LANES, LANES))] + slab_specs,
        out_shape=[jax.ShapeDtypeStruct((b, s, d + LANES), jnp.float32),
                   jax.ShapeDtypeStruct((b, n_steps, 8, MIX_TILE), jnp.int32),
                   jax.ShapeDtypeStruct((LANES, LANES), jnp.int32)]
                  + [jax.ShapeDtypeStruct(w.shape, bf16) for w in slabs],
        scratch_shapes=[pltpu.VMEM((b, GLA_DV, GLA_QK), jnp.float32),
                        pltpu.VMEM((b, 8, cw), jnp.float32),
                        pltpu.VMEM((LANES, LANES), jnp.float32),
                        pltpu.VMEM((MIX_TILE, MIX_TILE), jnp.int32),
                        pltpu.VMEM((MIX_TILE, MIX_TILE), bf16)],
        compiler_params=pltpu.CompilerParams(dimension_semantics=("arbitrary",),
                                             vmem_limit_bytes=VMEM_LIMIT),
        name="mixer",
    )(*args, *slabs)
    return (rows.reshape(b * s, d + LANES), meta.reshape(b * n_steps, 8, MIX_TILE), counts,
            wg16.reshape(n_exp, d, de), wu16.reshape(n_exp, d, de), wd16.reshape(n_exp, de, d))


def _sort_plan(meta, counts, n_tok):
    i32 = jnp.int32
    n_tiles = n_tok // MOE_TILE + N_CLASSES
    cls = meta[:, INFO_CLS, :].reshape(n_tok)
    rank = meta[:, INFO_RANK, :].reshape(n_tok)
    cnt = counts[:N_CLASSES, 0]
    tiles_per_cls = (cnt + MOE_TILE - 1) // MOE_TILE
    tile_end = jnp.cumsum(tiles_per_cls)
    n_used = tile_end[-1:]
    tstart = jnp.concatenate([tile_end - tiles_per_cls, n_used])
    tile_id = jnp.minimum(jnp.arange(n_tiles, dtype=i32), n_used - 1)
    tile_cls = jnp.sum((tile_id[:, None] >= tile_end[None, :]).astype(i32), axis=1)
    grp, pair = tile_cls // N_PAIRS, tile_cls % N_PAIRS
    e_a, e_b = grp * EXPERTS_PER_GROUP, grp * EXPERTS_PER_GROUP
    for pp in range(N_PAIRS):
        e_a = e_a + jnp.where(pair == pp, PAIR_A[pp], 0)
        e_b = e_b + jnp.where(pair == pp, PAIR_B[pp], 0)
    return dict(cls=cls, rank=rank, cnt=cnt, tstart=tstart, e_a=e_a, e_b=e_b, n_used=n_used, n_tiles=n_tiles)


def _experts(plan, rows, g_moe, wg, wu, wd):
    n_tiles = plan["n_tiles"]
    d, de = wg.shape[-2:]
    w_a = lambda shape: pl.BlockSpec(shape, lambda i, ea, eb, *_: (ea[i], 0, 0))
    w_b = lambda shape: pl.BlockSpec(shape, lambda i, ea, eb, *_: (eb[i], 0, 0))
    return pl.pallas_call(
        _expert_kernel,
        grid_spec=pltpu.PrefetchScalarGridSpec(
            num_scalar_prefetch=7, grid=(n_tiles,),
            in_specs=[pl.BlockSpec(memory_space=pl.ANY),
                      pl.BlockSpec((1, d), lambda i, *_: (0, 0)),
                      w_a((1, d, de)), w_a((1, d, de)), w_a((1, de, d)),
                      w_b((1, d, de)), w_b((1, d, de)), w_b((1, de, d))],
            out_specs=pl.BlockSpec((MOE_TILE, d), lambda i, *_: (i, 0)),
            scratch_shapes=[pltpu.VMEM((2, MOE_TILE, rows.shape[1]), jnp.float32),
                            pltpu.SemaphoreType.DMA((2,)),
                            pltpu.SMEM((n_tiles * MOE_TILE,), jnp.int32)]),
        out_shape=jax.ShapeDtypeStruct((n_tiles * MOE_TILE, d), jnp.float32),
        compiler_params=pltpu.CompilerParams(dimension_semantics=("arbitrary",),
                                             vmem_limit_bytes=VMEM_LIMIT),
        name="experts",
    )(plan["e_a"], plan["e_b"], plan["n_used"], plan["tstart"], plan["cnt"], plan["cls"], plan["rank"],
      rows, g_moe[None, :], wg, wu, wd, wg, wu, wd)


def _ple_final(plan, rows, p, y_sorted, g_ple, w_ple_gate, w_ple_proj, g_final):
    n_tok, dp = p.shape
    d = y_sorted.shape[1]
    bf16 = jnp.bfloat16
    const = lambda shape: pl.BlockSpec(shape, lambda i, *_: (0,) * len(shape))
    tile = lambda width: pl.BlockSpec((MOE_TILE, width), lambda i, *_: (i, 0))
    return pl.pallas_call(
        _ple_final_kernel,
        grid_spec=pltpu.PrefetchScalarGridSpec(
            num_scalar_prefetch=3, grid=(n_tok // MOE_TILE,),
            in_specs=[tile(d),
                      tile(dp),
                      pl.BlockSpec(memory_space=pl.ANY),
                      const((1, d)), const((d, d)), const((dp, d)), const((1, d))],
            out_specs=tile(d),
            scratch_shapes=[pltpu.VMEM((2, MOE_TILE, d), jnp.float32),
                            pltpu.SemaphoreType.DMA((2,))]),
        out_shape=jax.ShapeDtypeStruct((n_tok, d), jnp.float32),
        compiler_params=pltpu.CompilerParams(dimension_semantics=("arbitrary",),
                                             vmem_limit_bytes=VMEM_LIMIT),
        name="ple_final",
    )(plan["tstart"], plan["cls"], plan["rank"], rows, p, y_sorted,
      g_ple[None, :], w_ple_gate.astype(bf16), w_ple_proj.astype(bf16), g_final[None, :])


def kernel(x, p, g_mix, w_in, w_gla_gate, b_gla_gate, g_gla_out, w_conv, w_out, g_moe, w_group, b_group,
           w_router, b_router, w_exp_gate, w_exp_up, w_exp_down, g_ple, w_ple_gate, w_ple_proj, g_final):
    depth = w_in.shape[0]
    assert depth == 1, "the final norm is fused into the last (only) layer"
    b, s, d = x.shape
    n_tok = b * s
    assert s % MIX_TILE == 0 and n_tok % MOE_TILE == 0 and n_tok % PLACE_UNROLL == 0
    rows, meta, counts, wg16, wu16, wd16 = _mixer(
        x, g_mix[0], w_in[0], w_gla_gate[0], b_gla_gate[0], g_gla_out[0], w_conv[0], w_out[0], g_moe[0],
        w_group[0], b_group[0], w_router[0], b_router[0], w_exp_gate[0], w_exp_up[0], w_exp_down[0])
    plan = _sort_plan(meta, counts, n_tok)
    y_sorted = _experts(plan, rows, g_moe[0], wg16, wu16, wd16)
    out = _ple_final(plan, rows, p[0].reshape(n_tok, -1), y_sorted, g_ple[0], w_ple_gate[0], w_ple_proj[0],
                     g_final)
    return out.reshape(b, s, d)
```

```python
import functools

import jax
import jax.numpy as jnp
from jax import lax
from jax.experimental import pallas as pl
from jax.experimental.pallas import tpu as pltpu

EPS = 1e-6
GLA_HEADS = 4
GLA_DK = 64
GLA_DV = 128
GLA_QK = GLA_HEADS * GLA_DK
GLA_V = GLA_HEADS * GLA_DV
GLA_LOWRANK = 16
GLA_TAU = 16.0
CONV_K = 3
N_GROUPS = 4
EXPERTS_PER_GROUP = 4
N_EXPERTS = N_GROUPS * EXPERTS_PER_GROUP

LANES = 128
MIX_TILE = 256
MIX_LEVELS = 8
ROUTE_ROWS = 32
ROUTE_EXPERT_ROW0 = 8
MOE_TILE = 256
VMEM_LIMIT = 56 * 1024 * 1024

PAIR_A = (0, 2, 2, 0, 0, 1)
PAIR_B = (1, 1, 3, 3, 2, 3)
PAIR_OF_KEY = {1: 0, 6: 1, 11: 2, 3: 3, 2: 4, 7: 5}
N_PAIRS = len(PAIR_A)
N_CLASSES = N_GROUPS * N_PAIRS
INFO_CLS, INFO_RANK, INFO_WA, INFO_WB = 0, 1, 2, 3
PLACE_UNROLL = 8

_NT = (((1,), (1,)), ((), ()))
_DONE = object()


def _rms(x, g):
    return x * lax.rsqrt(jnp.mean(x * x, axis=-1, keepdims=True) + EPS) * g


def _dot(a, b):
    return jnp.dot(a, b, preferred_element_type=jnp.float32)


def _dot_nt(a, b):
    return lax.dot_general(a, b, _NT, preferred_element_type=jnp.float32)


def _split_bf16(a):
    hi = a.astype(jnp.bfloat16)
    return hi, (a - hi.astype(jnp.float32)).astype(jnp.bfloat16)


def _shift_rows(x, shift):
    return pltpu.roll(x, shift % x.shape[0], axis=0)


def _mixer_kernel(x_ref, gmix_ref, wqkvg_ref, wa_ref, wc3_ref, wgate_ref, bgate_ref, ggla_ref,
                  wconv_ref, wout_ref, gmoe_ref, wrt_ref, brt_ref, wge_ref, wue_ref, wde_ref,
                  rows_ref, meta_ref, counts_ref, wge16_ref, wue16_ref, wde16_ref,
                  st_ref, carry_ref, count_ref, level_ref, tril_ref):
    bf16 = jnp.bfloat16
    T = MIX_TILE

    wge16_ref[...] = wge_ref[...].astype(bf16)
    wue16_ref[...] = wue_ref[...].astype(bf16)
    wde16_ref[...] = wde_ref[...].astype(bf16)

    @pl.when(pl.program_id(0) == 0)
    def _():
        st_ref[...] = jnp.zeros_like(st_ref)
        carry_ref[...] = jnp.zeros_like(carry_ref)
        count_ref[...] = jnp.zeros_like(count_ref)
        tt = lax.broadcasted_iota(jnp.int32, (T, T), 0)
        ss = lax.broadcasted_iota(jnp.int32, (T, T), 1)
        txs = jnp.bitwise_xor(tt, ss)
        level = jnp.zeros((T, T), jnp.int32)
        for j in range(1, MIX_LEVELS):
            level = level + (txs >= (1 << j)).astype(jnp.int32)
        level_ref[...] = jnp.where(tt > ss, level, jnp.where(tt == ss, MIX_LEVELS, -1))
        tril_ref[...] = (ss <= tt).astype(bf16)

    tiles = [_mixer_tile(x_ref.at[pl.ds(b, 1)], gmix_ref, wqkvg_ref, wa_ref, wc3_ref, wgate_ref, bgate_ref,
                         ggla_ref, wconv_ref, wout_ref, gmoe_ref, wrt_ref, brt_ref,
                         rows_ref.at[b], meta_ref.at[b], counts_ref,
                         st_ref.at[b], carry_ref.at[b], count_ref, level_ref, tril_ref)
             for b in range(x_ref.shape[0])]
    while tiles:
        tiles = [t for t in tiles if next(t, _DONE) is not _DONE]


def _mixer_tile(x_ref, gmix_ref, wqkvg_ref, wa_ref, wc3_ref, wgate_ref, bgate_ref, ggla_ref,
                wconv_ref, wout_ref, gmoe_ref, wrt_ref, brt_ref,
                rows_ref, meta_ref, counts_ref,
                st_ref, carry_ref, count_ref, level_ref, tril_ref):
    f32, bf16 = jnp.float32, jnp.bfloat16
    T = MIX_TILE
    D = x_ref.shape[-1]

    x = x_ref[0]
    hb = _rms(x, gmix_ref[...]).astype(bf16)
    qkvg = _dot(hb, wqkvg_ref[...])
    q = qkvg[:, :GLA_QK] * (GLA_DK ** -0.5)
    k = qkvg[:, GLA_QK:2 * GLA_QK]
    v = qkvg[:, 2 * GLA_QK:2 * GLA_QK + GLA_V]
    g = qkvg[:, 2 * GLA_QK + GLA_V:]
    a_low = _dot(hb, wa_ref[...])
    a_hi, a_lo = _split_bf16(a_low)
    z = _dot(jnp.concatenate([a_hi, a_lo, a_hi], axis=1), wgate_ref[...]) + bgate_ref[...]
    la = (jnp.minimum(z, 0.0) - jnp.log(1.0 + jnp.exp(-jnp.abs(z)))) * (1.0 / GLA_TAU)
    yield

    row = lax.broadcasted_iota(jnp.int32, (T, GLA_QK), 0)

    def next_level(l, q_l, k_l, block):
        upper = ((row >> l) & 1) == 1
        below = _shift_rows(block, 1 << l)
        above = _shift_rows(block, -(1 << l))
        return (q_l * jnp.where(upper, below, 1.0), k_l * jnp.where(upper, 1.0, above),
                block * jnp.where(upper, below, above))

    decay = jnp.exp(la)
    H = T // 2
    assert GLA_DV == H
    half_level = level_ref[0:H, 0:H]
    lane_head_st = lax.broadcasted_iota(jnp.int32, (H, GLA_QK), 1) // GLA_DK

    def per_head_rows(a):
        return jnp.concatenate([jnp.where(lane_head_st == h, a, jnp.zeros_like(a)) for h in range(GLA_HEADS)],
                               axis=0)

    def head_blocks(p):
        return [p[:, h * H:(h + 1) * H] for h in range(GLA_HEADS)]

    diag0 = [jnp.zeros((H, H), f32) for _ in range(GLA_HEADS)]
    diag1 = [jnp.zeros((H, H), f32) for _ in range(GLA_HEADS)]

    def add_level(l, q_l, k_l):
        sel = half_level == l
        ql, kl = q_l.astype(bf16), k_l.astype(bf16)
        p0 = head_blocks(_dot_nt(ql[:H], per_head_rows(kl[:H])))
        p1 = head_blocks(_dot_nt(ql[H:], per_head_rows(kl[H:])))
        for h in range(GLA_HEADS):
            diag0[h] = jnp.where(sel, p0[h], diag0[h])
            diag1[h] = jnp.where(sel, p1[h], diag1[h])

    add_level(MIX_LEVELS, q, k)
    q_l, k_l, block = q * decay, k, decay
    for l in range(MIX_LEVELS - 1):
        add_level(l, q_l, k_l)
        q_l, k_l, block = next_level(l, q_l, k_l, block)
        if l % 2 == 1:
            yield
    low = head_blocks(_dot_nt(q_l[H:].astype(bf16), per_head_rows(k_l[:H].astype(bf16))))
    zero_block = jnp.zeros((H, H), f32)
    scores = [jnp.concatenate([jnp.concatenate([diag0[h], zero_block], axis=1),
                               jnp.concatenate([low[h], diag1[h]], axis=1)], axis=0) for h in range(GLA_HEADS)]

    yield

    q_in, k_out, tile_decay = next_level(MIX_LEVELS - 1, q_l, k_l, block)
    st = st_ref[...]
    o_state = _dot_nt(q_in.astype(bf16), per_head_rows(st.astype(bf16)))
    upd = _dot(v.T.astype(bf16), k_out.astype(bf16))
    new_st = st * tile_decay[:H]
    ggla = ggla_ref[...]
    y_heads = []
    for h in range(GLA_HEADS):
        v_h = v[:, h * GLA_DV:(h + 1) * GLA_DV]
        o = _dot(scores[h].astype(bf16), v_h.astype(bf16)) + o_state[:, h * GLA_DV:(h + 1) * GLA_DV]
        new_st = new_st + jnp.where(lane_head_st == h, upd[h * GLA_DV:(h + 1) * GLA_DV], 0.0)
        g_h = g[:, h * GLA_DV:(h + 1) * GLA_DV]
        y_heads.append(_rms(o, ggla) * (g_h * jax.nn.sigmoid(g_h)))
        if h % 2 == 1:
            yield
    st_ref[...] = new_st

    yield

    c3 = _dot(hb, wc3_ref[...])
    cw = c3.shape[1] // 3
    cb, cu = c3[:, :cw], c3[:, cw:2 * cw] * c3[:, 2 * cw:]
    crow = lax.broadcasted_iota(jnp.int32, (T, cw), 0)
    prev2, prev1 = carry_ref[0:1, :], carry_ref[1:2, :]
    m1 = jnp.where(crow == 0, prev1, _shift_rows(cu, 1))
    m2 = jnp.where(crow == 0, prev2, jnp.where(crow == 1, prev1, _shift_rows(cu, 2)))
    wconv = wconv_ref[...]
    y_conv = cb * (wconv[0:1, :] * m2 + wconv[1:2, :] * m1 + wconv[2:3, :] * cu)
    carry_ref[0:2, :] = cu[T - 2:, :]

    y = jnp.concatenate(y_heads + [y_conv], axis=1).astype(bf16)
    x1 = x + _dot(y, wout_ref[...])
    rows_ref[:, :D] = x1

    yield

    h2 = _rms(x1, gmoe_ref[...])
    h2_hi, h2_lo = _split_bf16(h2)
    part = _dot_nt(wrt_ref[...], h2_hi)
    logits = (part[:ROUTE_ROWS] + part[ROUTE_ROWS:] + _dot_nt(wrt_ref[:ROUTE_ROWS, :], h2_lo)) + brt_ref[...]
    gl = [logits[i:i + 1, :] for i in range(N_GROUPS)]
    gmax = functools.reduce(jnp.maximum, gl)
    gsum = functools.reduce(lambda a, b: a + b, [jnp.exp(t - gmax) for t in gl])
    p_grp = 1.0 / gsum
    g_sel = jnp.full_like(gmax, N_GROUPS - 1).astype(jnp.int32)
    for i in reversed(range(N_GROUPS - 1)):
        g_sel = jnp.where(gl[i] == gmax, i, g_sel)
    ig = []
    for j in range(EXPERTS_PER_GROUP):
        acc = jnp.zeros_like(gmax)
        for gi in range(N_GROUPS):
            r0 = ROUTE_EXPERT_ROW0 + gi * EXPERTS_PER_GROUP + j
            acc = acc + jnp.where(g_sel == gi, logits[r0:r0 + 1, :], 0.0)
        ig.append(acc)

    def first_argmax(vals):
        m = functools.reduce(jnp.maximum, vals)
        idx = jnp.full_like(m, len(vals) - 1).astype(jnp.int32)
        for i in reversed(range(len(vals) - 1)):
            idx = jnp.where(vals[i] == m, i, idx)
        return m, idx

    m1_, i1 = first_argmax(ig)
    m2_, i2 = first_argmax([jnp.where(i1 == j, -jnp.inf, ig[j]) for j in range(EXPERTS_PER_GROUP)])
    e21 = jnp.exp(m2_ - m1_)
    w1 = p_grp / (1.0 + e21)
    w2 = p_grp * e21 / (1.0 + e21)
    key = jnp.minimum(i1, i2) * EXPERTS_PER_GROUP + jnp.maximum(i1, i2)
    pair = jnp.zeros_like(key)
    a_loc = jnp.zeros_like(key)
    for kk, pp in PAIR_OF_KEY.items():
        pair = jnp.where(key == kk, pp, pair)
        a_loc = jnp.where(key == kk, PAIR_A[pp], a_loc)
    w_a = jnp.where(i1 == a_loc, w1, w2)
    w_b = jnp.where(i1 == a_loc, w2, w1)
    cls = g_sel * N_PAIRS + pair
    rr = lax.broadcasted_iota(jnp.int32, (LANES, T), 0)
    rec_t = (jnp.where(rr == INFO_WA, jnp.broadcast_to(w_a, (LANES, T)), 0.0)
             + jnp.where(rr == INFO_WB, jnp.broadcast_to(w_b, (LANES, T)), 0.0))
    rows_ref[:, D:] = rec_t.T

    onehot = (rr == jnp.broadcast_to(cls, (LANES, T))).astype(f32)
    count = count_ref[:, 0:1]
    before = _dot_nt(onehot.astype(bf16), tril_ref[...]) - onehot + count
    rank = jnp.sum(onehot * before, axis=0, keepdims=True).astype(jnp.int32)
    r8 = lax.broadcasted_iota(jnp.int32, (8, T), 0)
    meta_ref[0] = jnp.where(r8 == INFO_CLS, jnp.broadcast_to(cls, (8, T)),
                            jnp.where(r8 == INFO_RANK, jnp.broadcast_to(rank, (8, T)), 0))
    new_count = jnp.broadcast_to(count + jnp.sum(onehot, axis=1, keepdims=True), count_ref.shape)
    count_ref[...] = new_count
    counts_ref[...] = new_count.astype(jnp.int32)


class _RowGather:
    def __init__(self, index_of, src_hbm, buf, sems):
        self.index_of, self.src_hbm, self.buf, self.sems = index_of, src_hbm, buf, sems

    def start(self, tile, slot, part=0, n_parts=1):
        per_part = MOE_TILE // n_parts
        for r in range(part * per_part, (part + 1) * per_part):
            pltpu.make_async_copy(self.src_hbm.at[pl.ds(self.index_of(tile * MOE_TILE + r), 1), :],
                                  self.buf.at[slot, pl.ds(r, 1), :], self.sems.at[slot]).start()

    def wait(self, slot):
        pltpu.make_async_copy(self.src_hbm.at[pl.ds(0, MOE_TILE), :], self.buf.at[slot], self.sems.at[slot]).wait()


def _sorted_slot(tstart_ref, cls_ref, rank_ref, t):
    return tstart_ref[cls_ref[t]] * MOE_TILE + rank_ref[t]


def _expert_kernel(ea_ref, eb_ref, nused_ref, tstart_ref, cnt_ref, cls_ref, rank_ref,
                   rows_hbm, gmoe_ref, wga_ref, wua_ref, wda_ref, wgb_ref, wub_ref, wdb_ref,
                   y_ref, buf, sems, src_ref, h2_ref):
    bf16 = jnp.bfloat16
    D = y_ref.shape[-1]
    n_tok = cls_ref.shape[0]
    step, n_used = pl.program_id(0), nused_ref[0]
    gather = _RowGather(lambda i: src_ref[i], rows_hbm, buf, sems)

    @pl.when(step == 0)
    def _():
        for c in range(N_CLASSES):
            base = tstart_ref[c] * MOE_TILE

            def pad(r, carry, base=base):
                src_ref[base + r] = 0
                return carry

            lax.fori_loop(cnt_ref[c], (tstart_ref[c + 1] - tstart_ref[c]) * MOE_TILE, pad, 0)

        def place(it, carry):
            for u in range(PLACE_UNROLL):
                t = it * PLACE_UNROLL + u
                src_ref[_sorted_slot(tstart_ref, cls_ref, rank_ref, t)] = t
            return carry

        lax.fori_loop(0, n_tok // PLACE_UNROLL, place, 0)
        gather.start(0, 0)

    slot = step % 2

    def expert(wg_ref, wu_ref, wd_ref):
        h2 = h2_ref[...]
        gate = _dot(h2, wg_ref[0])
        hid = (gate * jax.nn.sigmoid(gate)) * _dot(h2, wu_ref[0])
        return _dot(hid.astype(bf16), wd_ref[0])

    @pl.when(step + 1 < n_used)
    def _():
        gather.start(step + 1, 1 - slot, part=0, n_parts=2)

    @pl.when(step < n_used)
    def _():
        gather.wait(slot)
        h2_ref[...] = _rms(buf[slot, :, :D], gmoe_ref[...]).astype(bf16)
        y_ref[...] = buf[slot, :, D + INFO_WA:D + INFO_WA + 1] * expert(wga_ref, wua_ref, wda_ref)

    @pl.when(step + 1 < n_used)
    def _():
        gather.start(step + 1, 1 - slot, part=1, n_parts=2)

    @pl.when(step < n_used)
    def _():
        y_ref[...] += buf[slot, :, D + INFO_WB:D + INFO_WB + 1] * expert(wgb_ref, wub_ref, wdb_ref)

    @pl.when(step >= n_used)
    def _():
        y_ref[...] = jnp.zeros_like(y_ref)


def _ple_final_kernel(tstart_ref, cls_ref, rank_ref, x1_ref, p_ref, y_hbm, gple_ref, wpg_ref, wpp_ref, gfin_ref,
                      out_ref, buf, sems):
    bf16 = jnp.bfloat16
    step, n_steps = pl.program_id(0), pl.num_programs(0)
    gather = _RowGather(lambda t: _sorted_slot(tstart_ref, cls_ref, rank_ref, t), y_hbm, buf, sems)

    @pl.when(step == 0)
    def _():
        gather.start(0, 0)

    slot = step % 2

    @pl.when(step + 1 < n_steps)
    def _():
        gather.start(step + 1, 1 - slot)

    gather.wait(slot)
    x2 = x1_ref[...] + buf[slot]
    gate_p = jax.nn.sigmoid(_dot(_rms(x2, gple_ref[...]).astype(bf16), wpg_ref[...]))
    x3 = x2 + gate_p * _dot(p_ref[...].astype(bf16), wpp_ref[...])
    out_ref[...] = _rms(x3, gfin_ref[...])


def _const_spec(shape):
    return pl.BlockSpec(shape, lambda *_: (0,) * len(shape))


def _mixer(x, g_mix, w_in, w_gla_gate, b_gla_gate, g_gla_out, w_conv, w_out, g_moe,
           w_group, b_group, w_router, b_router, w_exp_gate, w_exp_up, w_exp_down):
    b, s, d = x.shape
    n_steps = s // MIX_TILE
    n_exp, _, de = w_exp_gate.shape
    assert (n_exp * d) % (16 * n_steps) == 0 and (n_exp * de) % (16 * n_steps) == 0
    up_rows, down_rows = n_exp * d // n_steps, n_exp * de // n_steps
    bf16 = jnp.bfloat16
    n_qkvg = 2 * GLA_QK + 2 * GLA_V
    w_qkvg = w_in[:, :n_qkvg].astype(bf16)
    w_a = w_in[:, n_qkvg:n_qkvg + GLA_LOWRANK].astype(bf16)
    w_c3 = w_in[:, n_qkvg + GLA_LOWRANK:].astype(bf16)
    cw = w_c3.shape[1] // 3
    wrt = jnp.zeros((ROUTE_ROWS, d), jnp.float32)
    wrt = wrt.at[:N_GROUPS].set(w_group.T).at[ROUTE_EXPERT_ROW0:ROUTE_EXPERT_ROW0 + N_EXPERTS].set(w_router.T)
    brt = jnp.zeros((ROUTE_ROWS, 1), jnp.float32)
    brt = brt.at[:N_GROUPS, 0].set(b_group).at[ROUTE_EXPERT_ROW0:ROUTE_EXPERT_ROW0 + N_EXPERTS, 0].set(b_router)
    wrt_split = jnp.concatenate(_split_bf16(wrt), axis=0)
    gate_hi, gate_lo = _split_bf16(w_gla_gate)
    w_gate_split = jnp.concatenate([gate_hi, gate_hi, gate_lo], axis=0)
    args = (x, g_mix[None, :], w_qkvg, w_a, w_c3, w_gate_split, b_gla_gate[None, :], g_gla_out[None, :],
            w_conv, w_out.astype(bf16), g_moe[None, :], wrt_split, brt)
    slabs = (w_exp_gate.reshape(n_exp * d, de), w_exp_up.reshape(n_exp * d, de), w_exp_down.reshape(n_exp * de, d))
    slab_specs = [pl.BlockSpec((rows_, width), lambda j: (j, 0))
                  for rows_, width in ((up_rows, de), (up_rows, de), (down_rows, d))]
    in_specs = [pl.BlockSpec((b, MIX_TILE, d), lambda j: (0, j, 0))]
    in_specs += [_const_spec(a.shape) for a in args[1:]] + slab_specs
    rows, meta, counts, wg16, wu16, wd16 = pl.pallas_call(
        _mixer_kernel,
        grid=(n_steps,),
        in_specs=in_specs,
        out_specs=[pl.BlockSpec((b, MIX_TILE, d + LANES), lambda j: (0, j, 0)),
                   pl.BlockSpec((b, 1, 8, MIX_TILE), lambda j: (0, j, 0, 0)),
                   _const_spec((LANES, LANES))] + slab_specs,
        out_shape=[jax.ShapeDtypeStruct((b, s, d + LANES), jnp.float32),
                   jax.ShapeDtypeStruct((b, n_steps, 8, MIX_TILE), jnp.int32),
                   jax.ShapeDtypeStruct((LANES, LANES), jnp.int32)]
                  + [jax.ShapeDtypeStruct(w.shape, bf16) for w in slabs],
        scratch_shapes=[pltpu.VMEM((b, GLA_DV, GLA_QK), jnp.float32),
                        pltpu.VMEM((b, 8, cw), jnp.float32),
                        pltpu.VMEM((LANES, LANES), jnp.float32),
                        pltpu.VMEM((MIX_TILE, MIX_TILE), jnp.int32),
                        pltpu.VMEM((MIX_TILE, MIX_TILE), bf16)],
        compiler_params=pltpu.CompilerParams(dimension_semantics=("arbitrary",),
                                             vmem_limit_bytes=VMEM_LIMIT),
        name="mixer",
    )(*args, *slabs)
    return (rows.reshape(b * s, d + LANES), meta.reshape(b * n_steps, 8, MIX_TILE), counts,
            wg16.reshape(n_exp, d, de), wu16.reshape(n_exp, d, de), wd16.reshape(n_exp, de, d))


def _sort_plan(meta, counts, n_tok):
    i32 = jnp.int32
    n_tiles = n_tok // MOE_TILE + N_CLASSES
    cls = meta[:, INFO_CLS, :].reshape(n_tok)
    rank = meta[:, INFO_RANK, :].reshape(n_tok)
    cnt = counts[:N_CLASSES, 0]
    tiles_per_cls = (cnt + MOE_TILE - 1) // MOE_TILE
    tile_end = jnp.cumsum(tiles_per_cls)
    n_used = tile_end[-1:]
    tstart = jnp.concatenate([tile_end - tiles_per_cls, n_used])
    tile_id = jnp.minimum(jnp.arange(n_tiles, dtype=i32), n_used - 1)
    tile_cls = jnp.sum((tile_id[:, None] >= tile_end[None, :]).astype(i32), axis=1)
    grp, pair = tile_cls // N_PAIRS, tile_cls % N_PAIRS
    e_a, e_b = grp * EXPERTS_PER_GROUP, grp * EXPERTS_PER_GROUP
    for pp in range(N_PAIRS):
        e_a = e_a + jnp.where(pair == pp, PAIR_A[pp], 0)
        e_b = e_b + jnp.where(pair == pp, PAIR_B[pp], 0)
    return dict(cls=cls, rank=rank, cnt=cnt, tstart=tstart, e_a=e_a, e_b=e_b, n_used=n_used, n_tiles=n_tiles)


def _experts(plan, rows, g_moe, wg, wu, wd):
    n_tiles = plan["n_tiles"]
    d, de = wg.shape[-2:]
    w_a = lambda shape: pl.BlockSpec(shape, lambda i, ea, eb, *_: (ea[i], 0, 0))
    w_b = lambda shape: pl.BlockSpec(shape, lambda i, ea, eb, *_: (eb[i], 0, 0))
    return pl.pallas_call(
        _expert_kernel,
        grid_spec=pltpu.PrefetchScalarGridSpec(
            num_scalar_prefetch=7, grid=(n_tiles,),
            in_specs=[pl.BlockSpec(memory_space=pl.ANY),
                      pl.BlockSpec((1, d), lambda i, *_: (0, 0)),
                      w_a((1, d, de)), w_a((1, d, de)), w_a((1, de, d)),
                      w_b((1, d, de)), w_b((1, d, de)), w_b((1, de, d))],
            out_specs=pl.BlockSpec((MOE_TILE, d), lambda i, *_: (i, 0)),
            scratch_shapes=[pltpu.VMEM((2, MOE_TILE, rows.shape[1]), jnp.float32),
                            pltpu.SemaphoreType.DMA((2,)),
                            pltpu.SMEM((n_tiles * MOE_TILE,), jnp.int32),
                            pltpu.VMEM((MOE_TILE, d), jnp.bfloat16)]),
        out_shape=jax.ShapeDtypeStruct((n_tiles * MOE_TILE, d), jnp.float32),
        compiler_params=pltpu.CompilerParams(dimension_semantics=("arbitrary",),
                                             vmem_limit_bytes=VMEM_LIMIT),
        name="experts",
    )(plan["e_a"], plan["e_b"], plan["n_used"], plan["tstart"], plan["cnt"], plan["cls"], plan["rank"],
      rows, g_moe[None, :], wg, wu, wd, wg, wu, wd)


def _ple_final(plan, rows, p, y_sorted, g_ple, w_ple_gate, w_ple_proj, g_final):
    n_tok, dp = p.shape
    d = y_sorted.shape[1]
    bf16 = jnp.bfloat16
    const = lambda shape: pl.BlockSpec(shape, lambda i, *_: (0,) * len(shape))
    tile = lambda width: pl.BlockSpec((MOE_TILE, width), lambda i, *_: (i, 0))
    return pl.pallas_call(
        _ple_final_kernel,
        grid_spec=pltpu.PrefetchScalarGridSpec(
            num_scalar_prefetch=3, grid=(n_tok // MOE_TILE,),
            in_specs=[tile(d),
                      tile(dp),
                      pl.BlockSpec(memory_space=pl.ANY),
                      const((1, d)), const((d, d)), const((dp, d)), const((1, d))],
            out_specs=tile(d),
            scratch_shapes=[pltpu.VMEM((2, MOE_TILE, d), jnp.float32),
                            pltpu.SemaphoreType.DMA((2,))]),
        out_shape=jax.ShapeDtypeStruct((n_tok, d), jnp.float32),
        compiler_params=pltpu.CompilerParams(dimension_semantics=("arbitrary",),
                                             vmem_limit_bytes=VMEM_LIMIT),
        name="ple_final",
    )(plan["tstart"], plan["cls"], plan["rank"], rows, p, y_sorted,
      g_ple[None, :], w_ple_gate.astype(bf16), w_ple_proj.astype(bf16), g_final[None, :])


def kernel(x, p, g_mix, w_in, w_gla_gate, b_gla_gate, g_gla_out, w_conv, w_out, g_moe, w_group, b_group,
           w_router, b_router, w_exp_gate, w_exp_up, w_exp_down, g_ple, w_ple_gate, w_ple_proj, g_final):
    depth = w_in.shape[0]
    assert depth == 1, "the final norm is fused into the last (only) layer"
    b, s, d = x.shape
    n_tok = b * s
    assert s % MIX_TILE == 0 and n_tok % MOE_TILE == 0 and n_tok % PLACE_UNROLL == 0
    rows, meta, counts, wg16, wu16, wd16 = _mixer(
        x, g_mix[0], w_in[0], w_gla_gate[0], b_gla_gate[0], g_gla_out[0], w_conv[0], w_out[0], g_moe[0],
        w_group[0], b_group[0], w_router[0], b_router[0], w_exp_gate[0], w_exp_up[0], w_exp_down[0])
    plan = _sort_plan(meta, counts, n_tok)
    y_sorted = _experts(plan, rows, g_moe[0], wg16, wu16, wd16)
    out = _ple_final(plan, rows, p[0].reshape(n_tok, -1), y_sorted, g_ple[0], w_ple_gate[0], w_ple_proj[0],
                     g_final)
    return out.reshape(b, s, d)
```

```python
import functools

import jax
import jax.numpy as jnp
from jax import lax
from jax.experimental import pallas as pl
from jax.experimental.pallas import tpu as pltpu

EPS = 1e-6
GLA_HEADS = 4
GLA_DK = 64
GLA_DV = 128
GLA_QK = GLA_HEADS * GLA_DK
GLA_V = GLA_HEADS * GLA_DV
GLA_LOWRANK = 16
GLA_TAU = 16.0
CONV_K = 3
N_GROUPS = 4
EXPERTS_PER_GROUP = 4
N_EXPERTS = N_GROUPS * EXPERTS_PER_GROUP

LANES = 128
MIX_TILE = 256
MIX_LEVELS = 8
ROUTE_ROWS = 32
ROUTE_EXPERT_ROW0 = 8
MOE_TILE = 256
VMEM_LIMIT = 56 * 1024 * 1024

PAIR_A = (0, 2, 2, 0, 0, 1)
PAIR_B = (1, 1, 3, 3, 2, 3)
PAIR_OF_KEY = {1: 0, 6: 1, 11: 2, 3: 3, 2: 4, 7: 5}
N_PAIRS = len(PAIR_A)
N_CLASSES = N_GROUPS * N_PAIRS
INFO_CLS, INFO_RANK, INFO_WA, INFO_WB = 0, 1, 2, 3
PLACE_UNROLL = 8

_NT = (((1,), (1,)), ((), ()))
_DONE = object()


def _rms(x, g):
    return x * lax.rsqrt(jnp.mean(x * x, axis=-1, keepdims=True) + EPS) * g


def _dot(a, b):
    return jnp.dot(a, b, preferred_element_type=jnp.float32)


def _dot_nt(a, b):
    return lax.dot_general(a, b, _NT, preferred_element_type=jnp.float32)


def _split_bf16(a):
    hi = a.astype(jnp.bfloat16)
    return hi, (a - hi.astype(jnp.float32)).astype(jnp.bfloat16)


def _shift_rows(x, shift):
    return pltpu.roll(x, shift % x.shape[0], axis=0)


def _mixer_kernel(x_ref, gmix_ref, wqkvg_ref, wa_ref, wc3_ref, wgate_ref, bgate_ref, ggla_ref,
                  wconv_ref, wout_ref, gmoe_ref, wrt_ref, brt_ref, wge_ref, wue_ref, wde_ref,
                  rows_ref, meta_ref, counts_ref, wge16_ref, wue16_ref, wde16_ref,
                  st_ref, carry_ref, count_ref, level_ref, tril_ref):
    bf16 = jnp.bfloat16
    T = MIX_TILE

    wge16_ref[...] = wge_ref[...].astype(bf16)
    wue16_ref[...] = wue_ref[...].astype(bf16)
    wde16_ref[...] = wde_ref[...].astype(bf16)

    @pl.when(pl.program_id(0) == 0)
    def _():
        st_ref[...] = jnp.zeros_like(st_ref)
        carry_ref[...] = jnp.zeros_like(carry_ref)
        count_ref[...] = jnp.zeros_like(count_ref)
        tt = lax.broadcasted_iota(jnp.int32, (T, T), 0)
        ss = lax.broadcasted_iota(jnp.int32, (T, T), 1)
        txs = jnp.bitwise_xor(tt, ss)
        level = jnp.zeros((T, T), jnp.int32)
        for j in range(1, MIX_LEVELS):
            level = level + (txs >= (1 << j)).astype(jnp.int32)
        level_ref[...] = jnp.where(tt > ss, level, jnp.where(tt == ss, MIX_LEVELS, -1))
        tril_ref[...] = (ss <= tt).astype(bf16)

    tiles = [_mixer_tile(x_ref.at[pl.ds(b, 1)], gmix_ref, wqkvg_ref, wa_ref, wc3_ref, wgate_ref, bgate_ref,
                         ggla_ref, wconv_ref, wout_ref, gmoe_ref, wrt_ref, brt_ref,
                         rows_ref.at[b], meta_ref.at[b], counts_ref,
                         st_ref.at[b], carry_ref.at[b], count_ref, level_ref, tril_ref)
             for b in range(x_ref.shape[0])]
    while tiles:
        tiles = [t for t in tiles if next(t, _DONE) is not _DONE]


def _mixer_tile(x_ref, gmix_ref, wqkvg_ref, wa_ref, wc3_ref, wgate_ref, bgate_ref, ggla_ref,
                wconv_ref, wout_ref, gmoe_ref, wrt_ref, brt_ref,
                rows_ref, meta_ref, counts_ref,
                st_ref, carry_ref, count_ref, level_ref, tril_ref):
    f32, bf16 = jnp.float32, jnp.bfloat16
    T = MIX_TILE
    D = x_ref.shape[-1]

    x = x_ref[0]
    hb = _rms(x, gmix_ref[...]).astype(bf16)
    qkvg = _dot(hb, wqkvg_ref[...])
    q = qkvg[:, :GLA_QK] * (GLA_DK ** -0.5)
    k = qkvg[:, GLA_QK:2 * GLA_QK]
    v = qkvg[:, 2 * GLA_QK:2 * GLA_QK + GLA_V]
    g = qkvg[:, 2 * GLA_QK + GLA_V:]
    a_low = _dot(hb, wa_ref[...])
    a_hi, a_lo = _split_bf16(a_low)
    z = _dot(jnp.concatenate([a_hi, a_lo, a_hi], axis=1), wgate_ref[...]) + bgate_ref[...]
    la = (jnp.minimum(z, 0.0) - jnp.log(1.0 + jnp.exp(-jnp.abs(z)))) * (1.0 / GLA_TAU)
    yield

    row = lax.broadcasted_iota(jnp.int32, (T, GLA_QK), 0)

    def next_level(l, q_l, k_l, block):
        upper = ((row >> l) & 1) == 1
        below = _shift_rows(block, 1 << l)
        above = _shift_rows(block, -(1 << l))
        return (q_l * jnp.where(upper, below, 1.0), k_l * jnp.where(upper, 1.0, above),
                block * jnp.where(upper, below, above))

    decay = jnp.exp(la)
    H = T // 2
    assert GLA_DV == H
    half_level = level_ref[0:H, 0:H]
    lane_head_st = lax.broadcasted_iota(jnp.int32, (H, GLA_QK), 1) // GLA_DK

    def per_head_rows(a):
        return jnp.concatenate([jnp.where(lane_head_st == h, a, jnp.zeros_like(a)) for h in range(GLA_HEADS)],
                               axis=0)

    def head_blocks(p):
        return [p[:, h * H:(h + 1) * H] for h in range(GLA_HEADS)]

    diag0 = [jnp.zeros((H, H), f32) for _ in range(GLA_HEADS)]
    diag1 = [jnp.zeros((H, H), f32) for _ in range(GLA_HEADS)]

    def add_level(l, q_l, k_l):
        sel = half_level == l
        ql, kl = q_l.astype(bf16), k_l.astype(bf16)
        p0 = head_blocks(_dot_nt(ql[:H], per_head_rows(kl[:H])))
        p1 = head_blocks(_dot_nt(ql[H:], per_head_rows(kl[H:])))
        for h in range(GLA_HEADS):
            diag0[h] = jnp.where(sel, p0[h], diag0[h])
            diag1[h] = jnp.where(sel, p1[h], diag1[h])

    add_level(MIX_LEVELS, q, k)
    q_l, k_l, block = q * decay, k, decay
    for l in range(MIX_LEVELS - 1):
        add_level(l, q_l, k_l)
        q_l, k_l, block = next_level(l, q_l, k_l, block)
        if l % 2 == 1:
            yield
    low = head_blocks(_dot_nt(q_l[H:].astype(bf16), per_head_rows(k_l[:H].astype(bf16))))
    zero_block = jnp.zeros((H, H), f32)
    scores = [jnp.concatenate([jnp.concatenate([diag0[h], zero_block], axis=1),
                               jnp.concatenate([low[h], diag1[h]], axis=1)], axis=0) for h in range(GLA_HEADS)]

    yield

    q_in, k_out, tile_decay = next_level(MIX_LEVELS - 1, q_l, k_l, block)
    st = st_ref[...]
    o_state = _dot_nt(q_in.astype(bf16), per_head_rows(st.astype(bf16)))
    upd = _dot(v.T.astype(bf16), k_out.astype(bf16))
    new_st = st * tile_decay[:H]
    ggla = ggla_ref[...]
    y_heads = []
    for h in range(GLA_HEADS):
        v_h = v[:, h * GLA_DV:(h + 1) * GLA_DV]
        o = _dot(scores[h].astype(bf16), v_h.astype(bf16)) + o_state[:, h * GLA_DV:(h + 1) * GLA_DV]
        new_st = new_st + jnp.where(lane_head_st == h, upd[h * GLA_DV:(h + 1) * GLA_DV], 0.0)
        g_h = g[:, h * GLA_DV:(h + 1) * GLA_DV]
        y_heads.append(_rms(o, ggla) * (g_h * jax.nn.sigmoid(g_h)))
        if h % 2 == 1:
            yield
    st_ref[...] = new_st

    yield

    c3 = _dot(hb, wc3_ref[...])
    cw = c3.shape[1] // 3
    cb, cu = c3[:, :cw], c3[:, cw:2 * cw] * c3[:, 2 * cw:]
    crow = lax.broadcasted_iota(jnp.int32, (T, cw), 0)
    prev2, prev1 = carry_ref[0:1, :], carry_ref[1:2, :]
    m1 = jnp.where(crow == 0, prev1, _shift_rows(cu, 1))
    m2 = jnp.where(crow == 0, prev2, jnp.where(crow == 1, prev1, _shift_rows(cu, 2)))
    wconv = wconv_ref[...]
    y_conv = cb * (wconv[0:1, :] * m2 + wconv[1:2, :] * m1 + wconv[2:3, :] * cu)
    carry_ref[0:2, :] = cu[T - 2:, :]

    y = jnp.concatenate(y_heads + [y_conv], axis=1).astype(bf16)
    x1 = x + _dot(y, wout_ref[...])
    rows_ref[:, :D] = x1

    yield

    h2 = _rms(x1, gmoe_ref[...])
    h2_hi, h2_lo = _split_bf16(h2)
    part = _dot_nt(wrt_ref[...], h2_hi)
    logits = (part[:ROUTE_ROWS] + part[ROUTE_ROWS:] + _dot_nt(wrt_ref[:ROUTE_ROWS, :], h2_lo)) + brt_ref[...]
    gl = [logits[i:i + 1, :] for i in range(N_GROUPS)]
    gmax = functools.reduce(jnp.maximum, gl)
    gsum = functools.reduce(lambda a, b: a + b, [jnp.exp(t - gmax) for t in gl])
    p_grp = 1.0 / gsum
    g_sel = jnp.full_like(gmax, N_GROUPS - 1).astype(jnp.int32)
    for i in reversed(range(N_GROUPS - 1)):
        g_sel = jnp.where(gl[i] == gmax, i, g_sel)
    ig = []
    for j in range(EXPERTS_PER_GROUP):
        acc = jnp.zeros_like(gmax)
        for gi in range(N_GROUPS):
            r0 = ROUTE_EXPERT_ROW0 + gi * EXPERTS_PER_GROUP + j
            acc = acc + jnp.where(g_sel == gi, logits[r0:r0 + 1, :], 0.0)
        ig.append(acc)

    def first_argmax(vals):
        m = functools.reduce(jnp.maximum, vals)
        idx = jnp.full_like(m, len(vals) - 1).astype(jnp.int32)
        for i in reversed(range(len(vals) - 1)):
            idx = jnp.where(vals[i] == m, i, idx)
        return m, idx

    m1_, i1 = first_argmax(ig)
    m2_, i2 = first_argmax([jnp.where(i1 == j, -jnp.inf, ig[j]) for j in range(EXPERTS_PER_GROUP)])
    e21 = jnp.exp(m2_ - m1_)
    w1 = p_grp / (1.0 + e21)
    w2 = p_grp * e21 / (1.0 + e21)
    key = jnp.minimum(i1, i2) * EXPERTS_PER_GROUP + jnp.maximum(i1, i2)
    pair = jnp.zeros_like(key)
    a_loc = jnp.zeros_like(key)
    for kk, pp in PAIR_OF_KEY.items():
        pair = jnp.where(key == kk, pp, pair)
        a_loc = jnp.where(key == kk, PAIR_A[pp], a_loc)
    w_a = jnp.where(i1 == a_loc, w1, w2)
    w_b = jnp.where(i1 == a_loc, w2, w1)
    cls = g_sel * N_PAIRS + pair
    rr = lax.broadcasted_iota(jnp.int32, (LANES, T), 0)
    rec_t = (jnp.where(rr == INFO_WA, jnp.broadcast_to(w_a, (LANES, T)), 0.0)
             + jnp.where(rr == INFO_WB, jnp.broadcast_to(w_b, (LANES, T)), 0.0))
    rows_ref[:, D:] = rec_t.T

    onehot = (rr == jnp.broadcast_to(cls, (LANES, T))).astype(f32)
    count = count_ref[:, 0:1]
    before = _dot_nt(onehot.astype(bf16), tril_ref[...]) - onehot + count
    rank = jnp.sum(onehot * before, axis=0, keepdims=True).astype(jnp.int32)
    r8 = lax.broadcasted_iota(jnp.int32, (8, T), 0)
    meta_ref[0] = jnp.where(r8 == INFO_CLS, jnp.broadcast_to(cls, (8, T)),
                            jnp.where(r8 == INFO_RANK, jnp.broadcast_to(rank, (8, T)), 0))
    new_count = jnp.broadcast_to(count + jnp.sum(onehot, axis=1, keepdims=True), count_ref.shape)
    count_ref[...] = new_count
    counts_ref[...] = new_count.astype(jnp.int32)


class _RowGather:
    def __init__(self, index_of, src_hbm, buf, sems):
        self.index_of, self.src_hbm, self.buf, self.sems = index_of, src_hbm, buf, sems

    def start(self, tile, slot):
        for r in range(MOE_TILE):
            pltpu.make_async_copy(self.src_hbm.at[pl.ds(self.index_of(tile * MOE_TILE + r), 1), :],
                                  self.buf.at[slot, pl.ds(r, 1), :], self.sems.at[slot]).start()

    def wait(self, slot):
        pltpu.make_async_copy(self.src_hbm.at[pl.ds(0, MOE_TILE), :], self.buf.at[slot], self.sems.at[slot]).wait()


def _expert_kernel(ea_ref, eb_ref, nused_ref, tstart_ref, cnt_ref, slot_ref,
                   rows_hbm, gmoe_ref, wga_ref, wua_ref, wda_ref, wgb_ref, wub_ref, wdb_ref,
                   y_ref, buf, sems, src_ref):
    bf16 = jnp.bfloat16
    D = y_ref.shape[-1]
    n_tok = slot_ref.shape[0]
    step, n_used = pl.program_id(0), nused_ref[0]
    gather = _RowGather(lambda i: src_ref[i], rows_hbm, buf, sems)

    @pl.when(step == 0)
    def _():
        for c in range(N_CLASSES):
            base = tstart_ref[c] * MOE_TILE

            def pad(r, carry, base=base):
                src_ref[base + r] = 0
                return carry

            lax.fori_loop(cnt_ref[c], (tstart_ref[c + 1] - tstart_ref[c]) * MOE_TILE, pad, 0)

        def place(it, carry):
            for u in range(PLACE_UNROLL):
                t = it * PLACE_UNROLL + u
                src_ref[slot_ref[t]] = t
            return carry

        lax.fori_loop(0, n_tok // PLACE_UNROLL, place, 0)
        gather.start(0, 0)

    slot = step % 2

    @pl.when(step + 1 < n_used)
    def _():
        gather.start(step + 1, 1 - slot)

    @pl.when(step < n_used)
    def _():
        gather.wait(slot)
        rows = buf[slot]
        rec = rows[:, D:]
        h2 = _rms(rows[:, :D], gmoe_ref[...]).astype(bf16)

        def expert(wg_ref, wu_ref, wd_ref):
            gate = _dot(h2, wg_ref[0])
            hid = (gate * jax.nn.sigmoid(gate)) * _dot(h2, wu_ref[0])
            return _dot(hid.astype(bf16), wd_ref[0])

        y = rec[:, INFO_WA:INFO_WA + 1] * expert(wga_ref, wua_ref, wda_ref)
        y_ref[...] = y + rec[:, INFO_WB:INFO_WB + 1] * expert(wgb_ref, wub_ref, wdb_ref)

    @pl.when(step >= n_used)
    def _():
        y_ref[...] = jnp.zeros_like(y_ref)


def _ple_final_kernel(slot_ref, x1_ref, p_ref, y_hbm, gple_ref, wpg_ref, wpp_ref, gfin_ref,
                      out_ref, buf, sems):
    bf16 = jnp.bfloat16
    step, n_steps = pl.program_id(0), pl.num_programs(0)
    gather = _RowGather(lambda t: slot_ref[t], y_hbm, buf, sems)

    @pl.when(step == 0)
    def _():
        gather.start(0, 0)

    slot = step % 2

    @pl.when(step + 1 < n_steps)
    def _():
        gather.start(step + 1, 1 - slot)

    gather.wait(slot)
    x2 = x1_ref[...] + buf[slot]
    gate_p = jax.nn.sigmoid(_dot(_rms(x2, gple_ref[...]).astype(bf16), wpg_ref[...]))
    x3 = x2 + gate_p * _dot(p_ref[...].astype(bf16), wpp_ref[...])
    out_ref[...] = _rms(x3, gfin_ref[...])


def _const_spec(shape):
    return pl.BlockSpec(shape, lambda *_: (0,) * len(shape))


def _mixer(x, g_mix, w_in, w_gla_gate, b_gla_gate, g_gla_out, w_conv, w_out, g_moe,
           w_group, b_group, w_router, b_router, w_exp_gate, w_exp_up, w_exp_down):
    b, s, d = x.shape
    n_steps = s // MIX_TILE
    n_exp, _, de = w_exp_gate.shape
    assert (n_exp * d) % (16 * n_steps) == 0 and (n_exp * de) % (16 * n_steps) == 0
    up_rows, down_rows = n_exp * d // n_steps, n_exp * de // n_steps
    bf16 = jnp.bfloat16
    n_qkvg = 2 * GLA_QK + 2 * GLA_V
    w_qkvg = w_in[:, :n_qkvg].astype(bf16)
    w_a = w_in[:, n_qkvg:n_qkvg + GLA_LOWRANK].astype(bf16)
    w_c3 = w_in[:, n_qkvg + GLA_LOWRANK:].astype(bf16)
    cw = w_c3.shape[1] // 3
    wrt = jnp.zeros((ROUTE_ROWS, d), jnp.float32)
    wrt = wrt.at[:N_GROUPS].set(w_group.T).at[ROUTE_EXPERT_ROW0:ROUTE_EXPERT_ROW0 + N_EXPERTS].set(w_router.T)
    brt = jnp.zeros((ROUTE_ROWS, 1), jnp.float32)
    brt = brt.at[:N_GROUPS, 0].set(b_group).at[ROUTE_EXPERT_ROW0:ROUTE_EXPERT_ROW0 + N_EXPERTS, 0].set(b_router)
    wrt_split = jnp.concatenate(_split_bf16(wrt), axis=0)
    gate_hi, gate_lo = _split_bf16(w_gla_gate)
    w_gate_split = jnp.concatenate([gate_hi, gate_hi, gate_lo], axis=0)
    args = (x, g_mix[None, :], w_qkvg, w_a, w_c3, w_gate_split, b_gla_gate[None, :], g_gla_out[None, :],
            w_conv, w_out.astype(bf16), g_moe[None, :], wrt_split, brt)
    slabs = (w_exp_gate.reshape(n_exp * d, de), w_exp_up.reshape(n_exp * d, de), w_exp_down.reshape(n_exp * de, d))
    slab_specs = [pl.BlockSpec((rows_, width), lambda j: (j, 0))
                  for rows_, width in ((up_rows, de), (up_rows, de), (down_rows, d))]
    in_specs = [pl.BlockSpec((b, MIX_TILE, d), lambda j: (0, j, 0))]
    in_specs += [_const_spec(a.shape) for a in args[1:]] + slab_specs
    rows, meta, counts, wg16, wu16, wd16 = pl.pallas_call(
        _mixer_kernel,
        grid=(n_steps,),
        in_specs=in_specs,
        out_specs=[pl.BlockSpec((b, MIX_TILE, d + LANES), lambda j: (0, j, 0)),
                   pl.BlockSpec((b, 1, 8, MIX_TILE), lambda j: (0, j, 0, 0)),
                   _const_spec((LANES, LANES))] + slab_specs,
        out_shape=[jax.ShapeDtypeStruct((b, s, d + LANES), jnp.float32),
                   jax.ShapeDtypeStruct((b, n_steps, 8, MIX_TILE), jnp.int32),
                   jax.ShapeDtypeStruct((LANES, LANES), jnp.int32)]
                  + [jax.ShapeDtypeStruct(w.shape, bf16) for w in slabs],
        scratch_shapes=[pltpu.VMEM((b, GLA_DV, GLA_QK), jnp.float32),
                        pltpu.VMEM((b, 8, cw), jnp.float32),
                        pltpu.VMEM((LANES, LANES), jnp.float32),
                        pltpu.VMEM((MIX_TILE, MIX_TILE), jnp.int32),
                        pltpu.VMEM((MIX_TILE, MIX_TILE), bf16)],
        compiler_params=pltpu.CompilerParams(dimension_semantics=("arbitrary",),
                                             vmem_limit_bytes=VMEM_LIMIT),
        name="mixer",
    )(*args, *slabs)
    return (rows.reshape(b * s, d + LANES), meta.reshape(b * n_steps, 8, MIX_TILE), counts,
            wg16.reshape(n_exp, d, de), wu16.reshape(n_exp, d, de), wd16.reshape(n_exp, de, d))


def _sort_plan(meta, counts, n_tok):
    i32 = jnp.int32
    n_tiles = n_tok // MOE_TILE + N_CLASSES
    cls = meta[:, INFO_CLS, :].reshape(n_tok)
    rank = meta[:, INFO_RANK, :].reshape(n_tok)
    cnt = counts[:N_CLASSES, 0]
    tiles_per_cls = (cnt + MOE_TILE - 1) // MOE_TILE
    tile_end = jnp.cumsum(tiles_per_cls)
    n_used = tile_end[-1:]
    tstart = jnp.concatenate([tile_end - tiles_per_cls, n_used])
    tile_id = jnp.minimum(jnp.arange(n_tiles, dtype=i32), n_used - 1)
    tile_cls = jnp.sum((tile_id[:, None] >= tile_end[None, :]).astype(i32), axis=1)
    grp, pair = tile_cls // N_PAIRS, tile_cls % N_PAIRS
    e_a, e_b = grp * EXPERTS_PER_GROUP, grp * EXPERTS_PER_GROUP
    for pp in range(N_PAIRS):
        e_a = e_a + jnp.where(pair == pp, PAIR_A[pp], 0)
        e_b = e_b + jnp.where(pair == pp, PAIR_B[pp], 0)
    first_tile = functools.reduce(lambda acc, c: jnp.where(cls == c, tstart[c], acc), range(N_CLASSES),
                                  jnp.zeros_like(cls))
    slot = first_tile * MOE_TILE + rank
    return dict(slot=slot, cnt=cnt, tstart=tstart, e_a=e_a, e_b=e_b, n_used=n_used, n_tiles=n_tiles)


def _experts(plan, rows, g_moe, wg, wu, wd):
    n_tiles = plan["n_tiles"]
    d, de = wg.shape[-2:]
    w_a = lambda shape: pl.BlockSpec(shape, lambda i, ea, eb, *_: (ea[i], 0, 0))
    w_b = lambda shape: pl.BlockSpec(shape, lambda i, ea, eb, *_: (eb[i], 0, 0))
    return pl.pallas_call(
        _expert_kernel,
        grid_spec=pltpu.PrefetchScalarGridSpec(
            num_scalar_prefetch=6, grid=(n_tiles,),
            in_specs=[pl.BlockSpec(memory_space=pl.ANY),
                      pl.BlockSpec((1, d), lambda i, *_: (0, 0)),
                      w_a((1, d, de)), w_a((1, d, de)), w_a((1, de, d)),
                      w_b((1, d, de)), w_b((1, d, de)), w_b((1, de, d))],
            out_specs=pl.BlockSpec((MOE_TILE, d), lambda i, *_: (i, 0)),
            scratch_shapes=[pltpu.VMEM((2, MOE_TILE, rows.shape[1]), jnp.float32),
                            pltpu.SemaphoreType.DMA((2,)),
                            pltpu.SMEM((n_tiles * MOE_TILE,), jnp.int32)]),
        out_shape=jax.ShapeDtypeStruct((n_tiles * MOE_TILE, d), jnp.float32),
        compiler_params=pltpu.CompilerParams(dimension_semantics=("arbitrary",),
                                             vmem_limit_bytes=VMEM_LIMIT),
        name="experts",
    )(plan["e_a"], plan["e_b"], plan["n_used"], plan["tstart"], plan["cnt"], plan["slot"],
      rows, g_moe[None, :], wg, wu, wd, wg, wu, wd)


def _ple_final(plan, rows, p, y_sorted, g_ple, w_ple_gate, w_ple_proj, g_final):
    n_tok, dp = p.shape
    d = y_sorted.shape[1]
    bf16 = jnp.bfloat16
    const = lambda shape: pl.BlockSpec(shape, lambda i, *_: (0,) * len(shape))
    tile = lambda width: pl.BlockSpec((MOE_TILE, width), lambda i, *_: (i, 0))
    return pl.pallas_call(
        _ple_final_kernel,
        grid_spec=pltpu.PrefetchScalarGridSpec(
            num_scalar_prefetch=1, grid=(n_tok // MOE_TILE,),
            in_specs=[tile(d),
                      tile(dp),
                      pl.BlockSpec(memory_space=pl.ANY),
                      const((1, d)), const((d, d)), const((dp, d)), const((1, d))],
            out_specs=tile(d),
            scratch_shapes=[pltpu.VMEM((2, MOE_TILE, d), jnp.float32),
                            pltpu.SemaphoreType.DMA((2,))]),
        out_shape=jax.ShapeDtypeStruct((n_tok, d), jnp.float32),
        compiler_params=pltpu.CompilerParams(dimension_semantics=("arbitrary",),
                                             vmem_limit_bytes=VMEM_LIMIT),
        name="ple_final",
    )(plan["slot"], rows, p, y_sorted,
      g_ple[None, :], w_ple_gate.astype(bf16), w_ple_proj.astype(bf16), g_final[None, :])


def kernel(x, p, g_mix, w_in, w_gla_gate, b_gla_gate, g_gla_out, w_conv, w_out, g_moe, w_group, b_group,
           w_router, b_router, w_exp_gate, w_exp_up, w_exp_down, g_ple, w_ple_gate, w_ple_proj, g_final):
    depth = w_in.shape[0]
    assert depth == 1, "the final norm is fused into the last (only) layer"
    b, s, d = x.shape
    n_tok = b * s
    assert s % MIX_TILE == 0 and n_tok % MOE_TILE == 0 and n_tok % PLACE_UNROLL == 0
    rows, meta, counts, wg16, wu16, wd16 = _mixer(
        x, g_mix[0], w_in[0], w_gla_gate[0], b_gla_gate[0], g_gla_out[0], w_conv[0], w_out[0], g_moe[0],
        w_group[0], b_group[0], w_router[0], b_router[0], w_exp_gate[0], w_exp_up[0], w_exp_down[0])
    plan = _sort_plan(meta, counts, n_tok)
    y_sorted = _experts(plan, rows, g_moe[0], wg16, wu16, wd16)
    out = _ple_final(plan, rows, p[0].reshape(n_tok, -1), y_sorted, g_ple[0], w_ple_gate[0], w_ple_proj[0],
                     g_final)
    return out.reshape(b, s, d)
```

```python
import functools

import jax
import jax.numpy as jnp
from jax import lax
from jax.experimental import pallas as pl
from jax.experimental.pallas import tpu as pltpu

EPS = 1e-6
GLA_HEADS = 4
GLA_DK = 64
GLA_DV = 128
GLA_QK = GLA_HEADS * GLA_DK
GLA_V = GLA_HEADS * GLA_DV
GLA_LOWRANK = 16
GLA_TAU = 16.0
CONV_K = 3
N_GROUPS = 4
EXPERTS_PER_GROUP = 4
N_EXPERTS = N_GROUPS * EXPERTS_PER_GROUP

LANES = 128
MIX_TILE = 256
MIX_LEVELS = 8
ROUTE_ROWS = 32
ROUTE_EXPERT_ROW0 = 8
MOE_TILE = 256
VMEM_LIMIT = 56 * 1024 * 1024

PAIR_A = (0, 2, 2, 0, 0, 1)
PAIR_B = (1, 1, 3, 3, 2, 3)
PAIR_OF_KEY = {1: 0, 6: 1, 11: 2, 3: 3, 2: 4, 7: 5}
N_PAIRS = len(PAIR_A)
N_CLASSES = N_GROUPS * N_PAIRS
INFO_CLS, INFO_RANK, INFO_WA, INFO_WB = 0, 1, 2, 3
PLACE_UNROLL = 8
ROW_RECORD = 9
Y_RECORD = 8

_NT = (((1,), (1,)), ((), ()))
_DONE = object()


def _rms(x, g):
    return x * lax.rsqrt(jnp.mean(x * x, axis=-1, keepdims=True) + EPS) * g


def _dot(a, b):
    return jnp.dot(a, b, preferred_element_type=jnp.float32)


def _dot_nt(a, b):
    return lax.dot_general(a, b, _NT, preferred_element_type=jnp.float32)


def _split_bf16(a):
    hi = a.astype(jnp.bfloat16)
    return hi, (a - hi.astype(jnp.float32)).astype(jnp.bfloat16)


def _shift_rows(x, shift):
    return pltpu.roll(x, shift % x.shape[0], axis=0)


def _mixer_kernel(x_ref, gmix_ref, wqkvg_ref, wa_ref, wc3_ref, wgate_ref, bgate_ref, ggla_ref,
                  wconv_ref, wout_ref, gmoe_ref, wrt_ref, brt_ref, wge_ref, wue_ref, wde_ref,
                  rows_ref, meta_ref, counts_ref, wge16_ref, wue16_ref, wde16_ref,
                  st_ref, carry_ref, count_ref, level_ref, tril_ref):
    bf16 = jnp.bfloat16
    T = MIX_TILE

    wge16_ref[...] = wge_ref[...].astype(bf16)
    wue16_ref[...] = wue_ref[...].astype(bf16)
    wde16_ref[...] = wde_ref[...].astype(bf16)

    @pl.when(pl.program_id(0) == 0)
    def _():
        st_ref[...] = jnp.zeros_like(st_ref)
        carry_ref[...] = jnp.zeros_like(carry_ref)
        count_ref[...] = jnp.zeros_like(count_ref)
        tt = lax.broadcasted_iota(jnp.int32, (T, T), 0)
        ss = lax.broadcasted_iota(jnp.int32, (T, T), 1)
        txs = jnp.bitwise_xor(tt, ss)
        level = jnp.zeros((T, T), jnp.int32)
        for j in range(1, MIX_LEVELS):
            level = level + (txs >= (1 << j)).astype(jnp.int32)
        level_ref[...] = jnp.where(tt > ss, level, jnp.where(tt == ss, MIX_LEVELS, -1))
        tril_ref[...] = (ss <= tt).astype(bf16)

    tiles = [_mixer_tile(x_ref.at[pl.ds(b, 1)], gmix_ref, wqkvg_ref, wa_ref, wc3_ref, wgate_ref, bgate_ref,
                         ggla_ref, wconv_ref, wout_ref, gmoe_ref, wrt_ref, brt_ref,
                         rows_ref.at[b], meta_ref.at[b], counts_ref,
                         st_ref.at[b], carry_ref.at[b], count_ref, level_ref, tril_ref)
             for b in range(x_ref.shape[0])]
    while tiles:
        tiles = [t for t in tiles if next(t, _DONE) is not _DONE]


def _mixer_tile(x_ref, gmix_ref, wqkvg_ref, wa_ref, wc3_ref, wgate_ref, bgate_ref, ggla_ref,
                wconv_ref, wout_ref, gmoe_ref, wrt_ref, brt_ref,
                rows_ref, meta_ref, counts_ref,
                st_ref, carry_ref, count_ref, level_ref, tril_ref):
    f32, bf16 = jnp.float32, jnp.bfloat16
    T = MIX_TILE
    D = x_ref.shape[-1]

    x = x_ref[0]
    hb = _rms(x, gmix_ref[...]).astype(bf16)
    qkvg = _dot(hb, wqkvg_ref[...])
    q = qkvg[:, :GLA_QK] * (GLA_DK ** -0.5)
    k = qkvg[:, GLA_QK:2 * GLA_QK]
    v = qkvg[:, 2 * GLA_QK:2 * GLA_QK + GLA_V]
    g = qkvg[:, 2 * GLA_QK + GLA_V:]
    a_low = _dot(hb, wa_ref[...])
    a_hi, a_lo = _split_bf16(a_low)
    z = _dot(jnp.concatenate([a_hi, a_lo, a_hi], axis=1), wgate_ref[...]) + bgate_ref[...]
    la = (jnp.minimum(z, 0.0) - jnp.log(1.0 + jnp.exp(-jnp.abs(z)))) * (1.0 / GLA_TAU)
    yield

    row = lax.broadcasted_iota(jnp.int32, (T, GLA_QK), 0)

    def next_level(l, q_l, k_l, block):
        upper = ((row >> l) & 1) == 1
        below = _shift_rows(block, 1 << l)
        above = _shift_rows(block, -(1 << l))
        return (q_l * jnp.where(upper, below, 1.0), k_l * jnp.where(upper, 1.0, above),
                block * jnp.where(upper, below, above))

    decay = jnp.exp(la)
    H = T // 2
    assert GLA_DV == H
    half_level = level_ref[0:H, 0:H]
    lane_head_st = lax.broadcasted_iota(jnp.int32, (H, GLA_QK), 1) // GLA_DK

    def per_head_rows(a):
        return jnp.concatenate([jnp.where(lane_head_st == h, a, jnp.zeros_like(a)) for h in range(GLA_HEADS)],
                               axis=0)

    def head_blocks(p):
        return [p[:, h * H:(h + 1) * H] for h in range(GLA_HEADS)]

    diag0 = [jnp.zeros((H, H), f32) for _ in range(GLA_HEADS)]
    diag1 = [jnp.zeros((H, H), f32) for _ in range(GLA_HEADS)]

    def add_level(l, q_l, k_l):
        sel = half_level == l
        ql, kl = q_l.astype(bf16), k_l.astype(bf16)
        p0 = head_blocks(_dot_nt(ql[:H], per_head_rows(kl[:H])))
        p1 = head_blocks(_dot_nt(ql[H:], per_head_rows(kl[H:])))
        for h in range(GLA_HEADS):
            diag0[h] = jnp.where(sel, p0[h], diag0[h])
            diag1[h] = jnp.where(sel, p1[h], diag1[h])

    add_level(MIX_LEVELS, q, k)
    q_l, k_l, block = q * decay, k, decay
    for l in range(MIX_LEVELS - 1):
        add_level(l, q_l, k_l)
        q_l, k_l, block = next_level(l, q_l, k_l, block)
        if l % 2 == 1:
            yield
    low = head_blocks(_dot_nt(q_l[H:].astype(bf16), per_head_rows(k_l[:H].astype(bf16))))
    zero_block = jnp.zeros((H, H), f32)
    scores = [jnp.concatenate([jnp.concatenate([diag0[h], zero_block], axis=1),
                               jnp.concatenate([low[h], diag1[h]], axis=1)], axis=0) for h in range(GLA_HEADS)]

    yield

    q_in, k_out, tile_decay = next_level(MIX_LEVELS - 1, q_l, k_l, block)
    st = st_ref[...]
    o_state = _dot_nt(q_in.astype(bf16), per_head_rows(st.astype(bf16)))
    upd = _dot(v.T.astype(bf16), k_out.astype(bf16))
    new_st = st * tile_decay[:H]
    ggla = ggla_ref[...]
    y_heads = []
    for h in range(GLA_HEADS):
        v_h = v[:, h * GLA_DV:(h + 1) * GLA_DV]
        o = _dot(scores[h].astype(bf16), v_h.astype(bf16)) + o_state[:, h * GLA_DV:(h + 1) * GLA_DV]
        new_st = new_st + jnp.where(lane_head_st == h, upd[h * GLA_DV:(h + 1) * GLA_DV], 0.0)
        g_h = g[:, h * GLA_DV:(h + 1) * GLA_DV]
        y_heads.append(_rms(o, ggla) * (g_h * jax.nn.sigmoid(g_h)))
        if h % 2 == 1:
            yield
    st_ref[...] = new_st

    yield

    c3 = _dot(hb, wc3_ref[...])
    cw = c3.shape[1] // 3
    cb, cu = c3[:, :cw], c3[:, cw:2 * cw] * c3[:, 2 * cw:]
    crow = lax.broadcasted_iota(jnp.int32, (T, cw), 0)
    prev2, prev1 = carry_ref[0:1, :], carry_ref[1:2, :]
    m1 = jnp.where(crow == 0, prev1, _shift_rows(cu, 1))
    m2 = jnp.where(crow == 0, prev2, jnp.where(crow == 1, prev1, _shift_rows(cu, 2)))
    wconv = wconv_ref[...]
    y_conv = cb * (wconv[0:1, :] * m2 + wconv[1:2, :] * m1 + wconv[2:3, :] * cu)
    carry_ref[0:2, :] = cu[T - 2:, :]

    y = jnp.concatenate(y_heads + [y_conv], axis=1).astype(bf16)
    x1 = x + _dot(y, wout_ref[...])

    yield

    h2 = _rms(x1, gmoe_ref[...])
    h2_hi, h2_lo = _split_bf16(h2)
    part = _dot_nt(wrt_ref[...], h2_hi)
    logits = (part[:ROUTE_ROWS] + part[ROUTE_ROWS:] + _dot_nt(wrt_ref[:ROUTE_ROWS, :], h2_lo)) + brt_ref[...]
    gl = [logits[i:i + 1, :] for i in range(N_GROUPS)]
    gmax = functools.reduce(jnp.maximum, gl)
    gsum = functools.reduce(lambda a, b: a + b, [jnp.exp(t - gmax) for t in gl])
    p_grp = 1.0 / gsum
    g_sel = jnp.full_like(gmax, N_GROUPS - 1).astype(jnp.int32)
    for i in reversed(range(N_GROUPS - 1)):
        g_sel = jnp.where(gl[i] == gmax, i, g_sel)
    ig = []
    for j in range(EXPERTS_PER_GROUP):
        acc = jnp.zeros_like(gmax)
        for gi in range(N_GROUPS):
            r0 = ROUTE_EXPERT_ROW0 + gi * EXPERTS_PER_GROUP + j
            acc = acc + jnp.where(g_sel == gi, logits[r0:r0 + 1, :], 0.0)
        ig.append(acc)

    def first_argmax(vals):
        m = functools.reduce(jnp.maximum, vals)
        idx = jnp.full_like(m, len(vals) - 1).astype(jnp.int32)
        for i in reversed(range(len(vals) - 1)):
            idx = jnp.where(vals[i] == m, i, idx)
        return m, idx

    m1_, i1 = first_argmax(ig)
    m2_, i2 = first_argmax([jnp.where(i1 == j, -jnp.inf, ig[j]) for j in range(EXPERTS_PER_GROUP)])
    e21 = jnp.exp(m2_ - m1_)
    w1 = p_grp / (1.0 + e21)
    w2 = p_grp * e21 / (1.0 + e21)
    key = jnp.minimum(i1, i2) * EXPERTS_PER_GROUP + jnp.maximum(i1, i2)
    pair = jnp.zeros_like(key)
    a_loc = jnp.zeros_like(key)
    for kk, pp in PAIR_OF_KEY.items():
        pair = jnp.where(key == kk, pp, pair)
        a_loc = jnp.where(key == kk, PAIR_A[pp], a_loc)
    w_a = jnp.where(i1 == a_loc, w1, w2)
    w_b = jnp.where(i1 == a_loc, w2, w1)
    cls = g_sel * N_PAIRS + pair
    rr = lax.broadcasted_iota(jnp.int32, (LANES, T), 0)
    rec_t = (jnp.where(rr == INFO_WA, jnp.broadcast_to(w_a, (LANES, T)), 0.0)
             + jnp.where(rr == INFO_WB, jnp.broadcast_to(w_b, (LANES, T)), 0.0))
    pieces = [x1[:, c * LANES:(c + 1) * LANES] for c in range(D // LANES)] + [rec_t.T]
    assert len(pieces) == ROW_RECORD
    _store_records(rows_ref, pieces, ROW_RECORD)

    onehot = (rr == jnp.broadcast_to(cls, (LANES, T))).astype(f32)
    count = count_ref[:, 0:1]
    before = _dot_nt(onehot.astype(bf16), tril_ref[...]) - onehot + count
    rank = jnp.sum(onehot * before, axis=0, keepdims=True).astype(jnp.int32)
    r8 = lax.broadcasted_iota(jnp.int32, (8, T), 0)
    meta_ref[0] = jnp.where(r8 == INFO_CLS, jnp.broadcast_to(cls, (8, T)),
                            jnp.where(r8 == INFO_RANK, jnp.broadcast_to(rank, (8, T)), 0))
    new_count = jnp.broadcast_to(count + jnp.sum(onehot, axis=1, keepdims=True), count_ref.shape)
    count_ref[...] = new_count
    counts_ref[...] = new_count.astype(jnp.int32)


def _store_records(ref, pieces, record_rows):
    n = ref.shape[0] // record_rows
    for c, piece in enumerate(pieces):
        ref[pl.ds(c, n, stride=record_rows), :] = piece


def _load_records(ref, first, count, record_rows, lead=()):
    n = ref.shape[-2] // record_rows
    return jnp.concatenate([ref[lead + (pl.ds(first + c, n, stride=record_rows), slice(None))]
                            for c in range(count)], axis=1)


class _RowGather:
    def __init__(self, index_of, src_hbm, buf, sems, record_rows):
        self.index_of, self.src_hbm, self.buf, self.sems, self.rr = index_of, src_hbm, buf, sems, record_rows

    def _issue(self, tile, slot):
        rr = self.rr
        for r in range(MOE_TILE):
            first = self.index_of(tile * MOE_TILE + r) * rr
            pltpu.make_async_copy(self.src_hbm.at[pl.ds(first, rr), :],
                                  self.buf.at[slot, pl.ds(r * rr, rr), :], self.sems.at[slot]).start()

    def start(self, tile, slot):
        if isinstance(slot, int):
            self._issue(tile, slot)
        else:
            for static_slot in range(2):
                pl.when(slot == static_slot)(functools.partial(self._issue, tile, static_slot))

    def wait(self, slot):
        pltpu.make_async_copy(self.src_hbm.at[pl.ds(0, MOE_TILE * self.rr), :], self.buf.at[slot],
                              self.sems.at[slot]).wait()


def _expert_kernel(ea_ref, eb_ref, nused_ref, tstart_ref, cnt_ref, slot_ref,
                   rows_hbm, gmoe_ref, wga_ref, wua_ref, wda_ref, wgb_ref, wub_ref, wdb_ref,
                   y_ref, buf, sems, src_ref):
    bf16 = jnp.bfloat16
    n_x = gmoe_ref.shape[-1] // LANES
    n_tok = slot_ref.shape[0]
    step, n_used = pl.program_id(0), nused_ref[0]
    gather = _RowGather(lambda i: src_ref[i], rows_hbm, buf, sems, ROW_RECORD)

    @pl.when(step == 0)
    def _():
        for c in range(N_CLASSES):
            base = tstart_ref[c] * MOE_TILE

            def pad(r, carry, base=base):
                src_ref[base + r] = 0
                return carry

            lax.fori_loop(cnt_ref[c], (tstart_ref[c + 1] - tstart_ref[c]) * MOE_TILE, pad, 0)

        def place(it, carry):
            for u in range(PLACE_UNROLL):
                t = it * PLACE_UNROLL + u
                src_ref[slot_ref[t]] = t
            return carry

        lax.fori_loop(0, n_tok // PLACE_UNROLL, place, 0)
        gather.start(0, 0)

    slot = step % 2

    @pl.when(step + 1 < n_used)
    def _():
        gather.start(step + 1, 1 - slot)

    @pl.when(step < n_used)
    def _():
        gather.wait(slot)
        rec = _load_records(buf, n_x, 1, ROW_RECORD, lead=(slot,))
        h2 = _rms(_load_records(buf, 0, n_x, ROW_RECORD, lead=(slot,)), gmoe_ref[...]).astype(bf16)

        def expert(wg_ref, wu_ref, wd_ref):
            gate = _dot(h2, wg_ref[0])
            hid = (gate * jax.nn.sigmoid(gate)) * _dot(h2, wu_ref[0])
            return _dot(hid.astype(bf16), wd_ref[0])

        y = rec[:, INFO_WA:INFO_WA + 1] * expert(wga_ref, wua_ref, wda_ref)
        y = y + rec[:, INFO_WB:INFO_WB + 1] * expert(wgb_ref, wub_ref, wdb_ref)
        _store_records(y_ref, [y[:, c * LANES:(c + 1) * LANES] for c in range(n_x)], Y_RECORD)

    @pl.when(step >= n_used)
    def _():
        y_ref[...] = jnp.zeros_like(y_ref)


def _ple_final_kernel(slot_ref, x1_ref, p_ref, y_hbm, gple_ref, wpg_ref, wpp_ref, gfin_ref,
                      out_ref, buf, sems):
    bf16 = jnp.bfloat16
    step, n_steps = pl.program_id(0), pl.num_programs(0)
    n_x = gple_ref.shape[-1] // LANES
    gather = _RowGather(lambda t: slot_ref[t], y_hbm, buf, sems, Y_RECORD)

    @pl.when(step == 0)
    def _():
        gather.start(0, 0)

    slot = step % 2

    @pl.when(step + 1 < n_steps)
    def _():
        gather.start(step + 1, 1 - slot)

    gather.wait(slot)
    x2 = _load_records(x1_ref, 0, n_x, ROW_RECORD) + _load_records(buf, 0, n_x, Y_RECORD, lead=(slot,))
    gate_p = jax.nn.sigmoid(_dot(_rms(x2, gple_ref[...]).astype(bf16), wpg_ref[...]))
    x3 = x2 + gate_p * _dot(p_ref[...].astype(bf16), wpp_ref[...])
    out_ref[...] = _rms(x3, gfin_ref[...])


def _const_spec(shape):
    return pl.BlockSpec(shape, lambda *_: (0,) * len(shape))


def _mixer(x, g_mix, w_in, w_gla_gate, b_gla_gate, g_gla_out, w_conv, w_out, g_moe,
           w_group, b_group, w_router, b_router, w_exp_gate, w_exp_up, w_exp_down):
    b, s, d = x.shape
    n_steps = s // MIX_TILE
    n_exp, _, de = w_exp_gate.shape
    assert (n_exp * d) % (16 * n_steps) == 0 and (n_exp * de) % (16 * n_steps) == 0
    up_rows, down_rows = n_exp * d // n_steps, n_exp * de // n_steps
    bf16 = jnp.bfloat16
    n_qkvg = 2 * GLA_QK + 2 * GLA_V
    w_qkvg = w_in[:, :n_qkvg].astype(bf16)
    w_a = w_in[:, n_qkvg:n_qkvg + GLA_LOWRANK].astype(bf16)
    w_c3 = w_in[:, n_qkvg + GLA_LOWRANK:].astype(bf16)
    cw = w_c3.shape[1] // 3
    wrt = jnp.zeros((ROUTE_ROWS, d), jnp.float32)
    wrt = wrt.at[:N_GROUPS].set(w_group.T).at[ROUTE_EXPERT_ROW0:ROUTE_EXPERT_ROW0 + N_EXPERTS].set(w_router.T)
    brt = jnp.zeros((ROUTE_ROWS, 1), jnp.float32)
    brt = brt.at[:N_GROUPS, 0].set(b_group).at[ROUTE_EXPERT_ROW0:ROUTE_EXPERT_ROW0 + N_EXPERTS, 0].set(b_router)
    wrt_split = jnp.concatenate(_split_bf16(wrt), axis=0)
    gate_hi, gate_lo = _split_bf16(w_gla_gate)
    w_gate_split = jnp.concatenate([gate_hi, gate_hi, gate_lo], axis=0)
    args = (x, g_mix[None, :], w_qkvg, w_a, w_c3, w_gate_split, b_gla_gate[None, :], g_gla_out[None, :],
            w_conv, w_out.astype(bf16), g_moe[None, :], wrt_split, brt)
    slabs = (w_exp_gate.reshape(n_exp * d, de), w_exp_up.reshape(n_exp * d, de), w_exp_down.reshape(n_exp * de, d))
    slab_specs = [pl.BlockSpec((rows_, width), lambda j: (j, 0))
                  for rows_, width in ((up_rows, de), (up_rows, de), (down_rows, d))]
    in_specs = [pl.BlockSpec((b, MIX_TILE, d), lambda j: (0, j, 0))]
    in_specs += [_const_spec(a.shape) for a in args[1:]] + slab_specs
    rows, meta, counts, wg16, wu16, wd16 = pl.pallas_call(
        _mixer_kernel,
        grid=(n_steps,),
        in_specs=in_specs,
        out_specs=[pl.BlockSpec((b, MIX_TILE * ROW_RECORD, LANES), lambda j: (0, j, 0)),
                   pl.BlockSpec((b, 1, 8, MIX_TILE), lambda j: (0, j, 0, 0)),
                   _const_spec((LANES, LANES))] + slab_specs,
        out_shape=[jax.ShapeDtypeStruct((b, s * ROW_RECORD, LANES), jnp.float32),
                   jax.ShapeDtypeStruct((b, n_steps, 8, MIX_TILE), jnp.int32),
                   jax.ShapeDtypeStruct((LANES, LANES), jnp.int32)]
                  + [jax.ShapeDtypeStruct(w.shape, bf16) for w in slabs],
        scratch_shapes=[pltpu.VMEM((b, GLA_DV, GLA_QK), jnp.float32),
                        pltpu.VMEM((b, 8, cw), jnp.float32),
                        pltpu.VMEM((LANES, LANES), jnp.float32),
                        pltpu.VMEM((MIX_TILE, MIX_TILE), jnp.int32),
                        pltpu.VMEM((MIX_TILE, MIX_TILE), bf16)],
        compiler_params=pltpu.CompilerParams(dimension_semantics=("arbitrary",),
                                             vmem_limit_bytes=VMEM_LIMIT),
        name="mixer",
    )(*args, *slabs)
    return (rows.reshape(b * s * ROW_RECORD, LANES), meta.reshape(b * n_steps, 8, MIX_TILE), counts,
            wg16.reshape(n_exp, d, de), wu16.reshape(n_exp, d, de), wd16.reshape(n_exp, de, d))


def _sort_plan(meta, counts, n_tok):
    i32 = jnp.int32
    n_tiles = n_tok // MOE_TILE + N_CLASSES
    cls = meta[:, INFO_CLS, :].reshape(n_tok)
    rank = meta[:, INFO_RANK, :].reshape(n_tok)
    cnt = counts[:N_CLASSES, 0]
    tiles_per_cls = (cnt + MOE_TILE - 1) // MOE_TILE
    tile_end = jnp.cumsum(tiles_per_cls)
    n_used = tile_end[-1:]
    tstart = jnp.concatenate([tile_end - tiles_per_cls, n_used])
    tile_id = jnp.minimum(jnp.arange(n_tiles, dtype=i32), n_used - 1)
    tile_cls = jnp.sum((tile_id[:, None] >= tile_end[None, :]).astype(i32), axis=1)
    grp, pair = tile_cls // N_PAIRS, tile_cls % N_PAIRS
    e_a, e_b = grp * EXPERTS_PER_GROUP, grp * EXPERTS_PER_GROUP
    for pp in range(N_PAIRS):
        e_a = e_a + jnp.where(pair == pp, PAIR_A[pp], 0)
        e_b = e_b + jnp.where(pair == pp, PAIR_B[pp], 0)
    first_tile = functools.reduce(lambda acc, c: jnp.where(cls == c, tstart[c], acc), range(N_CLASSES),
                                  jnp.zeros_like(cls))
    slot = first_tile * MOE_TILE + rank
    return dict(slot=slot, cnt=cnt, tstart=tstart, e_a=e_a, e_b=e_b, n_used=n_used, n_tiles=n_tiles)


def _experts(plan, rows, g_moe, wg, wu, wd):
    n_tiles = plan["n_tiles"]
    d, de = wg.shape[-2:]
    w_a = lambda shape: pl.BlockSpec(shape, lambda i, ea, eb, *_: (ea[i], 0, 0))
    w_b = lambda shape: pl.BlockSpec(shape, lambda i, ea, eb, *_: (eb[i], 0, 0))
    return pl.pallas_call(
        _expert_kernel,
        grid_spec=pltpu.PrefetchScalarGridSpec(
            num_scalar_prefetch=6, grid=(n_tiles,),
            in_specs=[pl.BlockSpec(memory_space=pl.ANY),
                      pl.BlockSpec((1, d), lambda i, *_: (0, 0)),
                      w_a((1, d, de)), w_a((1, d, de)), w_a((1, de, d)),
                      w_b((1, d, de)), w_b((1, d, de)), w_b((1, de, d))],
            out_specs=pl.BlockSpec((MOE_TILE * Y_RECORD, LANES), lambda i, *_: (i, 0)),
            scratch_shapes=[pltpu.VMEM((2, MOE_TILE * ROW_RECORD, LANES), jnp.float32),
                            pltpu.SemaphoreType.DMA((2,)),
                            pltpu.SMEM((n_tiles * MOE_TILE,), jnp.int32)]),
        out_shape=jax.ShapeDtypeStruct((n_tiles * MOE_TILE * Y_RECORD, LANES), jnp.float32),
        compiler_params=pltpu.CompilerParams(dimension_semantics=("arbitrary",),
                                             vmem_limit_bytes=VMEM_LIMIT),
        name="experts",
    )(plan["e_a"], plan["e_b"], plan["n_used"], plan["tstart"], plan["cnt"], plan["slot"],
      rows, g_moe[None, :], wg, wu, wd, wg, wu, wd)


def _ple_final(plan, rows, p, y_sorted, g_ple, w_ple_gate, w_ple_proj, g_final):
    n_tok, dp = p.shape
    d = w_ple_gate.shape[0]
    bf16 = jnp.bfloat16
    const = lambda shape: pl.BlockSpec(shape, lambda i, *_: (0,) * len(shape))
    tile = lambda width: pl.BlockSpec((MOE_TILE, width), lambda i, *_: (i, 0))
    return pl.pallas_call(
        _ple_final_kernel,
        grid_spec=pltpu.PrefetchScalarGridSpec(
            num_scalar_prefetch=1, grid=(n_tok // MOE_TILE,),
            in_specs=[pl.BlockSpec((MOE_TILE * ROW_RECORD, LANES), lambda i, *_: (i, 0)),
                      tile(dp),
                      pl.BlockSpec(memory_space=pl.ANY),
                      const((1, d)), const((d, d)), const((dp, d)), const((1, d))],
            out_specs=tile(d),
            scratch_shapes=[pltpu.VMEM((2, MOE_TILE * Y_RECORD, LANES), jnp.float32),
                            pltpu.SemaphoreType.DMA((2,))]),
        out_shape=jax.ShapeDtypeStruct((n_tok, d), jnp.float32),
        compiler_params=pltpu.CompilerParams(dimension_semantics=("arbitrary",),
                                             vmem_limit_bytes=VMEM_LIMIT),
        name="ple_final",
    )(plan["slot"], rows, p, y_sorted,
      g_ple[None, :], w_ple_gate.astype(bf16), w_ple_proj.astype(bf16), g_final[None, :])


def kernel(x, p, g_mix, w_in, w_gla_gate, b_gla_gate, g_gla_out, w_conv, w_out, g_moe, w_group, b_group,
           w_router, b_router, w_exp_gate, w_exp_up, w_exp_down, g_ple, w_ple_gate, w_ple_proj, g_final):
    depth = w_in.shape[0]
    assert depth == 1, "the final norm is fused into the last (only) layer"
    b, s, d = x.shape
    n_tok = b * s
    assert s % MIX_TILE == 0 and n_tok % MOE_TILE == 0 and n_tok % PLACE_UNROLL == 0
    rows, meta, counts, wg16, wu16, wd16 = _mixer(
        x, g_mix[0], w_in[0], w_gla_gate[0], b_gla_gate[0], g_gla_out[0], w_conv[0], w_out[0], g_moe[0],
        w_group[0], b_group[0], w_router[0], b_router[0], w_exp_gate[0], w_exp_up[0], w_exp_down[0])
    plan = _sort_plan(meta, counts, n_tok)
    y_sorted = _experts(plan, rows, g_moe[0], wg16, wu16, wd16)
    out = _ple_final(plan, rows, p[0].reshape(n_tok, -1), y_sorted, g_ple[0], w_ple_gate[0], w_ple_proj[0],
                     g_final)
    return out.reshape(b, s, d)
```

```python
import functools

import jax
import jax.numpy as jnp
from jax import lax
from jax.experimental import pallas as pl
from jax.experimental.pallas import tpu as pltpu

EPS = 1e-6
GLA_HEADS = 4
GLA_DK = 64
GLA_DV = 128
GLA_QK = GLA_HEADS * GLA_DK
GLA_V = GLA_HEADS * GLA_DV
GLA_LOWRANK = 16
GLA_TAU = 16.0
CONV_K = 3
N_GROUPS = 4
EXPERTS_PER_GROUP = 4
N_EXPERTS = N_GROUPS * EXPERTS_PER_GROUP

LANES = 128
MIX_TILE = 256
MIX_LEVELS = 8
MIX_TILES_PER_STEP = 1
ROUTE_ROWS = 32
ROUTE_EXPERT_ROW0 = 8
MOE_TILE = 256
VMEM_LIMIT = 56 * 1024 * 1024

PAIR_A = (0, 2, 2, 0, 0, 1)
PAIR_B = (1, 1, 3, 3, 2, 3)
PAIR_OF_KEY = {1: 0, 6: 1, 11: 2, 3: 3, 2: 4, 7: 5}
N_PAIRS = len(PAIR_A)
N_CLASSES = N_GROUPS * N_PAIRS
INFO_CLS, INFO_RANK, INFO_WA, INFO_WB = 0, 1, 2, 3
PLACE_UNROLL = 8
ROW_RECORD = 9
Y_RECORD = 8
GATHER_GROUP = 32
PLE_TILE = 512

_NT = (((1,), (1,)), ((), ()))
_DONE = object()


def _rms(x, g):
    return x * lax.rsqrt(jnp.mean(x * x, axis=-1, keepdims=True) + EPS) * g


def _dot(a, b):
    return jnp.dot(a, b, preferred_element_type=jnp.float32)


def _dot_nt(a, b):
    return lax.dot_general(a, b, _NT, preferred_element_type=jnp.float32)


def _split_bf16(a):
    hi = a.astype(jnp.bfloat16)
    return hi, (a - hi.astype(jnp.float32)).astype(jnp.bfloat16)


def _shift_rows(x, shift):
    return pltpu.roll(x, shift % x.shape[0], axis=0)


def _mixer_kernel(x_ref, gmix_ref, wqkvg_ref, wa_ref, wc3_ref, wgate_ref, bgate_ref, ggla_ref,
                  wconv_ref, wout_ref, gmoe_ref, wrt_ref, brt_ref, wge_ref, wue_ref, wde_ref,
                  rows_ref, meta_ref, counts_ref, wge16_ref, wue16_ref, wde16_ref,
                  st_ref, carry_ref, count_ref, level_ref, tril_ref):
    bf16 = jnp.bfloat16
    T = MIX_TILE

    wge16_ref[...] = wge_ref[...].astype(bf16)
    wue16_ref[...] = wue_ref[...].astype(bf16)
    wde16_ref[...] = wde_ref[...].astype(bf16)

    @pl.when(pl.program_id(0) == 0)
    def _():
        st_ref[...] = jnp.zeros_like(st_ref)
        carry_ref[...] = jnp.zeros_like(carry_ref)
        count_ref[...] = jnp.zeros_like(count_ref)
        tt = lax.broadcasted_iota(jnp.int32, (T, T), 0)
        ss = lax.broadcasted_iota(jnp.int32, (T, T), 1)
        txs = jnp.bitwise_xor(tt, ss)
        level = jnp.zeros((T, T), jnp.int32)
        for j in range(1, MIX_LEVELS):
            level = level + (txs >= (1 << j)).astype(jnp.int32)
        level_ref[...] = jnp.where(tt > ss, level, jnp.where(tt == ss, MIX_LEVELS, -1))
        tril_ref[...] = (ss <= tt).astype(bf16)

    tiles = [_mixer_tile(x_ref.at[pl.ds(b, 1), pl.ds(i * T, T)], gmix_ref, wqkvg_ref, wa_ref, wc3_ref, wgate_ref,
                         bgate_ref, ggla_ref, wconv_ref, wout_ref, gmoe_ref, wrt_ref, brt_ref,
                         rows_ref.at[b, pl.ds(i * T * ROW_RECORD, T * ROW_RECORD)], meta_ref.at[b, pl.ds(i, 1)],
                         counts_ref, st_ref.at[b], carry_ref.at[b], count_ref, level_ref, tril_ref)
             for i in range(MIX_TILES_PER_STEP) for b in range(x_ref.shape[0])]
    while tiles:
        tiles = [t for t in tiles if next(t, _DONE) is not _DONE]


def _mixer_tile(x_ref, gmix_ref, wqkvg_ref, wa_ref, wc3_ref, wgate_ref, bgate_ref, ggla_ref,
                wconv_ref, wout_ref, gmoe_ref, wrt_ref, brt_ref,
                rows_ref, meta_ref, counts_ref,
                st_ref, carry_ref, count_ref, level_ref, tril_ref):
    f32, bf16 = jnp.float32, jnp.bfloat16
    T = MIX_TILE
    D = x_ref.shape[-1]

    x = x_ref[0]
    hb = _rms(x, gmix_ref[...]).astype(bf16)
    qkvg = _dot(hb, wqkvg_ref[...])
    q = qkvg[:, :GLA_QK] * (GLA_DK ** -0.5)
    k = qkvg[:, GLA_QK:2 * GLA_QK]
    v = qkvg[:, 2 * GLA_QK:2 * GLA_QK + GLA_V]
    g = qkvg[:, 2 * GLA_QK + GLA_V:]
    a_low = _dot(hb, wa_ref[...])
    a_hi, a_lo = _split_bf16(a_low)
    z = _dot(jnp.concatenate([a_hi, a_lo, a_hi], axis=1), wgate_ref[...]) + bgate_ref[...]
    la = (jnp.minimum(z, 0.0) - jnp.log(1.0 + jnp.exp(-jnp.abs(z)))) * (1.0 / GLA_TAU)
    yield

    row = lax.broadcasted_iota(jnp.int32, (T, GLA_QK), 0)

    def next_level(l, q_l, k_l, block):
        upper = ((row >> l) & 1) == 1
        below = _shift_rows(block, 1 << l)
        above = _shift_rows(block, -(1 << l))
        return (q_l * jnp.where(upper, below, 1.0), k_l * jnp.where(upper, 1.0, above),
                block * jnp.where(upper, below, above))

    decay = jnp.exp(la)
    H = T // 2
    assert GLA_DV == H
    half_level = level_ref[0:H, 0:H]
    lane_head_st = lax.broadcasted_iota(jnp.int32, (H, GLA_QK), 1) // GLA_DK

    def per_head_rows(a):
        return jnp.concatenate([jnp.where(lane_head_st == h, a, jnp.zeros_like(a)) for h in range(GLA_HEADS)],
                               axis=0)

    def head_blocks(p):
        return [p[:, h * H:(h + 1) * H] for h in range(GLA_HEADS)]

    diag0 = [jnp.zeros((H, H), f32) for _ in range(GLA_HEADS)]
    diag1 = [jnp.zeros((H, H), f32) for _ in range(GLA_HEADS)]

    def add_level(l, q_l, k_l):
        sel = half_level == l
        ql, kl = q_l.astype(bf16), k_l.astype(bf16)
        p0 = head_blocks(_dot_nt(ql[:H], per_head_rows(kl[:H])))
        p1 = head_blocks(_dot_nt(ql[H:], per_head_rows(kl[H:])))
        for h in range(GLA_HEADS):
            diag0[h] = jnp.where(sel, p0[h], diag0[h])
            diag1[h] = jnp.where(sel, p1[h], diag1[h])

    add_level(MIX_LEVELS, q, k)
    q_l, k_l, block = q * decay, k, decay
    for l in range(MIX_LEVELS - 1):
        add_level(l, q_l, k_l)
        q_l, k_l, block = next_level(l, q_l, k_l, block)
        if l % 2 == 1:
            yield
    low = head_blocks(_dot_nt(q_l[H:].astype(bf16), per_head_rows(k_l[:H].astype(bf16))))
    zero_block = jnp.zeros((H, H), f32)
    scores = [jnp.concatenate([jnp.concatenate([diag0[h], zero_block], axis=1),
                               jnp.concatenate([low[h], diag1[h]], axis=1)], axis=0) for h in range(GLA_HEADS)]

    yield

    q_in, k_out, tile_decay = next_level(MIX_LEVELS - 1, q_l, k_l, block)
    st = st_ref[...]
    o_state = _dot_nt(q_in.astype(bf16), per_head_rows(st.astype(bf16)))
    upd = _dot(v.T.astype(bf16), k_out.astype(bf16))
    new_st = st * tile_decay[:H]
    for h in range(GLA_HEADS):
        new_st = new_st + jnp.where(lane_head_st == h, upd[h * GLA_DV:(h + 1) * GLA_DV], 0.0)
    st_ref[...] = new_st
    ggla = ggla_ref[...]
    y_heads = []
    for h in range(GLA_HEADS):
        v_h = v[:, h * GLA_DV:(h + 1) * GLA_DV]
        o = _dot(scores[h].astype(bf16), v_h.astype(bf16)) + o_state[:, h * GLA_DV:(h + 1) * GLA_DV]
        g_h = g[:, h * GLA_DV:(h + 1) * GLA_DV]
        y_heads.append(_rms(o, ggla) * (g_h * jax.nn.sigmoid(g_h)))
        if h % 2 == 1:
            yield

    yield

    c3 = _dot(hb, wc3_ref[...])
    cw = c3.shape[1] // 3
    cb, cu = c3[:, :cw], c3[:, cw:2 * cw] * c3[:, 2 * cw:]
    crow = lax.broadcasted_iota(jnp.int32, (T, cw), 0)
    prev2, prev1 = carry_ref[0:1, :], carry_ref[1:2, :]
    m1 = jnp.where(crow == 0, prev1, _shift_rows(cu, 1))
    m2 = jnp.where(crow == 0, prev2, jnp.where(crow == 1, prev1, _shift_rows(cu, 2)))
    wconv = wconv_ref[...]
    y_conv = cb * (wconv[0:1, :] * m2 + wconv[1:2, :] * m1 + wconv[2:3, :] * cu)
    carry_ref[0:2, :] = cu[T - 2:, :]

    y = jnp.concatenate(y_heads + [y_conv], axis=1).astype(bf16)
    x1 = x + _dot(y, wout_ref[...])

    yield

    h2 = _rms(x1, gmoe_ref[...])
    h2_hi, h2_lo = _split_bf16(h2)
    part = _dot_nt(wrt_ref[...], h2_hi)
    logits = (part[:ROUTE_ROWS] + part[ROUTE_ROWS:] + _dot_nt(wrt_ref[:ROUTE_ROWS, :], h2_lo)) + brt_ref[...]
    gl = [logits[i:i + 1, :] for i in range(N_GROUPS)]
    gmax = functools.reduce(jnp.maximum, gl)
    gsum = functools.reduce(lambda a, b: a + b, [jnp.exp(t - gmax) for t in gl])
    p_grp = 1.0 / gsum
    g_sel = jnp.full_like(gmax, N_GROUPS - 1).astype(jnp.int32)
    for i in reversed(range(N_GROUPS - 1)):
        g_sel = jnp.where(gl[i] == gmax, i, g_sel)
    ig = []
    for j in range(EXPERTS_PER_GROUP):
        acc = jnp.zeros_like(gmax)
        for gi in range(N_GROUPS):
            r0 = ROUTE_EXPERT_ROW0 + gi * EXPERTS_PER_GROUP + j
            acc = acc + jnp.where(g_sel == gi, logits[r0:r0 + 1, :], 0.0)
        ig.append(acc)

    def first_argmax(vals):
        m = functools.reduce(jnp.maximum, vals)
        idx = jnp.full_like(m, len(vals) - 1).astype(jnp.int32)
        for i in reversed(range(len(vals) - 1)):
            idx = jnp.where(vals[i] == m, i, idx)
        return m, idx

    m1_, i1 = first_argmax(ig)
    m2_, i2 = first_argmax([jnp.where(i1 == j, -jnp.inf, ig[j]) for j in range(EXPERTS_PER_GROUP)])
    e21 = jnp.exp(m2_ - m1_)
    w1 = p_grp / (1.0 + e21)
    w2 = p_grp * e21 / (1.0 + e21)
    key = jnp.minimum(i1, i2) * EXPERTS_PER_GROUP + jnp.maximum(i1, i2)
    pair = jnp.zeros_like(key)
    a_loc = jnp.zeros_like(key)
    for kk, pp in PAIR_OF_KEY.items():
        pair = jnp.where(key == kk, pp, pair)
        a_loc = jnp.where(key == kk, PAIR_A[pp], a_loc)
    w_a = jnp.where(i1 == a_loc, w1, w2)
    w_b = jnp.where(i1 == a_loc, w2, w1)
    cls = g_sel * N_PAIRS + pair
    rr = lax.broadcasted_iota(jnp.int32, (LANES, T), 0)
    rec_t = (jnp.where(rr == INFO_WA, jnp.broadcast_to(w_a, (LANES, T)), 0.0)
             + jnp.where(rr == INFO_WB, jnp.broadcast_to(w_b, (LANES, T)), 0.0))
    pieces = [x1[:, c * LANES:(c + 1) * LANES] for c in range(D // LANES)] + [rec_t.T]
    assert len(pieces) == ROW_RECORD
    _store_records(rows_ref, pieces, ROW_RECORD)

    onehot = (rr == jnp.broadcast_to(cls, (LANES, T))).astype(f32)
    count = count_ref[:, 0:1]
    before = _dot_nt(onehot.astype(bf16), tril_ref[...]) - onehot + count
    rank = jnp.sum(onehot * before, axis=0, keepdims=True).astype(jnp.int32)
    r8 = lax.broadcasted_iota(jnp.int32, (8, T), 0)
    meta_ref[0] = jnp.where(r8 == INFO_CLS, jnp.broadcast_to(cls, (8, T)),
                            jnp.where(r8 == INFO_RANK, jnp.broadcast_to(rank, (8, T)), 0))
    new_count = jnp.broadcast_to(count + jnp.sum(onehot, axis=1, keepdims=True), count_ref.shape)
    count_ref[...] = new_count
    counts_ref[...] = new_count.astype(jnp.int32)


def _store_records(ref, pieces, record_rows):
    n = ref.shape[0] // record_rows
    for c, piece in enumerate(pieces):
        ref[pl.ds(c, n, stride=record_rows), :] = piece


def _load_records(ref, first, count, record_rows, lead=()):
    n = ref.shape[-2] // record_rows
    return jnp.concatenate([ref[lead + (pl.ds(first + c, n, stride=record_rows), slice(None))]
                            for c in range(count)], axis=1)


class _RowGather:
    def __init__(self, index_of, src_hbm, buf, sems, record_rows, n_records, n_valid_of=None):
        self.index_of, self.src_hbm, self.buf, self.sems = index_of, src_hbm, buf, sems
        self.rr, self.n, self.n_valid_of = record_rows, n_records, n_valid_of

    def _groups(self, tile, body):
        for g0 in range(0, self.n, GATHER_GROUP):
            if self.n_valid_of is None:
                body(g0)
            else:
                pl.when(g0 < self.n_valid_of(tile))(functools.partial(body, g0))

    def _issue(self, tile, slot):
        rr = self.rr

        def group(g0):
            for r in range(g0, g0 + GATHER_GROUP):
                first = self.index_of(tile * self.n + r) * rr
                pltpu.make_async_copy(self.src_hbm.at[pl.ds(first, rr), :],
                                      self.buf.at[slot, pl.ds(r * rr, rr), :], self.sems.at[slot]).start()

        self._groups(tile, group)

    def start(self, tile, slot):
        if isinstance(slot, int):
            self._issue(tile, slot)
        else:
            for static_slot in range(2):
                pl.when(slot == static_slot)(functools.partial(self._issue, tile, static_slot))

    def wait(self, tile, slot):
        rows = GATHER_GROUP * self.rr

        def group(g0):
            pltpu.make_async_copy(self.src_hbm.at[pl.ds(0, rows), :],
                                  self.buf.at[slot, pl.ds(g0 * self.rr, rows), :], self.sems.at[slot]).wait()

        self._groups(tile, group)


def _expert_kernel(ea_ref, eb_ref, nused_ref, nvalid_ref, tstart_ref, cnt_ref, slot_ref,
                   rows_hbm, gmoe_ref, wga_ref, wua_ref, wda_ref, wgb_ref, wub_ref, wdb_ref,
                   y_ref, buf, sems, src_ref):
    bf16 = jnp.bfloat16
    n_x = gmoe_ref.shape[-1] // LANES
    n_tok = slot_ref.shape[0]
    step, n_used = pl.program_id(0), nused_ref[0]
    gather = _RowGather(lambda i: src_ref[i], rows_hbm, buf, sems, ROW_RECORD, MOE_TILE,
                        n_valid_of=lambda tile: nvalid_ref[tile])

    @pl.when(step == 0)
    def _():
        buf[...] = jnp.zeros_like(buf)
        for c in range(N_CLASSES):
            base = tstart_ref[c] * MOE_TILE

            def pad(r, carry, base=base):
                src_ref[base + r] = 0
                return carry

            lax.fori_loop(cnt_ref[c], (tstart_ref[c + 1] - tstart_ref[c]) * MOE_TILE, pad, 0)

        def place(it, carry):
            for u in range(PLACE_UNROLL):
                t = it * PLACE_UNROLL + u
                src_ref[slot_ref[t]] = t
            return carry

        lax.fori_loop(0, n_tok // PLACE_UNROLL, place, 0)
        gather.start(0, 0)

    slot = step % 2

    @pl.when(step + 1 < n_used)
    def _():
        gather.start(step + 1, 1 - slot)

    @pl.when(step < n_used)
    def _():
        gather.wait(step, slot)
        rec = _load_records(buf, n_x, 1, ROW_RECORD, lead=(slot,))
        h2 = _rms(_load_records(buf, 0, n_x, ROW_RECORD, lead=(slot,)), gmoe_ref[...]).astype(bf16)

        def expert(wg_ref, wu_ref, wd_ref):
            gate = _dot(h2, wg_ref[0])
            hid = (gate * jax.nn.sigmoid(gate)) * _dot(h2, wu_ref[0])
            return _dot(hid.astype(bf16), wd_ref[0])

        y = rec[:, INFO_WA:INFO_WA + 1] * expert(wga_ref, wua_ref, wda_ref)
        y = y + rec[:, INFO_WB:INFO_WB + 1] * expert(wgb_ref, wub_ref, wdb_ref)
        _store_records(y_ref, [y[:, c * LANES:(c + 1) * LANES] for c in range(n_x)], Y_RECORD)

    @pl.when(step >= n_used)
    def _():
        y_ref[...] = jnp.zeros_like(y_ref)


def _ple_final_kernel(slot_ref, x1_ref, p_ref, y_hbm, gple_ref, wpg_ref, wpp_ref, gfin_ref,
                      out_ref, buf, sems):
    bf16 = jnp.bfloat16
    step, n_steps = pl.program_id(0), pl.num_programs(0)
    n_x = gple_ref.shape[-1] // LANES
    gather = _RowGather(lambda t: slot_ref[t], y_hbm, buf, sems, Y_RECORD, PLE_TILE)

    @pl.when(step == 0)
    def _():
        gather.start(0, 0)

    slot = step % 2

    @pl.when(step + 1 < n_steps)
    def _():
        gather.start(step + 1, 1 - slot)

    gather.wait(step, slot)
    x2 = _load_records(x1_ref, 0, n_x, ROW_RECORD) + _load_records(buf, 0, n_x, Y_RECORD, lead=(slot,))
    gate_p = jax.nn.sigmoid(_dot(_rms(x2, gple_ref[...]).astype(bf16), wpg_ref[...]))
    x3 = x2 + gate_p * _dot(p_ref[...].astype(bf16), wpp_ref[...])
    out_ref[...] = _rms(x3, gfin_ref[...])


def _const_spec(shape):
    return pl.BlockSpec(shape, lambda *_: (0,) * len(shape), pipeline_mode=pl.Buffered(1))


def _mixer(x, g_mix, w_in, w_gla_gate, b_gla_gate, g_gla_out, w_conv, w_out, g_moe,
           w_group, b_group, w_router, b_router, w_exp_gate, w_exp_up, w_exp_down):
    b, s, d = x.shape
    step_tokens = MIX_TILE * MIX_TILES_PER_STEP
    n_steps = s // step_tokens
    n_exp, _, de = w_exp_gate.shape
    assert (n_exp * d) % (16 * n_steps) == 0 and (n_exp * de) % (16 * n_steps) == 0
    up_rows, down_rows = n_exp * d // n_steps, n_exp * de // n_steps
    bf16 = jnp.bfloat16
    n_qkvg = 2 * GLA_QK + 2 * GLA_V
    w_qkvg = w_in[:, :n_qkvg].astype(bf16)
    w_a = w_in[:, n_qkvg:n_qkvg + GLA_LOWRANK].astype(bf16)
    w_c3 = w_in[:, n_qkvg + GLA_LOWRANK:].astype(bf16)
    cw = w_c3.shape[1] // 3
    wrt = jnp.zeros((ROUTE_ROWS, d), jnp.float32)
    wrt = wrt.at[:N_GROUPS].set(w_group.T).at[ROUTE_EXPERT_ROW0:ROUTE_EXPERT_ROW0 + N_EXPERTS].set(w_router.T)
    brt = jnp.zeros((ROUTE_ROWS, 1), jnp.float32)
    brt = brt.at[:N_GROUPS, 0].set(b_group).at[ROUTE_EXPERT_ROW0:ROUTE_EXPERT_ROW0 + N_EXPERTS, 0].set(b_router)
    wrt_split = jnp.concatenate(_split_bf16(wrt), axis=0)
    gate_hi, gate_lo = _split_bf16(w_gla_gate)
    w_gate_split = jnp.concatenate([gate_hi, gate_hi, gate_lo], axis=0)
    args = (x, g_mix[None, :], w_qkvg, w_a, w_c3, w_gate_split, b_gla_gate[None, :], g_gla_out[None, :],
            w_conv, w_out.astype(bf16), g_moe[None, :], wrt_split, brt)
    slabs = (w_exp_gate.reshape(n_exp * d, de), w_exp_up.reshape(n_exp * d, de), w_exp_down.reshape(n_exp * de, d))
    slab_specs = [pl.BlockSpec((rows_, width), lambda j: (j, 0))
                  for rows_, width in ((up_rows, de), (up_rows, de), (down_rows, d))]
    in_specs = [pl.BlockSpec((b, step_tokens, d), lambda j: (0, j, 0))]
    in_specs += [_const_spec(a.shape) for a in args[1:]] + slab_specs
    rows, meta, counts, wg16, wu16, wd16 = pl.pallas_call(
        _mixer_kernel,
        grid=(n_steps,),
        in_specs=in_specs,
        out_specs=[pl.BlockSpec((b, step_tokens * ROW_RECORD, LANES), lambda j: (0, j, 0)),
                   pl.BlockSpec((b, MIX_TILES_PER_STEP, 8, MIX_TILE), lambda j: (0, j, 0, 0)),
                   pl.BlockSpec((LANES, LANES), lambda j: (0, 0))] + slab_specs,
        out_shape=[jax.ShapeDtypeStruct((b, s * ROW_RECORD, LANES), jnp.float32),
                   jax.ShapeDtypeStruct((b, s // MIX_TILE, 8, MIX_TILE), jnp.int32),
                   jax.ShapeDtypeStruct((LANES, LANES), jnp.int32)]
                  + [jax.ShapeDtypeStruct(w.shape, bf16) for w in slabs],
        scratch_shapes=[pltpu.VMEM((b, GLA_DV, GLA_QK), jnp.float32),
                        pltpu.VMEM((b, 8, cw), jnp.float32),
                        pltpu.VMEM((LANES, LANES), jnp.float32),
                        pltpu.VMEM((MIX_TILE, MIX_TILE), jnp.int32),
                        pltpu.VMEM((MIX_TILE, MIX_TILE), bf16)],
        compiler_params=pltpu.CompilerParams(dimension_semantics=("arbitrary",),
                                             vmem_limit_bytes=VMEM_LIMIT),
        name="mixer",
    )(*args, *slabs)
    return (rows.reshape(b * s * ROW_RECORD, LANES), meta.reshape(b * s // MIX_TILE, 8, MIX_TILE), counts,
            wg16.reshape(n_exp, d, de), wu16.reshape(n_exp, d, de), wd16.reshape(n_exp, de, d))


def _sort_plan(meta, counts, n_tok):
    i32 = jnp.int32
    n_tiles = n_tok // MOE_TILE + N_CLASSES
    cls = meta[:, INFO_CLS, :].reshape(n_tok)
    rank = meta[:, INFO_RANK, :].reshape(n_tok)
    cnt = counts[:N_CLASSES, 0]
    tiles_per_cls = (cnt + MOE_TILE - 1) // MOE_TILE
    tile_end = jnp.cumsum(tiles_per_cls)
    n_used = tile_end[-1:]
    tstart = jnp.concatenate([tile_end - tiles_per_cls, n_used])
    tile_id = jnp.minimum(jnp.arange(n_tiles, dtype=i32), n_used - 1)
    tile_cls = jnp.sum((tile_id[:, None] >= tile_end[None, :]).astype(i32), axis=1)
    grp, pair = tile_cls // N_PAIRS, tile_cls % N_PAIRS
    e_a, e_b = grp * EXPERTS_PER_GROUP, grp * EXPERTS_PER_GROUP
    for pp in range(N_PAIRS):
        e_a = e_a + jnp.where(pair == pp, PAIR_A[pp], 0)
        e_b = e_b + jnp.where(pair == pp, PAIR_B[pp], 0)
    tile_base = functools.reduce(lambda acc, c: jnp.where(tile_cls == c, tstart[c], acc), range(N_CLASSES),
                                 jnp.zeros_like(tile_cls))
    tile_cnt = functools.reduce(lambda acc, c: jnp.where(tile_cls == c, cnt[c], acc), range(N_CLASSES),
                                jnp.zeros_like(tile_cls))
    n_valid = jnp.clip(tile_cnt - (tile_id - tile_base) * MOE_TILE, 0, MOE_TILE)
    first_tile = functools.reduce(lambda acc, c: jnp.where(cls == c, tstart[c], acc), range(N_CLASSES),
                                  jnp.zeros_like(cls))
    slot = first_tile * MOE_TILE + rank
    return dict(slot=slot, n_valid=n_valid, cnt=cnt, tstart=tstart, e_a=e_a, e_b=e_b, n_used=n_used, n_tiles=n_tiles)


def _experts(plan, rows, g_moe, wg, wu, wd):
    n_tiles = plan["n_tiles"]
    d, de = wg.shape[-2:]
    w_a = lambda shape: pl.BlockSpec(shape, lambda i, ea, eb, *_: (ea[i], 0, 0))
    w_b = lambda shape: pl.BlockSpec(shape, lambda i, ea, eb, *_: (eb[i], 0, 0))
    return pl.pallas_call(
        _expert_kernel,
        grid_spec=pltpu.PrefetchScalarGridSpec(
            num_scalar_prefetch=7, grid=(n_tiles,),
            in_specs=[pl.BlockSpec(memory_space=pl.ANY),
                      pl.BlockSpec((1, d), lambda i, *_: (0, 0)),
                      w_a((1, d, de)), w_a((1, d, de)), w_a((1, de, d)),
                      w_b((1, d, de)), w_b((1, d, de)), w_b((1, de, d))],
            out_specs=pl.BlockSpec((MOE_TILE * Y_RECORD, LANES), lambda i, *_: (i, 0)),
            scratch_shapes=[pltpu.VMEM((2, MOE_TILE * ROW_RECORD, LANES), jnp.float32),
                            pltpu.SemaphoreType.DMA((2,)),
                            pltpu.SMEM((n_tiles * MOE_TILE,), jnp.int32)]),
        out_shape=jax.ShapeDtypeStruct((n_tiles * MOE_TILE * Y_RECORD, LANES), jnp.float32),
        compiler_params=pltpu.CompilerParams(dimension_semantics=("arbitrary",),
                                             vmem_limit_bytes=VMEM_LIMIT),
        name="experts",
    )(plan["e_a"], plan["e_b"], plan["n_used"], plan["n_valid"], plan["tstart"], plan["cnt"], plan["slot"],
      rows, g_moe[None, :], wg, wu, wd, wg, wu, wd)


def _ple_final(plan, rows, p, y_sorted, g_ple, w_ple_gate, w_ple_proj, g_final):
    n_tok, dp = p.shape
    d = w_ple_gate.shape[0]
    bf16 = jnp.bfloat16
    const = lambda shape: pl.BlockSpec(shape, lambda i, *_: (0,) * len(shape))
    tile = lambda width: pl.BlockSpec((PLE_TILE, width), lambda i, *_: (i, 0))
    return pl.pallas_call(
        _ple_final_kernel,
        grid_spec=pltpu.PrefetchScalarGridSpec(
            num_scalar_prefetch=1, grid=(n_tok // PLE_TILE,),
            in_specs=[pl.BlockSpec((PLE_TILE * ROW_RECORD, LANES), lambda i, *_: (i, 0)),
                      tile(dp),
                      pl.BlockSpec(memory_space=pl.ANY),
                      const((1, d)), const((d, d)), const((dp, d)), const((1, d))],
            out_specs=tile(d),
            scratch_shapes=[pltpu.VMEM((2, PLE_TILE * Y_RECORD, LANES), jnp.float32),
                            pltpu.SemaphoreType.DMA((2,))]),
        out_shape=jax.ShapeDtypeStruct((n_tok, d), jnp.float32),
        compiler_params=pltpu.CompilerParams(dimension_semantics=("arbitrary",),
                                             vmem_limit_bytes=VMEM_LIMIT),
        name="ple_final",
    )(plan["slot"], rows, p, y_sorted,
      g_ple[None, :], w_ple_gate.astype(bf16), w_ple_proj.astype(bf16), g_final[None, :])


def kernel(x, p, g_mix, w_in, w_gla_gate, b_gla_gate, g_gla_out, w_conv, w_out, g_moe, w_group, b_group,
           w_router, b_router, w_exp_gate, w_exp_up, w_exp_down, g_ple, w_ple_gate, w_ple_proj, g_final):
    depth = w_in.shape[0]
    assert depth == 1, "the final norm is fused into the last (only) layer"
    b, s, d = x.shape
    n_tok = b * s
    assert s % (MIX_TILE * MIX_TILES_PER_STEP) == 0 and n_tok % MOE_TILE == 0 and n_tok % PLE_TILE == 0 and n_tok % PLACE_UNROLL == 0
    rows, meta, counts, wg16, wu16, wd16 = _mixer(
        x, g_mix[0], w_in[0], w_gla_gate[0], b_gla_gate[0], g_gla_out[0], w_conv[0], w_out[0], g_moe[0],
        w_group[0], b_group[0], w_router[0], b_router[0], w_exp_gate[0], w_exp_up[0], w_exp_down[0])
    plan = _sort_plan(meta, counts, n_tok)
    y_sorted = _experts(plan, rows, g_moe[0], wg16, wu16, wd16)
    out = _ple_final(plan, rows, p[0].reshape(n_tok, -1), y_sorted, g_ple[0], w_ple_gate[0], w_ple_proj[0],
                     g_final)
    return out.reshape(b, s, d)
```

```python
import functools

import jax
import jax.numpy as jnp
from jax import lax
from jax.experimental import pallas as pl
from jax.experimental.pallas import tpu as pltpu

EPS = 1e-6
GLA_HEADS = 4
GLA_DK = 64
GLA_DV = 128
GLA_QK = GLA_HEADS * GLA_DK
GLA_V = GLA_HEADS * GLA_DV
GLA_LOWRANK = 16
GLA_TAU = 16.0
CONV_K = 3
N_GROUPS = 4
EXPERTS_PER_GROUP = 4
N_EXPERTS = N_GROUPS * EXPERTS_PER_GROUP

LANES = 128
MIX_TILE = 256
MIX_LEVELS = 8
MIX_TILES_PER_STEP = 1
ROUTE_ROWS = 32
ROUTE_EXPERT_ROW0 = 8
MOE_TILE = 256
VMEM_LIMIT = 56 * 1024 * 1024

PAIR_A = (0, 2, 2, 0, 0, 1)
PAIR_B = (1, 1, 3, 3, 2, 3)
PAIR_OF_KEY = {1: 0, 6: 1, 11: 2, 3: 3, 2: 4, 7: 5}
N_PAIRS = len(PAIR_A)
N_CLASSES = N_GROUPS * N_PAIRS
INFO_CLS, INFO_RANK, INFO_WA, INFO_WB = 0, 1, 2, 3
PLACE_UNROLL = 8
ROW_RECORD = 9
Y_RECORD = 8
GATHER_GROUP = 32
PLE_TILE = 512

_NT = (((1,), (1,)), ((), ()))
_DONE = object()


def _rms(x, g):
    return x * lax.rsqrt(jnp.mean(x * x, axis=-1, keepdims=True) + EPS) * g


def _dot(a, b):
    return jnp.dot(a, b, preferred_element_type=jnp.float32)


def _dot_nt(a, b):
    return lax.dot_general(a, b, _NT, preferred_element_type=jnp.float32)


def _split_bf16(a):
    hi = a.astype(jnp.bfloat16)
    return hi, (a - hi.astype(jnp.float32)).astype(jnp.bfloat16)


def _shift_rows(x, shift):
    return pltpu.roll(x, shift % x.shape[0], axis=0)


def _mixer_kernel(x_ref, gmix_ref, wqkvg_ref, wa_ref, wc3_ref, wgate_ref, bgate_ref, ggla_ref,
                  wconv_ref, wout_ref, gmoe_ref, wrt_ref, brt_ref, wge_ref, wue_ref, wde_ref,
                  rows_ref, meta_ref, counts_ref, wge16_ref, wue16_ref, wde16_ref,
                  st_ref, carry_ref, count_ref, level_ref, tril_ref):
    bf16 = jnp.bfloat16
    T = MIX_TILE

    wge16_ref[...] = wge_ref[...].astype(bf16)
    wue16_ref[...] = wue_ref[...].astype(bf16)
    wde16_ref[...] = wde_ref[...].astype(bf16)

    @pl.when(pl.program_id(0) == 0)
    def _():
        st_ref[...] = jnp.zeros_like(st_ref)
        carry_ref[...] = jnp.zeros_like(carry_ref)
        count_ref[...] = jnp.zeros_like(count_ref)
        tt = lax.broadcasted_iota(jnp.int32, (T, T), 0)
        ss = lax.broadcasted_iota(jnp.int32, (T, T), 1)
        txs = jnp.bitwise_xor(tt, ss)
        level = jnp.zeros((T, T), jnp.int32)
        for j in range(1, MIX_LEVELS):
            level = level + (txs >= (1 << j)).astype(jnp.int32)
        level_ref[...] = jnp.where(tt > ss, level, jnp.where(tt == ss, MIX_LEVELS, -1))
        tril_ref[...] = (ss <= tt).astype(bf16)

    tiles = [_mixer_tile(x_ref.at[pl.ds(b, 1), pl.ds(i * T, T)], gmix_ref, wqkvg_ref, wa_ref, wc3_ref, wgate_ref,
                         bgate_ref, ggla_ref, wconv_ref, wout_ref, gmoe_ref, wrt_ref, brt_ref,
                         rows_ref.at[b, pl.ds(i * T * ROW_RECORD, T * ROW_RECORD)], meta_ref.at[b, pl.ds(i, 1)],
                         counts_ref, st_ref.at[b], carry_ref.at[b], count_ref, level_ref, tril_ref)
             for i in range(MIX_TILES_PER_STEP) for b in range(x_ref.shape[0])]
    while tiles:
        tiles = [t for t in tiles if next(t, _DONE) is not _DONE]


def _mixer_tile(x_ref, gmix_ref, wqkvg_ref, wa_ref, wc3_ref, wgate_ref, bgate_ref, ggla_ref,
                wconv_ref, wout_ref, gmoe_ref, wrt_ref, brt_ref,
                rows_ref, meta_ref, counts_ref,
                st_ref, carry_ref, count_ref, level_ref, tril_ref):
    f32, bf16 = jnp.float32, jnp.bfloat16
    T = MIX_TILE
    D = x_ref.shape[-1]

    x = x_ref[0]
    hb = _rms(x, gmix_ref[...]).astype(bf16)
    qkvg = _dot(hb, wqkvg_ref[...])
    q = qkvg[:, :GLA_QK] * (GLA_DK ** -0.5)
    k = qkvg[:, GLA_QK:2 * GLA_QK]
    v = qkvg[:, 2 * GLA_QK:2 * GLA_QK + GLA_V]
    g = qkvg[:, 2 * GLA_QK + GLA_V:]
    a_low = _dot(hb, wa_ref[...])
    a_hi, a_lo = _split_bf16(a_low)
    z = _dot(jnp.concatenate([a_hi, a_lo, a_hi], axis=1), wgate_ref[...]) + bgate_ref[...]
    la = (jnp.minimum(z, 0.0) - jnp.log(1.0 + jnp.exp(-jnp.abs(z)))) * (1.0 / GLA_TAU)
    yield

    row = lax.broadcasted_iota(jnp.int32, (T, GLA_QK), 0)

    def next_level(l, q_l, k_l, block):
        upper = ((row >> l) & 1) == 1
        below = _shift_rows(block, 1 << l)
        above = _shift_rows(block, -(1 << l))
        return (q_l * jnp.where(upper, below, 1.0), k_l * jnp.where(upper, 1.0, above),
                block * jnp.where(upper, below, above))

    decay = jnp.exp(la)
    H = T // 2
    assert GLA_DV == H
    half_level = level_ref[0:H, 0:H]
    lane_head_st = lax.broadcasted_iota(jnp.int32, (H, GLA_QK), 1) // GLA_DK

    def per_head_rows(a):
        return jnp.concatenate([jnp.where(lane_head_st == h, a, jnp.zeros_like(a)) for h in range(GLA_HEADS)],
                               axis=0)

    def head_blocks(p):
        return [p[:, h * H:(h + 1) * H] for h in range(GLA_HEADS)]

    diag0 = [jnp.zeros((H, H), f32) for _ in range(GLA_HEADS)]
    diag1 = [jnp.zeros((H, H), f32) for _ in range(GLA_HEADS)]

    def add_level(l, q_l, k_l):
        sel = half_level == l
        ql, kl = q_l.astype(bf16), k_l.astype(bf16)
        p0 = head_blocks(_dot_nt(ql[:H], per_head_rows(kl[:H])))
        p1 = head_blocks(_dot_nt(ql[H:], per_head_rows(kl[H:])))
        for h in range(GLA_HEADS):
            diag0[h] = jnp.where(sel, p0[h], diag0[h])
            diag1[h] = jnp.where(sel, p1[h], diag1[h])

    add_level(MIX_LEVELS, q, k)
    q_l, k_l, block = q * decay, k, decay
    for l in range(MIX_LEVELS - 1):
        add_level(l, q_l, k_l)
        q_l, k_l, block = next_level(l, q_l, k_l, block)
        if l % 2 == 1:
            yield
    low = head_blocks(_dot_nt(q_l[H:].astype(bf16), per_head_rows(k_l[:H].astype(bf16))))
    zero_block = jnp.zeros((H, H), f32)
    scores = [jnp.concatenate([jnp.concatenate([diag0[h], zero_block], axis=1),
                               jnp.concatenate([low[h], diag1[h]], axis=1)], axis=0) for h in range(GLA_HEADS)]

    yield

    q_in, k_out, tile_decay = next_level(MIX_LEVELS - 1, q_l, k_l, block)
    st = st_ref[...]
    o_state = _dot_nt(q_in.astype(bf16), per_head_rows(st.astype(bf16)))
    upd = _dot(v.T.astype(bf16), k_out.astype(bf16))
    new_st = st * tile_decay[:H]
    for h in range(GLA_HEADS):
        new_st = new_st + jnp.where(lane_head_st == h, upd[h * GLA_DV:(h + 1) * GLA_DV], 0.0)
    st_ref[...] = new_st
    ggla = ggla_ref[...]
    y_heads = []
    for h in range(GLA_HEADS):
        v_h = v[:, h * GLA_DV:(h + 1) * GLA_DV]
        o = _dot(scores[h].astype(bf16), v_h.astype(bf16)) + o_state[:, h * GLA_DV:(h + 1) * GLA_DV]
        g_h = g[:, h * GLA_DV:(h + 1) * GLA_DV]
        y_heads.append(_rms(o, ggla) * (g_h * jax.nn.sigmoid(g_h)))
        if h % 2 == 1:
            yield

    yield

    c3 = _dot(hb, wc3_ref[...])
    cw = c3.shape[1] // 3
    cb, cu = c3[:, :cw], c3[:, cw:2 * cw] * c3[:, 2 * cw:]
    crow = lax.broadcasted_iota(jnp.int32, (T, cw), 0)
    prev2, prev1 = carry_ref[0:1, :], carry_ref[1:2, :]
    m1 = jnp.where(crow == 0, prev1, _shift_rows(cu, 1))
    m2 = jnp.where(crow == 0, prev2, jnp.where(crow == 1, prev1, _shift_rows(cu, 2)))
    wconv = wconv_ref[...]
    y_conv = cb * (wconv[0:1, :] * m2 + wconv[1:2, :] * m1 + wconv[2:3, :] * cu)
    carry_ref[0:2, :] = cu[T - 2:, :]

    y = jnp.concatenate(y_heads + [y_conv], axis=1).astype(bf16)
    x1 = x + _dot(y, wout_ref[...])

    yield

    h2 = _rms(x1, gmoe_ref[...])
    h2_hi, h2_lo = _split_bf16(h2)
    part = _dot_nt(wrt_ref[...], h2_hi)
    logits = (part[:ROUTE_ROWS] + part[ROUTE_ROWS:] + _dot_nt(wrt_ref[:ROUTE_ROWS, :], h2_lo)) + brt_ref[...]
    gl = [logits[i:i + 1, :] for i in range(N_GROUPS)]
    gmax = functools.reduce(jnp.maximum, gl)
    gsum = functools.reduce(lambda a, b: a + b, [jnp.exp(t - gmax) for t in gl])
    p_grp = 1.0 / gsum
    g_sel = jnp.full_like(gmax, N_GROUPS - 1).astype(jnp.int32)
    for i in reversed(range(N_GROUPS - 1)):
        g_sel = jnp.where(gl[i] == gmax, i, g_sel)
    ig = []
    for j in range(EXPERTS_PER_GROUP):
        acc = jnp.zeros_like(gmax)
        for gi in range(N_GROUPS):
            r0 = ROUTE_EXPERT_ROW0 + gi * EXPERTS_PER_GROUP + j
            acc = acc + jnp.where(g_sel == gi, logits[r0:r0 + 1, :], 0.0)
        ig.append(acc)

    def first_argmax(vals):
        m = functools.reduce(jnp.maximum, vals)
        idx = jnp.full_like(m, len(vals) - 1).astype(jnp.int32)
        for i in reversed(range(len(vals) - 1)):
            idx = jnp.where(vals[i] == m, i, idx)
        return m, idx

    m1_, i1 = first_argmax(ig)
    m2_, i2 = first_argmax([jnp.where(i1 == j, -jnp.inf, ig[j]) for j in range(EXPERTS_PER_GROUP)])
    e21 = jnp.exp(m2_ - m1_)
    w1 = p_grp / (1.0 + e21)
    w2 = p_grp * e21 / (1.0 + e21)
    key = jnp.minimum(i1, i2) * EXPERTS_PER_GROUP + jnp.maximum(i1, i2)
    pair = jnp.zeros_like(key)
    a_loc = jnp.zeros_like(key)
    for kk, pp in PAIR_OF_KEY.items():
        pair = jnp.where(key == kk, pp, pair)
        a_loc = jnp.where(key == kk, PAIR_A[pp], a_loc)
    w_a = jnp.where(i1 == a_loc, w1, w2)
    w_b = jnp.where(i1 == a_loc, w2, w1)
    cls = g_sel * N_PAIRS + pair
    rr = lax.broadcasted_iota(jnp.int32, (LANES, T), 0)
    rec_t = (jnp.where(rr == INFO_WA, jnp.broadcast_to(w_a, (LANES, T)), 0.0)
             + jnp.where(rr == INFO_WB, jnp.broadcast_to(w_b, (LANES, T)), 0.0))
    pieces = [x1[:, c * LANES:(c + 1) * LANES] for c in range(D // LANES)] + [rec_t.T]
    assert len(pieces) == ROW_RECORD
    _store_records(rows_ref, pieces, ROW_RECORD)

    onehot = (rr == jnp.broadcast_to(cls, (LANES, T))).astype(f32)
    count = count_ref[:, 0:1]
    before = _dot_nt(onehot.astype(bf16), tril_ref[...]) - onehot + count
    rank = jnp.sum(onehot * before, axis=0, keepdims=True).astype(jnp.int32)
    r8 = lax.broadcasted_iota(jnp.int32, (8, T), 0)
    meta_ref[0] = jnp.where(r8 == INFO_CLS, jnp.broadcast_to(cls, (8, T)),
                            jnp.where(r8 == INFO_RANK, jnp.broadcast_to(rank, (8, T)), 0))
    new_count = jnp.broadcast_to(count + jnp.sum(onehot, axis=1, keepdims=True), count_ref.shape)
    count_ref[...] = new_count
    counts_ref[...] = new_count.astype(jnp.int32)


def _store_records(ref, pieces, record_rows):
    n = ref.shape[0] // record_rows
    for c, piece in enumerate(pieces):
        ref[pl.ds(c, n, stride=record_rows), :] = piece


def _load_records(ref, first, count, record_rows, lead=()):
    n = ref.shape[-2] // record_rows
    return jnp.concatenate([ref[lead + (pl.ds(first + c, n, stride=record_rows), slice(None))]
                            for c in range(count)], axis=1)


class _RowGather:
    def __init__(self, index_of, src_hbm, buf, sems, record_rows, n_records, n_valid_of=None):
        self.index_of, self.src_hbm, self.buf, self.sems = index_of, src_hbm, buf, sems
        self.rr, self.n, self.n_valid_of = record_rows, n_records, n_valid_of

    def _groups(self, tile, body):
        for g0 in range(0, self.n, GATHER_GROUP):
            if self.n_valid_of is None:
                body(g0)
            else:
                pl.when(g0 < self.n_valid_of(tile))(functools.partial(body, g0))

    def _issue(self, tile, slot):
        rr = self.rr

        def group(g0):
            for r in range(g0, g0 + GATHER_GROUP):
                first = self.index_of(tile * self.n + r) * rr
                pltpu.make_async_copy(self.src_hbm.at[pl.ds(first, rr), :],
                                      self.buf.at[slot, pl.ds(r * rr, rr), :], self.sems.at[slot]).start()

        self._groups(tile, group)

    def start(self, tile, slot):
        if isinstance(slot, int):
            self._issue(tile, slot)
        else:
            for static_slot in range(2):
                pl.when(slot == static_slot)(functools.partial(self._issue, tile, static_slot))

    def wait(self, tile, slot):
        rows = GATHER_GROUP * self.rr

        def group(g0):
            pltpu.make_async_copy(self.src_hbm.at[pl.ds(0, rows), :],
                                  self.buf.at[slot, pl.ds(g0 * self.rr, rows), :], self.sems.at[slot]).wait()

        self._groups(tile, group)


def _expert_kernel(ea_ref, eb_ref, nused_ref, nvalid_ref, tstart_ref, cnt_ref, slot_ref,
                   rows_hbm, gmoe_ref, wga_ref, wua_ref, wda_ref, wgb_ref, wub_ref, wdb_ref,
                   y_ref, buf, sems, src_ref):
    bf16 = jnp.bfloat16
    n_x = gmoe_ref.shape[-1] // LANES
    n_tok = slot_ref.shape[0]
    step, n_used = pl.program_id(0), nused_ref[0]
    gather = _RowGather(lambda i: src_ref[i], rows_hbm, buf, sems, ROW_RECORD, MOE_TILE,
                        n_valid_of=lambda tile: nvalid_ref[tile])

    @pl.when(step == 0)
    def _():
        buf[...] = jnp.zeros_like(buf)
        for c in range(N_CLASSES):
            base = tstart_ref[c] * MOE_TILE

            def pad(r, carry, base=base):
                src_ref[base + r] = jnp.minimum(base + r, n_tok - 1)
                return carry

            lax.fori_loop(cnt_ref[c], (tstart_ref[c + 1] - tstart_ref[c]) * MOE_TILE, pad, 0)

        def place(it, carry):
            for u in range(PLACE_UNROLL):
                t = it * PLACE_UNROLL + u
                src_ref[slot_ref[t]] = t
            return carry

        lax.fori_loop(0, n_tok // PLACE_UNROLL, place, 0)
        gather.start(0, 0)

    slot = step % 2

    @pl.when(step + 1 < n_used)
    def _():
        gather.start(step + 1, 1 - slot)

    @pl.when(step < n_used)
    def _():
        gather.wait(step, slot)
        rec = _load_records(buf, n_x, 1, ROW_RECORD, lead=(slot,))
        h2 = _rms(_load_records(buf, 0, n_x, ROW_RECORD, lead=(slot,)), gmoe_ref[...]).astype(bf16)

        def expert(wg_ref, wu_ref, wd_ref):
            gate = _dot(h2, wg_ref[0])
            hid = (gate * jax.nn.sigmoid(gate)) * _dot(h2, wu_ref[0])
            return _dot(hid.astype(bf16), wd_ref[0])

        y = rec[:, INFO_WA:INFO_WA + 1] * expert(wga_ref, wua_ref, wda_ref)
        y = y + rec[:, INFO_WB:INFO_WB + 1] * expert(wgb_ref, wub_ref, wdb_ref)
        _store_records(y_ref, [y[:, c * LANES:(c + 1) * LANES] for c in range(n_x)], Y_RECORD)

    @pl.when(step >= n_used)
    def _():
        y_ref[...] = jnp.zeros_like(y_ref)


def _ple_final_kernel(slot_ref, x1_ref, p_ref, y_hbm, gple_ref, wpg_ref, wpp_ref, gfin_ref,
                      out_ref, buf, sems):
    bf16 = jnp.bfloat16
    step, n_steps = pl.program_id(0), pl.num_programs(0)
    n_x = gple_ref.shape[-1] // LANES
    gather = _RowGather(lambda t: slot_ref[t], y_hbm, buf, sems, Y_RECORD, PLE_TILE)

    @pl.when(step == 0)
    def _():
        gather.start(0, 0)

    slot = step % 2

    @pl.when(step + 1 < n_steps)
    def _():
        gather.start(step + 1, 1 - slot)

    gather.wait(step, slot)
    x2 = _load_records(x1_ref, 0, n_x, ROW_RECORD) + _load_records(buf, 0, n_x, Y_RECORD, lead=(slot,))
    gate_p = jax.nn.sigmoid(_dot(_rms(x2, gple_ref[...]).astype(bf16), wpg_ref[...]))
    x3 = x2 + gate_p * _dot(p_ref[...].astype(bf16), wpp_ref[...])
    out_ref[...] = _rms(x3, gfin_ref[...])


def _const_spec(shape):
    return pl.BlockSpec(shape, lambda *_: (0,) * len(shape), pipeline_mode=pl.Buffered(1))


def _mixer(x, g_mix, w_in, w_gla_gate, b_gla_gate, g_gla_out, w_conv, w_out, g_moe,
           w_group, b_group, w_router, b_router, w_exp_gate, w_exp_up, w_exp_down):
    b, s, d = x.shape
    step_tokens = MIX_TILE * MIX_TILES_PER_STEP
    n_steps = s // step_tokens
    n_exp, _, de = w_exp_gate.shape
    assert (n_exp * d) % (16 * n_steps) == 0 and (n_exp * de) % (16 * n_steps) == 0
    up_rows, down_rows = n_exp * d // n_steps, n_exp * de // n_steps
    bf16 = jnp.bfloat16
    n_qkvg = 2 * GLA_QK + 2 * GLA_V
    w_qkvg = w_in[:, :n_qkvg].astype(bf16)
    w_a = w_in[:, n_qkvg:n_qkvg + GLA_LOWRANK].astype(bf16)
    w_c3 = w_in[:, n_qkvg + GLA_LOWRANK:].astype(bf16)
    cw = w_c3.shape[1] // 3
    wrt = jnp.zeros((ROUTE_ROWS, d), jnp.float32)
    wrt = wrt.at[:N_GROUPS].set(w_group.T).at[ROUTE_EXPERT_ROW0:ROUTE_EXPERT_ROW0 + N_EXPERTS].set(w_router.T)
    brt = jnp.zeros((ROUTE_ROWS, 1), jnp.float32)
    brt = brt.at[:N_GROUPS, 0].set(b_group).at[ROUTE_EXPERT_ROW0:ROUTE_EXPERT_ROW0 + N_EXPERTS, 0].set(b_router)
    wrt_split = jnp.concatenate(_split_bf16(wrt), axis=0)
    gate_hi, gate_lo = _split_bf16(w_gla_gate)
    w_gate_split = jnp.concatenate([gate_hi, gate_hi, gate_lo], axis=0)
    args = (x, g_mix[None, :], w_qkvg, w_a, w_c3, w_gate_split, b_gla_gate[None, :], g_gla_out[None, :],
            w_conv, w_out.astype(bf16), g_moe[None, :], wrt_split, brt)
    slabs = (w_exp_gate.reshape(n_exp * d, de), w_exp_up.reshape(n_exp * d, de), w_exp_down.reshape(n_exp * de, d))
    slab_specs = [pl.BlockSpec((rows_, width), lambda j: (j, 0))
                  for rows_, width in ((up_rows, de), (up_rows, de), (down_rows, d))]
    in_specs = [pl.BlockSpec((b, step_tokens, d), lambda j: (0, j, 0))]
    in_specs += [_const_spec(a.shape) for a in args[1:]] + slab_specs
    rows, meta, counts, wg16, wu16, wd16 = pl.pallas_call(
        _mixer_kernel,
        grid=(n_steps,),
        in_specs=in_specs,
        out_specs=[pl.BlockSpec((b, step_tokens * ROW_RECORD, LANES), lambda j: (0, j, 0)),
                   pl.BlockSpec((b, MIX_TILES_PER_STEP, 8, MIX_TILE), lambda j: (0, j, 0, 0)),
                   pl.BlockSpec((LANES, LANES), lambda j: (0, 0))] + slab_specs,
        out_shape=[jax.ShapeDtypeStruct((b, s * ROW_RECORD, LANES), jnp.float32),
                   jax.ShapeDtypeStruct((b, s // MIX_TILE, 8, MIX_TILE), jnp.int32),
                   jax.ShapeDtypeStruct((LANES, LANES), jnp.int32)]
                  + [jax.ShapeDtypeStruct(w.shape, bf16) for w in slabs],
        scratch_shapes=[pltpu.VMEM((b, GLA_DV, GLA_QK), jnp.float32),
                        pltpu.VMEM((b, 8, cw), jnp.float32),
                        pltpu.VMEM((LANES, LANES), jnp.float32),
                        pltpu.VMEM((MIX_TILE, MIX_TILE), jnp.int32),
                        pltpu.VMEM((MIX_TILE, MIX_TILE), bf16)],
        compiler_params=pltpu.CompilerParams(dimension_semantics=("arbitrary",),
                                             vmem_limit_bytes=VMEM_LIMIT),
        name="mixer",
    )(*args, *slabs)
    return (rows.reshape(b * s * ROW_RECORD, LANES), meta.reshape(b * s // MIX_TILE, 8, MIX_TILE), counts,
            wg16.reshape(n_exp, d, de), wu16.reshape(n_exp, d, de), wd16.reshape(n_exp, de, d))


def _sort_plan(meta, counts, n_tok):
    i32 = jnp.int32
    n_tiles = n_tok // MOE_TILE + N_CLASSES
    cls = meta[:, INFO_CLS, :].reshape(n_tok)
    rank = meta[:, INFO_RANK, :].reshape(n_tok)
    cnt = counts[:N_CLASSES, 0]
    tiles_per_cls = (cnt + MOE_TILE - 1) // MOE_TILE
    tile_end = jnp.cumsum(tiles_per_cls)
    n_used = tile_end[-1:]
    tstart = jnp.concatenate([tile_end - tiles_per_cls, n_used])
    tile_id = jnp.minimum(jnp.arange(n_tiles, dtype=i32), n_used - 1)
    tile_cls = jnp.sum((tile_id[:, None] >= tile_end[None, :]).astype(i32), axis=1)
    grp, pair = tile_cls // N_PAIRS, tile_cls % N_PAIRS
    e_a, e_b = grp * EXPERTS_PER_GROUP, grp * EXPERTS_PER_GROUP
    for pp in range(N_PAIRS):
        e_a = e_a + jnp.where(pair == pp, PAIR_A[pp], 0)
        e_b = e_b + jnp.where(pair == pp, PAIR_B[pp], 0)
    of_tile = (tile_cls[:, None] == jnp.arange(N_CLASSES, dtype=i32)[None, :]).astype(i32)
    tile_base = jnp.sum(of_tile * tstart[None, :N_CLASSES], axis=1)
    tile_cnt = jnp.sum(of_tile * cnt[None, :], axis=1)
    n_valid = jnp.clip(tile_cnt - (tile_id - tile_base) * MOE_TILE, 0, MOE_TILE)
    first_tile = functools.reduce(lambda acc, c: jnp.where(cls == c, tstart[c], acc), range(N_CLASSES),
                                  jnp.zeros_like(cls))
    slot = first_tile * MOE_TILE + rank
    return dict(slot=slot, n_valid=n_valid, cnt=cnt, tstart=tstart, e_a=e_a, e_b=e_b, n_used=n_used, n_tiles=n_tiles)


def _experts(plan, rows, g_moe, wg, wu, wd):
    n_tiles = plan["n_tiles"]
    d, de = wg.shape[-2:]
    w_a = lambda shape: pl.BlockSpec(shape, lambda i, ea, eb, *_: (ea[i], 0, 0))
    w_b = lambda shape: pl.BlockSpec(shape, lambda i, ea, eb, *_: (eb[i], 0, 0))
    return pl.pallas_call(
        _expert_kernel,
        grid_spec=pltpu.PrefetchScalarGridSpec(
            num_scalar_prefetch=7, grid=(n_tiles,),
            in_specs=[pl.BlockSpec(memory_space=pl.ANY),
                      pl.BlockSpec((1, d), lambda i, *_: (0, 0)),
                      w_a((1, d, de)), w_a((1, d, de)), w_a((1, de, d)),
                      w_b((1, d, de)), w_b((1, d, de)), w_b((1, de, d))],
            out_specs=pl.BlockSpec((MOE_TILE * Y_RECORD, LANES), lambda i, *_: (i, 0)),
            scratch_shapes=[pltpu.VMEM((2, MOE_TILE * ROW_RECORD, LANES), jnp.float32),
                            pltpu.SemaphoreType.DMA((2,)),
                            pltpu.SMEM((n_tiles * MOE_TILE,), jnp.int32)]),
        out_shape=jax.ShapeDtypeStruct((n_tiles * MOE_TILE * Y_RECORD, LANES), jnp.float32),
        compiler_params=pltpu.CompilerParams(dimension_semantics=("arbitrary",),
                                             vmem_limit_bytes=VMEM_LIMIT),
        name="experts",
    )(plan["e_a"], plan["e_b"], plan["n_used"], plan["n_valid"], plan["tstart"], plan["cnt"], plan["slot"],
      rows, g_moe[None, :], wg, wu, wd, wg, wu, wd)


def _ple_final(plan, rows, p, y_sorted, g_ple, w_ple_gate, w_ple_proj, g_final):
    n_tok, dp = p.shape
    d = w_ple_gate.shape[0]
    bf16 = jnp.bfloat16
    const = lambda shape: pl.BlockSpec(shape, lambda i, *_: (0,) * len(shape))
    tile = lambda width: pl.BlockSpec((PLE_TILE, width), lambda i, *_: (i, 0))
    return pl.pallas_call(
        _ple_final_kernel,
        grid_spec=pltpu.PrefetchScalarGridSpec(
            num_scalar_prefetch=1, grid=(n_tok // PLE_TILE,),
            in_specs=[pl.BlockSpec((PLE_TILE * ROW_RECORD, LANES), lambda i, *_: (i, 0)),
                      tile(dp),
                      pl.BlockSpec(memory_space=pl.ANY),
                      const((1, d)), const((d, d)), const((dp, d)), const((1, d))],
            out_specs=tile(d),
            scratch_shapes=[pltpu.VMEM((2, PLE_TILE * Y_RECORD, LANES), jnp.float32),
                            pltpu.SemaphoreType.DMA((2,))]),
        out_shape=jax.ShapeDtypeStruct((n_tok, d), jnp.float32),
        compiler_params=pltpu.CompilerParams(dimension_semantics=("arbitrary",),
                                             vmem_limit_bytes=VMEM_LIMIT),
        name="ple_final",
    )(plan["slot"], rows, p, y_sorted,
      g_ple[None, :], w_ple_gate.astype(bf16), w_ple_proj.astype(bf16), g_final[None, :])


def kernel(x, p, g_mix, w_in, w_gla_gate, b_gla_gate, g_gla_out, w_conv, w_out, g_moe, w_group, b_group,
           w_router, b_router, w_exp_gate, w_exp_up, w_exp_down, g_ple, w_ple_gate, w_ple_proj, g_final):
    depth = w_in.shape[0]
    assert depth == 1, "the final norm is fused into the last (only) layer"
    b, s, d = x.shape
    n_tok = b * s
    assert s % (MIX_TILE * MIX_TILES_PER_STEP) == 0 and n_tok % MOE_TILE == 0 and n_tok % PLE_TILE == 0 and n_tok % PLACE_UNROLL == 0
    rows, meta, counts, wg16, wu16, wd16 = _mixer(
        x, g_mix[0], w_in[0], w_gla_gate[0], b_gla_gate[0], g_gla_out[0], w_conv[0], w_out[0], g_moe[0],
        w_group[0], b_group[0], w_router[0], b_router[0], w_exp_gate[0], w_exp_up[0], w_exp_down[0])
    plan = _sort_plan(meta, counts, n_tok)
    y_sorted = _experts(plan, rows, g_moe[0], wg16, wu16, wd16)
    out = _ple_final(plan, rows, p[0].reshape(n_tok, -1), y_sorted, g_ple[0], w_ple_gate[0], w_ple_proj[0],
                     g_final)
    return out.reshape(b, s, d)
```

```python
import functools

import jax
import jax.numpy as jnp
from jax import lax
from jax.experimental import pallas as pl
from jax.experimental.pallas import tpu as pltpu

EPS = 1e-6
GLA_HEADS = 4
GLA_DK = 64
GLA_DV = 128
GLA_QK = GLA_HEADS * GLA_DK
GLA_V = GLA_HEADS * GLA_DV
GLA_LOWRANK = 16
GLA_TAU = 16.0
CONV_K = 3
N_GROUPS = 4
EXPERTS_PER_GROUP = 4
N_EXPERTS = N_GROUPS * EXPERTS_PER_GROUP

LANES = 128
MIX_TILE = 256
MIX_LEVELS = 8
MIX_TILES_PER_STEP = 1
ROUTE_ROWS = 32
ROUTE_EXPERT_ROW0 = 8
MOE_TILE = 256
VMEM_LIMIT = 56 * 1024 * 1024

PAIR_A = (0, 2, 2, 0, 0, 1)
PAIR_B = (1, 1, 3, 3, 2, 3)
PAIR_OF_KEY = {1: 0, 6: 1, 11: 2, 3: 3, 2: 4, 7: 5}
N_PAIRS = len(PAIR_A)
N_CLASSES = N_GROUPS * N_PAIRS
INFO_CLS, INFO_RANK, INFO_WA, INFO_WB = 0, 1, 2, 3
PLACE_UNROLL = 8
ROW_RECORD = 9
Y_RECORD = 8
GATHER_GROUP = 32
PLE_TILE = 512

_NT = (((1,), (1,)), ((), ()))
_DONE = object()


def _rms(x, g):
    return x * lax.rsqrt(jnp.mean(x * x, axis=-1, keepdims=True) + EPS) * g


def _dot(a, b):
    return jnp.dot(a, b, preferred_element_type=jnp.float32)


def _dot_nt(a, b):
    return lax.dot_general(a, b, _NT, preferred_element_type=jnp.float32)


def _split_bf16(a):
    hi = a.astype(jnp.bfloat16)
    return hi, (a - hi.astype(jnp.float32)).astype(jnp.bfloat16)


def _shift_rows(x, shift):
    return pltpu.roll(x, shift % x.shape[0], axis=0)


def _mixer_kernel(x_ref, gmix_ref, wqkvg_ref, wa_ref, wc3_ref, wgate_ref, bgate_ref, ggla_ref,
                  wconv_ref, wout_ref, gmoe_ref, wrt_ref, brt_ref, wge_ref, wue_ref, wde_ref,
                  rows_ref, meta_ref, counts_ref, wge16_ref, wue16_ref, wde16_ref,
                  st_ref, carry_ref, count_ref, level_ref, tril_ref):
    bf16 = jnp.bfloat16
    T = MIX_TILE

    wge16_ref[...] = wge_ref[...].astype(bf16)
    wue16_ref[...] = wue_ref[...].astype(bf16)
    wde16_ref[...] = wde_ref[...].astype(bf16)

    @pl.when(pl.program_id(0) == 0)
    def _():
        st_ref[...] = jnp.zeros_like(st_ref)
        carry_ref[...] = jnp.zeros_like(carry_ref)
        count_ref[...] = jnp.zeros_like(count_ref)
        tt = lax.broadcasted_iota(jnp.int32, (T, T), 0)
        ss = lax.broadcasted_iota(jnp.int32, (T, T), 1)
        txs = jnp.bitwise_xor(tt, ss)
        level = jnp.zeros((T, T), jnp.int32)
        for j in range(1, MIX_LEVELS):
            level = level + (txs >= (1 << j)).astype(jnp.int32)
        level_ref[...] = jnp.where(tt > ss, level, jnp.where(tt == ss, MIX_LEVELS, -1))
        tril_ref[...] = (ss <= tt).astype(bf16)

    tiles = [_mixer_tile(x_ref.at[pl.ds(b, 1), pl.ds(i * T, T)], gmix_ref, wqkvg_ref, wa_ref, wc3_ref, wgate_ref,
                         bgate_ref, ggla_ref, wconv_ref, wout_ref, gmoe_ref, wrt_ref, brt_ref,
                         rows_ref.at[b, pl.ds(i * T * ROW_RECORD, T * ROW_RECORD)], meta_ref.at[b, pl.ds(i, 1)],
                         counts_ref, st_ref.at[b], carry_ref.at[b], count_ref, level_ref, tril_ref)
             for i in range(MIX_TILES_PER_STEP) for b in range(x_ref.shape[0])]
    while tiles:
        tiles = [t for t in tiles if next(t, _DONE) is not _DONE]


def _mixer_tile(x_ref, gmix_ref, wqkvg_ref, wa_ref, wc3_ref, wgate_ref, bgate_ref, ggla_ref,
                wconv_ref, wout_ref, gmoe_ref, wrt_ref, brt_ref,
                rows_ref, meta_ref, counts_ref,
                st_ref, carry_ref, count_ref, level_ref, tril_ref):
    f32, bf16 = jnp.float32, jnp.bfloat16
    T = MIX_TILE
    D = x_ref.shape[-1]

    x = x_ref[0]
    hb = _rms(x, gmix_ref[...]).astype(bf16)
    qkvg = _dot(hb, wqkvg_ref[...])
    q = qkvg[:, :GLA_QK] * (GLA_DK ** -0.5)
    k = qkvg[:, GLA_QK:2 * GLA_QK]
    v = qkvg[:, 2 * GLA_QK:2 * GLA_QK + GLA_V]
    g = qkvg[:, 2 * GLA_QK + GLA_V:]
    a_low = _dot(hb, wa_ref[...])
    a_hi, a_lo = _split_bf16(a_low)
    z = _dot(jnp.concatenate([a_hi, a_lo, a_hi], axis=1), wgate_ref[...]) + bgate_ref[...]
    la = (jnp.minimum(z, 0.0) - jnp.log(1.0 + jnp.exp(-jnp.abs(z)))) * (1.0 / GLA_TAU)
    yield

    row = lax.broadcasted_iota(jnp.int32, (T, GLA_QK), 0)

    def next_level(l, q_l, k_l, block):
        upper = ((row >> l) & 1) == 1
        below = _shift_rows(block, 1 << l)
        above = _shift_rows(block, -(1 << l))
        return (q_l * jnp.where(upper, below, 1.0), k_l * jnp.where(upper, 1.0, above),
                block * jnp.where(upper, below, above))

    decay = jnp.exp(la)
    H = T // 2
    assert GLA_DV == H
    half_level = level_ref[0:H, 0:H]
    lane_head_st = lax.broadcasted_iota(jnp.int32, (H, GLA_QK), 1) // GLA_DK

    heads_per_tile = LANES // GLA_DK
    lane_head_tile = lax.broadcasted_iota(jnp.int32, (H, LANES), 1) // GLA_DK

    def head_scores(q_half, k_half):
        out = []
        for tile in range(GLA_QK // LANES):
            q_t = q_half[:, tile * LANES:(tile + 1) * LANES]
            k_t = k_half[:, tile * LANES:(tile + 1) * LANES]
            k_rows = jnp.concatenate([jnp.where(lane_head_tile == j, k_t, jnp.zeros_like(k_t))
                                      for j in range(heads_per_tile)], axis=0)
            p = _dot_nt(q_t, k_rows)
            out += [p[:, j * H:(j + 1) * H] for j in range(heads_per_tile)]
        return out

    diag0 = [jnp.zeros((H, H), f32) for _ in range(GLA_HEADS)]
    diag1 = [jnp.zeros((H, H), f32) for _ in range(GLA_HEADS)]

    def add_level(l, q_l, k_l):
        sel = half_level == l
        ql, kl = q_l.astype(bf16), k_l.astype(bf16)
        p0 = head_scores(ql[:H], kl[:H])
        p1 = head_scores(ql[H:], kl[H:])
        for h in range(GLA_HEADS):
            diag0[h] = jnp.where(sel, p0[h], diag0[h])
            diag1[h] = jnp.where(sel, p1[h], diag1[h])

    add_level(MIX_LEVELS, q, k)
    q_l, k_l, block = q * decay, k, decay
    for l in range(MIX_LEVELS - 1):
        add_level(l, q_l, k_l)
        q_l, k_l, block = next_level(l, q_l, k_l, block)
        yield
    low = head_scores(q_l[H:].astype(bf16), k_l[:H].astype(bf16))
    zero_block = jnp.zeros((H, H), f32)
    scores = [jnp.concatenate([jnp.concatenate([diag0[h], zero_block], axis=1),
                               jnp.concatenate([low[h], diag1[h]], axis=1)], axis=0) for h in range(GLA_HEADS)]

    yield

    q_in, k_out, tile_decay = next_level(MIX_LEVELS - 1, q_l, k_l, block)
    st = st_ref[...]
    o_state = head_scores(q_in.astype(bf16), st.astype(bf16))
    upd = _dot(v.T.astype(bf16), k_out.astype(bf16))
    new_st = st * tile_decay[:H]
    for h in range(GLA_HEADS):
        new_st = new_st + jnp.where(lane_head_st == h, upd[h * GLA_DV:(h + 1) * GLA_DV], 0.0)
    st_ref[...] = new_st
    ggla = ggla_ref[...]
    y_heads = []
    for h in range(GLA_HEADS):
        v_h = v[:, h * GLA_DV:(h + 1) * GLA_DV]
        o = _dot(scores[h].astype(bf16), v_h.astype(bf16)) + o_state[h]
        g_h = g[:, h * GLA_DV:(h + 1) * GLA_DV]
        y_heads.append(_rms(o, ggla) * (g_h * jax.nn.sigmoid(g_h)))
        yield

    yield

    c3 = _dot(hb, wc3_ref[...])
    cw = c3.shape[1] // 3
    cb, cu = c3[:, :cw], c3[:, cw:2 * cw] * c3[:, 2 * cw:]
    crow = lax.broadcasted_iota(jnp.int32, (T, cw), 0)
    prev2, prev1 = carry_ref[0:1, :], carry_ref[1:2, :]
    m1 = jnp.where(crow == 0, prev1, _shift_rows(cu, 1))
    m2 = jnp.where(crow == 0, prev2, jnp.where(crow == 1, prev1, _shift_rows(cu, 2)))
    wconv = wconv_ref[...]
    y_conv = cb * (wconv[0:1, :] * m2 + wconv[1:2, :] * m1 + wconv[2:3, :] * cu)
    carry_ref[0:2, :] = cu[T - 2:, :]

    y = jnp.concatenate(y_heads + [y_conv], axis=1).astype(bf16)
    x1 = x + _dot(y, wout_ref[...])

    yield

    h2 = _rms(x1, gmoe_ref[...])
    h2_hi, h2_lo = _split_bf16(h2)
    part = _dot_nt(wrt_ref[...], h2_hi)
    logits = (part[:ROUTE_ROWS] + part[ROUTE_ROWS:] + _dot_nt(wrt_ref[:ROUTE_ROWS, :], h2_lo)) + brt_ref[...]
    gl = [logits[i:i + 1, :] for i in range(N_GROUPS)]
    gmax = functools.reduce(jnp.maximum, gl)
    gsum = functools.reduce(lambda a, b: a + b, [jnp.exp(t - gmax) for t in gl])
    p_grp = 1.0 / gsum
    g_sel = jnp.full_like(gmax, N_GROUPS - 1).astype(jnp.int32)
    for i in reversed(range(N_GROUPS - 1)):
        g_sel = jnp.where(gl[i] == gmax, i, g_sel)
    ig = []
    for j in range(EXPERTS_PER_GROUP):
        acc = jnp.zeros_like(gmax)
        for gi in range(N_GROUPS):
            r0 = ROUTE_EXPERT_ROW0 + gi * EXPERTS_PER_GROUP + j
            acc = acc + jnp.where(g_sel == gi, logits[r0:r0 + 1, :], 0.0)
        ig.append(acc)

    def first_argmax(vals):
        m = functools.reduce(jnp.maximum, vals)
        idx = jnp.full_like(m, len(vals) - 1).astype(jnp.int32)
        for i in reversed(range(len(vals) - 1)):
            idx = jnp.where(vals[i] == m, i, idx)
        return m, idx

    m1_, i1 = first_argmax(ig)
    m2_, i2 = first_argmax([jnp.where(i1 == j, -jnp.inf, ig[j]) for j in range(EXPERTS_PER_GROUP)])
    e21 = jnp.exp(m2_ - m1_)
    w1 = p_grp / (1.0 + e21)
    w2 = p_grp * e21 / (1.0 + e21)
    key = jnp.minimum(i1, i2) * EXPERTS_PER_GROUP + jnp.maximum(i1, i2)
    pair = jnp.zeros_like(key)
    a_loc = jnp.zeros_like(key)
    for kk, pp in PAIR_OF_KEY.items():
        pair = jnp.where(key == kk, pp, pair)
        a_loc = jnp.where(key == kk, PAIR_A[pp], a_loc)
    w_a = jnp.where(i1 == a_loc, w1, w2)
    w_b = jnp.where(i1 == a_loc, w2, w1)
    cls = g_sel * N_PAIRS + pair
    rr = lax.broadcasted_iota(jnp.int32, (LANES, T), 0)
    rec_t = (jnp.where(rr == INFO_WA, jnp.broadcast_to(w_a, (LANES, T)), 0.0)
             + jnp.where(rr == INFO_WB, jnp.broadcast_to(w_b, (LANES, T)), 0.0))
    pieces = [x1[:, c * LANES:(c + 1) * LANES] for c in range(D // LANES)] + [rec_t.T]
    assert len(pieces) == ROW_RECORD
    _store_records(rows_ref, pieces, ROW_RECORD)

    onehot = (rr == jnp.broadcast_to(cls, (LANES, T))).astype(f32)
    count = count_ref[:, 0:1]
    before = _dot_nt(onehot.astype(bf16), tril_ref[...]) - onehot + count
    rank = jnp.sum(onehot * before, axis=0, keepdims=True).astype(jnp.int32)
    r8 = lax.broadcasted_iota(jnp.int32, (8, T), 0)
    meta_ref[0] = jnp.where(r8 == INFO_CLS, jnp.broadcast_to(cls, (8, T)),
                            jnp.where(r8 == INFO_RANK, jnp.broadcast_to(rank, (8, T)), 0))
    new_count = jnp.broadcast_to(count + jnp.sum(onehot, axis=1, keepdims=True), count_ref.shape)
    count_ref[...] = new_count
    counts_ref[...] = new_count.astype(jnp.int32)


def _store_records(ref, pieces, record_rows):
    n = ref.shape[0] // record_rows
    for c, piece in enumerate(pieces):
        ref[pl.ds(c, n, stride=record_rows), :] = piece


def _load_records(ref, first, count, record_rows, lead=()):
    n = ref.shape[-2] // record_rows
    return jnp.concatenate([ref[lead + (pl.ds(first + c, n, stride=record_rows), slice(None))]
                            for c in range(count)], axis=1)


class _RowGather:
    def __init__(self, index_of, src_hbm, buf, sems, record_rows, n_records, n_valid_of=None):
        self.index_of, self.src_hbm, self.buf, self.sems = index_of, src_hbm, buf, sems
        self.rr, self.n, self.n_valid_of = record_rows, n_records, n_valid_of

    def _groups(self, tile, body):
        for g0 in range(0, self.n, GATHER_GROUP):
            if self.n_valid_of is None:
                body(g0)
            else:
                pl.when(g0 < self.n_valid_of(tile))(functools.partial(body, g0))

    def _issue(self, tile, slot):
        rr = self.rr

        def group(g0):
            for r in range(g0, g0 + GATHER_GROUP):
                first = self.index_of(tile * self.n + r) * rr
                pltpu.make_async_copy(self.src_hbm.at[pl.ds(first, rr), :],
                                      self.buf.at[slot, pl.ds(r * rr, rr), :], self.sems.at[slot]).start()

        self._groups(tile, group)

    def start(self, tile, slot):
        if isinstance(slot, int):
            self._issue(tile, slot)
        else:
            for static_slot in range(2):
                pl.when(slot == static_slot)(functools.partial(self._issue, tile, static_slot))

    def wait(self, tile, slot):
        rows = GATHER_GROUP * self.rr

        def group(g0):
            pltpu.make_async_copy(self.src_hbm.at[pl.ds(0, rows), :],
                                  self.buf.at[slot, pl.ds(g0 * self.rr, rows), :], self.sems.at[slot]).wait()

        self._groups(tile, group)


def _expert_kernel(ea_ref, eb_ref, nused_ref, nvalid_ref, tstart_ref, cnt_ref, slot_ref,
                   rows_hbm, gmoe_ref, wga_ref, wua_ref, wda_ref, wgb_ref, wub_ref, wdb_ref,
                   y_ref, buf, sems, src_ref):
    bf16 = jnp.bfloat16
    n_x = gmoe_ref.shape[-1] // LANES
    n_tok = slot_ref.shape[0]
    step, n_used = pl.program_id(0), nused_ref[0]
    gather = _RowGather(lambda i: src_ref[i], rows_hbm, buf, sems, ROW_RECORD, MOE_TILE,
                        n_valid_of=lambda tile: nvalid_ref[tile])

    @pl.when(step == 0)
    def _():
        buf[...] = jnp.zeros_like(buf)
        for c in range(N_CLASSES):
            base = tstart_ref[c] * MOE_TILE

            def pad(r, carry, base=base):
                src_ref[base + r] = jnp.minimum(base + r, n_tok - 1)
                return carry

            lax.fori_loop(cnt_ref[c], (tstart_ref[c + 1] - tstart_ref[c]) * MOE_TILE, pad, 0)

        def place(it, carry):
            for u in range(PLACE_UNROLL):
                t = it * PLACE_UNROLL + u
                src_ref[slot_ref[t]] = t
            return carry

        lax.fori_loop(0, n_tok // PLACE_UNROLL, place, 0)
        gather.start(0, 0)

    slot = step % 2

    @pl.when(step + 1 < n_used)
    def _():
        gather.start(step + 1, 1 - slot)

    @pl.when(step < n_used)
    def _():
        gather.wait(step, slot)
        rec = _load_records(buf, n_x, 1, ROW_RECORD, lead=(slot,))
        h2 = _rms(_load_records(buf, 0, n_x, ROW_RECORD, lead=(slot,)), gmoe_ref[...]).astype(bf16)

        def expert(wg_ref, wu_ref, wd_ref):
            gate = _dot(h2, wg_ref[0])
            hid = (gate * jax.nn.sigmoid(gate)) * _dot(h2, wu_ref[0])
            return _dot(hid.astype(bf16), wd_ref[0])

        y = rec[:, INFO_WA:INFO_WA + 1] * expert(wga_ref, wua_ref, wda_ref)
        y = y + rec[:, INFO_WB:INFO_WB + 1] * expert(wgb_ref, wub_ref, wdb_ref)
        _store_records(y_ref, [y[:, c * LANES:(c + 1) * LANES] for c in range(n_x)], Y_RECORD)

    @pl.when(step >= n_used)
    def _():
        y_ref[...] = jnp.zeros_like(y_ref)


def _ple_final_kernel(slot_ref, x1_ref, p_ref, y_hbm, gple_ref, wpg_ref, wpp_ref, gfin_ref,
                      out_ref, buf, sems):
    bf16 = jnp.bfloat16
    step, n_steps = pl.program_id(0), pl.num_programs(0)
    n_x = gple_ref.shape[-1] // LANES
    gather = _RowGather(lambda t: slot_ref[t], y_hbm, buf, sems, Y_RECORD, PLE_TILE)

    @pl.when(step == 0)
    def _():
        gather.start(0, 0)

    slot = step % 2

    @pl.when(step + 1 < n_steps)
    def _():
        gather.start(step + 1, 1 - slot)

    gather.wait(step, slot)
    x2 = _load_records(x1_ref, 0, n_x, ROW_RECORD) + _load_records(buf, 0, n_x, Y_RECORD, lead=(slot,))
    gate_p = jax.nn.sigmoid(_dot(_rms(x2, gple_ref[...]).astype(bf16), wpg_ref[...]))
    x3 = x2 + gate_p * _dot(p_ref[...].astype(bf16), wpp_ref[...])
    out_ref[...] = _rms(x3, gfin_ref[...])


def _const_spec(shape):
    return pl.BlockSpec(shape, lambda *_: (0,) * len(shape), pipeline_mode=pl.Buffered(1))


def _mixer(x, g_mix, w_in, w_gla_gate, b_gla_gate, g_gla_out, w_conv, w_out, g_moe,
           w_group, b_group, w_router, b_router, w_exp_gate, w_exp_up, w_exp_down):
    b, s, d = x.shape
    step_tokens = MIX_TILE * MIX_TILES_PER_STEP
    n_steps = s // step_tokens
    n_exp, _, de = w_exp_gate.shape
    assert (n_exp * d) % (16 * n_steps) == 0 and (n_exp * de) % (16 * n_steps) == 0
    up_rows, down_rows = n_exp * d // n_steps, n_exp * de // n_steps
    bf16 = jnp.bfloat16
    n_qkvg = 2 * GLA_QK + 2 * GLA_V
    w_qkvg = w_in[:, :n_qkvg].astype(bf16)
    w_a = w_in[:, n_qkvg:n_qkvg + GLA_LOWRANK].astype(bf16)
    w_c3 = w_in[:, n_qkvg + GLA_LOWRANK:].astype(bf16)
    cw = w_c3.shape[1] // 3
    wrt = jnp.zeros((ROUTE_ROWS, d), jnp.float32)
    wrt = wrt.at[:N_GROUPS].set(w_group.T).at[ROUTE_EXPERT_ROW0:ROUTE_EXPERT_ROW0 + N_EXPERTS].set(w_router.T)
    brt = jnp.zeros((ROUTE_ROWS, 1), jnp.float32)
    brt = brt.at[:N_GROUPS, 0].set(b_group).at[ROUTE_EXPERT_ROW0:ROUTE_EXPERT_ROW0 + N_EXPERTS, 0].set(b_router)
    wrt_split = jnp.concatenate(_split_bf16(wrt), axis=0)
    gate_hi, gate_lo = _split_bf16(w_gla_gate)
    w_gate_split = jnp.concatenate([gate_hi, gate_hi, gate_lo], axis=0)
    args = (x, g_mix[None, :], w_qkvg, w_a, w_c3, w_gate_split, b_gla_gate[None, :], g_gla_out[None, :],
            w_conv, w_out.astype(bf16), g_moe[None, :], wrt_split, brt)
    slabs = (w_exp_gate.reshape(n_exp * d, de), w_exp_up.reshape(n_exp * d, de), w_exp_down.reshape(n_exp * de, d))
    slab_specs = [pl.BlockSpec((rows_, width), lambda j: (j, 0))
                  for rows_, width in ((up_rows, de), (up_rows, de), (down_rows, d))]
    in_specs = [pl.BlockSpec((b, step_tokens, d), lambda j: (0, j, 0))]
    in_specs += [_const_spec(a.shape) for a in args[1:]] + slab_specs
    rows, meta, counts, wg16, wu16, wd16 = pl.pallas_call(
        _mixer_kernel,
        grid=(n_steps,),
        in_specs=in_specs,
        out_specs=[pl.BlockSpec((b, step_tokens * ROW_RECORD, LANES), lambda j: (0, j, 0)),
                   pl.BlockSpec((b, MIX_TILES_PER_STEP, 8, MIX_TILE), lambda j: (0, j, 0, 0)),
                   pl.BlockSpec((LANES, LANES), lambda j: (0, 0))] + slab_specs,
        out_shape=[jax.ShapeDtypeStruct((b, s * ROW_RECORD, LANES), jnp.float32),
                   jax.ShapeDtypeStruct((b, s // MIX_TILE, 8, MIX_TILE), jnp.int32),
                   jax.ShapeDtypeStruct((LANES, LANES), jnp.int32)]
                  + [jax.ShapeDtypeStruct(w.shape, bf16) for w in slabs],
        scratch_shapes=[pltpu.VMEM((b, GLA_DV, GLA_QK), jnp.float32),
                        pltpu.VMEM((b, 8, cw), jnp.float32),
                        pltpu.VMEM((LANES, LANES), jnp.float32),
                        pltpu.VMEM((MIX_TILE, MIX_TILE), jnp.int32),
                        pltpu.VMEM((MIX_TILE, MIX_TILE), bf16)],
        compiler_params=pltpu.CompilerParams(dimension_semantics=("arbitrary",),
                                             vmem_limit_bytes=VMEM_LIMIT),
        name="mixer",
    )(*args, *slabs)
    return (rows.reshape(b * s * ROW_RECORD, LANES), meta.reshape(b * s // MIX_TILE, 8, MIX_TILE), counts,
            wg16.reshape(n_exp, d, de), wu16.reshape(n_exp, d, de), wd16.reshape(n_exp, de, d))


def _sort_plan(meta, counts, n_tok):
    i32 = jnp.int32
    n_tiles = n_tok // MOE_TILE + N_CLASSES
    cls = meta[:, INFO_CLS, :].reshape(n_tok)
    rank = meta[:, INFO_RANK, :].reshape(n_tok)
    cnt = counts[:N_CLASSES, 0]
    tiles_per_cls = (cnt + MOE_TILE - 1) // MOE_TILE
    tile_end = jnp.cumsum(tiles_per_cls)
    n_used = tile_end[-1:]
    tstart = jnp.concatenate([tile_end - tiles_per_cls, n_used])
    tile_id = jnp.minimum(jnp.arange(n_tiles, dtype=i32), n_used - 1)
    tile_cls = jnp.sum((tile_id[:, None] >= tile_end[None, :]).astype(i32), axis=1)
    grp, pair = tile_cls // N_PAIRS, tile_cls % N_PAIRS
    e_a, e_b = grp * EXPERTS_PER_GROUP, grp * EXPERTS_PER_GROUP
    for pp in range(N_PAIRS):
        e_a = e_a + jnp.where(pair == pp, PAIR_A[pp], 0)
        e_b = e_b + jnp.where(pair == pp, PAIR_B[pp], 0)
    of_tile = (tile_cls[:, None] == jnp.arange(N_CLASSES, dtype=i32)[None, :]).astype(i32)
    tile_base = jnp.sum(of_tile * tstart[None, :N_CLASSES], axis=1)
    tile_cnt = jnp.sum(of_tile * cnt[None, :], axis=1)
    n_valid = jnp.clip(tile_cnt - (tile_id - tile_base) * MOE_TILE, 0, MOE_TILE)
    first_tile = functools.reduce(lambda acc, c: jnp.where(cls == c, tstart[c], acc), range(N_CLASSES),
                                  jnp.zeros_like(cls))
    slot = first_tile * MOE_TILE + rank
    return dict(slot=slot, n_valid=n_valid, cnt=cnt, tstart=tstart, e_a=e_a, e_b=e_b, n_used=n_used, n_tiles=n_tiles)


def _experts(plan, rows, g_moe, wg, wu, wd):
    n_tiles = plan["n_tiles"]
    d, de = wg.shape[-2:]
    w_a = lambda shape: pl.BlockSpec(shape, lambda i, ea, eb, *_: (ea[i], 0, 0))
    w_b = lambda shape: pl.BlockSpec(shape, lambda i, ea, eb, *_: (eb[i], 0, 0))
    return pl.pallas_call(
        _expert_kernel,
        grid_spec=pltpu.PrefetchScalarGridSpec(
            num_scalar_prefetch=7, grid=(n_tiles,),
            in_specs=[pl.BlockSpec(memory_space=pl.ANY),
                      pl.BlockSpec((1, d), lambda i, *_: (0, 0)),
                      w_a((1, d, de)), w_a((1, d, de)), w_a((1, de, d)),
                      w_b((1, d, de)), w_b((1, d, de)), w_b((1, de, d))],
            out_specs=pl.BlockSpec((MOE_TILE * Y_RECORD, LANES), lambda i, *_: (i, 0)),
            scratch_shapes=[pltpu.VMEM((2, MOE_TILE * ROW_RECORD, LANES), jnp.float32),
                            pltpu.SemaphoreType.DMA((2,)),
                            pltpu.SMEM((n_tiles * MOE_TILE,), jnp.int32)]),
        out_shape=jax.ShapeDtypeStruct((n_tiles * MOE_TILE * Y_RECORD, LANES), jnp.float32),
        compiler_params=pltpu.CompilerParams(dimension_semantics=("arbitrary",),
                                             vmem_limit_bytes=VMEM_LIMIT),
        name="experts",
    )(plan["e_a"], plan["e_b"], plan["n_used"], plan["n_valid"], plan["tstart"], plan["cnt"], plan["slot"],
      rows, g_moe[None, :], wg, wu, wd, wg, wu, wd)


def _ple_final(plan, rows, p, y_sorted, g_ple, w_ple_gate, w_ple_proj, g_final):
    n_tok, dp = p.shape
    d = w_ple_gate.shape[0]
    bf16 = jnp.bfloat16
    const = lambda shape: pl.BlockSpec(shape, lambda i, *_: (0,) * len(shape))
    tile = lambda width: pl.BlockSpec((PLE_TILE, width), lambda i, *_: (i, 0))
    return pl.pallas_call(
        _ple_final_kernel,
        grid_spec=pltpu.PrefetchScalarGridSpec(
            num_scalar_prefetch=1, grid=(n_tok // PLE_TILE,),
            in_specs=[pl.BlockSpec((PLE_TILE * ROW_RECORD, LANES), lambda i, *_: (i, 0)),
                      tile(dp),
                      pl.BlockSpec(memory_space=pl.ANY),
                      const((1, d)), const((d, d)), const((dp, d)), const((1, d))],
            out_specs=tile(d),
            scratch_shapes=[pltpu.VMEM((2, PLE_TILE * Y_RECORD, LANES), jnp.float32),
                            pltpu.SemaphoreType.DMA((2,))]),
        out_shape=jax.ShapeDtypeStruct((n_tok, d), jnp.float32),
        compiler_params=pltpu.CompilerParams(dimension_semantics=("arbitrary",),
                                             vmem_limit_bytes=VMEM_LIMIT),
        name="ple_final",
    )(plan["slot"], rows, p, y_sorted,
      g_ple[None, :], w_ple_gate.astype(bf16), w_ple_proj.astype(bf16), g_final[None, :])


def kernel(x, p, g_mix, w_in, w_gla_gate, b_gla_gate, g_gla_out, w_conv, w_out, g_moe, w_group, b_group,
           w_router, b_router, w_exp_gate, w_exp_up, w_exp_down, g_ple, w_ple_gate, w_ple_proj, g_final):
    depth = w_in.shape[0]
    assert depth == 1, "the final norm is fused into the last (only) layer"
    b, s, d = x.shape
    n_tok = b * s
    assert s % (MIX_TILE * MIX_TILES_PER_STEP) == 0 and n_tok % MOE_TILE == 0 and n_tok % PLE_TILE == 0 and n_tok % PLACE_UNROLL == 0
    rows, meta, counts, wg16, wu16, wd16 = _mixer(
        x, g_mix[0], w_in[0], w_gla_gate[0], b_gla_gate[0], g_gla_out[0], w_conv[0], w_out[0], g_moe[0],
        w_group[0], b_group[0], w_router[0], b_router[0], w_exp_gate[0], w_exp_up[0], w_exp_down[0])
    plan = _sort_plan(meta, counts, n_tok)
    y_sorted = _experts(plan, rows, g_moe[0], wg16, wu16, wd16)
    out = _ple_final(plan, rows, p[0].reshape(n_tok, -1), y_sorted, g_ple[0], w_ple_gate[0], w_ple_proj[0],
                     g_final)
    return out.reshape(b, s, d)
```

```python
import functools

import jax
import jax.numpy as jnp
from jax import lax
from jax.experimental import pallas as pl
from jax.experimental.pallas import tpu as pltpu

EPS = 1e-6
GLA_HEADS = 4
GLA_DK = 64
GLA_DV = 128
GLA_QK = GLA_HEADS * GLA_DK
GLA_V = GLA_HEADS * GLA_DV
GLA_LOWRANK = 16
GLA_TAU = 16.0
CONV_K = 3
N_GROUPS = 4
EXPERTS_PER_GROUP = 4
N_EXPERTS = N_GROUPS * EXPERTS_PER_GROUP

LANES = 128
MIX_TILE = 256
MIX_LEVELS = 8
SUBLANES = 8
SUB_LEVELS = 3
MIX_TILES_PER_STEP = 1
ROUTE_ROWS = 32
ROUTE_EXPERT_ROW0 = 8
MOE_TILE = 256
VMEM_LIMIT = 56 * 1024 * 1024

PAIR_A = (0, 2, 2, 0, 0, 1)
PAIR_B = (1, 1, 3, 3, 2, 3)
PAIR_OF_KEY = {1: 0, 6: 1, 11: 2, 3: 3, 2: 4, 7: 5}
N_PAIRS = len(PAIR_A)
N_CLASSES = N_GROUPS * N_PAIRS
INFO_CLS, INFO_RANK, INFO_WA, INFO_WB = 0, 1, 2, 3
PLACE_UNROLL = 8
ROW_RECORD = 9
Y_RECORD = 8
GATHER_GROUP = 32
PLE_TILE = 512

_NT = (((1,), (1,)), ((), ()))
_DONE = object()


def _rms(x, g):
    return x * lax.rsqrt(jnp.mean(x * x, axis=-1, keepdims=True) + EPS) * g


def _dot(a, b):
    return jnp.dot(a, b, preferred_element_type=jnp.float32)


def _dot_nt(a, b):
    return lax.dot_general(a, b, _NT, preferred_element_type=jnp.float32)


def _split_bf16(a):
    hi = a.astype(jnp.bfloat16)
    return hi, (a - hi.astype(jnp.float32)).astype(jnp.bfloat16)


def _shift_rows(x, shift):
    return pltpu.roll(x, shift % x.shape[0], axis=0)


def _mixer_kernel(x_ref, gmix_ref, wqkvg_ref, wa_ref, wc3_ref, wgate_ref, bgate_ref, ggla_ref,
                  wconv_ref, wout_ref, gmoe_ref, wrt_ref, brt_ref, wge_ref, wue_ref, wde_ref,
                  rows_ref, meta_ref, counts_ref, wge16_ref, wue16_ref, wde16_ref,
                  st_ref, carry_ref, count_ref, level_ref, tril_ref):
    bf16 = jnp.bfloat16
    T = MIX_TILE

    wge16_ref[...] = wge_ref[...].astype(bf16)
    wue16_ref[...] = wue_ref[...].astype(bf16)
    wde16_ref[...] = wde_ref[...].astype(bf16)

    @pl.when(pl.program_id(0) == 0)
    def _():
        st_ref[...] = jnp.zeros_like(st_ref)
        carry_ref[...] = jnp.zeros_like(carry_ref)
        count_ref[...] = jnp.zeros_like(count_ref)
        tt = lax.broadcasted_iota(jnp.int32, (T, T), 0)
        ss = lax.broadcasted_iota(jnp.int32, (T, T), 1)
        txs = jnp.bitwise_xor(tt, ss)
        level = jnp.zeros((T, T), jnp.int32)
        for j in range(1, MIX_LEVELS):
            level = level + (txs >= (1 << j)).astype(jnp.int32)
        level_ref[...] = jnp.where(tt > ss, level, jnp.where(tt == ss, MIX_LEVELS, -1))
        tril_ref[...] = (ss <= tt).astype(bf16)

    tiles = [_mixer_tile(x_ref.at[pl.ds(b, 1), pl.ds(i * T, T)], gmix_ref, wqkvg_ref, wa_ref, wc3_ref, wgate_ref,
                         bgate_ref, ggla_ref, wconv_ref, wout_ref, gmoe_ref, wrt_ref, brt_ref,
                         rows_ref.at[b, pl.ds(i * T * ROW_RECORD, T * ROW_RECORD)], meta_ref.at[b, pl.ds(i, 1)],
                         counts_ref, st_ref.at[b], carry_ref.at[b], count_ref, level_ref, tril_ref)
             for i in range(MIX_TILES_PER_STEP) for b in range(x_ref.shape[0])]
    while tiles:
        tiles = [t for t in tiles if next(t, _DONE) is not _DONE]


def _mixer_tile(x_ref, gmix_ref, wqkvg_ref, wa_ref, wc3_ref, wgate_ref, bgate_ref, ggla_ref,
                wconv_ref, wout_ref, gmoe_ref, wrt_ref, brt_ref,
                rows_ref, meta_ref, counts_ref,
                st_ref, carry_ref, count_ref, level_ref, tril_ref):
    f32, bf16 = jnp.float32, jnp.bfloat16
    T = MIX_TILE
    D = x_ref.shape[-1]

    x = x_ref[0]
    hb = _rms(x, gmix_ref[...]).astype(bf16)
    qkvg = _dot(hb, wqkvg_ref[...])
    q = qkvg[:, :GLA_QK] * (GLA_DK ** -0.5)
    k = qkvg[:, GLA_QK:2 * GLA_QK]
    v = qkvg[:, 2 * GLA_QK:2 * GLA_QK + GLA_V]
    g = qkvg[:, 2 * GLA_QK + GLA_V:]
    a_low = _dot(hb, wa_ref[...])
    a_hi, a_lo = _split_bf16(a_low)
    z = _dot(jnp.concatenate([a_hi, a_lo, a_hi], axis=1), wgate_ref[...]) + bgate_ref[...]
    la = (jnp.minimum(z, 0.0) - jnp.log(1.0 + jnp.exp(-jnp.abs(z)))) * (1.0 / GLA_TAU)
    yield

    row = lax.broadcasted_iota(jnp.int32, (T, GLA_QK), 0)

    def next_level(l, q_l, k_l, block):
        upper = ((row >> l) & 1) == 1
        below = _shift_rows(block, 1 << l)
        above = _shift_rows(block, -(1 << l))
        return (q_l * jnp.where(upper, below, 1.0), k_l * jnp.where(upper, 1.0, above),
                block * jnp.where(upper, below, above))

    decay = jnp.exp(la)
    H = T // 2
    assert GLA_DV == H
    half_level = level_ref[0:H, 0:H]
    lane_head_st = lax.broadcasted_iota(jnp.int32, (H, GLA_QK), 1) // GLA_DK

    heads_per_tile = LANES // GLA_DK
    lane_head_tile = lax.broadcasted_iota(jnp.int32, (H, LANES), 1) // GLA_DK

    def head_scores(q_half, k_half):
        out = []
        for tile in range(GLA_QK // LANES):
            q_t = q_half[:, tile * LANES:(tile + 1) * LANES]
            k_t = k_half[:, tile * LANES:(tile + 1) * LANES]
            k_rows = jnp.concatenate([jnp.where(lane_head_tile == j, k_t, jnp.zeros_like(k_t))
                                      for j in range(heads_per_tile)], axis=0)
            p = _dot_nt(q_t, k_rows)
            out += [p[:, j * H:(j + 1) * H] for j in range(heads_per_tile)]
        return out

    diag0 = [jnp.zeros((H, H), f32) for _ in range(GLA_HEADS)]
    diag1 = [jnp.zeros((H, H), f32) for _ in range(GLA_HEADS)]

    def add_level(l, q_l, k_l):
        sel = half_level == l
        ql, kl = q_l.astype(bf16), k_l.astype(bf16)
        p0 = head_scores(ql[:H], kl[:H])
        p1 = head_scores(ql[H:], kl[H:])
        for h in range(GLA_HEADS):
            diag0[h] = jnp.where(sel, p0[h], diag0[h])
            diag1[h] = jnp.where(sel, p1[h], diag1[h])

    def split_groups(a):
        return [a[i * SUBLANES:(i + 1) * SUBLANES] for i in range(T // SUBLANES)]

    def join_groups(groups):
        return jnp.concatenate(groups, axis=0)

    def next_level_groups(l, q_g, k_g, block_g):
        m = 1 << (l - SUB_LEVELS)
        products = {}
        q_n, k_n, block_n = [], [], []
        for gi in range(len(q_g)):
            lo, hi = gi & ~m, gi | m
            key = (id(block_g[lo]), id(block_g[hi]))
            if key not in products:
                products[key] = block_g[lo] * block_g[hi]
            q_n.append(q_g[gi] * block_g[lo] if gi & m else q_g[gi])
            k_n.append(k_g[gi] if gi & m else k_g[gi] * block_g[hi])
            block_n.append(products[key])
        return q_n, k_n, block_n

    add_level(MIX_LEVELS, q, k)
    q_l, k_l, block = q * decay, k, decay
    for l in range(SUB_LEVELS):
        add_level(l, q_l, k_l)
        q_l, k_l, block = next_level(l, q_l, k_l, block)
        yield
    q_g, k_g, block_g = split_groups(q_l), split_groups(k_l), split_groups(block)
    half = len(q_g) // 2
    for l in range(SUB_LEVELS, MIX_LEVELS - 1):
        add_level(l, join_groups(q_g), join_groups(k_g))
        q_g, k_g, block_g = next_level_groups(l, q_g, k_g, block_g)
        yield
    low = head_scores(join_groups(q_g[half:]).astype(bf16), join_groups(k_g[:half]).astype(bf16))
    zero_block = jnp.zeros((H, H), f32)
    scores = [jnp.concatenate([jnp.concatenate([diag0[h], zero_block], axis=1),
                               jnp.concatenate([low[h], diag1[h]], axis=1)], axis=0) for h in range(GLA_HEADS)]

    yield

    q_g, k_g, block_g = next_level_groups(MIX_LEVELS - 1, q_g, k_g, block_g)
    st = st_ref[...]
    o_state = head_scores(join_groups(q_g).astype(bf16), st.astype(bf16))
    upd = _dot(v.T.astype(bf16), join_groups(k_g).astype(bf16))
    new_st = st * block_g[0][0:1]
    for h in range(GLA_HEADS):
        new_st = new_st + jnp.where(lane_head_st == h, upd[h * GLA_DV:(h + 1) * GLA_DV], 0.0)
    st_ref[...] = new_st
    ggla = ggla_ref[...]
    y_heads = []
    for h in range(GLA_HEADS):
        v_h = v[:, h * GLA_DV:(h + 1) * GLA_DV]
        o = _dot(scores[h].astype(bf16), v_h.astype(bf16)) + o_state[h]
        g_h = g[:, h * GLA_DV:(h + 1) * GLA_DV]
        y_heads.append(_rms(o, ggla) * (g_h * jax.nn.sigmoid(g_h)))
        yield

    yield

    c3 = _dot(hb, wc3_ref[...])
    cw = c3.shape[1] // 3
    cb, cu = c3[:, :cw], c3[:, cw:2 * cw] * c3[:, 2 * cw:]
    crow = lax.broadcasted_iota(jnp.int32, (T, cw), 0)
    prev2, prev1 = carry_ref[0:1, :], carry_ref[1:2, :]
    m1 = jnp.where(crow == 0, prev1, _shift_rows(cu, 1))
    m2 = jnp.where(crow == 0, prev2, jnp.where(crow == 1, prev1, _shift_rows(cu, 2)))
    wconv = wconv_ref[...]
    y_conv = cb * (wconv[0:1, :] * m2 + wconv[1:2, :] * m1 + wconv[2:3, :] * cu)
    carry_ref[0:2, :] = cu[T - 2:, :]

    y = jnp.concatenate(y_heads + [y_conv], axis=1).astype(bf16)
    x1 = x + _dot(y, wout_ref[...])

    yield

    h2 = _rms(x1, gmoe_ref[...])
    h2_hi, h2_lo = _split_bf16(h2)
    part = _dot_nt(wrt_ref[...], h2_hi)
    logits = (part[:ROUTE_ROWS] + part[ROUTE_ROWS:] + _dot_nt(wrt_ref[:ROUTE_ROWS, :], h2_lo)) + brt_ref[...]
    gl = [logits[i:i + 1, :] for i in range(N_GROUPS)]
    gmax = functools.reduce(jnp.maximum, gl)
    gsum = functools.reduce(lambda a, b: a + b, [jnp.exp(t - gmax) for t in gl])
    p_grp = 1.0 / gsum
    g_sel = jnp.full_like(gmax, N_GROUPS - 1).astype(jnp.int32)
    for i in reversed(range(N_GROUPS - 1)):
        g_sel = jnp.where(gl[i] == gmax, i, g_sel)
    ig = []
    for j in range(EXPERTS_PER_GROUP):
        acc = jnp.zeros_like(gmax)
        for gi in range(N_GROUPS):
            r0 = ROUTE_EXPERT_ROW0 + gi * EXPERTS_PER_GROUP + j
            acc = acc + jnp.where(g_sel == gi, logits[r0:r0 + 1, :], 0.0)
        ig.append(acc)

    def first_argmax(vals):
        m = functools.reduce(jnp.maximum, vals)
        idx = jnp.full_like(m, len(vals) - 1).astype(jnp.int32)
        for i in reversed(range(len(vals) - 1)):
            idx = jnp.where(vals[i] == m, i, idx)
        return m, idx

    m1_, i1 = first_argmax(ig)
    m2_, i2 = first_argmax([jnp.where(i1 == j, -jnp.inf, ig[j]) for j in range(EXPERTS_PER_GROUP)])
    e21 = jnp.exp(m2_ - m1_)
    w1 = p_grp / (1.0 + e21)
    w2 = p_grp * e21 / (1.0 + e21)
    key = jnp.minimum(i1, i2) * EXPERTS_PER_GROUP + jnp.maximum(i1, i2)
    pair = jnp.zeros_like(key)
    a_loc = jnp.zeros_like(key)
    for kk, pp in PAIR_OF_KEY.items():
        pair = jnp.where(key == kk, pp, pair)
        a_loc = jnp.where(key == kk, PAIR_A[pp], a_loc)
    w_a = jnp.where(i1 == a_loc, w1, w2)
    w_b = jnp.where(i1 == a_loc, w2, w1)
    cls = g_sel * N_PAIRS + pair
    rr = lax.broadcasted_iota(jnp.int32, (LANES, T), 0)
    rec_t = (jnp.where(rr == INFO_WA, jnp.broadcast_to(w_a, (LANES, T)), 0.0)
             + jnp.where(rr == INFO_WB, jnp.broadcast_to(w_b, (LANES, T)), 0.0))
    pieces = [x1[:, c * LANES:(c + 1) * LANES] for c in range(D // LANES)] + [rec_t.T]
    assert len(pieces) == ROW_RECORD
    _store_records(rows_ref, pieces, ROW_RECORD)

    onehot = (rr == jnp.broadcast_to(cls, (LANES, T))).astype(f32)
    count = count_ref[:, 0:1]
    before = _dot_nt(onehot.astype(bf16), tril_ref[...]) - onehot + count
    rank = jnp.sum(onehot * before, axis=0, keepdims=True).astype(jnp.int32)
    r8 = lax.broadcasted_iota(jnp.int32, (8, T), 0)
    meta_ref[0] = jnp.where(r8 == INFO_CLS, jnp.broadcast_to(cls, (8, T)),
                            jnp.where(r8 == INFO_RANK, jnp.broadcast_to(rank, (8, T)), 0))
    new_count = jnp.broadcast_to(count + jnp.sum(onehot, axis=1, keepdims=True), count_ref.shape)
    count_ref[...] = new_count
    counts_ref[...] = new_count.astype(jnp.int32)


def _store_records(ref, pieces, record_rows):
    n = ref.shape[0] // record_rows
    for c, piece in enumerate(pieces):
        ref[pl.ds(c, n, stride=record_rows), :] = piece


def _load_records(ref, first, count, record_rows, lead=()):
    n = ref.shape[-2] // record_rows
    return jnp.concatenate([ref[lead + (pl.ds(first + c, n, stride=record_rows), slice(None))]
                            for c in range(count)], axis=1)


class _RowGather:
    def __init__(self, index_of, src_hbm, buf, sems, record_rows, n_records, n_valid_of=None):
        self.index_of, self.src_hbm, self.buf, self.sems = index_of, src_hbm, buf, sems
        self.rr, self.n, self.n_valid_of = record_rows, n_records, n_valid_of

    def _groups(self, tile, body):
        for g0 in range(0, self.n, GATHER_GROUP):
            if self.n_valid_of is None:
                body(g0)
            else:
                pl.when(g0 < self.n_valid_of(tile))(functools.partial(body, g0))

    def _issue(self, tile, slot):
        rr = self.rr

        def group(g0):
            for r in range(g0, g0 + GATHER_GROUP):
                first = self.index_of(tile * self.n + r) * rr
                pltpu.make_async_copy(self.src_hbm.at[pl.ds(first, rr), :],
                                      self.buf.at[slot, pl.ds(r * rr, rr), :], self.sems.at[slot]).start()

        self._groups(tile, group)

    def start(self, tile, slot):
        if isinstance(slot, int):
            self._issue(tile, slot)
        else:
            for static_slot in range(2):
                pl.when(slot == static_slot)(functools.partial(self._issue, tile, static_slot))

    def wait(self, tile, slot):
        rows = GATHER_GROUP * self.rr

        def group(g0):
            pltpu.make_async_copy(self.src_hbm.at[pl.ds(0, rows), :],
                                  self.buf.at[slot, pl.ds(g0 * self.rr, rows), :], self.sems.at[slot]).wait()

        self._groups(tile, group)


def _expert_kernel(ea_ref, eb_ref, nused_ref, nvalid_ref, tstart_ref, cnt_ref, slot_ref,
                   rows_hbm, gmoe_ref, wga_ref, wua_ref, wda_ref, wgb_ref, wub_ref, wdb_ref,
                   y_ref, buf, sems, src_ref):
    bf16 = jnp.bfloat16
    n_x = gmoe_ref.shape[-1] // LANES
    n_tok = slot_ref.shape[0]
    step, n_used = pl.program_id(0), nused_ref[0]
    gather = _RowGather(lambda i: src_ref[i], rows_hbm, buf, sems, ROW_RECORD, MOE_TILE,
                        n_valid_of=lambda tile: nvalid_ref[tile])

    @pl.when(step == 0)
    def _():
        buf[...] = jnp.zeros_like(buf)
        for c in range(N_CLASSES):
            base = tstart_ref[c] * MOE_TILE

            def pad(r, carry, base=base):
                src_ref[base + r] = jnp.minimum(base + r, n_tok - 1)
                return carry

            lax.fori_loop(cnt_ref[c], (tstart_ref[c + 1] - tstart_ref[c]) * MOE_TILE, pad, 0)

        def place(it, carry):
            for u in range(PLACE_UNROLL):
                t = it * PLACE_UNROLL + u
                src_ref[slot_ref[t]] = t
            return carry

        lax.fori_loop(0, n_tok // PLACE_UNROLL, place, 0)
        gather.start(0, 0)

    slot = step % 2

    @pl.when(step + 1 < n_used)
    def _():
        gather.start(step + 1, 1 - slot)

    @pl.when(step < n_used)
    def _():
        gather.wait(step, slot)
        rec = _load_records(buf, n_x, 1, ROW_RECORD, lead=(slot,))
        h2 = _rms(_load_records(buf, 0, n_x, ROW_RECORD, lead=(slot,)), gmoe_ref[...]).astype(bf16)

        def expert(wg_ref, wu_ref, wd_ref):
            gate = _dot(h2, wg_ref[0])
            hid = (gate * jax.nn.sigmoid(gate)) * _dot(h2, wu_ref[0])
            return _dot(hid.astype(bf16), wd_ref[0])

        y = rec[:, INFO_WA:INFO_WA + 1] * expert(wga_ref, wua_ref, wda_ref)
        y = y + rec[:, INFO_WB:INFO_WB + 1] * expert(wgb_ref, wub_ref, wdb_ref)
        _store_records(y_ref, [y[:, c * LANES:(c + 1) * LANES] for c in range(n_x)], Y_RECORD)

    @pl.when(step >= n_used)
    def _():
        y_ref[...] = jnp.zeros_like(y_ref)


def _ple_final_kernel(slot_ref, x1_ref, p_ref, y_hbm, gple_ref, wpg_ref, wpp_ref, gfin_ref,
                      out_ref, buf, sems):
    bf16 = jnp.bfloat16
    step, n_steps = pl.program_id(0), pl.num_programs(0)
    n_x = gple_ref.shape[-1] // LANES
    gather = _RowGather(lambda t: slot_ref[t], y_hbm, buf, sems, Y_RECORD, PLE_TILE)

    @pl.when(step == 0)
    def _():
        gather.start(0, 0)

    slot = step % 2

    @pl.when(step + 1 < n_steps)
    def _():
        gather.start(step + 1, 1 - slot)

    gather.wait(step, slot)
    x2 = _load_records(x1_ref, 0, n_x, ROW_RECORD) + _load_records(buf, 0, n_x, Y_RECORD, lead=(slot,))
    gate_p = jax.nn.sigmoid(_dot(_rms(x2, gple_ref[...]).astype(bf16), wpg_ref[...]))
    x3 = x2 + gate_p * _dot(p_ref[...].astype(bf16), wpp_ref[...])
    out_ref[...] = _rms(x3, gfin_ref[...])


def _const_spec(shape):
    return pl.BlockSpec(shape, lambda *_: (0,) * len(shape), pipeline_mode=pl.Buffered(1))


def _mixer(x, g_mix, w_in, w_gla_gate, b_gla_gate, g_gla_out, w_conv, w_out, g_moe,
           w_group, b_group, w_router, b_router, w_exp_gate, w_exp_up, w_exp_down):
    b, s, d = x.shape
    step_tokens = MIX_TILE * MIX_TILES_PER_STEP
    n_steps = s // step_tokens
    n_exp, _, de = w_exp_gate.shape
    assert (n_exp * d) % (16 * n_steps) == 0 and (n_exp * de) % (16 * n_steps) == 0
    up_rows, down_rows = n_exp * d // n_steps, n_exp * de // n_steps
    bf16 = jnp.bfloat16
    n_qkvg = 2 * GLA_QK + 2 * GLA_V
    w_qkvg = w_in[:, :n_qkvg].astype(bf16)
    w_a = w_in[:, n_qkvg:n_qkvg + GLA_LOWRANK].astype(bf16)
    w_c3 = w_in[:, n_qkvg + GLA_LOWRANK:].astype(bf16)
    cw = w_c3.shape[1] // 3
    wrt = jnp.zeros((ROUTE_ROWS, d), jnp.float32)
    wrt = wrt.at[:N_GROUPS].set(w_group.T).at[ROUTE_EXPERT_ROW0:ROUTE_EXPERT_ROW0 + N_EXPERTS].set(w_router.T)
    brt = jnp.zeros((ROUTE_ROWS, 1), jnp.float32)
    brt = brt.at[:N_GROUPS, 0].set(b_group).at[ROUTE_EXPERT_ROW0:ROUTE_EXPERT_ROW0 + N_EXPERTS, 0].set(b_router)
    wrt_split = jnp.concatenate(_split_bf16(wrt), axis=0)
    gate_hi, gate_lo = _split_bf16(w_gla_gate)
    w_gate_split = jnp.concatenate([gate_hi, gate_hi, gate_lo], axis=0)
    args = (x, g_mix[None, :], w_qkvg, w_a, w_c3, w_gate_split, b_gla_gate[None, :], g_gla_out[None, :],
            w_conv, w_out.astype(bf16), g_moe[None, :], wrt_split, brt)
    slabs = (w_exp_gate.reshape(n_exp * d, de), w_exp_up.reshape(n_exp * d, de), w_exp_down.reshape(n_exp * de, d))
    slab_specs = [pl.BlockSpec((rows_, width), lambda j: (j, 0))
                  for rows_, width in ((up_rows, de), (up_rows, de), (down_rows, d))]
    in_specs = [pl.BlockSpec((b, step_tokens, d), lambda j: (0, j, 0))]
    in_specs += [_const_spec(a.shape) for a in args[1:]] + slab_specs
    rows, meta, counts, wg16, wu16, wd16 = pl.pallas_call(
        _mixer_kernel,
        grid=(n_steps,),
        in_specs=in_specs,
        out_specs=[pl.BlockSpec((b, step_tokens * ROW_RECORD, LANES), lambda j: (0, j, 0)),
                   pl.BlockSpec((b, MIX_TILES_PER_STEP, 8, MIX_TILE), lambda j: (0, j, 0, 0)),
                   pl.BlockSpec((LANES, LANES), lambda j: (0, 0))] + slab_specs,
        out_shape=[jax.ShapeDtypeStruct((b, s * ROW_RECORD, LANES), jnp.float32),
                   jax.ShapeDtypeStruct((b, s // MIX_TILE, 8, MIX_TILE), jnp.int32),
                   jax.ShapeDtypeStruct((LANES, LANES), jnp.int32)]
                  + [jax.ShapeDtypeStruct(w.shape, bf16) for w in slabs],
        scratch_shapes=[pltpu.VMEM((b, GLA_DV, GLA_QK), jnp.float32),
                        pltpu.VMEM((b, 8, cw), jnp.float32),
                        pltpu.VMEM((LANES, LANES), jnp.float32),
                        pltpu.VMEM((MIX_TILE, MIX_TILE), jnp.int32),
                        pltpu.VMEM((MIX_TILE, MIX_TILE), bf16)],
        compiler_params=pltpu.CompilerParams(dimension_semantics=("arbitrary",),
                                             vmem_limit_bytes=VMEM_LIMIT),
        name="mixer",
    )(*args, *slabs)
    return (rows.reshape(b * s * ROW_RECORD, LANES), meta.reshape(b * s // MIX_TILE, 8, MIX_TILE), counts,
            wg16.reshape(n_exp, d, de), wu16.reshape(n_exp, d, de), wd16.reshape(n_exp, de, d))


def _sort_plan(meta, counts, n_tok):
    i32 = jnp.int32
    n_tiles = n_tok // MOE_TILE + N_CLASSES
    cls = meta[:, INFO_CLS, :].reshape(n_tok)
    rank = meta[:, INFO_RANK, :].reshape(n_tok)
    cnt = counts[:N_CLASSES, 0]
    tiles_per_cls = (cnt + MOE_TILE - 1) // MOE_TILE
    tile_end = jnp.cumsum(tiles_per_cls)
    n_used = tile_end[-1:]
    tstart = jnp.concatenate([tile_end - tiles_per_cls, n_used])
    tile_id = jnp.minimum(jnp.arange(n_tiles, dtype=i32), n_used - 1)
    tile_cls = jnp.sum((tile_id[:, None] >= tile_end[None, :]).astype(i32), axis=1)
    grp, pair = tile_cls // N_PAIRS, tile_cls % N_PAIRS
    e_a, e_b = grp * EXPERTS_PER_GROUP, grp * EXPERTS_PER_GROUP
    for pp in range(N_PAIRS):
        e_a = e_a + jnp.where(pair == pp, PAIR_A[pp], 0)
        e_b = e_b + jnp.where(pair == pp, PAIR_B[pp], 0)
    of_tile = (tile_cls[:, None] == jnp.arange(N_CLASSES, dtype=i32)[None, :]).astype(i32)
    tile_base = jnp.sum(of_tile * tstart[None, :N_CLASSES], axis=1)
    tile_cnt = jnp.sum(of_tile * cnt[None, :], axis=1)
    n_valid = jnp.clip(tile_cnt - (tile_id - tile_base) * MOE_TILE, 0, MOE_TILE)
    first_tile = functools.reduce(lambda acc, c: jnp.where(cls == c, tstart[c], acc), range(N_CLASSES),
                                  jnp.zeros_like(cls))
    slot = first_tile * MOE_TILE + rank
    return dict(slot=slot, n_valid=n_valid, cnt=cnt, tstart=tstart, e_a=e_a, e_b=e_b, n_used=n_used, n_tiles=n_tiles)


def _experts(plan, rows, g_moe, wg, wu, wd):
    n_tiles = plan["n_tiles"]
    d, de = wg.shape[-2:]
    w_a = lambda shape: pl.BlockSpec(shape, lambda i, ea, eb, *_: (ea[i], 0, 0))
    w_b = lambda shape: pl.BlockSpec(shape, lambda i, ea, eb, *_: (eb[i], 0, 0))
    return pl.pallas_call(
        _expert_kernel,
        grid_spec=pltpu.PrefetchScalarGridSpec(
            num_scalar_prefetch=7, grid=(n_tiles,),
            in_specs=[pl.BlockSpec(memory_space=pl.ANY),
                      pl.BlockSpec((1, d), lambda i, *_: (0, 0)),
                      w_a((1, d, de)), w_a((1, d, de)), w_a((1, de, d)),
                      w_b((1, d, de)), w_b((1, d, de)), w_b((1, de, d))],
            out_specs=pl.BlockSpec((MOE_TILE * Y_RECORD, LANES), lambda i, *_: (i, 0)),
            scratch_shapes=[pltpu.VMEM((2, MOE_TILE * ROW_RECORD, LANES), jnp.float32),
                            pltpu.SemaphoreType.DMA((2,)),
                            pltpu.SMEM((n_tiles * MOE_TILE,), jnp.int32)]),
        out_shape=jax.ShapeDtypeStruct((n_tiles * MOE_TILE * Y_RECORD, LANES), jnp.float32),
        compiler_params=pltpu.CompilerParams(dimension_semantics=("arbitrary",),
                                             vmem_limit_bytes=VMEM_LIMIT),
        name="experts",
    )(plan["e_a"], plan["e_b"], plan["n_used"], plan["n_valid"], plan["tstart"], plan["cnt"], plan["slot"],
      rows, g_moe[None, :], wg, wu, wd, wg, wu, wd)


def _ple_final(plan, rows, p, y_sorted, g_ple, w_ple_gate, w_ple_proj, g_final):
    n_tok, dp = p.shape
    d = w_ple_gate.shape[0]
    bf16 = jnp.bfloat16
    const = lambda shape: pl.BlockSpec(shape, lambda i, *_: (0,) * len(shape))
    tile = lambda width: pl.BlockSpec((PLE_TILE, width), lambda i, *_: (i, 0))
    return pl.pallas_call(
        _ple_final_kernel,
        grid_spec=pltpu.PrefetchScalarGridSpec(
            num_scalar_prefetch=1, grid=(n_tok // PLE_TILE,),
            in_specs=[pl.BlockSpec((PLE_TILE * ROW_RECORD, LANES), lambda i, *_: (i, 0)),
                      tile(dp),
                      pl.BlockSpec(memory_space=pl.ANY),
                      const((1, d)), const((d, d)), const((dp, d)), const((1, d))],
            out_specs=tile(d),
            scratch_shapes=[pltpu.VMEM((2, PLE_TILE * Y_RECORD, LANES), jnp.float32),
                            pltpu.SemaphoreType.DMA((2,))]),
        out_shape=jax.ShapeDtypeStruct((n_tok, d), jnp.float32),
        compiler_params=pltpu.CompilerParams(dimension_semantics=("arbitrary",),
                                             vmem_limit_bytes=VMEM_LIMIT),
        name="ple_final",
    )(plan["slot"], rows, p, y_sorted,
      g_ple[None, :], w_ple_gate.astype(bf16), w_ple_proj.astype(bf16), g_final[None, :])


def kernel(x, p, g_mix, w_in, w_gla_gate, b_gla_gate, g_gla_out, w_conv, w_out, g_moe, w_group, b_group,
           w_router, b_router, w_exp_gate, w_exp_up, w_exp_down, g_ple, w_ple_gate, w_ple_proj, g_final):
    depth = w_in.shape[0]
    assert depth == 1, "the final norm is fused into the last (only) layer"
    b, s, d = x.shape
    n_tok = b * s
    assert s % (MIX_TILE * MIX_TILES_PER_STEP) == 0 and n_tok % MOE_TILE == 0 and n_tok % PLE_TILE == 0 and n_tok % PLACE_UNROLL == 0
    rows, meta, counts, wg16, wu16, wd16 = _mixer(
        x, g_mix[0], w_in[0], w_gla_gate[0], b_gla_gate[0], g_gla_out[0], w_conv[0], w_out[0], g_moe[0],
        w_group[0], b_group[0], w_router[0], b_router[0], w_exp_gate[0], w_exp_up[0], w_exp_down[0])
    plan = _sort_plan(meta, counts, n_tok)
    y_sorted = _experts(plan, rows, g_moe[0], wg16, wu16, wd16)
    out = _ple_final(plan, rows, p[0].reshape(n_tok, -1), y_sorted, g_ple[0], w_ple_gate[0], w_ple_proj[0],
                     g_final)
    return out.reshape(b, s, d)
```

```python
import functools

import jax
import jax.numpy as jnp
from jax import lax
from jax.experimental import pallas as pl
from jax.experimental.pallas import tpu as pltpu

EPS = 1e-6
GLA_HEADS = 4
GLA_DK = 64
GLA_DV = 128
GLA_QK = GLA_HEADS * GLA_DK
GLA_V = GLA_HEADS * GLA_DV
GLA_LOWRANK = 16
GLA_TAU = 16.0
CONV_K = 3
N_GROUPS = 4
EXPERTS_PER_GROUP = 4
N_EXPERTS = N_GROUPS * EXPERTS_PER_GROUP

LANES = 128
MIX_TILE = 256
MIX_LEVELS = 8
SUBLANES = 8
BF16_ROWS = 16
SUB_LEVELS = 3
MIX_TILES_PER_STEP = 1
ROUTE_ROWS = 32
ROUTE_EXPERT_ROW0 = 8
MOE_TILE = 256
VMEM_LIMIT = 56 * 1024 * 1024

PAIR_A = (0, 2, 2, 0, 0, 1)
PAIR_B = (1, 1, 3, 3, 2, 3)
PAIR_OF_KEY = {1: 0, 6: 1, 11: 2, 3: 3, 2: 4, 7: 5}
N_PAIRS = len(PAIR_A)
N_CLASSES = N_GROUPS * N_PAIRS
INFO_CLS, INFO_RANK, INFO_WA, INFO_WB = 0, 1, 2, 3
PLACE_UNROLL = 8
ROW_RECORD = 9
Y_RECORD = 8
GATHER_GROUP = 32
PLE_TILE = 1024

_NT = (((1,), (1,)), ((), ()))
_DONE = object()


def _rms(x, g):
    return x * lax.rsqrt(jnp.mean(x * x, axis=-1, keepdims=True) + EPS) * g


def _dot(a, b):
    return jnp.dot(a, b, preferred_element_type=jnp.float32)


def _dot_nt(a, b):
    return lax.dot_general(a, b, _NT, preferred_element_type=jnp.float32)


def _split_bf16(a):
    hi = a.astype(jnp.bfloat16)
    return hi, (a - hi.astype(jnp.float32)).astype(jnp.bfloat16)


def _shift_rows(x, shift):
    return pltpu.roll(x, shift % x.shape[0], axis=0)


def _mixer_kernel(x_ref, gmix_ref, wqkvg_ref, wa_ref, wc3_ref, wgate_ref, bgate_ref, ggla_ref,
                  wconv_ref, wout_ref, gmoe_ref, wrt_ref, brt_ref, wge_ref, wue_ref, wde_ref,
                  rows_ref, meta_ref, counts_ref, wge16_ref, wue16_ref, wde16_ref,
                  st_ref, carry_ref, count_ref, level_ref, tril_ref):
    bf16 = jnp.bfloat16
    T = MIX_TILE

    wge16_ref[...] = wge_ref[...].astype(bf16)
    wue16_ref[...] = wue_ref[...].astype(bf16)
    wde16_ref[...] = wde_ref[...].astype(bf16)

    @pl.when(pl.program_id(0) == 0)
    def _():
        st_ref[...] = jnp.zeros_like(st_ref)
        carry_ref[...] = jnp.zeros_like(carry_ref)
        count_ref[...] = jnp.zeros_like(count_ref)
        tt = lax.broadcasted_iota(jnp.int32, (T, T), 0)
        ss = lax.broadcasted_iota(jnp.int32, (T, T), 1)
        txs = jnp.bitwise_xor(tt, ss)
        level = jnp.zeros((T, T), jnp.int32)
        for j in range(1, MIX_LEVELS):
            level = level + (txs >= (1 << j)).astype(jnp.int32)
        level_ref[...] = jnp.where(tt > ss, level, jnp.where(tt == ss, MIX_LEVELS, -1))
        tril_ref[...] = (ss <= tt).astype(bf16)

    tiles = [_mixer_tile(x_ref.at[pl.ds(b, 1), pl.ds(i * T, T)], gmix_ref, wqkvg_ref, wa_ref, wc3_ref, wgate_ref,
                         bgate_ref, ggla_ref, wconv_ref, wout_ref, gmoe_ref, wrt_ref, brt_ref,
                         rows_ref.at[b, pl.ds(i * T * ROW_RECORD, T * ROW_RECORD)], meta_ref.at[b, pl.ds(i, 1)],
                         counts_ref, st_ref.at[b], carry_ref.at[b], count_ref, level_ref, tril_ref)
             for i in range(MIX_TILES_PER_STEP) for b in range(x_ref.shape[0])]
    while tiles:
        tiles = [t for t in tiles if next(t, _DONE) is not _DONE]


def _mixer_tile(x_ref, gmix_ref, wqkvg_ref, wa_ref, wc3_ref, wgate_ref, bgate_ref, ggla_ref,
                wconv_ref, wout_ref, gmoe_ref, wrt_ref, brt_ref,
                rows_ref, meta_ref, counts_ref,
                st_ref, carry_ref, count_ref, level_ref, tril_ref):
    f32, bf16 = jnp.float32, jnp.bfloat16
    T = MIX_TILE
    D = x_ref.shape[-1]

    x = x_ref[0]
    hb = _rms(x, gmix_ref[...]).astype(bf16)
    qkvg = _dot(hb, wqkvg_ref[...])
    q = qkvg[:, :GLA_QK] * (GLA_DK ** -0.5)
    k = qkvg[:, GLA_QK:2 * GLA_QK]
    v = qkvg[:, 2 * GLA_QK:2 * GLA_QK + GLA_V]
    g = qkvg[:, 2 * GLA_QK + GLA_V:]
    a_low = _dot(hb, wa_ref[...])
    a_hi, a_lo = _split_bf16(a_low)
    z = _dot(jnp.concatenate([a_hi, a_lo, a_hi], axis=1), wgate_ref[...]) + bgate_ref[...]
    la = (jnp.minimum(z, 0.0) - jnp.log(1.0 + jnp.exp(-jnp.abs(z)))) * (1.0 / GLA_TAU)
    yield

    row = lax.broadcasted_iota(jnp.int32, (T, GLA_QK), 0)

    def next_level(l, q_l, k_l, block):
        upper = ((row >> l) & 1) == 1
        below = _shift_rows(block, 1 << l)
        above = _shift_rows(block, -(1 << l))
        return (q_l * jnp.where(upper, below, 1.0), k_l * jnp.where(upper, 1.0, above),
                block * jnp.where(upper, below, above))

    decay = jnp.exp(la)
    H = T // 2
    assert GLA_DV == H
    half_level = level_ref[0:H, 0:H]
    lane_head_st = lax.broadcasted_iota(jnp.int32, (H, GLA_QK), 1) // GLA_DK

    heads_per_tile = LANES // GLA_DK
    lane_head_tile = lax.broadcasted_iota(jnp.int32, (H, LANES), 1) // GLA_DK

    def head_scores(q_half, k_half):
        out = []
        for tile in range(GLA_QK // LANES):
            q_t = q_half[:, tile * LANES:(tile + 1) * LANES]
            k_t = k_half[:, tile * LANES:(tile + 1) * LANES]
            k_rows = jnp.concatenate([jnp.where(lane_head_tile == j, k_t, jnp.zeros_like(k_t))
                                      for j in range(heads_per_tile)], axis=0)
            p = _dot_nt(q_t, k_rows)
            out += [p[:, j * H:(j + 1) * H] for j in range(heads_per_tile)]
        return out

    diag0 = [jnp.zeros((H, H), f32) for _ in range(GLA_HEADS)]
    diag1 = [jnp.zeros((H, H), f32) for _ in range(GLA_HEADS)]

    def add_level(l, q_l, k_l):
        sel = half_level == l
        ql, kl = q_l.astype(bf16), k_l.astype(bf16)
        p0 = head_scores(ql[:H], kl[:H])
        p1 = head_scores(ql[H:], kl[H:])
        for h in range(GLA_HEADS):
            diag0[h] = jnp.where(sel, p0[h], diag0[h])
            diag1[h] = jnp.where(sel, p1[h], diag1[h])

    def split_groups(a):
        return [a[i * SUBLANES:(i + 1) * SUBLANES] for i in range(T // SUBLANES)]

    def join_groups(groups):
        return jnp.concatenate(groups, axis=0)

    def next_level_groups(l, q_g, k_g, block_g):
        m = 1 << (l - SUB_LEVELS)
        products = {}
        q_n, k_n, block_n = [], [], []
        for gi in range(len(q_g)):
            lo, hi = gi & ~m, gi | m
            key = (id(block_g[lo]), id(block_g[hi]))
            if key not in products:
                products[key] = block_g[lo] * block_g[hi]
            q_n.append(q_g[gi] * block_g[lo] if gi & m else q_g[gi])
            k_n.append(k_g[gi] if gi & m else k_g[gi] * block_g[hi])
            block_n.append(products[key])
        return q_n, k_n, block_n

    add_level(MIX_LEVELS, q, k)
    q_l, k_l, block = q * decay, k, decay
    for l in range(SUB_LEVELS):
        add_level(l, q_l, k_l)
        q_l, k_l, block = next_level(l, q_l, k_l, block)
        yield
    q_g, k_g, block_g = split_groups(q_l), split_groups(k_l), split_groups(block)
    half = len(q_g) // 2
    for l in range(SUB_LEVELS, MIX_LEVELS - 1):
        add_level(l, join_groups(q_g), join_groups(k_g))
        q_g, k_g, block_g = next_level_groups(l, q_g, k_g, block_g)
        yield
    low = head_scores(join_groups(q_g[half:]).astype(bf16), join_groups(k_g[:half]).astype(bf16))
    zero_block = jnp.zeros((H, H), f32)
    scores = [jnp.concatenate([jnp.concatenate([diag0[h], zero_block], axis=1),
                               jnp.concatenate([low[h], diag1[h]], axis=1)], axis=0) for h in range(GLA_HEADS)]

    yield

    q_g, k_g, block_g = next_level_groups(MIX_LEVELS - 1, q_g, k_g, block_g)
    st = st_ref[...]
    o_state = head_scores(join_groups(q_g).astype(bf16), st.astype(bf16))
    upd = _dot(v.T.astype(bf16), join_groups(k_g).astype(bf16))
    new_st = st * block_g[0][0:1]
    for h in range(GLA_HEADS):
        new_st = new_st + jnp.where(lane_head_st == h, upd[h * GLA_DV:(h + 1) * GLA_DV], 0.0)
    st_ref[...] = new_st
    ggla = ggla_ref[...]
    y_heads = []
    for h in range(GLA_HEADS):
        v_h = v[:, h * GLA_DV:(h + 1) * GLA_DV]
        o = _dot(scores[h].astype(bf16), v_h.astype(bf16)) + o_state[h]
        g_h = g[:, h * GLA_DV:(h + 1) * GLA_DV]
        y_heads.append(_rms(o, ggla) * (g_h * jax.nn.sigmoid(g_h)))
        yield

    yield

    c3 = _dot(hb, wc3_ref[...])
    cw = c3.shape[1] // 3
    cb, cu = c3[:, :cw], c3[:, cw:2 * cw] * c3[:, 2 * cw:]
    crow = lax.broadcasted_iota(jnp.int32, (T, cw), 0)
    prev2, prev1 = carry_ref[0:1, :], carry_ref[1:2, :]
    m1 = jnp.where(crow == 0, prev1, _shift_rows(cu, 1))
    m2 = jnp.where(crow == 0, prev2, jnp.where(crow == 1, prev1, _shift_rows(cu, 2)))
    wconv = wconv_ref[...]
    y_conv = cb * (wconv[0:1, :] * m2 + wconv[1:2, :] * m1 + wconv[2:3, :] * cu)
    carry_ref[0:2, :] = cu[T - 2:, :]

    y = jnp.concatenate(y_heads + [y_conv], axis=1).astype(bf16)
    x1 = x + _dot(y, wout_ref[...])

    yield

    h2 = _rms(x1, gmoe_ref[...])
    h2_hi, h2_lo = _split_bf16(h2)
    part = _dot_nt(wrt_ref[...], h2_hi)
    logits = (part[:ROUTE_ROWS] + part[ROUTE_ROWS:] + _dot_nt(wrt_ref[:ROUTE_ROWS, :], h2_lo)) + brt_ref[...]
    gl = [logits[i:i + 1, :] for i in range(N_GROUPS)]
    gmax = functools.reduce(jnp.maximum, gl)
    gsum = functools.reduce(lambda a, b: a + b, [jnp.exp(t - gmax) for t in gl])
    p_grp = 1.0 / gsum
    g_sel = jnp.full_like(gmax, N_GROUPS - 1).astype(jnp.int32)
    for i in reversed(range(N_GROUPS - 1)):
        g_sel = jnp.where(gl[i] == gmax, i, g_sel)
    ig = []
    for j in range(EXPERTS_PER_GROUP):
        acc = jnp.zeros_like(gmax)
        for gi in range(N_GROUPS):
            r0 = ROUTE_EXPERT_ROW0 + gi * EXPERTS_PER_GROUP + j
            acc = acc + jnp.where(g_sel == gi, logits[r0:r0 + 1, :], 0.0)
        ig.append(acc)

    def first_argmax(vals):
        m = functools.reduce(jnp.maximum, vals)
        idx = jnp.full_like(m, len(vals) - 1).astype(jnp.int32)
        for i in reversed(range(len(vals) - 1)):
            idx = jnp.where(vals[i] == m, i, idx)
        return m, idx

    m1_, i1 = first_argmax(ig)
    m2_, i2 = first_argmax([jnp.where(i1 == j, -jnp.inf, ig[j]) for j in range(EXPERTS_PER_GROUP)])
    e21 = jnp.exp(m2_ - m1_)
    w1 = p_grp / (1.0 + e21)
    w2 = p_grp * e21 / (1.0 + e21)
    key = jnp.minimum(i1, i2) * EXPERTS_PER_GROUP + jnp.maximum(i1, i2)
    pair = jnp.zeros_like(key)
    a_loc = jnp.zeros_like(key)
    for kk, pp in PAIR_OF_KEY.items():
        pair = jnp.where(key == kk, pp, pair)
        a_loc = jnp.where(key == kk, PAIR_A[pp], a_loc)
    w_a = jnp.where(i1 == a_loc, w1, w2)
    w_b = jnp.where(i1 == a_loc, w2, w1)
    cls = g_sel * N_PAIRS + pair
    rr = lax.broadcasted_iota(jnp.int32, (LANES, T), 0)
    rec_t = (jnp.where(rr == INFO_WA, jnp.broadcast_to(w_a, (LANES, T)), 0.0)
             + jnp.where(rr == INFO_WB, jnp.broadcast_to(w_b, (LANES, T)), 0.0))
    pieces = [x1[:, c * LANES:(c + 1) * LANES] for c in range(D // LANES)] + [rec_t.T]
    assert len(pieces) == ROW_RECORD
    _store_records(rows_ref, pieces, ROW_RECORD)

    onehot = (rr == jnp.broadcast_to(cls, (LANES, T))).astype(f32)
    count = count_ref[:, 0:1]
    before = _dot_nt(onehot.astype(bf16), tril_ref[...]) - onehot + count
    rank = jnp.sum(onehot * before, axis=0, keepdims=True).astype(jnp.int32)
    r8 = lax.broadcasted_iota(jnp.int32, (8, T), 0)
    meta_ref[0] = jnp.where(r8 == INFO_CLS, jnp.broadcast_to(cls, (8, T)),
                            jnp.where(r8 == INFO_RANK, jnp.broadcast_to(rank, (8, T)), 0))
    new_count = jnp.broadcast_to(count + jnp.sum(onehot, axis=1, keepdims=True), count_ref.shape)
    count_ref[...] = new_count
    counts_ref[...] = new_count.astype(jnp.int32)


def _store_records(ref, pieces, record_rows):
    n = ref.shape[0] // record_rows
    for c, piece in enumerate(pieces):
        ref[pl.ds(c, n, stride=record_rows), :] = piece


def _load_records(ref, first, count, record_rows, lead=()):
    n = ref.shape[-2] // record_rows
    return jnp.concatenate([ref[lead + (pl.ds(first + c, n, stride=record_rows), slice(None))]
                            for c in range(count)], axis=1)


class _RowGather:
    def __init__(self, index_of, src_hbm, buf, sems, record_rows, n_records, n_valid_of=None):
        self.index_of, self.src_hbm, self.buf, self.sems = index_of, src_hbm, buf, sems
        self.rr, self.n, self.n_valid_of = record_rows, n_records, n_valid_of

    def _groups(self, tile, body):
        for g0 in range(0, self.n, GATHER_GROUP):
            if self.n_valid_of is None:
                body(g0)
            else:
                pl.when(g0 < self.n_valid_of(tile))(functools.partial(body, g0))

    def _issue(self, tile, slot):
        rr = self.rr

        def group(g0):
            for r in range(g0, g0 + GATHER_GROUP):
                first = self.index_of(tile * self.n + r) * rr
                pltpu.make_async_copy(self.src_hbm.at[pl.ds(first, rr), :],
                                      self.buf.at[slot, pl.ds(r * rr, rr), :], self.sems.at[slot]).start()

        self._groups(tile, group)

    def start(self, tile, slot):
        if isinstance(slot, int):
            self._issue(tile, slot)
        else:
            for static_slot in range(2):
                pl.when(slot == static_slot)(functools.partial(self._issue, tile, static_slot))

    def wait(self, tile, slot):
        rows = GATHER_GROUP * self.rr

        def group(g0):
            pltpu.make_async_copy(self.src_hbm.at[pl.ds(0, rows), :],
                                  self.buf.at[slot, pl.ds(g0 * self.rr, rows), :], self.sems.at[slot]).wait()

        self._groups(tile, group)


def _expert_kernel(ea_ref, eb_ref, nused_ref, nvalid_ref, tstart_ref, cnt_ref, slot_ref,
                   rows_hbm, gmoe_ref, wga_ref, wua_ref, wda_ref, wgb_ref, wub_ref, wdb_ref,
                   y_ref, buf, sems, src_ref):
    bf16 = jnp.bfloat16
    n_x = gmoe_ref.shape[-1] // LANES
    n_tok = slot_ref.shape[0]
    step, n_used = pl.program_id(0), nused_ref[0]
    gather = _RowGather(lambda i: src_ref[i], rows_hbm, buf, sems, ROW_RECORD, MOE_TILE,
                        n_valid_of=lambda tile: nvalid_ref[tile])

    @pl.when(step == 0)
    def _():
        buf[...] = jnp.zeros_like(buf)
        for c in range(N_CLASSES):
            base = tstart_ref[c] * MOE_TILE

            def pad(r, carry, base=base):
                src_ref[base + r] = jnp.minimum(base + r, n_tok - 1)
                return carry

            lax.fori_loop(cnt_ref[c], (tstart_ref[c + 1] - tstart_ref[c]) * MOE_TILE, pad, 0)

        def place(it, carry):
            for u in range(PLACE_UNROLL):
                t = it * PLACE_UNROLL + u
                src_ref[slot_ref[t]] = t
            return carry

        lax.fori_loop(0, n_tok // PLACE_UNROLL, place, 0)
        gather.start(0, 0)

    slot = step % 2

    @pl.when(step + 1 < n_used)
    def _():
        gather.start(step + 1, 1 - slot)

    @pl.when(step < n_used)
    def _():
        gather.wait(step, slot)
        rec = _load_records(buf, n_x, 1, ROW_RECORD, lead=(slot,))
        h2 = _rms(_load_records(buf, 0, n_x, ROW_RECORD, lead=(slot,)), gmoe_ref[...]).astype(bf16)

        def expert(wg_ref, wu_ref, wd_ref):
            gate = _dot(h2, wg_ref[0])
            hid = (gate * jax.nn.sigmoid(gate)) * _dot(h2, wu_ref[0])
            return _dot(hid.astype(bf16), wd_ref[0])

        y = rec[:, INFO_WA:INFO_WA + 1] * expert(wga_ref, wua_ref, wda_ref)
        y = y + rec[:, INFO_WB:INFO_WB + 1] * expert(wgb_ref, wub_ref, wdb_ref)
        _store_records(y_ref, [y[:, c * LANES:(c + 1) * LANES] for c in range(n_x)], Y_RECORD)

    @pl.when(step >= n_used)
    def _():
        y_ref[...] = jnp.zeros_like(y_ref)


def _ple_final_kernel(slot_ref, x1_ref, p_ref, y_hbm, gple_ref, wpg_ref, wpp_ref, gfin_ref,
                      out_ref, buf, sems):
    bf16 = jnp.bfloat16
    step, n_steps = pl.program_id(0), pl.num_programs(0)
    n_x = gple_ref.shape[-1] // LANES
    gather = _RowGather(lambda t: slot_ref[t], y_hbm, buf, sems, Y_RECORD, PLE_TILE)

    @pl.when(step == 0)
    def _():
        gather.start(0, 0)

    slot = step % 2

    @pl.when(step + 1 < n_steps)
    def _():
        gather.start(step + 1, 1 - slot)

    gather.wait(step, slot)
    x2 = _load_records(x1_ref, 0, n_x, ROW_RECORD) + _load_records(buf, 0, n_x, Y_RECORD, lead=(slot,))
    gate_p = jax.nn.sigmoid(_dot(_rms(x2, gple_ref[...]).astype(bf16), wpg_ref[...]))
    x3 = x2 + gate_p * _dot(p_ref[...].astype(bf16), wpp_ref[...])
    out_ref[...] = _rms(x3, gfin_ref[...])


def _const_spec(shape):
    return pl.BlockSpec(shape, lambda *_: (0,) * len(shape), pipeline_mode=pl.Buffered(1))


def _mixer(x, g_mix, w_in, w_gla_gate, b_gla_gate, g_gla_out, w_conv, w_out, g_moe,
           w_group, b_group, w_router, b_router, w_exp_gate, w_exp_up, w_exp_down):
    b, s, d = x.shape
    step_tokens = MIX_TILE * MIX_TILES_PER_STEP
    n_steps = s // step_tokens
    n_exp, _, de = w_exp_gate.shape
    assert (n_exp * d) % (BF16_ROWS * n_steps) == 0 and (n_exp * de) % (BF16_ROWS * n_steps) == 0
    up_rows, down_rows = n_exp * d // n_steps, n_exp * de // n_steps
    bf16 = jnp.bfloat16
    n_qkvg = 2 * GLA_QK + 2 * GLA_V
    w_qkvg = w_in[:, :n_qkvg].astype(bf16)
    w_a = w_in[:, n_qkvg:n_qkvg + GLA_LOWRANK].astype(bf16)
    w_c3 = w_in[:, n_qkvg + GLA_LOWRANK:].astype(bf16)
    cw = w_c3.shape[1] // 3
    wrt = jnp.zeros((ROUTE_ROWS, d), jnp.float32)
    wrt = wrt.at[:N_GROUPS].set(w_group.T).at[ROUTE_EXPERT_ROW0:ROUTE_EXPERT_ROW0 + N_EXPERTS].set(w_router.T)
    brt = jnp.zeros((ROUTE_ROWS, 1), jnp.float32)
    brt = brt.at[:N_GROUPS, 0].set(b_group).at[ROUTE_EXPERT_ROW0:ROUTE_EXPERT_ROW0 + N_EXPERTS, 0].set(b_router)
    wrt_split = jnp.concatenate(_split_bf16(wrt), axis=0)
    gate_hi, gate_lo = _split_bf16(w_gla_gate)
    w_gate_split = jnp.concatenate([gate_hi, gate_hi, gate_lo], axis=0)
    args = (x, g_mix[None, :], w_qkvg, w_a, w_c3, w_gate_split, b_gla_gate[None, :], g_gla_out[None, :],
            w_conv, w_out.astype(bf16), g_moe[None, :], wrt_split, brt)
    slabs = (w_exp_gate.reshape(n_exp * d, de), w_exp_up.reshape(n_exp * d, de), w_exp_down.reshape(n_exp * de, d))
    slab_specs = [pl.BlockSpec((rows_, width), lambda j: (j, 0))
                  for rows_, width in ((up_rows, de), (up_rows, de), (down_rows, d))]
    in_specs = [pl.BlockSpec((b, step_tokens, d), lambda j: (0, j, 0))]
    in_specs += [_const_spec(a.shape) for a in args[1:]] + slab_specs
    rows, meta, counts, wg16, wu16, wd16 = pl.pallas_call(
        _mixer_kernel,
        grid=(n_steps,),
        in_specs=in_specs,
        out_specs=[pl.BlockSpec((b, step_tokens * ROW_RECORD, LANES), lambda j: (0, j, 0)),
                   pl.BlockSpec((b, MIX_TILES_PER_STEP, 8, MIX_TILE), lambda j: (0, j, 0, 0)),
                   pl.BlockSpec((LANES, LANES), lambda j: (0, 0))] + slab_specs,
        out_shape=[jax.ShapeDtypeStruct((b, s * ROW_RECORD, LANES), jnp.float32),
                   jax.ShapeDtypeStruct((b, s // MIX_TILE, 8, MIX_TILE), jnp.int32),
                   jax.ShapeDtypeStruct((LANES, LANES), jnp.int32)]
                  + [jax.ShapeDtypeStruct(w.shape, bf16) for w in slabs],
        scratch_shapes=[pltpu.VMEM((b, GLA_DV, GLA_QK), jnp.float32),
                        pltpu.VMEM((b, 8, cw), jnp.float32),
                        pltpu.VMEM((LANES, LANES), jnp.float32),
                        pltpu.VMEM((MIX_TILE, MIX_TILE), jnp.int32),
                        pltpu.VMEM((MIX_TILE, MIX_TILE), bf16)],
        compiler_params=pltpu.CompilerParams(dimension_semantics=("arbitrary",),
                                             vmem_limit_bytes=VMEM_LIMIT),
        name="mixer",
    )(*args, *slabs)
    return (rows.reshape(b * s * ROW_RECORD, LANES), meta.reshape(b * s // MIX_TILE, 8, MIX_TILE), counts,
            wg16.reshape(n_exp, d, de), wu16.reshape(n_exp, d, de), wd16.reshape(n_exp, de, d))


def _sort_plan(meta, counts, n_tok):
    i32 = jnp.int32
    n_tiles = n_tok // MOE_TILE + N_CLASSES
    cls = meta[:, INFO_CLS, :].reshape(n_tok)
    rank = meta[:, INFO_RANK, :].reshape(n_tok)
    cnt = counts[:N_CLASSES, 0]
    tiles_per_cls = (cnt + MOE_TILE - 1) // MOE_TILE
    tile_end = jnp.cumsum(tiles_per_cls)
    n_used = tile_end[-1:]
    tstart = jnp.concatenate([tile_end - tiles_per_cls, n_used])
    tile_id = jnp.minimum(jnp.arange(n_tiles, dtype=i32), n_used - 1)
    tile_cls = jnp.sum((tile_id[:, None] >= tile_end[None, :]).astype(i32), axis=1)
    grp, pair = tile_cls // N_PAIRS, tile_cls % N_PAIRS
    e_a, e_b = grp * EXPERTS_PER_GROUP, grp * EXPERTS_PER_GROUP
    for pp in range(N_PAIRS):
        e_a = e_a + jnp.where(pair == pp, PAIR_A[pp], 0)
        e_b = e_b + jnp.where(pair == pp, PAIR_B[pp], 0)
    of_tile = (tile_cls[:, None] == jnp.arange(N_CLASSES, dtype=i32)[None, :]).astype(i32)
    tile_base = jnp.sum(of_tile * tstart[None, :N_CLASSES], axis=1)
    tile_cnt = jnp.sum(of_tile * cnt[None, :], axis=1)
    n_valid = jnp.clip(tile_cnt - (tile_id - tile_base) * MOE_TILE, 0, MOE_TILE)
    first_tile = functools.reduce(lambda acc, c: jnp.where(cls == c, tstart[c], acc), range(N_CLASSES),
                                  jnp.zeros_like(cls))
    slot = first_tile * MOE_TILE + rank
    return dict(slot=slot, n_valid=n_valid, cnt=cnt, tstart=tstart, e_a=e_a, e_b=e_b, n_used=n_used, n_tiles=n_tiles)


def _experts(plan, rows, g_moe, wg, wu, wd):
    n_tiles = plan["n_tiles"]
    d, de = wg.shape[-2:]
    w_a = lambda shape: pl.BlockSpec(shape, lambda i, ea, eb, *_: (ea[i], 0, 0))
    w_b = lambda shape: pl.BlockSpec(shape, lambda i, ea, eb, *_: (eb[i], 0, 0))
    return pl.pallas_call(
        _expert_kernel,
        grid_spec=pltpu.PrefetchScalarGridSpec(
            num_scalar_prefetch=7, grid=(n_tiles,),
            in_specs=[pl.BlockSpec(memory_space=pl.ANY),
                      pl.BlockSpec((1, d), lambda i, *_: (0, 0)),
                      w_a((1, d, de)), w_a((1, d, de)), w_a((1, de, d)),
                      w_b((1, d, de)), w_b((1, d, de)), w_b((1, de, d))],
            out_specs=pl.BlockSpec((MOE_TILE * Y_RECORD, LANES), lambda i, *_: (i, 0)),
            scratch_shapes=[pltpu.VMEM((2, MOE_TILE * ROW_RECORD, LANES), jnp.float32),
                            pltpu.SemaphoreType.DMA((2,)),
                            pltpu.SMEM((n_tiles * MOE_TILE,), jnp.int32)]),
        out_shape=jax.ShapeDtypeStruct((n_tiles * MOE_TILE * Y_RECORD, LANES), jnp.float32),
        compiler_params=pltpu.CompilerParams(dimension_semantics=("arbitrary",),
                                             vmem_limit_bytes=VMEM_LIMIT),
        name="experts",
    )(plan["e_a"], plan["e_b"], plan["n_used"], plan["n_valid"], plan["tstart"], plan["cnt"], plan["slot"],
      rows, g_moe[None, :], wg, wu, wd, wg, wu, wd)


def _ple_final(plan, rows, p, y_sorted, g_ple, w_ple_gate, w_ple_proj, g_final):
    n_tok, dp = p.shape
    d = w_ple_gate.shape[0]
    bf16 = jnp.bfloat16
    const = lambda shape: pl.BlockSpec(shape, lambda i, *_: (0,) * len(shape))
    tile = lambda width: pl.BlockSpec((PLE_TILE, width), lambda i, *_: (i, 0))
    return pl.pallas_call(
        _ple_final_kernel,
        grid_spec=pltpu.PrefetchScalarGridSpec(
            num_scalar_prefetch=1, grid=(n_tok // PLE_TILE,),
            in_specs=[pl.BlockSpec((PLE_TILE * ROW_RECORD, LANES), lambda i, *_: (i, 0)),
                      tile(dp),
                      pl.BlockSpec(memory_space=pl.ANY),
                      const((1, d)), const((d, d)), const((dp, d)), const((1, d))],
            out_specs=tile(d),
            scratch_shapes=[pltpu.VMEM((2, PLE_TILE * Y_RECORD, LANES), jnp.float32),
                            pltpu.SemaphoreType.DMA((2,))]),
        out_shape=jax.ShapeDtypeStruct((n_tok, d), jnp.float32),
        compiler_params=pltpu.CompilerParams(dimension_semantics=("arbitrary",),
                                             vmem_limit_bytes=VMEM_LIMIT),
        name="ple_final",
    )(plan["slot"], rows, p, y_sorted,
      g_ple[None, :], w_ple_gate.astype(bf16), w_ple_proj.astype(bf16), g_final[None, :])


def kernel(x, p, g_mix, w_in, w_gla_gate, b_gla_gate, g_gla_out, w_conv, w_out, g_moe, w_group, b_group,
           w_router, b_router, w_exp_gate, w_exp_up, w_exp_down, g_ple, w_ple_gate, w_ple_proj, g_final):
    depth = w_in.shape[0]
    assert depth == 1, "the final norm is fused into the last (only) layer"
    b, s, d = x.shape
    n_tok = b * s
    assert s % (MIX_TILE * MIX_TILES_PER_STEP) == 0 and n_tok % MOE_TILE == 0 and n_tok % PLE_TILE == 0 and n_tok % PLACE_UNROLL == 0
    rows, meta, counts, wg16, wu16, wd16 = _mixer(
        x, g_mix[0], w_in[0], w_gla_gate[0], b_gla_gate[0], g_gla_out[0], w_conv[0], w_out[0], g_moe[0],
        w_group[0], b_group[0], w_router[0], b_router[0], w_exp_gate[0], w_exp_up[0], w_exp_down[0])
    plan = _sort_plan(meta, counts, n_tok)
    y_sorted = _experts(plan, rows, g_moe[0], wg16, wu16, wd16)
    out = _ple_final(plan, rows, p[0].reshape(n_tok, -1), y_sorted, g_ple[0], w_ple_gate[0], w_ple_proj[0],
                     g_final)
    return out.reshape(b, s, d)
```

```python
import functools

import jax
import jax.numpy as jnp
from jax import lax
from jax.experimental import pallas as pl
from jax.experimental.pallas import tpu as pltpu

EPS = 1e-6
GLA_HEADS = 4
GLA_DK = 64
GLA_DV = 128
GLA_QK = GLA_HEADS * GLA_DK
GLA_V = GLA_HEADS * GLA_DV
GLA_LOWRANK = 16
GLA_TAU = 16.0
CONV_K = 3
N_GROUPS = 4
EXPERTS_PER_GROUP = 4
N_EXPERTS = N_GROUPS * EXPERTS_PER_GROUP

LANES = 128
MIX_TILE = 256
MIX_LEVELS = 8
SUBLANES = 8
BF16_ROWS = 16
SUB_LEVELS = 3
MIX_TILES_PER_STEP = 1
ROUTE_ROWS = 32
ROUTE_EXPERT_ROW0 = 8
MOE_TILE = 256
VMEM_LIMIT = 56 * 1024 * 1024

PAIR_A = (0, 2, 2, 0, 0, 1)
PAIR_B = (1, 1, 3, 3, 2, 3)
PAIR_OF_KEY = {1: 0, 6: 1, 11: 2, 3: 3, 2: 4, 7: 5}
N_PAIRS = len(PAIR_A)
N_CLASSES = N_GROUPS * N_PAIRS
INFO_CLS, INFO_RANK, INFO_WA, INFO_WB = 0, 1, 2, 3
PLACE_UNROLL = 16
ROW_RECORD = 9
Y_RECORD = 8
GATHER_GROUP = 32
PLE_TILE = 512

_NT = (((1,), (1,)), ((), ()))
_DONE = object()


def _rms(x, g):
    return x * lax.rsqrt(jnp.mean(x * x, axis=-1, keepdims=True) + EPS) * g


def _dot(a, b):
    return jnp.dot(a, b, preferred_element_type=jnp.float32)


def _dot_nt(a, b):
    return lax.dot_general(a, b, _NT, preferred_element_type=jnp.float32)


def _split_bf16(a):
    hi = a.astype(jnp.bfloat16)
    return hi, (a - hi.astype(jnp.float32)).astype(jnp.bfloat16)


def _shift_rows(x, shift):
    return pltpu.roll(x, shift % x.shape[0], axis=0)


def _mixer_kernel(x_ref, gmix_ref, wqkvg_ref, wa_ref, wc3_ref, wgate_ref, bgate_ref, ggla_ref,
                  wconv_ref, wout_ref, gmoe_ref, wrt_ref, brt_ref, wge_ref, wue_ref, wde_ref,
                  rows_ref, meta_ref, counts_ref, wge16_ref, wue16_ref, wde16_ref,
                  st_ref, carry_ref, count_ref, level_ref, tril_ref):
    bf16 = jnp.bfloat16
    T = MIX_TILE

    wge16_ref[...] = wge_ref[...].astype(bf16)
    wue16_ref[...] = wue_ref[...].astype(bf16)
    wde16_ref[...] = wde_ref[...].astype(bf16)

    @pl.when(pl.program_id(0) == 0)
    def _():
        st_ref[...] = jnp.zeros_like(st_ref)
        carry_ref[...] = jnp.zeros_like(carry_ref)
        count_ref[...] = jnp.zeros_like(count_ref)
        tt = lax.broadcasted_iota(jnp.int32, (T, T), 0)
        ss = lax.broadcasted_iota(jnp.int32, (T, T), 1)
        txs = jnp.bitwise_xor(tt, ss)
        level = jnp.zeros((T, T), jnp.int32)
        for j in range(1, MIX_LEVELS):
            level = level + (txs >= (1 << j)).astype(jnp.int32)
        level_ref[...] = jnp.where(tt > ss, level, jnp.where(tt == ss, MIX_LEVELS, -1))
        tril_ref[...] = (ss <= tt).astype(bf16)

    tiles = [_mixer_tile(x_ref.at[pl.ds(b, 1), pl.ds(i * T, T)], gmix_ref, wqkvg_ref, wa_ref, wc3_ref, wgate_ref,
                         bgate_ref, ggla_ref, wconv_ref, wout_ref, gmoe_ref, wrt_ref, brt_ref,
                         rows_ref.at[b, pl.ds(i * T * ROW_RECORD, T * ROW_RECORD)], meta_ref.at[b, pl.ds(i, 1)],
                         counts_ref, st_ref.at[b], carry_ref.at[b], count_ref, level_ref, tril_ref)
             for i in range(MIX_TILES_PER_STEP) for b in range(x_ref.shape[0])]
    while tiles:
        tiles = [t for t in tiles if next(t, _DONE) is not _DONE]


def _mixer_tile(x_ref, gmix_ref, wqkvg_ref, wa_ref, wc3_ref, wgate_ref, bgate_ref, ggla_ref,
                wconv_ref, wout_ref, gmoe_ref, wrt_ref, brt_ref,
                rows_ref, meta_ref, counts_ref,
                st_ref, carry_ref, count_ref, level_ref, tril_ref):
    f32, bf16 = jnp.float32, jnp.bfloat16
    T = MIX_TILE
    D = x_ref.shape[-1]

    x = x_ref[0]
    hb = _rms(x, gmix_ref[...]).astype(bf16)
    qkvg = _dot(hb, wqkvg_ref[...])
    q = qkvg[:, :GLA_QK] * (GLA_DK ** -0.5)
    k = qkvg[:, GLA_QK:2 * GLA_QK]
    v = qkvg[:, 2 * GLA_QK:2 * GLA_QK + GLA_V]
    g = qkvg[:, 2 * GLA_QK + GLA_V:]
    a_low = _dot(hb, wa_ref[...])
    a_hi, a_lo = _split_bf16(a_low)
    z = _dot(jnp.concatenate([a_hi, a_lo, a_hi], axis=1), wgate_ref[...]) + bgate_ref[...]
    la = (jnp.minimum(z, 0.0) - jnp.log(1.0 + jnp.exp(-jnp.abs(z)))) * (1.0 / GLA_TAU)
    yield

    row = lax.broadcasted_iota(jnp.int32, (T, GLA_QK), 0)

    def next_level(l, q_l, k_l, block):
        upper = ((row >> l) & 1) == 1
        below = _shift_rows(block, 1 << l)
        above = _shift_rows(block, -(1 << l))
        return (q_l * jnp.where(upper, below, 1.0), k_l * jnp.where(upper, 1.0, above),
                block * jnp.where(upper, below, above))

    decay = jnp.exp(la)
    H = T // 2
    assert GLA_DV == H
    half_level = level_ref[0:H, 0:H]
    lane_head_st = lax.broadcasted_iota(jnp.int32, (H, GLA_QK), 1) // GLA_DK

    heads_per_tile = LANES // GLA_DK
    lane_head_tile = lax.broadcasted_iota(jnp.int32, (H, LANES), 1) // GLA_DK

    def head_scores(q_half, k_half):
        out = []
        for tile in range(GLA_QK // LANES):
            q_t = q_half[:, tile * LANES:(tile + 1) * LANES]
            k_t = k_half[:, tile * LANES:(tile + 1) * LANES]
            k_rows = jnp.concatenate([jnp.where(lane_head_tile == j, k_t, jnp.zeros_like(k_t))
                                      for j in range(heads_per_tile)], axis=0)
            p = _dot_nt(q_t, k_rows)
            out += [p[:, j * H:(j + 1) * H] for j in range(heads_per_tile)]
        return out

    diag0 = [jnp.zeros((H, H), f32) for _ in range(GLA_HEADS)]
    diag1 = [jnp.zeros((H, H), f32) for _ in range(GLA_HEADS)]

    def add_level(l, q_l, k_l):
        sel = half_level == l
        ql, kl = q_l.astype(bf16), k_l.astype(bf16)
        p0 = head_scores(ql[:H], kl[:H])
        p1 = head_scores(ql[H:], kl[H:])
        for h in range(GLA_HEADS):
            diag0[h] = jnp.where(sel, p0[h], diag0[h])
            diag1[h] = jnp.where(sel, p1[h], diag1[h])

    def split_groups(a):
        return [a[i * SUBLANES:(i + 1) * SUBLANES] for i in range(T // SUBLANES)]

    def join_groups(groups):
        return jnp.concatenate(groups, axis=0)

    def next_level_groups(l, q_g, k_g, block_g):
        m = 1 << (l - SUB_LEVELS)
        products = {}
        q_n, k_n, block_n = [], [], []
        for gi in range(len(q_g)):
            lo, hi = gi & ~m, gi | m
            key = (id(block_g[lo]), id(block_g[hi]))
            if key not in products:
                products[key] = block_g[lo] * block_g[hi]
            q_n.append(q_g[gi] * block_g[lo] if gi & m else q_g[gi])
            k_n.append(k_g[gi] if gi & m else k_g[gi] * block_g[hi])
            block_n.append(products[key])
        return q_n, k_n, block_n

    add_level(MIX_LEVELS, q, k)
    q_l, k_l, block = q * decay, k, decay
    for l in range(SUB_LEVELS):
        add_level(l, q_l, k_l)
        q_l, k_l, block = next_level(l, q_l, k_l, block)
        yield
    q_g, k_g, block_g = split_groups(q_l), split_groups(k_l), split_groups(block)
    half = len(q_g) // 2
    for l in range(SUB_LEVELS, MIX_LEVELS - 1):
        add_level(l, join_groups(q_g), join_groups(k_g))
        q_g, k_g, block_g = next_level_groups(l, q_g, k_g, block_g)
        yield
    low = head_scores(join_groups(q_g[half:]).astype(bf16), join_groups(k_g[:half]).astype(bf16))
    zero_block = jnp.zeros((H, H), f32)
    scores = [jnp.concatenate([jnp.concatenate([diag0[h], zero_block], axis=1),
                               jnp.concatenate([low[h], diag1[h]], axis=1)], axis=0) for h in range(GLA_HEADS)]

    yield

    q_g, k_g, block_g = next_level_groups(MIX_LEVELS - 1, q_g, k_g, block_g)
    st = st_ref[...]
    o_state = head_scores(join_groups(q_g).astype(bf16), st.astype(bf16))
    upd = _dot(v.T.astype(bf16), join_groups(k_g).astype(bf16))
    new_st = st * block_g[0][0:1]
    for h in range(GLA_HEADS):
        new_st = new_st + jnp.where(lane_head_st == h, upd[h * GLA_DV:(h + 1) * GLA_DV], 0.0)
    st_ref[...] = new_st
    ggla = ggla_ref[...]
    y_heads = []
    for h in range(GLA_HEADS):
        v_h = v[:, h * GLA_DV:(h + 1) * GLA_DV]
        o = _dot(scores[h].astype(bf16), v_h.astype(bf16)) + o_state[h]
        g_h = g[:, h * GLA_DV:(h + 1) * GLA_DV]
        y_heads.append(_rms(o, ggla) * (g_h * jax.nn.sigmoid(g_h)))
        yield

    yield

    c3 = _dot(hb, wc3_ref[...])
    cw = c3.shape[1] // 3
    cb, cu = c3[:, :cw], c3[:, cw:2 * cw] * c3[:, 2 * cw:]
    crow = lax.broadcasted_iota(jnp.int32, (T, cw), 0)
    prev2, prev1 = carry_ref[0:1, :], carry_ref[1:2, :]
    m1 = jnp.where(crow == 0, prev1, _shift_rows(cu, 1))
    m2 = jnp.where(crow == 0, prev2, jnp.where(crow == 1, prev1, _shift_rows(cu, 2)))
    wconv = wconv_ref[...]
    y_conv = cb * (wconv[0:1, :] * m2 + wconv[1:2, :] * m1 + wconv[2:3, :] * cu)
    carry_ref[0:2, :] = cu[T - 2:, :]

    y = jnp.concatenate(y_heads + [y_conv], axis=1).astype(bf16)
    x1 = x + _dot(y, wout_ref[...])

    yield

    h2 = _rms(x1, gmoe_ref[...])
    h2_hi, h2_lo = _split_bf16(h2)
    part = _dot_nt(wrt_ref[...], h2_hi)
    logits = (part[:ROUTE_ROWS] + part[ROUTE_ROWS:] + _dot_nt(wrt_ref[:ROUTE_ROWS, :], h2_lo)) + brt_ref[...]
    gl = [logits[i:i + 1, :] for i in range(N_GROUPS)]
    gmax = functools.reduce(jnp.maximum, gl)
    gsum = functools.reduce(lambda a, b: a + b, [jnp.exp(t - gmax) for t in gl])
    p_grp = 1.0 / gsum
    g_sel = jnp.full_like(gmax, N_GROUPS - 1).astype(jnp.int32)
    for i in reversed(range(N_GROUPS - 1)):
        g_sel = jnp.where(gl[i] == gmax, i, g_sel)
    ig = []
    for j in range(EXPERTS_PER_GROUP):
        acc = jnp.zeros_like(gmax)
        for gi in range(N_GROUPS):
            r0 = ROUTE_EXPERT_ROW0 + gi * EXPERTS_PER_GROUP + j
            acc = acc + jnp.where(g_sel == gi, logits[r0:r0 + 1, :], 0.0)
        ig.append(acc)

    def first_argmax(vals):
        m = functools.reduce(jnp.maximum, vals)
        idx = jnp.full_like(m, len(vals) - 1).astype(jnp.int32)
        for i in reversed(range(len(vals) - 1)):
            idx = jnp.where(vals[i] == m, i, idx)
        return m, idx

    m1_, i1 = first_argmax(ig)
    m2_, i2 = first_argmax([jnp.where(i1 == j, -jnp.inf, ig[j]) for j in range(EXPERTS_PER_GROUP)])
    e21 = jnp.exp(m2_ - m1_)
    w1 = p_grp / (1.0 + e21)
    w2 = p_grp * e21 / (1.0 + e21)
    key = jnp.minimum(i1, i2) * EXPERTS_PER_GROUP + jnp.maximum(i1, i2)
    pair = jnp.zeros_like(key)
    a_loc = jnp.zeros_like(key)
    for kk, pp in PAIR_OF_KEY.items():
        pair = jnp.where(key == kk, pp, pair)
        a_loc = jnp.where(key == kk, PAIR_A[pp], a_loc)
    w_a = jnp.where(i1 == a_loc, w1, w2)
    w_b = jnp.where(i1 == a_loc, w2, w1)
    cls = g_sel * N_PAIRS + pair
    rr = lax.broadcasted_iota(jnp.int32, (LANES, T), 0)
    rec_t = (jnp.where(rr == INFO_WA, jnp.broadcast_to(w_a, (LANES, T)), 0.0)
             + jnp.where(rr == INFO_WB, jnp.broadcast_to(w_b, (LANES, T)), 0.0))
    pieces = [x1[:, c * LANES:(c + 1) * LANES] for c in range(D // LANES)] + [rec_t.T]
    assert len(pieces) == ROW_RECORD
    _store_records(rows_ref, pieces, ROW_RECORD)

    onehot = (rr == jnp.broadcast_to(cls, (LANES, T))).astype(f32)
    count = count_ref[:, 0:1]
    before = _dot_nt(onehot.astype(bf16), tril_ref[...]) - onehot + count
    rank = jnp.sum(onehot * before, axis=0, keepdims=True).astype(jnp.int32)
    r8 = lax.broadcasted_iota(jnp.int32, (8, T), 0)
    meta_ref[0] = jnp.where(r8 == INFO_CLS, jnp.broadcast_to(cls, (8, T)),
                            jnp.where(r8 == INFO_RANK, jnp.broadcast_to(rank, (8, T)), 0))
    new_count = jnp.broadcast_to(count + jnp.sum(onehot, axis=1, keepdims=True), count_ref.shape)
    count_ref[...] = new_count
    counts_ref[...] = new_count.astype(jnp.int32)


def _store_records(ref, pieces, record_rows):
    n = ref.shape[0] // record_rows
    for c, piece in enumerate(pieces):
        ref[pl.ds(c, n, stride=record_rows), :] = piece


def _load_records(ref, first, count, record_rows, lead=()):
    n = ref.shape[-2] // record_rows
    return jnp.concatenate([ref[lead + (pl.ds(first + c, n, stride=record_rows), slice(None))]
                            for c in range(count)], axis=1)


class _RowGather:
    def __init__(self, index_of, src_hbm, buf, sems, record_rows, n_records, n_valid_of=None):
        self.index_of, self.src_hbm, self.buf, self.sems = index_of, src_hbm, buf, sems
        self.rr, self.n, self.n_valid_of = record_rows, n_records, n_valid_of

    def _groups(self, tile, body):
        for g0 in range(0, self.n, GATHER_GROUP):
            if self.n_valid_of is None:
                body(g0)
            else:
                pl.when(g0 < self.n_valid_of(tile))(functools.partial(body, g0))

    def _issue(self, tile, slot):
        rr = self.rr

        def group(g0):
            for r in range(g0, g0 + GATHER_GROUP):
                first = self.index_of(tile * self.n + r) * rr
                pltpu.make_async_copy(self.src_hbm.at[pl.ds(first, rr), :],
                                      self.buf.at[slot, pl.ds(r * rr, rr), :], self.sems.at[slot]).start()

        self._groups(tile, group)

    def start(self, tile, slot):
        if isinstance(slot, int):
            self._issue(tile, slot)
        else:
            for static_slot in range(2):
                pl.when(slot == static_slot)(functools.partial(self._issue, tile, static_slot))

    def wait(self, tile, slot):
        rows = GATHER_GROUP * self.rr

        def group(g0):
            pltpu.make_async_copy(self.src_hbm.at[pl.ds(0, rows), :],
                                  self.buf.at[slot, pl.ds(g0 * self.rr, rows), :], self.sems.at[slot]).wait()

        self._groups(tile, group)


def _expert_kernel(ea_ref, eb_ref, nused_ref, nvalid_ref, tstart_ref, cnt_ref, slot_ref,
                   rows_hbm, gmoe_ref, wga_ref, wua_ref, wda_ref, wgb_ref, wub_ref, wdb_ref,
                   y_ref, buf, sems, src_ref):
    bf16 = jnp.bfloat16
    n_x = gmoe_ref.shape[-1] // LANES
    n_tok = slot_ref.shape[0]
    step, n_used = pl.program_id(0), nused_ref[0]
    gather = _RowGather(lambda i: src_ref[i], rows_hbm, buf, sems, ROW_RECORD, MOE_TILE,
                        n_valid_of=lambda tile: nvalid_ref[tile])

    @pl.when(step == 0)
    def _():
        buf[...] = jnp.zeros_like(buf)
        for c in range(N_CLASSES):
            base = tstart_ref[c] * MOE_TILE

            def pad(r, carry, base=base):
                src_ref[base + r] = jnp.minimum(base + r, n_tok - 1)
                return carry

            lax.fori_loop(cnt_ref[c], (tstart_ref[c + 1] - tstart_ref[c]) * MOE_TILE, pad, 0)

        def place(it, carry):
            for u in range(PLACE_UNROLL):
                t = it * PLACE_UNROLL + u
                src_ref[slot_ref[t]] = t
            return carry

        lax.fori_loop(0, n_tok // PLACE_UNROLL, place, 0)
        gather.start(0, 0)

    slot = step % 2

    @pl.when(step + 1 < n_used)
    def _():
        gather.start(step + 1, 1 - slot)

    @pl.when(step < n_used)
    def _():
        gather.wait(step, slot)
        rec = _load_records(buf, n_x, 1, ROW_RECORD, lead=(slot,))
        h2 = _rms(_load_records(buf, 0, n_x, ROW_RECORD, lead=(slot,)), gmoe_ref[...]).astype(bf16)

        def expert(wg_ref, wu_ref, wd_ref):
            gate = _dot(h2, wg_ref[0])
            hid = (gate * jax.nn.sigmoid(gate)) * _dot(h2, wu_ref[0])
            return _dot(hid.astype(bf16), wd_ref[0])

        y = rec[:, INFO_WA:INFO_WA + 1] * expert(wga_ref, wua_ref, wda_ref)
        y = y + rec[:, INFO_WB:INFO_WB + 1] * expert(wgb_ref, wub_ref, wdb_ref)
        _store_records(y_ref, [y[:, c * LANES:(c + 1) * LANES] for c in range(n_x)], Y_RECORD)

    @pl.when(step >= n_used)
    def _():
        y_ref[...] = jnp.zeros_like(y_ref)


def _ple_final_kernel(slot_ref, x1_ref, p_ref, y_hbm, gple_ref, wpg_ref, wpp_ref, gfin_ref,
                      out_ref, buf, sems):
    bf16 = jnp.bfloat16
    step, n_steps = pl.program_id(0), pl.num_programs(0)
    n_x = gple_ref.shape[-1] // LANES
    gather = _RowGather(lambda t: slot_ref[t], y_hbm, buf, sems, Y_RECORD, PLE_TILE)

    @pl.when(step == 0)
    def _():
        gather.start(0, 0)

    slot = step % 2

    @pl.when(step + 1 < n_steps)
    def _():
        gather.start(step + 1, 1 - slot)

    gather.wait(step, slot)
    x2 = _load_records(x1_ref, 0, n_x, ROW_RECORD) + _load_records(buf, 0, n_x, Y_RECORD, lead=(slot,))
    gate_p = jax.nn.sigmoid(_dot(_rms(x2, gple_ref[...]).astype(bf16), wpg_ref[...]))
    x3 = x2 + gate_p * _dot(p_ref[...].astype(bf16), wpp_ref[...])
    out_ref[...] = _rms(x3, gfin_ref[...])


def _const_spec(shape):
    return pl.BlockSpec(shape, lambda *_: (0,) * len(shape), pipeline_mode=pl.Buffered(1))


def _mixer(x, g_mix, w_in, w_gla_gate, b_gla_gate, g_gla_out, w_conv, w_out, g_moe,
           w_group, b_group, w_router, b_router, w_exp_gate, w_exp_up, w_exp_down):
    b, s, d = x.shape
    step_tokens = MIX_TILE * MIX_TILES_PER_STEP
    n_steps = s // step_tokens
    n_exp, _, de = w_exp_gate.shape
    assert (n_exp * d) % (BF16_ROWS * n_steps) == 0 and (n_exp * de) % (BF16_ROWS * n_steps) == 0
    up_rows, down_rows = n_exp * d // n_steps, n_exp * de // n_steps
    bf16 = jnp.bfloat16
    n_qkvg = 2 * GLA_QK + 2 * GLA_V
    w_qkvg = w_in[:, :n_qkvg].astype(bf16)
    w_a = w_in[:, n_qkvg:n_qkvg + GLA_LOWRANK].astype(bf16)
    w_c3 = w_in[:, n_qkvg + GLA_LOWRANK:].astype(bf16)
    cw = w_c3.shape[1] // 3
    wrt = jnp.zeros((ROUTE_ROWS, d), jnp.float32)
    wrt = wrt.at[:N_GROUPS].set(w_group.T).at[ROUTE_EXPERT_ROW0:ROUTE_EXPERT_ROW0 + N_EXPERTS].set(w_router.T)
    brt = jnp.zeros((ROUTE_ROWS, 1), jnp.float32)
    brt = brt.at[:N_GROUPS, 0].set(b_group).at[ROUTE_EXPERT_ROW0:ROUTE_EXPERT_ROW0 + N_EXPERTS, 0].set(b_router)
    wrt_split = jnp.concatenate(_split_bf16(wrt), axis=0)
    gate_hi, gate_lo = _split_bf16(w_gla_gate)
    w_gate_split = jnp.concatenate([gate_hi, gate_hi, gate_lo], axis=0)
    args = (x, g_mix[None, :], w_qkvg, w_a, w_c3, w_gate_split, b_gla_gate[None, :], g_gla_out[None, :],
            w_conv, w_out.astype(bf16), g_moe[None, :], wrt_split, brt)
    slabs = (w_exp_gate.reshape(n_exp * d, de), w_exp_up.reshape(n_exp * d, de), w_exp_down.reshape(n_exp * de, d))
    slab_specs = [pl.BlockSpec((rows_, width), lambda j: (j, 0))
                  for rows_, width in ((up_rows, de), (up_rows, de), (down_rows, d))]
    in_specs = [pl.BlockSpec((b, step_tokens, d), lambda j: (0, j, 0))]
    in_specs += [_const_spec(a.shape) for a in args[1:]] + slab_specs
    rows, meta, counts, wg16, wu16, wd16 = pl.pallas_call(
        _mixer_kernel,
        grid=(n_steps,),
        in_specs=in_specs,
        out_specs=[pl.BlockSpec((b, step_tokens * ROW_RECORD, LANES), lambda j: (0, j, 0)),
                   pl.BlockSpec((b, MIX_TILES_PER_STEP, 8, MIX_TILE), lambda j: (0, j, 0, 0)),
                   pl.BlockSpec((LANES, LANES), lambda j: (0, 0))] + slab_specs,
        out_shape=[jax.ShapeDtypeStruct((b, s * ROW_RECORD, LANES), jnp.float32),
                   jax.ShapeDtypeStruct((b, s // MIX_TILE, 8, MIX_TILE), jnp.int32),
                   jax.ShapeDtypeStruct((LANES, LANES), jnp.int32)]
                  + [jax.ShapeDtypeStruct(w.shape, bf16) for w in slabs],
        scratch_shapes=[pltpu.VMEM((b, GLA_DV, GLA_QK), jnp.float32),
                        pltpu.VMEM((b, 8, cw), jnp.float32),
                        pltpu.VMEM((LANES, LANES), jnp.float32),
                        pltpu.VMEM((MIX_TILE, MIX_TILE), jnp.int32),
                        pltpu.VMEM((MIX_TILE, MIX_TILE), bf16)],
        compiler_params=pltpu.CompilerParams(dimension_semantics=("arbitrary",),
                                             vmem_limit_bytes=VMEM_LIMIT),
        name="mixer",
    )(*args, *slabs)
    return (rows.reshape(b * s * ROW_RECORD, LANES), meta.reshape(b * s // MIX_TILE, 8, MIX_TILE), counts,
            wg16.reshape(n_exp, d, de), wu16.reshape(n_exp, d, de), wd16.reshape(n_exp, de, d))


def _sort_plan(meta, counts, n_tok):
    i32 = jnp.int32
    n_tiles = n_tok // MOE_TILE + N_CLASSES
    cls = meta[:, INFO_CLS, :].reshape(n_tok)
    rank = meta[:, INFO_RANK, :].reshape(n_tok)
    cnt = counts[:N_CLASSES, 0]
    tiles_per_cls = (cnt + MOE_TILE - 1) // MOE_TILE
    tile_end = jnp.cumsum(tiles_per_cls)
    n_used = tile_end[-1:]
    tstart = jnp.concatenate([tile_end - tiles_per_cls, n_used])
    tile_id = jnp.minimum(jnp.arange(n_tiles, dtype=i32), n_used - 1)
    tile_cls = jnp.sum((tile_id[:, None] >= tile_end[None, :]).astype(i32), axis=1)
    grp, pair = tile_cls // N_PAIRS, tile_cls % N_PAIRS
    e_a, e_b = grp * EXPERTS_PER_GROUP, grp * EXPERTS_PER_GROUP
    for pp in range(N_PAIRS):
        e_a = e_a + jnp.where(pair == pp, PAIR_A[pp], 0)
        e_b = e_b + jnp.where(pair == pp, PAIR_B[pp], 0)
    of_tile = (tile_cls[:, None] == jnp.arange(N_CLASSES, dtype=i32)[None, :]).astype(i32)
    tile_base = jnp.sum(of_tile * tstart[None, :N_CLASSES], axis=1)
    tile_cnt = jnp.sum(of_tile * cnt[None, :], axis=1)
    n_valid = jnp.clip(tile_cnt - (tile_id - tile_base) * MOE_TILE, 0, MOE_TILE)
    first_tile = functools.reduce(lambda acc, c: jnp.where(cls == c, tstart[c], acc), range(N_CLASSES),
                                  jnp.zeros_like(cls))
    slot = first_tile * MOE_TILE + rank
    return dict(slot=slot, n_valid=n_valid, cnt=cnt, tstart=tstart, e_a=e_a, e_b=e_b, n_used=n_used, n_tiles=n_tiles)


def _experts(plan, rows, g_moe, wg, wu, wd):
    n_tiles = plan["n_tiles"]
    d, de = wg.shape[-2:]
    w_a = lambda shape: pl.BlockSpec(shape, lambda i, ea, eb, *_: (ea[i], 0, 0))
    w_b = lambda shape: pl.BlockSpec(shape, lambda i, ea, eb, *_: (eb[i], 0, 0))
    return pl.pallas_call(
        _expert_kernel,
        grid_spec=pltpu.PrefetchScalarGridSpec(
            num_scalar_prefetch=7, grid=(n_tiles,),
            in_specs=[pl.BlockSpec(memory_space=pl.ANY),
                      pl.BlockSpec((1, d), lambda i, *_: (0, 0)),
                      w_a((1, d, de)), w_a((1, d, de)), w_a((1, de, d)),
                      w_b((1, d, de)), w_b((1, d, de)), w_b((1, de, d))],
            out_specs=pl.BlockSpec((MOE_TILE * Y_RECORD, LANES), lambda i, *_: (i, 0)),
            scratch_shapes=[pltpu.VMEM((2, MOE_TILE * ROW_RECORD, LANES), jnp.float32),
                            pltpu.SemaphoreType.DMA((2,)),
                            pltpu.SMEM((n_tiles * MOE_TILE,), jnp.int32)]),
        out_shape=jax.ShapeDtypeStruct((n_tiles * MOE_TILE * Y_RECORD, LANES), jnp.float32),
        compiler_params=pltpu.CompilerParams(dimension_semantics=("arbitrary",),
                                             vmem_limit_bytes=VMEM_LIMIT),
        name="experts",
    )(plan["e_a"], plan["e_b"], plan["n_used"], plan["n_valid"], plan["tstart"], plan["cnt"], plan["slot"],
      rows, g_moe[None, :], wg, wu, wd, wg, wu, wd)


def _ple_final(plan, rows, p, y_sorted, g_ple, w_ple_gate, w_ple_proj, g_final):
    n_tok, dp = p.shape
    d = w_ple_gate.shape[0]
    bf16 = jnp.bfloat16
    const = lambda shape: pl.BlockSpec(shape, lambda i, *_: (0,) * len(shape))
    tile = lambda width: pl.BlockSpec((PLE_TILE, width), lambda i, *_: (i, 0))
    return pl.pallas_call(
        _ple_final_kernel,
        grid_spec=pltpu.PrefetchScalarGridSpec(
            num_scalar_prefetch=1, grid=(n_tok // PLE_TILE,),
            in_specs=[pl.BlockSpec((PLE_TILE * ROW_RECORD, LANES), lambda i, *_: (i, 0)),
                      tile(dp),
                      pl.BlockSpec(memory_space=pl.ANY),
                      const((1, d)), const((d, d)), const((dp, d)), const((1, d))],
            out_specs=tile(d),
            scratch_shapes=[pltpu.VMEM((2, PLE_TILE * Y_RECORD, LANES), jnp.float32),
                            pltpu.SemaphoreType.DMA((2,))]),
        out_shape=jax.ShapeDtypeStruct((n_tok, d), jnp.float32),
        compiler_params=pltpu.CompilerParams(dimension_semantics=("arbitrary",),
                                             vmem_limit_bytes=VMEM_LIMIT),
        name="ple_final",
    )(plan["slot"], rows, p, y_sorted,
      g_ple[None, :], w_ple_gate.astype(bf16), w_ple_proj.astype(bf16), g_final[None, :])


def kernel(x, p, g_mix, w_in, w_gla_gate, b_gla_gate, g_gla_out, w_conv, w_out, g_moe, w_group, b_group,
           w_router, b_router, w_exp_gate, w_exp_up, w_exp_down, g_ple, w_ple_gate, w_ple_proj, g_final):
    depth = w_in.shape[0]
    assert depth == 1, "the final norm is fused into the last (only) layer"
    b, s, d = x.shape
    n_tok = b * s
    assert s % (MIX_TILE * MIX_TILES_PER_STEP) == 0 and n_tok % MOE_TILE == 0 and n_tok % PLE_TILE == 0 and n_tok % PLACE_UNROLL == 0
    rows, meta, counts, wg16, wu16, wd16 = _mixer(
        x, g_mix[0], w_in[0], w_gla_gate[0], b_gla_gate[0], g_gla_out[0], w_conv[0], w_out[0], g_moe[0],
        w_group[0], b_group[0], w_router[0], b_router[0], w_exp_gate[0], w_exp_up[0], w_exp_down[0])
    plan = _sort_plan(meta, counts, n_tok)
    y_sorted = _experts(plan, rows, g_moe[0], wg16, wu16, wd16)
    out = _ple_final(plan, rows, p[0].reshape(n_tok, -1), y_sorted, g_ple[0], w_ple_gate[0], w_ple_proj[0],
                     g_final)
    return out.reshape(b, s, d)
```

```python
import functools

import jax
import jax.numpy as jnp
from jax import lax
from jax.experimental import pallas as pl
from jax.experimental.pallas import tpu as pltpu

EPS = 1e-6
GLA_HEADS = 4
GLA_DK = 64
GLA_DV = 128
GLA_QK = GLA_HEADS * GLA_DK
GLA_V = GLA_HEADS * GLA_DV
GLA_LOWRANK = 16
GLA_TAU = 16.0
CONV_K = 3
N_GROUPS = 4
EXPERTS_PER_GROUP = 4
N_EXPERTS = N_GROUPS * EXPERTS_PER_GROUP

LANES = 128
MIX_TILE = 256
MIX_LEVELS = 8
SUBLANES = 8
BF16_ROWS = 16
SUB_LEVELS = 3
MIX_TILES_PER_STEP = 1
ROUTE_ROWS = 32
ROUTE_EXPERT_ROW0 = 8
MOE_TILE = 256
VMEM_LIMIT = 56 * 1024 * 1024

PAIR_A = (0, 2, 2, 0, 0, 1)
PAIR_B = (1, 1, 3, 3, 2, 3)
PAIR_OF_KEY = {1: 0, 6: 1, 11: 2, 3: 3, 2: 4, 7: 5}
N_PAIRS = len(PAIR_A)
N_CLASSES = N_GROUPS * N_PAIRS
INFO_CLS, INFO_RANK, INFO_WA, INFO_WB = 0, 1, 2, 3
PLACE_UNROLL = 16
ROW_RECORD = 9
Y_RECORD = 8
GATHER_GROUP = 32
EXPERT_PREFETCH = 2
PLE_TILE = 512

_NT = (((1,), (1,)), ((), ()))
_DONE = object()


def _rms(x, g):
    return x * lax.rsqrt(jnp.mean(x * x, axis=-1, keepdims=True) + EPS) * g


def _dot(a, b):
    return jnp.dot(a, b, preferred_element_type=jnp.float32)


def _dot_nt(a, b):
    return lax.dot_general(a, b, _NT, preferred_element_type=jnp.float32)


def _split_bf16(a):
    hi = a.astype(jnp.bfloat16)
    return hi, (a - hi.astype(jnp.float32)).astype(jnp.bfloat16)


def _shift_rows(x, shift):
    return pltpu.roll(x, shift % x.shape[0], axis=0)


def _mixer_kernel(x_ref, gmix_ref, wqkvg_ref, wa_ref, wc3_ref, wgate_ref, bgate_ref, ggla_ref,
                  wconv_ref, wout_ref, gmoe_ref, wrt_ref, brt_ref, wge_ref, wue_ref, wde_ref,
                  rows_ref, meta_ref, counts_ref, wge16_ref, wue16_ref, wde16_ref,
                  st_ref, carry_ref, count_ref, level_ref, tril_ref):
    bf16 = jnp.bfloat16
    T = MIX_TILE

    wge16_ref[...] = wge_ref[...].astype(bf16)
    wue16_ref[...] = wue_ref[...].astype(bf16)
    wde16_ref[...] = wde_ref[...].astype(bf16)

    @pl.when(pl.program_id(0) == 0)
    def _():
        st_ref[...] = jnp.zeros_like(st_ref)
        carry_ref[...] = jnp.zeros_like(carry_ref)
        count_ref[...] = jnp.zeros_like(count_ref)
        tt = lax.broadcasted_iota(jnp.int32, (T, T), 0)
        ss = lax.broadcasted_iota(jnp.int32, (T, T), 1)
        txs = jnp.bitwise_xor(tt, ss)
        level = jnp.zeros((T, T), jnp.int32)
        for j in range(1, MIX_LEVELS):
            level = level + (txs >= (1 << j)).astype(jnp.int32)
        level_ref[...] = jnp.where(tt > ss, level, jnp.where(tt == ss, MIX_LEVELS, -1))
        tril_ref[...] = (ss <= tt).astype(bf16)

    tiles = [_mixer_tile(x_ref.at[pl.ds(b, 1), pl.ds(i * T, T)], gmix_ref, wqkvg_ref, wa_ref, wc3_ref, wgate_ref,
                         bgate_ref, ggla_ref, wconv_ref, wout_ref, gmoe_ref, wrt_ref, brt_ref,
                         rows_ref.at[b, pl.ds(i * T * ROW_RECORD, T * ROW_RECORD)], meta_ref.at[b, pl.ds(i, 1)],
                         counts_ref, st_ref.at[b], carry_ref.at[b], count_ref, level_ref, tril_ref)
             for i in range(MIX_TILES_PER_STEP) for b in range(x_ref.shape[0])]
    while tiles:
        tiles = [t for t in tiles if next(t, _DONE) is not _DONE]


def _mixer_tile(x_ref, gmix_ref, wqkvg_ref, wa_ref, wc3_ref, wgate_ref, bgate_ref, ggla_ref,
                wconv_ref, wout_ref, gmoe_ref, wrt_ref, brt_ref,
                rows_ref, meta_ref, counts_ref,
                st_ref, carry_ref, count_ref, level_ref, tril_ref):
    f32, bf16 = jnp.float32, jnp.bfloat16
    T = MIX_TILE
    D = x_ref.shape[-1]

    x = x_ref[0]
    hb = _rms(x, gmix_ref[...]).astype(bf16)
    qkvg = _dot(hb, wqkvg_ref[...])
    q = qkvg[:, :GLA_QK] * (GLA_DK ** -0.5)
    k = qkvg[:, GLA_QK:2 * GLA_QK]
    v = qkvg[:, 2 * GLA_QK:2 * GLA_QK + GLA_V]
    g = qkvg[:, 2 * GLA_QK + GLA_V:]
    a_low = _dot(hb, wa_ref[...])
    a_hi, a_lo = _split_bf16(a_low)
    z = _dot(jnp.concatenate([a_hi, a_lo, a_hi], axis=1), wgate_ref[...]) + bgate_ref[...]
    la = (jnp.minimum(z, 0.0) - jnp.log(1.0 + jnp.exp(-jnp.abs(z)))) * (1.0 / GLA_TAU)
    yield

    row = lax.broadcasted_iota(jnp.int32, (T, GLA_QK), 0)

    def next_level(l, q_l, k_l, block):
        upper = ((row >> l) & 1) == 1
        below = _shift_rows(block, 1 << l)
        above = _shift_rows(block, -(1 << l))
        return (q_l * jnp.where(upper, below, 1.0), k_l * jnp.where(upper, 1.0, above),
                block * jnp.where(upper, below, above))

    decay = jnp.exp(la)
    H = T // 2
    assert GLA_DV == H
    half_level = level_ref[0:H, 0:H]
    lane_head_st = lax.broadcasted_iota(jnp.int32, (H, GLA_QK), 1) // GLA_DK

    heads_per_tile = LANES // GLA_DK
    lane_head_tile = lax.broadcasted_iota(jnp.int32, (H, LANES), 1) // GLA_DK

    def head_scores(q_half, k_half):
        out = []
        for tile in range(GLA_QK // LANES):
            q_t = q_half[:, tile * LANES:(tile + 1) * LANES]
            k_t = k_half[:, tile * LANES:(tile + 1) * LANES]
            k_rows = jnp.concatenate([jnp.where(lane_head_tile == j, k_t, jnp.zeros_like(k_t))
                                      for j in range(heads_per_tile)], axis=0)
            p = _dot_nt(q_t, k_rows)
            out += [p[:, j * H:(j + 1) * H] for j in range(heads_per_tile)]
        return out

    diag0 = [jnp.zeros((H, H), f32) for _ in range(GLA_HEADS)]
    diag1 = [jnp.zeros((H, H), f32) for _ in range(GLA_HEADS)]

    def add_level(l, q_l, k_l):
        sel = half_level == l
        ql, kl = q_l.astype(bf16), k_l.astype(bf16)
        p0 = head_scores(ql[:H], kl[:H])
        p1 = head_scores(ql[H:], kl[H:])
        for h in range(GLA_HEADS):
            diag0[h] = jnp.where(sel, p0[h], diag0[h])
            diag1[h] = jnp.where(sel, p1[h], diag1[h])

    def split_groups(a):
        return [a[i * SUBLANES:(i + 1) * SUBLANES] for i in range(T // SUBLANES)]

    def join_groups(groups):
        return jnp.concatenate(groups, axis=0)

    def next_level_groups(l, q_g, k_g, block_g):
        m = 1 << (l - SUB_LEVELS)
        products = {}
        q_n, k_n, block_n = [], [], []
        for gi in range(len(q_g)):
            lo, hi = gi & ~m, gi | m
            key = (id(block_g[lo]), id(block_g[hi]))
            if key not in products:
                products[key] = block_g[lo] * block_g[hi]
            q_n.append(q_g[gi] * block_g[lo] if gi & m else q_g[gi])
            k_n.append(k_g[gi] if gi & m else k_g[gi] * block_g[hi])
            block_n.append(products[key])
        return q_n, k_n, block_n

    add_level(MIX_LEVELS, q, k)
    q_l, k_l, block = q * decay, k, decay
    for l in range(SUB_LEVELS):
        add_level(l, q_l, k_l)
        q_l, k_l, block = next_level(l, q_l, k_l, block)
        yield
    q_g, k_g, block_g = split_groups(q_l), split_groups(k_l), split_groups(block)
    half = len(q_g) // 2
    for l in range(SUB_LEVELS, MIX_LEVELS - 1):
        add_level(l, join_groups(q_g), join_groups(k_g))
        q_g, k_g, block_g = next_level_groups(l, q_g, k_g, block_g)
        yield
    low = head_scores(join_groups(q_g[half:]).astype(bf16), join_groups(k_g[:half]).astype(bf16))
    zero_block = jnp.zeros((H, H), f32)
    scores = [jnp.concatenate([jnp.concatenate([diag0[h], zero_block], axis=1),
                               jnp.concatenate([low[h], diag1[h]], axis=1)], axis=0) for h in range(GLA_HEADS)]

    yield

    q_g, k_g, block_g = next_level_groups(MIX_LEVELS - 1, q_g, k_g, block_g)
    st = st_ref[...]
    o_state = head_scores(join_groups(q_g).astype(bf16), st.astype(bf16))
    upd = _dot(v.T.astype(bf16), join_groups(k_g).astype(bf16))
    new_st = st * block_g[0][0:1]
    for h in range(GLA_HEADS):
        new_st = new_st + jnp.where(lane_head_st == h, upd[h * GLA_DV:(h + 1) * GLA_DV], 0.0)
    st_ref[...] = new_st
    ggla = ggla_ref[...]
    y_heads = []
    for h in range(GLA_HEADS):
        v_h = v[:, h * GLA_DV:(h + 1) * GLA_DV]
        o = _dot(scores[h].astype(bf16), v_h.astype(bf16)) + o_state[h]
        g_h = g[:, h * GLA_DV:(h + 1) * GLA_DV]
        y_heads.append(_rms(o, ggla) * (g_h * jax.nn.sigmoid(g_h)))
        yield

    yield

    c3 = _dot(hb, wc3_ref[...])
    cw = c3.shape[1] // 3
    cb, cu = c3[:, :cw], c3[:, cw:2 * cw] * c3[:, 2 * cw:]
    crow = lax.broadcasted_iota(jnp.int32, (T, cw), 0)
    prev2, prev1 = carry_ref[0:1, :], carry_ref[1:2, :]
    m1 = jnp.where(crow == 0, prev1, _shift_rows(cu, 1))
    m2 = jnp.where(crow == 0, prev2, jnp.where(crow == 1, prev1, _shift_rows(cu, 2)))
    wconv = wconv_ref[...]
    y_conv = cb * (wconv[0:1, :] * m2 + wconv[1:2, :] * m1 + wconv[2:3, :] * cu)
    carry_ref[0:2, :] = cu[T - 2:, :]

    y = jnp.concatenate(y_heads + [y_conv], axis=1).astype(bf16)
    x1 = x + _dot(y, wout_ref[...])

    yield

    h2 = _rms(x1, gmoe_ref[...])
    h2_hi, h2_lo = _split_bf16(h2)
    part = _dot_nt(wrt_ref[...], h2_hi)
    logits = (part[:ROUTE_ROWS] + part[ROUTE_ROWS:] + _dot_nt(wrt_ref[:ROUTE_ROWS, :], h2_lo)) + brt_ref[...]
    gl = [logits[i:i + 1, :] for i in range(N_GROUPS)]
    gmax = functools.reduce(jnp.maximum, gl)
    gsum = functools.reduce(lambda a, b: a + b, [jnp.exp(t - gmax) for t in gl])
    p_grp = 1.0 / gsum
    g_sel = jnp.full_like(gmax, N_GROUPS - 1).astype(jnp.int32)
    for i in reversed(range(N_GROUPS - 1)):
        g_sel = jnp.where(gl[i] == gmax, i, g_sel)
    ig = []
    for j in range(EXPERTS_PER_GROUP):
        acc = jnp.zeros_like(gmax)
        for gi in range(N_GROUPS):
            r0 = ROUTE_EXPERT_ROW0 + gi * EXPERTS_PER_GROUP + j
            acc = acc + jnp.where(g_sel == gi, logits[r0:r0 + 1, :], 0.0)
        ig.append(acc)

    def first_argmax(vals):
        m = functools.reduce(jnp.maximum, vals)
        idx = jnp.full_like(m, len(vals) - 1).astype(jnp.int32)
        for i in reversed(range(len(vals) - 1)):
            idx = jnp.where(vals[i] == m, i, idx)
        return m, idx

    m1_, i1 = first_argmax(ig)
    m2_, i2 = first_argmax([jnp.where(i1 == j, -jnp.inf, ig[j]) for j in range(EXPERTS_PER_GROUP)])
    e21 = jnp.exp(m2_ - m1_)
    w1 = p_grp / (1.0 + e21)
    w2 = p_grp * e21 / (1.0 + e21)
    key = jnp.minimum(i1, i2) * EXPERTS_PER_GROUP + jnp.maximum(i1, i2)
    pair = jnp.zeros_like(key)
    a_loc = jnp.zeros_like(key)
    for kk, pp in PAIR_OF_KEY.items():
        pair = jnp.where(key == kk, pp, pair)
        a_loc = jnp.where(key == kk, PAIR_A[pp], a_loc)
    w_a = jnp.where(i1 == a_loc, w1, w2)
    w_b = jnp.where(i1 == a_loc, w2, w1)
    cls = g_sel * N_PAIRS + pair
    rr = lax.broadcasted_iota(jnp.int32, (LANES, T), 0)
    rec_t = (jnp.where(rr == INFO_WA, jnp.broadcast_to(w_a, (LANES, T)), 0.0)
             + jnp.where(rr == INFO_WB, jnp.broadcast_to(w_b, (LANES, T)), 0.0))
    pieces = [x1[:, c * LANES:(c + 1) * LANES] for c in range(D // LANES)] + [rec_t.T]
    assert len(pieces) == ROW_RECORD
    _store_records(rows_ref, pieces, ROW_RECORD)

    onehot = (rr == jnp.broadcast_to(cls, (LANES, T))).astype(f32)
    count = count_ref[:, 0:1]
    before = _dot_nt(onehot.astype(bf16), tril_ref[...]) - onehot + count
    rank = jnp.sum(onehot * before, axis=0, keepdims=True).astype(jnp.int32)
    r8 = lax.broadcasted_iota(jnp.int32, (8, T), 0)
    meta_ref[0] = jnp.where(r8 == INFO_CLS, jnp.broadcast_to(cls, (8, T)),
                            jnp.where(r8 == INFO_RANK, jnp.broadcast_to(rank, (8, T)), 0))
    new_count = jnp.broadcast_to(count + jnp.sum(onehot, axis=1, keepdims=True), count_ref.shape)
    count_ref[...] = new_count
    counts_ref[...] = new_count.astype(jnp.int32)


def _store_records(ref, pieces, record_rows):
    n = ref.shape[0] // record_rows
    for c, piece in enumerate(pieces):
        ref[pl.ds(c, n, stride=record_rows), :] = piece


def _load_records(ref, first, count, record_rows, lead=()):
    n = ref.shape[-2] // record_rows
    return jnp.concatenate([ref[lead + (pl.ds(first + c, n, stride=record_rows), slice(None))]
                            for c in range(count)], axis=1)


class _RowGather:
    def __init__(self, index_of, src_hbm, buf, sems, record_rows, n_records, n_valid_of=None):
        self.index_of, self.src_hbm, self.buf, self.sems = index_of, src_hbm, buf, sems
        self.rr, self.n, self.n_valid_of = record_rows, n_records, n_valid_of

    def _groups(self, tile, body):
        for g0 in range(0, self.n, GATHER_GROUP):
            if self.n_valid_of is None:
                body(g0)
            else:
                pl.when(g0 < self.n_valid_of(tile))(functools.partial(body, g0))

    def _issue(self, tile, slot):
        rr = self.rr

        def group(g0):
            for r in range(g0, g0 + GATHER_GROUP):
                first = self.index_of(tile * self.n + r) * rr
                pltpu.make_async_copy(self.src_hbm.at[pl.ds(first, rr), :],
                                      self.buf.at[slot, pl.ds(r * rr, rr), :], self.sems.at[slot]).start()

        self._groups(tile, group)

    def start(self, tile, slot):
        if isinstance(slot, int):
            self._issue(tile, slot)
        else:
            for static_slot in range(self.buf.shape[0]):
                pl.when(slot == static_slot)(functools.partial(self._issue, tile, static_slot))

    def wait(self, tile, slot):
        rows = GATHER_GROUP * self.rr

        def group(g0):
            pltpu.make_async_copy(self.src_hbm.at[pl.ds(0, rows), :],
                                  self.buf.at[slot, pl.ds(g0 * self.rr, rows), :], self.sems.at[slot]).wait()

        self._groups(tile, group)


def _expert_kernel(ea_ref, eb_ref, nused_ref, nvalid_ref, tstart_ref, cnt_ref, slot_ref,
                   rows_hbm, gmoe_ref, wga_ref, wua_ref, wda_ref, wgb_ref, wub_ref, wdb_ref,
                   y_ref, buf, sems, src_ref):
    bf16 = jnp.bfloat16
    n_x = gmoe_ref.shape[-1] // LANES
    n_tok = slot_ref.shape[0]
    step, n_used = pl.program_id(0), nused_ref[0]
    gather = _RowGather(lambda i: src_ref[i], rows_hbm, buf, sems, ROW_RECORD, MOE_TILE,
                        n_valid_of=lambda tile: nvalid_ref[tile])

    @pl.when(step == 0)
    def _():
        buf[...] = jnp.zeros_like(buf)
        for c in range(N_CLASSES):
            base = tstart_ref[c] * MOE_TILE

            def pad(r, carry, base=base):
                src_ref[base + r] = jnp.minimum(base + r, n_tok - 1)
                return carry

            lax.fori_loop(cnt_ref[c], (tstart_ref[c + 1] - tstart_ref[c]) * MOE_TILE, pad, 0)

        def place(it, carry):
            for u in range(PLACE_UNROLL):
                t = it * PLACE_UNROLL + u
                src_ref[slot_ref[t]] = t
            return carry

        lax.fori_loop(0, n_tok // PLACE_UNROLL, place, 0)
        for ahead in range(EXPERT_PREFETCH):
            pl.when(ahead < n_used)(functools.partial(gather.start, ahead, ahead))

    n_slots = buf.shape[0]
    slot = step % n_slots

    @pl.when(step + EXPERT_PREFETCH < n_used)
    def _():
        gather.start(step + EXPERT_PREFETCH, (step + EXPERT_PREFETCH) % n_slots)

    @pl.when(step < n_used)
    def _():
        gather.wait(step, slot)
        rec = _load_records(buf, n_x, 1, ROW_RECORD, lead=(slot,))
        h2 = _rms(_load_records(buf, 0, n_x, ROW_RECORD, lead=(slot,)), gmoe_ref[...]).astype(bf16)

        def expert(wg_ref, wu_ref, wd_ref):
            gate = _dot(h2, wg_ref[0])
            hid = (gate * jax.nn.sigmoid(gate)) * _dot(h2, wu_ref[0])
            return _dot(hid.astype(bf16), wd_ref[0])

        y = rec[:, INFO_WA:INFO_WA + 1] * expert(wga_ref, wua_ref, wda_ref)
        y = y + rec[:, INFO_WB:INFO_WB + 1] * expert(wgb_ref, wub_ref, wdb_ref)
        _store_records(y_ref, [y[:, c * LANES:(c + 1) * LANES] for c in range(n_x)], Y_RECORD)

    @pl.when(step >= n_used)
    def _():
        y_ref[...] = jnp.zeros_like(y_ref)


def _ple_final_kernel(slot_ref, x1_ref, p_ref, y_hbm, gple_ref, wpg_ref, wpp_ref, gfin_ref,
                      out_ref, buf, sems):
    bf16 = jnp.bfloat16
    step, n_steps = pl.program_id(0), pl.num_programs(0)
    n_x = gple_ref.shape[-1] // LANES
    gather = _RowGather(lambda t: slot_ref[t], y_hbm, buf, sems, Y_RECORD, PLE_TILE)

    @pl.when(step == 0)
    def _():
        gather.start(0, 0)

    slot = step % 2

    @pl.when(step + 1 < n_steps)
    def _():
        gather.start(step + 1, 1 - slot)

    gather.wait(step, slot)
    x2 = _load_records(x1_ref, 0, n_x, ROW_RECORD) + _load_records(buf, 0, n_x, Y_RECORD, lead=(slot,))
    gate_p = jax.nn.sigmoid(_dot(_rms(x2, gple_ref[...]).astype(bf16), wpg_ref[...]))
    x3 = x2 + gate_p * _dot(p_ref[...].astype(bf16), wpp_ref[...])
    out_ref[...] = _rms(x3, gfin_ref[...])


def _const_spec(shape):
    return pl.BlockSpec(shape, lambda *_: (0,) * len(shape), pipeline_mode=pl.Buffered(1))


def _mixer(x, g_mix, w_in, w_gla_gate, b_gla_gate, g_gla_out, w_conv, w_out, g_moe,
           w_group, b_group, w_router, b_router, w_exp_gate, w_exp_up, w_exp_down):
    b, s, d = x.shape
    step_tokens = MIX_TILE * MIX_TILES_PER_STEP
    n_steps = s // step_tokens
    n_exp, _, de = w_exp_gate.shape
    assert (n_exp * d) % (BF16_ROWS * n_steps) == 0 and (n_exp * de) % (BF16_ROWS * n_steps) == 0
    up_rows, down_rows = n_exp * d // n_steps, n_exp * de // n_steps
    bf16 = jnp.bfloat16
    n_qkvg = 2 * GLA_QK + 2 * GLA_V
    w_qkvg = w_in[:, :n_qkvg].astype(bf16)
    w_a = w_in[:, n_qkvg:n_qkvg + GLA_LOWRANK].astype(bf16)
    w_c3 = w_in[:, n_qkvg + GLA_LOWRANK:].astype(bf16)
    cw = w_c3.shape[1] // 3
    wrt = jnp.zeros((ROUTE_ROWS, d), jnp.float32)
    wrt = wrt.at[:N_GROUPS].set(w_group.T).at[ROUTE_EXPERT_ROW0:ROUTE_EXPERT_ROW0 + N_EXPERTS].set(w_router.T)
    brt = jnp.zeros((ROUTE_ROWS, 1), jnp.float32)
    brt = brt.at[:N_GROUPS, 0].set(b_group).at[ROUTE_EXPERT_ROW0:ROUTE_EXPERT_ROW0 + N_EXPERTS, 0].set(b_router)
    wrt_split = jnp.concatenate(_split_bf16(wrt), axis=0)
    gate_hi, gate_lo = _split_bf16(w_gla_gate)
    w_gate_split = jnp.concatenate([gate_hi, gate_hi, gate_lo], axis=0)
    args = (x, g_mix[None, :], w_qkvg, w_a, w_c3, w_gate_split, b_gla_gate[None, :], g_gla_out[None, :],
            w_conv, w_out.astype(bf16), g_moe[None, :], wrt_split, brt)
    slabs = (w_exp_gate.reshape(n_exp * d, de), w_exp_up.reshape(n_exp * d, de), w_exp_down.reshape(n_exp * de, d))
    slab_specs = [pl.BlockSpec((rows_, width), lambda j: (j, 0))
                  for rows_, width in ((up_rows, de), (up_rows, de), (down_rows, d))]
    in_specs = [pl.BlockSpec((b, step_tokens, d), lambda j: (0, j, 0))]
    in_specs += [_const_spec(a.shape) for a in args[1:]] + slab_specs
    rows, meta, counts, wg16, wu16, wd16 = pl.pallas_call(
        _mixer_kernel,
        grid=(n_steps,),
        in_specs=in_specs,
        out_specs=[pl.BlockSpec((b, step_tokens * ROW_RECORD, LANES), lambda j: (0, j, 0)),
                   pl.BlockSpec((b, MIX_TILES_PER_STEP, 8, MIX_TILE), lambda j: (0, j, 0, 0)),
                   pl.BlockSpec((LANES, LANES), lambda j: (0, 0))] + slab_specs,
        out_shape=[jax.ShapeDtypeStruct((b, s * ROW_RECORD, LANES), jnp.float32),
                   jax.ShapeDtypeStruct((b, s // MIX_TILE, 8, MIX_TILE), jnp.int32),
                   jax.ShapeDtypeStruct((LANES, LANES), jnp.int32)]
                  + [jax.ShapeDtypeStruct(w.shape, bf16) for w in slabs],
        scratch_shapes=[pltpu.VMEM((b, GLA_DV, GLA_QK), jnp.float32),
                        pltpu.VMEM((b, 8, cw), jnp.float32),
                        pltpu.VMEM((LANES, LANES), jnp.float32),
                        pltpu.VMEM((MIX_TILE, MIX_TILE), jnp.int32),
                        pltpu.VMEM((MIX_TILE, MIX_TILE), bf16)],
        compiler_params=pltpu.CompilerParams(dimension_semantics=("arbitrary",),
                                             vmem_limit_bytes=VMEM_LIMIT),
        name="mixer",
    )(*args, *slabs)
    return (rows.reshape(b * s * ROW_RECORD, LANES), meta.reshape(b * s // MIX_TILE, 8, MIX_TILE), counts,
            wg16.reshape(n_exp, d, de), wu16.reshape(n_exp, d, de), wd16.reshape(n_exp, de, d))


def _sort_plan(meta, counts, n_tok):
    i32 = jnp.int32
    n_tiles = n_tok // MOE_TILE + N_CLASSES
    cls = meta[:, INFO_CLS, :].reshape(n_tok)
    rank = meta[:, INFO_RANK, :].reshape(n_tok)
    cnt = counts[:N_CLASSES, 0]
    tiles_per_cls = (cnt + MOE_TILE - 1) // MOE_TILE
    tile_end = jnp.cumsum(tiles_per_cls)
    n_used = tile_end[-1:]
    tstart = jnp.concatenate([tile_end - tiles_per_cls, n_used])
    tile_id = jnp.minimum(jnp.arange(n_tiles, dtype=i32), n_used - 1)
    tile_cls = jnp.sum((tile_id[:, None] >= tile_end[None, :]).astype(i32), axis=1)
    grp, pair = tile_cls // N_PAIRS, tile_cls % N_PAIRS
    e_a, e_b = grp * EXPERTS_PER_GROUP, grp * EXPERTS_PER_GROUP
    for pp in range(N_PAIRS):
        e_a = e_a + jnp.where(pair == pp, PAIR_A[pp], 0)
        e_b = e_b + jnp.where(pair == pp, PAIR_B[pp], 0)
    of_tile = (tile_cls[:, None] == jnp.arange(N_CLASSES, dtype=i32)[None, :]).astype(i32)
    tile_base = jnp.sum(of_tile * tstart[None, :N_CLASSES], axis=1)
    tile_cnt = jnp.sum(of_tile * cnt[None, :], axis=1)
    n_valid = jnp.clip(tile_cnt - (tile_id - tile_base) * MOE_TILE, 0, MOE_TILE)
    first_tile = functools.reduce(lambda acc, c: jnp.where(cls == c, tstart[c], acc), range(N_CLASSES),
                                  jnp.zeros_like(cls))
    slot = first_tile * MOE_TILE + rank
    return dict(slot=slot, n_valid=n_valid, cnt=cnt, tstart=tstart, e_a=e_a, e_b=e_b, n_used=n_used, n_tiles=n_tiles)


def _experts(plan, rows, g_moe, wg, wu, wd):
    n_tiles = plan["n_tiles"]
    d, de = wg.shape[-2:]
    w_a = lambda shape: pl.BlockSpec(shape, lambda i, ea, eb, *_: (ea[i], 0, 0))
    w_b = lambda shape: pl.BlockSpec(shape, lambda i, ea, eb, *_: (eb[i], 0, 0))
    return pl.pallas_call(
        _expert_kernel,
        grid_spec=pltpu.PrefetchScalarGridSpec(
            num_scalar_prefetch=7, grid=(n_tiles,),
            in_specs=[pl.BlockSpec(memory_space=pl.ANY),
                      pl.BlockSpec((1, d), lambda i, *_: (0, 0)),
                      w_a((1, d, de)), w_a((1, d, de)), w_a((1, de, d)),
                      w_b((1, d, de)), w_b((1, d, de)), w_b((1, de, d))],
            out_specs=pl.BlockSpec((MOE_TILE * Y_RECORD, LANES), lambda i, *_: (i, 0)),
            scratch_shapes=[pltpu.VMEM((EXPERT_PREFETCH + 1, MOE_TILE * ROW_RECORD, LANES), jnp.float32),
                            pltpu.SemaphoreType.DMA((EXPERT_PREFETCH + 1,)),
                            pltpu.SMEM((n_tiles * MOE_TILE,), jnp.int32)]),
        out_shape=jax.ShapeDtypeStruct((n_tiles * MOE_TILE * Y_RECORD, LANES), jnp.float32),
        compiler_params=pltpu.CompilerParams(dimension_semantics=("arbitrary",),
                                             vmem_limit_bytes=VMEM_LIMIT),
        name="experts",
    )(plan["e_a"], plan["e_b"], plan["n_used"], plan["n_valid"], plan["tstart"], plan["cnt"], plan["slot"],
      rows, g_moe[None, :], wg, wu, wd, wg, wu, wd)


def _ple_final(plan, rows, p, y_sorted, g_ple, w_ple_gate, w_ple_proj, g_final):
    n_tok, dp = p.shape
    d = w_ple_gate.shape[0]
    bf16 = jnp.bfloat16
    const = lambda shape: pl.BlockSpec(shape, lambda i, *_: (0,) * len(shape))
    tile = lambda width: pl.BlockSpec((PLE_TILE, width), lambda i, *_: (i, 0))
    return pl.pallas_call(
        _ple_final_kernel,
        grid_spec=pltpu.PrefetchScalarGridSpec(
            num_scalar_prefetch=1, grid=(n_tok // PLE_TILE,),
            in_specs=[pl.BlockSpec((PLE_TILE * ROW_RECORD, LANES), lambda i, *_: (i, 0)),
                      tile(dp),
                      pl.BlockSpec(memory_space=pl.ANY),
                      const((1, d)), const((d, d)), const((dp, d)), const((1, d))],
            out_specs=tile(d),
            scratch_shapes=[pltpu.VMEM((2, PLE_TILE * Y_RECORD, LANES), jnp.float32),
                            pltpu.SemaphoreType.DMA((2,))]),
        out_shape=jax.ShapeDtypeStruct((n_tok, d), jnp.float32),
        compiler_params=pltpu.CompilerParams(dimension_semantics=("arbitrary",),
                                             vmem_limit_bytes=VMEM_LIMIT),
        name="ple_final",
    )(plan["slot"], rows, p, y_sorted,
      g_ple[None, :], w_ple_gate.astype(bf16), w_ple_proj.astype(bf16), g_final[None, :])


def kernel(x, p, g_mix, w_in, w_gla_gate, b_gla_gate, g_gla_out, w_conv, w_out, g_moe, w_group, b_group,
           w_router, b_router, w_exp_gate, w_exp_up, w_exp_down, g_ple, w_ple_gate, w_ple_proj, g_final):
    depth = w_in.shape[0]
    assert depth == 1, "the final norm is fused into the last (only) layer"
    b, s, d = x.shape
    n_tok = b * s
    assert s % (MIX_TILE * MIX_TILES_PER_STEP) == 0 and n_tok % MOE_TILE == 0 and n_tok % PLE_TILE == 0 and n_tok % PLACE_UNROLL == 0
    rows, meta, counts, wg16, wu16, wd16 = _mixer(
        x, g_mix[0], w_in[0], w_gla_gate[0], b_gla_gate[0], g_gla_out[0], w_conv[0], w_out[0], g_moe[0],
        w_group[0], b_group[0], w_router[0], b_router[0], w_exp_gate[0], w_exp_up[0], w_exp_down[0])
    plan = _sort_plan(meta, counts, n_tok)
    y_sorted = _experts(plan, rows, g_moe[0], wg16, wu16, wd16)
    out = _ple_final(plan, rows, p[0].reshape(n_tok, -1), y_sorted, g_ple[0], w_ple_gate[0], w_ple_proj[0],
                     g_final)
    return out.reshape(b, s, d)
```

```python
import functools

import jax
import jax.numpy as jnp
from jax import lax
from jax.experimental import pallas as pl
from jax.experimental.pallas import tpu as pltpu

EPS = 1e-6
GLA_HEADS = 4
GLA_DK = 64
GLA_DV = 128
GLA_QK = GLA_HEADS * GLA_DK
GLA_V = GLA_HEADS * GLA_DV
GLA_LOWRANK = 16
GLA_TAU = 16.0
CONV_K = 3
N_GROUPS = 4
EXPERTS_PER_GROUP = 4
N_EXPERTS = N_GROUPS * EXPERTS_PER_GROUP

LANES = 128
MIX_TILE = 256
MIX_LEVELS = 8
SUBLANES = 8
BF16_ROWS = 16
SUB_LEVELS = 3
MIX_TILES_PER_STEP = 1
ROUTE_ROWS = 32
ROUTE_EXPERT_ROW0 = 8
MOE_TILE = 256
VMEM_LIMIT = 56 * 1024 * 1024

PAIR_A = (0, 2, 2, 0, 0, 1)
PAIR_B = (1, 1, 3, 3, 2, 3)
PAIR_OF_KEY = {1: 0, 6: 1, 11: 2, 3: 3, 2: 4, 7: 5}
N_PAIRS = len(PAIR_A)
N_CLASSES = N_GROUPS * N_PAIRS
INFO_CLS, INFO_RANK, INFO_WA, INFO_WB = 0, 1, 2, 3
PLACE_UNROLL = 16
ROW_RECORD = 9
Y_RECORD = 8
GATHER_GROUP = 32
PLE_TILE = 512

_NT = (((1,), (1,)), ((), ()))
_DONE = object()


def _rms(x, g):
    return x * lax.rsqrt(jnp.mean(x * x, axis=-1, keepdims=True) + EPS) * g


def _dot(a, b):
    return jnp.dot(a, b, preferred_element_type=jnp.float32)


def _dot_nt(a, b):
    return lax.dot_general(a, b, _NT, preferred_element_type=jnp.float32)


def _split_bf16(a):
    hi = a.astype(jnp.bfloat16)
    return hi, (a - hi.astype(jnp.float32)).astype(jnp.bfloat16)


def _shift_rows(x, shift):
    return pltpu.roll(x, shift % x.shape[0], axis=0)


def _mixer_kernel(x_ref, gmix_ref, wqkvg_ref, wa_ref, wc3_ref, wgate_ref, bgate_ref, ggla_ref,
                  wconv_ref, wout_ref, gmoe_ref, wrt_ref, brt_ref, wge_ref, wue_ref, wde_ref,
                  rows_ref, meta_ref, counts_ref, wge16_ref, wue16_ref, wde16_ref,
                  st_ref, carry_ref, count_ref, level_ref, tril_ref):
    bf16 = jnp.bfloat16
    T = MIX_TILE

    wge16_ref[...] = wge_ref[...].astype(bf16)
    wue16_ref[...] = wue_ref[...].astype(bf16)
    wde16_ref[...] = wde_ref[...].astype(bf16)

    @pl.when(pl.program_id(0) == 0)
    def _():
        st_ref[...] = jnp.zeros_like(st_ref)
        carry_ref[...] = jnp.zeros_like(carry_ref)
        count_ref[...] = jnp.zeros_like(count_ref)
        tt = lax.broadcasted_iota(jnp.int32, (T, T), 0)
        ss = lax.broadcasted_iota(jnp.int32, (T, T), 1)
        txs = jnp.bitwise_xor(tt, ss)
        level = jnp.zeros((T, T), jnp.int32)
        for j in range(1, MIX_LEVELS):
            level = level + (txs >= (1 << j)).astype(jnp.int32)
        level_ref[...] = jnp.where(tt > ss, level, jnp.where(tt == ss, MIX_LEVELS, -1))
        tril_ref[...] = (ss <= tt).astype(bf16)

    tiles = [_mixer_tile(x_ref.at[pl.ds(b, 1), pl.ds(i * T, T)], gmix_ref, wqkvg_ref, wa_ref, wc3_ref, wgate_ref,
                         bgate_ref, ggla_ref, wconv_ref, wout_ref, gmoe_ref, wrt_ref, brt_ref,
                         rows_ref.at[b, pl.ds(i * T * ROW_RECORD, T * ROW_RECORD)], meta_ref.at[b, pl.ds(i, 1)],
                         counts_ref, st_ref.at[b], carry_ref.at[b], count_ref, level_ref, tril_ref)
             for i in range(MIX_TILES_PER_STEP) for b in range(x_ref.shape[0])]
    while tiles:
        tiles = [t for t in tiles if next(t, _DONE) is not _DONE]


def _mixer_tile(x_ref, gmix_ref, wqkvg_ref, wa_ref, wc3_ref, wgate_ref, bgate_ref, ggla_ref,
                wconv_ref, wout_ref, gmoe_ref, wrt_ref, brt_ref,
                rows_ref, meta_ref, counts_ref,
                st_ref, carry_ref, count_ref, level_ref, tril_ref):
    f32, bf16 = jnp.float32, jnp.bfloat16
    T = MIX_TILE
    D = x_ref.shape[-1]

    x = x_ref[0]
    hb = _rms(x, gmix_ref[...]).astype(bf16)
    qk = _dot(hb, wqkvg_ref[:, :2 * GLA_QK])
    q = qk[:, :GLA_QK] * (GLA_DK ** -0.5)
    k = qk[:, GLA_QK:]
    a_low = _dot(hb, wa_ref[...])
    a_hi, a_lo = _split_bf16(a_low)
    z = _dot(jnp.concatenate([a_hi, a_lo, a_hi], axis=1), wgate_ref[...]) + bgate_ref[...]
    la = (jnp.minimum(z, 0.0) - jnp.log(1.0 + jnp.exp(-jnp.abs(z)))) * (1.0 / GLA_TAU)
    yield

    row = lax.broadcasted_iota(jnp.int32, (T, GLA_QK), 0)

    def next_level(l, q_l, k_l, block):
        upper = ((row >> l) & 1) == 1
        below = _shift_rows(block, 1 << l)
        above = _shift_rows(block, -(1 << l))
        return (q_l * jnp.where(upper, below, 1.0), k_l * jnp.where(upper, 1.0, above),
                block * jnp.where(upper, below, above))

    decay = jnp.exp(la)
    H = T // 2
    assert GLA_DV == H
    half_level = level_ref[0:H, 0:H]
    lane_head_st = lax.broadcasted_iota(jnp.int32, (H, GLA_QK), 1) // GLA_DK

    heads_per_tile = LANES // GLA_DK
    lane_head_tile = lax.broadcasted_iota(jnp.int32, (H, LANES), 1) // GLA_DK

    def head_scores(q_half, k_half):
        out = []
        for tile in range(GLA_QK // LANES):
            q_t = q_half[:, tile * LANES:(tile + 1) * LANES]
            k_t = k_half[:, tile * LANES:(tile + 1) * LANES]
            k_rows = jnp.concatenate([jnp.where(lane_head_tile == j, k_t, jnp.zeros_like(k_t))
                                      for j in range(heads_per_tile)], axis=0)
            p = _dot_nt(q_t, k_rows)
            out += [p[:, j * H:(j + 1) * H] for j in range(heads_per_tile)]
        return out

    diag0 = [jnp.zeros((H, H), f32) for _ in range(GLA_HEADS)]
    diag1 = [jnp.zeros((H, H), f32) for _ in range(GLA_HEADS)]

    def add_level(l, q_l, k_l):
        sel = half_level == l
        ql, kl = q_l.astype(bf16), k_l.astype(bf16)
        p0 = head_scores(ql[:H], kl[:H])
        p1 = head_scores(ql[H:], kl[H:])
        for h in range(GLA_HEADS):
            diag0[h] = jnp.where(sel, p0[h], diag0[h])
            diag1[h] = jnp.where(sel, p1[h], diag1[h])

    def split_groups(a):
        return [a[i * SUBLANES:(i + 1) * SUBLANES] for i in range(T // SUBLANES)]

    def join_groups(groups):
        return jnp.concatenate(groups, axis=0)

    def next_level_groups(l, q_g, k_g, block_g):
        m = 1 << (l - SUB_LEVELS)
        products = {}
        q_n, k_n, block_n = [], [], []
        for gi in range(len(q_g)):
            lo, hi = gi & ~m, gi | m
            key = (id(block_g[lo]), id(block_g[hi]))
            if key not in products:
                products[key] = block_g[lo] * block_g[hi]
            q_n.append(q_g[gi] * block_g[lo] if gi & m else q_g[gi])
            k_n.append(k_g[gi] if gi & m else k_g[gi] * block_g[hi])
            block_n.append(products[key])
        return q_n, k_n, block_n

    add_level(MIX_LEVELS, q, k)
    q_l, k_l, block = q * decay, k, decay
    cw = wc3_ref.shape[1] // 3
    conv_parts = []
    for l in range(SUB_LEVELS):
        add_level(l, q_l, k_l)
        q_l, k_l, block = next_level(l, q_l, k_l, block)
        conv_parts.append(_dot(hb, wc3_ref[:, l * cw:(l + 1) * cw]))
        yield
    q_g, k_g, block_g = split_groups(q_l), split_groups(k_l), split_groups(block)
    half = len(q_g) // 2
    for l in range(SUB_LEVELS, MIX_LEVELS - 1):
        add_level(l, join_groups(q_g), join_groups(k_g))
        q_g, k_g, block_g = next_level_groups(l, q_g, k_g, block_g)
        if l == SUB_LEVELS:
            v = _dot(hb, wqkvg_ref[:, 2 * GLA_QK:2 * GLA_QK + GLA_V])
        yield
    low = head_scores(join_groups(q_g[half:]).astype(bf16), join_groups(k_g[:half]).astype(bf16))
    zero_block = jnp.zeros((H, H), f32)
    scores = [jnp.concatenate([jnp.concatenate([diag0[h], zero_block], axis=1),
                               jnp.concatenate([low[h], diag1[h]], axis=1)], axis=0) for h in range(GLA_HEADS)]

    yield

    q_g, k_g, block_g = next_level_groups(MIX_LEVELS - 1, q_g, k_g, block_g)
    st = st_ref[...]
    o_state = head_scores(join_groups(q_g).astype(bf16), st.astype(bf16))
    upd = _dot(v.T.astype(bf16), join_groups(k_g).astype(bf16))
    new_st = st * block_g[0][0:1]
    for h in range(GLA_HEADS):
        new_st = new_st + jnp.where(lane_head_st == h, upd[h * GLA_DV:(h + 1) * GLA_DV], 0.0)
    st_ref[...] = new_st
    g = _dot(hb, wqkvg_ref[:, 2 * GLA_QK + GLA_V:])
    ggla = ggla_ref[...]
    y_heads = []
    for h in range(GLA_HEADS):
        v_h = v[:, h * GLA_DV:(h + 1) * GLA_DV]
        o = _dot(scores[h].astype(bf16), v_h.astype(bf16)) + o_state[h]
        g_h = g[:, h * GLA_DV:(h + 1) * GLA_DV]
        y_heads.append(_rms(o, ggla) * (g_h * jax.nn.sigmoid(g_h)))
        yield

    yield

    cb, cu = conv_parts[0], conv_parts[1] * conv_parts[2]
    crow = lax.broadcasted_iota(jnp.int32, (T, cw), 0)
    prev2, prev1 = carry_ref[0:1, :], carry_ref[1:2, :]
    m1 = jnp.where(crow == 0, prev1, _shift_rows(cu, 1))
    m2 = jnp.where(crow == 0, prev2, jnp.where(crow == 1, prev1, _shift_rows(cu, 2)))
    wconv = wconv_ref[...]
    y_conv = cb * (wconv[0:1, :] * m2 + wconv[1:2, :] * m1 + wconv[2:3, :] * cu)
    carry_ref[0:2, :] = cu[T - 2:, :]

    y = jnp.concatenate(y_heads + [y_conv], axis=1).astype(bf16)
    x1 = x + _dot(y, wout_ref[...])

    yield

    h2 = _rms(x1, gmoe_ref[...])
    h2_hi, h2_lo = _split_bf16(h2)
    part = _dot_nt(wrt_ref[...], h2_hi)
    logits = (part[:ROUTE_ROWS] + part[ROUTE_ROWS:] + _dot_nt(wrt_ref[:ROUTE_ROWS, :], h2_lo)) + brt_ref[...]
    gl = [logits[i:i + 1, :] for i in range(N_GROUPS)]
    gmax = functools.reduce(jnp.maximum, gl)
    gsum = functools.reduce(lambda a, b: a + b, [jnp.exp(t - gmax) for t in gl])
    p_grp = 1.0 / gsum
    g_sel = jnp.full_like(gmax, N_GROUPS - 1).astype(jnp.int32)
    for i in reversed(range(N_GROUPS - 1)):
        g_sel = jnp.where(gl[i] == gmax, i, g_sel)
    ig = []
    for j in range(EXPERTS_PER_GROUP):
        acc = jnp.zeros_like(gmax)
        for gi in range(N_GROUPS):
            r0 = ROUTE_EXPERT_ROW0 + gi * EXPERTS_PER_GROUP + j
            acc = acc + jnp.where(g_sel == gi, logits[r0:r0 + 1, :], 0.0)
        ig.append(acc)

    def first_argmax(vals):
        m = functools.reduce(jnp.maximum, vals)
        idx = jnp.full_like(m, len(vals) - 1).astype(jnp.int32)
        for i in reversed(range(len(vals) - 1)):
            idx = jnp.where(vals[i] == m, i, idx)
        return m, idx

    m1_, i1 = first_argmax(ig)
    m2_, i2 = first_argmax([jnp.where(i1 == j, -jnp.inf, ig[j]) for j in range(EXPERTS_PER_GROUP)])
    e21 = jnp.exp(m2_ - m1_)
    w1 = p_grp / (1.0 + e21)
    w2 = p_grp * e21 / (1.0 + e21)
    key = jnp.minimum(i1, i2) * EXPERTS_PER_GROUP + jnp.maximum(i1, i2)
    pair = jnp.zeros_like(key)
    a_loc = jnp.zeros_like(key)
    for kk, pp in PAIR_OF_KEY.items():
        pair = jnp.where(key == kk, pp, pair)
        a_loc = jnp.where(key == kk, PAIR_A[pp], a_loc)
    w_a = jnp.where(i1 == a_loc, w1, w2)
    w_b = jnp.where(i1 == a_loc, w2, w1)
    cls = g_sel * N_PAIRS + pair
    rr = lax.broadcasted_iota(jnp.int32, (LANES, T), 0)
    rec_t = (jnp.where(rr == INFO_WA, jnp.broadcast_to(w_a, (LANES, T)), 0.0)
             + jnp.where(rr == INFO_WB, jnp.broadcast_to(w_b, (LANES, T)), 0.0))
    pieces = [x1[:, c * LANES:(c + 1) * LANES] for c in range(D // LANES)] + [rec_t.T]
    assert len(pieces) == ROW_RECORD
    _store_records(rows_ref, pieces, ROW_RECORD)

    onehot = (rr == jnp.broadcast_to(cls, (LANES, T))).astype(f32)
    count = count_ref[:, 0:1]
    before = _dot_nt(onehot.astype(bf16), tril_ref[...]) - onehot + count
    rank = jnp.sum(onehot * before, axis=0, keepdims=True).astype(jnp.int32)
    r8 = lax.broadcasted_iota(jnp.int32, (8, T), 0)
    meta_ref[0] = jnp.where(r8 == INFO_CLS, jnp.broadcast_to(cls, (8, T)),
                            jnp.where(r8 == INFO_RANK, jnp.broadcast_to(rank, (8, T)), 0))
    new_count = jnp.broadcast_to(count + jnp.sum(onehot, axis=1, keepdims=True), count_ref.shape)
    count_ref[...] = new_count
    counts_ref[...] = new_count.astype(jnp.int32)


def _store_records(ref, pieces, record_rows):
    n = ref.shape[0] // record_rows
    for c, piece in enumerate(pieces):
        ref[pl.ds(c, n, stride=record_rows), :] = piece


def _load_records(ref, first, count, record_rows, lead=()):
    n = ref.shape[-2] // record_rows
    return jnp.concatenate([ref[lead + (pl.ds(first + c, n, stride=record_rows), slice(None))]
                            for c in range(count)], axis=1)


class _RowGather:
    def __init__(self, index_of, src_hbm, buf, sems, record_rows, n_records, n_valid_of=None):
        self.index_of, self.src_hbm, self.buf, self.sems = index_of, src_hbm, buf, sems
        self.rr, self.n, self.n_valid_of = record_rows, n_records, n_valid_of

    def _groups(self, tile, body):
        for g0 in range(0, self.n, GATHER_GROUP):
            if self.n_valid_of is None:
                body(g0)
            else:
                pl.when(g0 < self.n_valid_of(tile))(functools.partial(body, g0))

    def _issue(self, tile, slot):
        rr = self.rr

        def group(g0):
            for r in range(g0, g0 + GATHER_GROUP):
                first = self.index_of(tile * self.n + r) * rr
                pltpu.make_async_copy(self.src_hbm.at[pl.ds(first, rr), :],
                                      self.buf.at[slot, pl.ds(r * rr, rr), :], self.sems.at[slot]).start()

        self._groups(tile, group)

    def start(self, tile, slot):
        if isinstance(slot, int):
            self._issue(tile, slot)
        else:
            for static_slot in range(2):
                pl.when(slot == static_slot)(functools.partial(self._issue, tile, static_slot))

    def wait(self, tile, slot):
        rows = GATHER_GROUP * self.rr

        def group(g0):
            pltpu.make_async_copy(self.src_hbm.at[pl.ds(0, rows), :],
                                  self.buf.at[slot, pl.ds(g0 * self.rr, rows), :], self.sems.at[slot]).wait()

        self._groups(tile, group)


def _expert_kernel(ea_ref, eb_ref, nused_ref, nvalid_ref, tstart_ref, cnt_ref, slot_ref,
                   rows_hbm, gmoe_ref, wga_ref, wua_ref, wda_ref, wgb_ref, wub_ref, wdb_ref,
                   y_ref, buf, sems, src_ref):
    bf16 = jnp.bfloat16
    n_x = gmoe_ref.shape[-1] // LANES
    n_tok = slot_ref.shape[0]
    step, n_used = pl.program_id(0), nused_ref[0]
    gather = _RowGather(lambda i: src_ref[i], rows_hbm, buf, sems, ROW_RECORD, MOE_TILE,
                        n_valid_of=lambda tile: nvalid_ref[tile])

    @pl.when(step == 0)
    def _():
        buf[...] = jnp.zeros_like(buf)
        for c in range(N_CLASSES):
            base = tstart_ref[c] * MOE_TILE

            def pad(r, carry, base=base):
                src_ref[base + r] = jnp.minimum(base + r, n_tok - 1)
                return carry

            lax.fori_loop(cnt_ref[c], (tstart_ref[c + 1] - tstart_ref[c]) * MOE_TILE, pad, 0)

        def place(it, carry):
            for u in range(PLACE_UNROLL):
                t = it * PLACE_UNROLL + u
                src_ref[slot_ref[t]] = t
            return carry

        lax.fori_loop(0, n_tok // PLACE_UNROLL, place, 0)
        gather.start(0, 0)

    slot = step % 2

    @pl.when(step + 1 < n_used)
    def _():
        gather.start(step + 1, 1 - slot)

    @pl.when(step < n_used)
    def _():
        gather.wait(step, slot)
        rec = _load_records(buf, n_x, 1, ROW_RECORD, lead=(slot,))
        h2 = _rms(_load_records(buf, 0, n_x, ROW_RECORD, lead=(slot,)), gmoe_ref[...]).astype(bf16)

        def expert(wg_ref, wu_ref, wd_ref):
            gate = _dot(h2, wg_ref[0])
            hid = (gate * jax.nn.sigmoid(gate)) * _dot(h2, wu_ref[0])
            return _dot(hid.astype(bf16), wd_ref[0])

        y = rec[:, INFO_WA:INFO_WA + 1] * expert(wga_ref, wua_ref, wda_ref)
        y = y + rec[:, INFO_WB:INFO_WB + 1] * expert(wgb_ref, wub_ref, wdb_ref)
        _store_records(y_ref, [y[:, c * LANES:(c + 1) * LANES] for c in range(n_x)], Y_RECORD)

    @pl.when(step >= n_used)
    def _():
        y_ref[...] = jnp.zeros_like(y_ref)


def _ple_final_kernel(slot_ref, x1_ref, p_ref, y_hbm, gple_ref, wpg_ref, wpp_ref, gfin_ref,
                      out_ref, buf, sems):
    bf16 = jnp.bfloat16
    step, n_steps = pl.program_id(0), pl.num_programs(0)
    n_x = gple_ref.shape[-1] // LANES
    gather = _RowGather(lambda t: slot_ref[t], y_hbm, buf, sems, Y_RECORD, PLE_TILE)

    @pl.when(step == 0)
    def _():
        gather.start(0, 0)

    slot = step % 2

    @pl.when(step + 1 < n_steps)
    def _():
        gather.start(step + 1, 1 - slot)

    gather.wait(step, slot)
    x2 = _load_records(x1_ref, 0, n_x, ROW_RECORD) + _load_records(buf, 0, n_x, Y_RECORD, lead=(slot,))
    gate_p = jax.nn.sigmoid(_dot(_rms(x2, gple_ref[...]).astype(bf16), wpg_ref[...]))
    x3 = x2 + gate_p * _dot(p_ref[...].astype(bf16), wpp_ref[...])
    out_ref[...] = _rms(x3, gfin_ref[...])


def _const_spec(shape):
    return pl.BlockSpec(shape, lambda *_: (0,) * len(shape), pipeline_mode=pl.Buffered(1))


def _mixer(x, g_mix, w_in, w_gla_gate, b_gla_gate, g_gla_out, w_conv, w_out, g_moe,
           w_group, b_group, w_router, b_router, w_exp_gate, w_exp_up, w_exp_down):
    b, s, d = x.shape
    step_tokens = MIX_TILE * MIX_TILES_PER_STEP
    n_steps = s // step_tokens
    n_exp, _, de = w_exp_gate.shape
    assert (n_exp * d) % (BF16_ROWS * n_steps) == 0 and (n_exp * de) % (BF16_ROWS * n_steps) == 0
    up_rows, down_rows = n_exp * d // n_steps, n_exp * de // n_steps
    bf16 = jnp.bfloat16
    n_qkvg = 2 * GLA_QK + 2 * GLA_V
    w_qkvg = w_in[:, :n_qkvg].astype(bf16)
    w_a = w_in[:, n_qkvg:n_qkvg + GLA_LOWRANK].astype(bf16)
    w_c3 = w_in[:, n_qkvg + GLA_LOWRANK:].astype(bf16)
    cw = w_c3.shape[1] // 3
    wrt = jnp.zeros((ROUTE_ROWS, d), jnp.float32)
    wrt = wrt.at[:N_GROUPS].set(w_group.T).at[ROUTE_EXPERT_ROW0:ROUTE_EXPERT_ROW0 + N_EXPERTS].set(w_router.T)
    brt = jnp.zeros((ROUTE_ROWS, 1), jnp.float32)
    brt = brt.at[:N_GROUPS, 0].set(b_group).at[ROUTE_EXPERT_ROW0:ROUTE_EXPERT_ROW0 + N_EXPERTS, 0].set(b_router)
    wrt_split = jnp.concatenate(_split_bf16(wrt), axis=0)
    gate_hi, gate_lo = _split_bf16(w_gla_gate)
    w_gate_split = jnp.concatenate([gate_hi, gate_hi, gate_lo], axis=0)
    args = (x, g_mix[None, :], w_qkvg, w_a, w_c3, w_gate_split, b_gla_gate[None, :], g_gla_out[None, :],
            w_conv, w_out.astype(bf16), g_moe[None, :], wrt_split, brt)
    slabs = (w_exp_gate.reshape(n_exp * d, de), w_exp_up.reshape(n_exp * d, de), w_exp_down.reshape(n_exp * de, d))
    slab_specs = [pl.BlockSpec((rows_, width), lambda j: (j, 0))
                  for rows_, width in ((up_rows, de), (up_rows, de), (down_rows, d))]
    in_specs = [pl.BlockSpec((b, step_tokens, d), lambda j: (0, j, 0))]
    in_specs += [_const_spec(a.shape) for a in args[1:]] + slab_specs
    rows, meta, counts, wg16, wu16, wd16 = pl.pallas_call(
        _mixer_kernel,
        grid=(n_steps,),
        in_specs=in_specs,
        out_specs=[pl.BlockSpec((b, step_tokens * ROW_RECORD, LANES), lambda j: (0, j, 0)),
                   pl.BlockSpec((b, MIX_TILES_PER_STEP, 8, MIX_TILE), lambda j: (0, j, 0, 0)),
                   pl.BlockSpec((LANES, LANES), lambda j: (0, 0))] + slab_specs,
        out_shape=[jax.ShapeDtypeStruct((b, s * ROW_RECORD, LANES), jnp.float32),
                   jax.ShapeDtypeStruct((b, s // MIX_TILE, 8, MIX_TILE), jnp.int32),
                   jax.ShapeDtypeStruct((LANES, LANES), jnp.int32)]
                  + [jax.ShapeDtypeStruct(w.shape, bf16) for w in slabs],
        scratch_shapes=[pltpu.VMEM((b, GLA_DV, GLA_QK), jnp.float32),
                        pltpu.VMEM((b, 8, cw), jnp.float32),
                        pltpu.VMEM((LANES, LANES), jnp.float32),
                        pltpu.VMEM((MIX_TILE, MIX_TILE), jnp.int32),
                        pltpu.VMEM((MIX_TILE, MIX_TILE), bf16)],
        compiler_params=pltpu.CompilerParams(dimension_semantics=("arbitrary",),
                                             vmem_limit_bytes=VMEM_LIMIT),
        name="mixer",
    )(*args, *slabs)
    return (rows.reshape(b * s * ROW_RECORD, LANES), meta.reshape(b * s // MIX_TILE, 8, MIX_TILE), counts,
            wg16.reshape(n_exp, d, de), wu16.reshape(n_exp, d, de), wd16.reshape(n_exp, de, d))


def _sort_plan(meta, counts, n_tok):
    i32 = jnp.int32
    n_tiles = n_tok // MOE_TILE + N_CLASSES
    cls = meta[:, INFO_CLS, :].reshape(n_tok)
    rank = meta[:, INFO_RANK, :].reshape(n_tok)
    cnt = counts[:N_CLASSES, 0]
    tiles_per_cls = (cnt + MOE_TILE - 1) // MOE_TILE
    tile_end = jnp.cumsum(tiles_per_cls)
    n_used = tile_end[-1:]
    tstart = jnp.concatenate([tile_end - tiles_per_cls, n_used])
    tile_id = jnp.minimum(jnp.arange(n_tiles, dtype=i32), n_used - 1)
    tile_cls = jnp.sum((tile_id[:, None] >= tile_end[None, :]).astype(i32), axis=1)
    grp, pair = tile_cls // N_PAIRS, tile_cls % N_PAIRS
    e_a, e_b = grp * EXPERTS_PER_GROUP, grp * EXPERTS_PER_GROUP
    for pp in range(N_PAIRS):
        e_a = e_a + jnp.where(pair == pp, PAIR_A[pp], 0)
        e_b = e_b + jnp.where(pair == pp, PAIR_B[pp], 0)
    of_tile = (tile_cls[:, None] == jnp.arange(N_CLASSES, dtype=i32)[None, :]).astype(i32)
    tile_base = jnp.sum(of_tile * tstart[None, :N_CLASSES], axis=1)
    tile_cnt = jnp.sum(of_tile * cnt[None, :], axis=1)
    n_valid = jnp.clip(tile_cnt - (tile_id - tile_base) * MOE_TILE, 0, MOE_TILE)
    first_tile = functools.reduce(lambda acc, c: jnp.where(cls == c, tstart[c], acc), range(N_CLASSES),
                                  jnp.zeros_like(cls))
    slot = first_tile * MOE_TILE + rank
    return dict(slot=slot, n_valid=n_valid, cnt=cnt, tstart=tstart, e_a=e_a, e_b=e_b, n_used=n_used, n_tiles=n_tiles)


def _experts(plan, rows, g_moe, wg, wu, wd):
    n_tiles = plan["n_tiles"]
    d, de = wg.shape[-2:]
    w_a = lambda shape: pl.BlockSpec(shape, lambda i, ea, eb, *_: (ea[i], 0, 0))
    w_b = lambda shape: pl.BlockSpec(shape, lambda i, ea, eb, *_: (eb[i], 0, 0))
    return pl.pallas_call(
        _expert_kernel,
        grid_spec=pltpu.PrefetchScalarGridSpec(
            num_scalar_prefetch=7, grid=(n_tiles,),
            in_specs=[pl.BlockSpec(memory_space=pl.ANY),
                      pl.BlockSpec((1, d), lambda i, *_: (0, 0)),
                      w_a((1, d, de)), w_a((1, d, de)), w_a((1, de, d)),
                      w_b((1, d, de)), w_b((1, d, de)), w_b((1, de, d))],
            out_specs=pl.BlockSpec((MOE_TILE * Y_RECORD, LANES), lambda i, *_: (i, 0)),
            scratch_shapes=[pltpu.VMEM((2, MOE_TILE * ROW_RECORD, LANES), jnp.float32),
                            pltpu.SemaphoreType.DMA((2,)),
                            pltpu.SMEM((n_tiles * MOE_TILE,), jnp.int32)]),
        out_shape=jax.ShapeDtypeStruct((n_tiles * MOE_TILE * Y_RECORD, LANES), jnp.float32),
        compiler_params=pltpu.CompilerParams(dimension_semantics=("arbitrary",),
                                             vmem_limit_bytes=VMEM_LIMIT),
        name="experts",
    )(plan["e_a"], plan["e_b"], plan["n_used"], plan["n_valid"], plan["tstart"], plan["cnt"], plan["slot"],
      rows, g_moe[None, :], wg, wu, wd, wg, wu, wd)


def _ple_final(plan, rows, p, y_sorted, g_ple, w_ple_gate, w_ple_proj, g_final):
    n_tok, dp = p.shape
    d = w_ple_gate.shape[0]
    bf16 = jnp.bfloat16
    const = lambda shape: pl.BlockSpec(shape, lambda i, *_: (0,) * len(shape))
    tile = lambda width: pl.BlockSpec((PLE_TILE, width), lambda i, *_: (i, 0))
    return pl.pallas_call(
        _ple_final_kernel,
        grid_spec=pltpu.PrefetchScalarGridSpec(
            num_scalar_prefetch=1, grid=(n_tok // PLE_TILE,),
            in_specs=[pl.BlockSpec((PLE_TILE * ROW_RECORD, LANES), lambda i, *_: (i, 0)),
                      tile(dp),
                      pl.BlockSpec(memory_space=pl.ANY),
                      const((1, d)), const((d, d)), const((dp, d)), const((1, d))],
            out_specs=tile(d),
            scratch_shapes=[pltpu.VMEM((2, PLE_TILE * Y_RECORD, LANES), jnp.float32),
                            pltpu.SemaphoreType.DMA((2,))]),
        out_shape=jax.ShapeDtypeStruct((n_tok, d), jnp.float32),
        compiler_params=pltpu.CompilerParams(dimension_semantics=("arbitrary",),
                                             vmem_limit_bytes=VMEM_LIMIT),
        name="ple_final",
    )(plan["slot"], rows, p, y_sorted,
      g_ple[None, :], w_ple_gate.astype(bf16), w_ple_proj.astype(bf16), g_final[None, :])


def kernel(x, p, g_mix, w_in, w_gla_gate, b_gla_gate, g_gla_out, w_conv, w_out, g_moe, w_group, b_group,
           w_router, b_router, w_exp_gate, w_exp_up, w_exp_down, g_ple, w_ple_gate, w_ple_proj, g_final):
    depth = w_in.shape[0]
    assert depth == 1, "the final norm is fused into the last (only) layer"
    b, s, d = x.shape
    n_tok = b * s
    assert s % (MIX_TILE * MIX_TILES_PER_STEP) == 0 and n_tok % MOE_TILE == 0 and n_tok % PLE_TILE == 0 and n_tok % PLACE_UNROLL == 0
    rows, meta, counts, wg16, wu16, wd16 = _mixer(
        x, g_mix[0], w_in[0], w_gla_gate[0], b_gla_gate[0], g_gla_out[0], w_conv[0], w_out[0], g_moe[0],
        w_group[0], b_group[0], w_router[0], b_router[0], w_exp_gate[0], w_exp_up[0], w_exp_down[0])
    plan = _sort_plan(meta, counts, n_tok)
    y_sorted = _experts(plan, rows, g_moe[0], wg16, wu16, wd16)
    out = _ple_final(plan, rows, p[0].reshape(n_tok, -1), y_sorted, g_ple[0], w_ple_gate[0], w_ple_proj[0],
                     g_final)
    return out.reshape(b, s, d)
```

```python
import functools

import jax
import jax.numpy as jnp
from jax import lax
from jax.experimental import pallas as pl
from jax.experimental.pallas import tpu as pltpu

EPS = 1e-6
GLA_HEADS = 4
GLA_DK = 64
GLA_DV = 128
GLA_QK = GLA_HEADS * GLA_DK
GLA_V = GLA_HEADS * GLA_DV
GLA_LOWRANK = 16
GLA_TAU = 16.0
CONV_K = 3
N_GROUPS = 4
EXPERTS_PER_GROUP = 4
N_EXPERTS = N_GROUPS * EXPERTS_PER_GROUP

LANES = 128
MIX_TILE = 256
MIX_LEVELS = 8
SUBLANES = 8
BF16_ROWS = 16
SUB_LEVELS = 3
MIX_TILES_PER_STEP = 1
IN_A = 2 * GLA_QK + 2 * GLA_V
IN_CONV = IN_A + LANES
ROUTE_ROWS = 32
ROUTE_EXPERT_ROW0 = 8
MOE_TILE = 256
VMEM_LIMIT = 56 * 1024 * 1024

PAIR_A = (0, 2, 2, 0, 0, 1)
PAIR_B = (1, 1, 3, 3, 2, 3)
PAIR_OF_KEY = {1: 0, 6: 1, 11: 2, 3: 3, 2: 4, 7: 5}
N_PAIRS = len(PAIR_A)
N_CLASSES = N_GROUPS * N_PAIRS
INFO_CLS, INFO_RANK, INFO_WA, INFO_WB = 0, 1, 2, 3
PLACE_UNROLL = 16
ROW_RECORD = 9
Y_RECORD = 8
GATHER_GROUP = 32
PLE_TILE = 512

_NT = (((1,), (1,)), ((), ()))
_DONE = object()


def _rms(x, g):
    return x * lax.rsqrt(jnp.mean(x * x, axis=-1, keepdims=True) + EPS) * g


def _dot(a, b):
    return jnp.dot(a, b, preferred_element_type=jnp.float32)


def _dot_nt(a, b):
    return lax.dot_general(a, b, _NT, preferred_element_type=jnp.float32)


def _split_bf16(a):
    hi = a.astype(jnp.bfloat16)
    return hi, (a - hi.astype(jnp.float32)).astype(jnp.bfloat16)


def _shift_rows(x, shift):
    return pltpu.roll(x, shift % x.shape[0], axis=0)


def _mixer_kernel(x_ref, gmix_ref, win_ref, wgate_ref, bgate_ref, ggla_ref,
                  wconv_ref, wout_ref, gmoe_ref, wrt_ref, brt_ref, wge_ref, wue_ref, wde_ref,
                  rows_ref, meta_ref, counts_ref, wge16_ref, wue16_ref, wde16_ref,
                  st_ref, carry_ref, count_ref, level_ref, tril_ref):
    bf16 = jnp.bfloat16
    T = MIX_TILE

    wge16_ref[...] = wge_ref[...].astype(bf16)
    wue16_ref[...] = wue_ref[...].astype(bf16)
    wde16_ref[...] = wde_ref[...].astype(bf16)

    @pl.when(pl.program_id(0) == 0)
    def _():
        st_ref[...] = jnp.zeros_like(st_ref)
        carry_ref[...] = jnp.zeros_like(carry_ref)
        count_ref[...] = jnp.zeros_like(count_ref)
        tt = lax.broadcasted_iota(jnp.int32, (T, T), 0)
        ss = lax.broadcasted_iota(jnp.int32, (T, T), 1)
        txs = jnp.bitwise_xor(tt, ss)
        level = jnp.zeros((T, T), jnp.int32)
        for j in range(1, MIX_LEVELS):
            level = level + (txs >= (1 << j)).astype(jnp.int32)
        level_ref[...] = jnp.where(tt > ss, level, jnp.where(tt == ss, MIX_LEVELS, -1))
        tril_ref[...] = (ss <= tt).astype(bf16)

    tiles = [_mixer_tile(x_ref.at[pl.ds(b, 1), pl.ds(i * T, T)], gmix_ref, win_ref, wgate_ref,
                         bgate_ref, ggla_ref, wconv_ref, wout_ref, gmoe_ref, wrt_ref, brt_ref,
                         rows_ref.at[b, pl.ds(i * T * ROW_RECORD, T * ROW_RECORD)], meta_ref.at[b, pl.ds(i, 1)],
                         counts_ref, st_ref.at[b], carry_ref.at[b], count_ref, level_ref, tril_ref)
             for i in range(MIX_TILES_PER_STEP) for b in range(x_ref.shape[0])]
    while tiles:
        tiles = [t for t in tiles if next(t, _DONE) is not _DONE]


def _mixer_tile(x_ref, gmix_ref, win_ref, wgate_ref, bgate_ref, ggla_ref,
                wconv_ref, wout_ref, gmoe_ref, wrt_ref, brt_ref,
                rows_ref, meta_ref, counts_ref,
                st_ref, carry_ref, count_ref, level_ref, tril_ref):
    f32, bf16 = jnp.float32, jnp.bfloat16
    T = MIX_TILE
    D = x_ref.shape[-1]

    x = x_ref[0]
    hb = _rms(x, gmix_ref[...]).astype(bf16)
    qk = _dot(hb, win_ref[:, :2 * GLA_QK])
    q = qk[:, :GLA_QK] * (GLA_DK ** -0.5)
    k = qk[:, GLA_QK:]
    a_low = _dot(hb, win_ref[:, IN_A:IN_A + GLA_LOWRANK])
    a_hi, a_lo = _split_bf16(a_low)
    z = _dot(jnp.concatenate([a_hi, a_lo, a_hi], axis=1), wgate_ref[...]) + bgate_ref[...]
    la = (jnp.minimum(z, 0.0) - jnp.log(1.0 + jnp.exp(-jnp.abs(z)))) * (1.0 / GLA_TAU)
    yield

    row = lax.broadcasted_iota(jnp.int32, (T, GLA_QK), 0)

    def next_level(l, q_l, k_l, block):
        upper = ((row >> l) & 1) == 1
        below = _shift_rows(block, 1 << l)
        above = _shift_rows(block, -(1 << l))
        return (q_l * jnp.where(upper, below, 1.0), k_l * jnp.where(upper, 1.0, above),
                block * jnp.where(upper, below, above))

    decay = jnp.exp(la)
    H = T // 2
    assert GLA_DV == H
    half_level = level_ref[0:H, 0:H]
    lane_head_st = lax.broadcasted_iota(jnp.int32, (H, GLA_QK), 1) // GLA_DK

    heads_per_tile = LANES // GLA_DK
    lane_head_tile = lax.broadcasted_iota(jnp.int32, (H, LANES), 1) // GLA_DK

    def head_scores(q_half, k_half):
        out = []
        for tile in range(GLA_QK // LANES):
            q_t = q_half[:, tile * LANES:(tile + 1) * LANES]
            k_t = k_half[:, tile * LANES:(tile + 1) * LANES]
            k_rows = jnp.concatenate([jnp.where(lane_head_tile == j, k_t, jnp.zeros_like(k_t))
                                      for j in range(heads_per_tile)], axis=0)
            p = _dot_nt(q_t, k_rows)
            out += [p[:, j * H:(j + 1) * H] for j in range(heads_per_tile)]
        return out

    diag0 = [jnp.zeros((H, H), f32) for _ in range(GLA_HEADS)]
    diag1 = [jnp.zeros((H, H), f32) for _ in range(GLA_HEADS)]

    def add_level(l, q_l, k_l):
        sel = half_level == l
        ql, kl = q_l.astype(bf16), k_l.astype(bf16)
        p0 = head_scores(ql[:H], kl[:H])
        p1 = head_scores(ql[H:], kl[H:])
        for h in range(GLA_HEADS):
            diag0[h] = jnp.where(sel, p0[h], diag0[h])
            diag1[h] = jnp.where(sel, p1[h], diag1[h])

    def split_groups(a):
        return [a[i * SUBLANES:(i + 1) * SUBLANES] for i in range(T // SUBLANES)]

    def join_groups(groups):
        return jnp.concatenate(groups, axis=0)

    def next_level_groups(l, q_g, k_g, block_g):
        m = 1 << (l - SUB_LEVELS)
        products = {}
        q_n, k_n, block_n = [], [], []
        for gi in range(len(q_g)):
            lo, hi = gi & ~m, gi | m
            key = (id(block_g[lo]), id(block_g[hi]))
            if key not in products:
                products[key] = block_g[lo] * block_g[hi]
            q_n.append(q_g[gi] * block_g[lo] if gi & m else q_g[gi])
            k_n.append(k_g[gi] if gi & m else k_g[gi] * block_g[hi])
            block_n.append(products[key])
        return q_n, k_n, block_n

    add_level(MIX_LEVELS, q, k)
    q_l, k_l, block = q * decay, k, decay
    cw = (win_ref.shape[1] - IN_CONV) // CONV_K
    conv_parts = []
    for l in range(SUB_LEVELS):
        add_level(l, q_l, k_l)
        q_l, k_l, block = next_level(l, q_l, k_l, block)
        conv_parts.append(_dot(hb, win_ref[:, IN_CONV + l * cw:IN_CONV + (l + 1) * cw]))
        yield
    q_g, k_g, block_g = split_groups(q_l), split_groups(k_l), split_groups(block)
    half = len(q_g) // 2
    for l in range(SUB_LEVELS, MIX_LEVELS - 1):
        add_level(l, join_groups(q_g), join_groups(k_g))
        q_g, k_g, block_g = next_level_groups(l, q_g, k_g, block_g)
        if l == SUB_LEVELS:
            v = _dot(hb, win_ref[:, 2 * GLA_QK:2 * GLA_QK + GLA_V])
        yield
    low = head_scores(join_groups(q_g[half:]).astype(bf16), join_groups(k_g[:half]).astype(bf16))
    zero_block = jnp.zeros((H, H), f32)
    scores = [jnp.concatenate([jnp.concatenate([diag0[h], zero_block], axis=1),
                               jnp.concatenate([low[h], diag1[h]], axis=1)], axis=0) for h in range(GLA_HEADS)]

    yield

    q_g, k_g, block_g = next_level_groups(MIX_LEVELS - 1, q_g, k_g, block_g)
    st = st_ref[...]
    o_state = head_scores(join_groups(q_g).astype(bf16), st.astype(bf16))
    upd = _dot(v.T.astype(bf16), join_groups(k_g).astype(bf16))
    new_st = st * block_g[0][0:1]
    for h in range(GLA_HEADS):
        new_st = new_st + jnp.where(lane_head_st == h, upd[h * GLA_DV:(h + 1) * GLA_DV], 0.0)
    st_ref[...] = new_st
    g = _dot(hb, win_ref[:, 2 * GLA_QK + GLA_V:IN_A])
    ggla = ggla_ref[...]
    y_heads = []
    for h in range(GLA_HEADS):
        v_h = v[:, h * GLA_DV:(h + 1) * GLA_DV]
        o = _dot(scores[h].astype(bf16), v_h.astype(bf16)) + o_state[h]
        g_h = g[:, h * GLA_DV:(h + 1) * GLA_DV]
        y_heads.append(_rms(o, ggla) * (g_h * jax.nn.sigmoid(g_h)))
        yield

    yield

    cb, cu = conv_parts[0], conv_parts[1] * conv_parts[2]
    crow = lax.broadcasted_iota(jnp.int32, (T, cw), 0)
    prev2, prev1 = carry_ref[0:1, :], carry_ref[1:2, :]
    m1 = jnp.where(crow == 0, prev1, _shift_rows(cu, 1))
    m2 = jnp.where(crow == 0, prev2, jnp.where(crow == 1, prev1, _shift_rows(cu, 2)))
    wconv = wconv_ref[...]
    y_conv = cb * (wconv[0:1, :] * m2 + wconv[1:2, :] * m1 + wconv[2:3, :] * cu)
    carry_ref[0:2, :] = cu[T - 2:, :]

    y = jnp.concatenate(y_heads + [y_conv], axis=1).astype(bf16)
    x1 = x + _dot(y, wout_ref[...])

    yield

    h2 = _rms(x1, gmoe_ref[...])
    h2_hi, h2_lo = _split_bf16(h2)
    part = _dot_nt(wrt_ref[...], h2_hi)
    logits = (part[:ROUTE_ROWS] + part[ROUTE_ROWS:] + _dot_nt(wrt_ref[:ROUTE_ROWS, :], h2_lo)) + brt_ref[...]
    gl = [logits[i:i + 1, :] for i in range(N_GROUPS)]
    gmax = functools.reduce(jnp.maximum, gl)
    gsum = functools.reduce(lambda a, b: a + b, [jnp.exp(t - gmax) for t in gl])
    p_grp = 1.0 / gsum
    g_sel = jnp.full_like(gmax, N_GROUPS - 1).astype(jnp.int32)
    for i in reversed(range(N_GROUPS - 1)):
        g_sel = jnp.where(gl[i] == gmax, i, g_sel)
    ig = []
    for j in range(EXPERTS_PER_GROUP):
        acc = jnp.zeros_like(gmax)
        for gi in range(N_GROUPS):
            r0 = ROUTE_EXPERT_ROW0 + gi * EXPERTS_PER_GROUP + j
            acc = acc + jnp.where(g_sel == gi, logits[r0:r0 + 1, :], 0.0)
        ig.append(acc)

    def first_argmax(vals):
        m = functools.reduce(jnp.maximum, vals)
        idx = jnp.full_like(m, len(vals) - 1).astype(jnp.int32)
        for i in reversed(range(len(vals) - 1)):
            idx = jnp.where(vals[i] == m, i, idx)
        return m, idx

    m1_, i1 = first_argmax(ig)
    m2_, i2 = first_argmax([jnp.where(i1 == j, -jnp.inf, ig[j]) for j in range(EXPERTS_PER_GROUP)])
    e21 = jnp.exp(m2_ - m1_)
    w1 = p_grp / (1.0 + e21)
    w2 = p_grp * e21 / (1.0 + e21)
    key = jnp.minimum(i1, i2) * EXPERTS_PER_GROUP + jnp.maximum(i1, i2)
    pair = jnp.zeros_like(key)
    a_loc = jnp.zeros_like(key)
    for kk, pp in PAIR_OF_KEY.items():
        pair = jnp.where(key == kk, pp, pair)
        a_loc = jnp.where(key == kk, PAIR_A[pp], a_loc)
    w_a = jnp.where(i1 == a_loc, w1, w2)
    w_b = jnp.where(i1 == a_loc, w2, w1)
    cls = g_sel * N_PAIRS + pair
    rr = lax.broadcasted_iota(jnp.int32, (LANES, T), 0)
    rec_t = (jnp.where(rr == INFO_WA, jnp.broadcast_to(w_a, (LANES, T)), 0.0)
             + jnp.where(rr == INFO_WB, jnp.broadcast_to(w_b, (LANES, T)), 0.0))
    pieces = [x1[:, c * LANES:(c + 1) * LANES] for c in range(D // LANES)] + [rec_t.T]
    assert len(pieces) == ROW_RECORD
    _store_records(rows_ref, pieces, ROW_RECORD)

    onehot = (rr == jnp.broadcast_to(cls, (LANES, T))).astype(f32)
    count = count_ref[:, 0:1]
    before = _dot_nt(onehot.astype(bf16), tril_ref[...]) - onehot + count
    rank = jnp.sum(onehot * before, axis=0, keepdims=True).astype(jnp.int32)
    r8 = lax.broadcasted_iota(jnp.int32, (8, T), 0)
    meta_ref[0] = jnp.where(r8 == INFO_CLS, jnp.broadcast_to(cls, (8, T)),
                            jnp.where(r8 == INFO_RANK, jnp.broadcast_to(rank, (8, T)), 0))
    new_count = jnp.broadcast_to(count + jnp.sum(onehot, axis=1, keepdims=True), count_ref.shape)
    count_ref[...] = new_count
    counts_ref[...] = new_count.astype(jnp.int32)


def _store_records(ref, pieces, record_rows):
    n = ref.shape[0] // record_rows
    for c, piece in enumerate(pieces):
        ref[pl.ds(c, n, stride=record_rows), :] = piece


def _load_records(ref, first, count, record_rows, lead=()):
    n = ref.shape[-2] // record_rows
    return jnp.concatenate([ref[lead + (pl.ds(first + c, n, stride=record_rows), slice(None))]
                            for c in range(count)], axis=1)


class _RowGather:
    def __init__(self, index_of, src_hbm, buf, sems, record_rows, n_records, n_valid_of=None):
        self.index_of, self.src_hbm, self.buf, self.sems = index_of, src_hbm, buf, sems
        self.rr, self.n, self.n_valid_of = record_rows, n_records, n_valid_of

    def _groups(self, tile, body):
        for g0 in range(0, self.n, GATHER_GROUP):
            if self.n_valid_of is None:
                body(g0)
            else:
                pl.when(g0 < self.n_valid_of(tile))(functools.partial(body, g0))

    def _issue(self, tile, slot):
        rr = self.rr

        def group(g0):
            for r in range(g0, g0 + GATHER_GROUP):
                first = self.index_of(tile * self.n + r) * rr
                pltpu.make_async_copy(self.src_hbm.at[pl.ds(first, rr), :],
                                      self.buf.at[slot, pl.ds(r * rr, rr), :], self.sems.at[slot]).start()

        self._groups(tile, group)

    def start(self, tile, slot):
        if isinstance(slot, int):
            self._issue(tile, slot)
        else:
            for static_slot in range(2):
                pl.when(slot == static_slot)(functools.partial(self._issue, tile, static_slot))

    def wait(self, tile, slot):
        rows = GATHER_GROUP * self.rr

        def group(g0):
            pltpu.make_async_copy(self.src_hbm.at[pl.ds(0, rows), :],
                                  self.buf.at[slot, pl.ds(g0 * self.rr, rows), :], self.sems.at[slot]).wait()

        self._groups(tile, group)


def _expert_kernel(ea_ref, eb_ref, nused_ref, nvalid_ref, tstart_ref, cnt_ref, slot_ref,
                   rows_hbm, gmoe_ref, wga_ref, wua_ref, wda_ref, wgb_ref, wub_ref, wdb_ref,
                   y_ref, buf, sems, src_ref):
    bf16 = jnp.bfloat16
    n_x = gmoe_ref.shape[-1] // LANES
    n_tok = slot_ref.shape[0]
    step, n_used = pl.program_id(0), nused_ref[0]
    gather = _RowGather(lambda i: src_ref[i], rows_hbm, buf, sems, ROW_RECORD, MOE_TILE,
                        n_valid_of=lambda tile: nvalid_ref[tile])

    @pl.when(step == 0)
    def _():
        buf[...] = jnp.zeros_like(buf)
        for c in range(N_CLASSES):
            base = tstart_ref[c] * MOE_TILE

            def pad(r, carry, base=base):
                src_ref[base + r] = jnp.minimum(base + r, n_tok - 1)
                return carry

            lax.fori_loop(cnt_ref[c], (tstart_ref[c + 1] - tstart_ref[c]) * MOE_TILE, pad, 0)

        def place(it, carry):
            for u in range(PLACE_UNROLL):
                t = it * PLACE_UNROLL + u
                src_ref[slot_ref[t]] = t
            return carry

        lax.fori_loop(0, n_tok // PLACE_UNROLL, place, 0)
        gather.start(0, 0)

    slot = step % 2

    @pl.when(step + 1 < n_used)
    def _():
        gather.start(step + 1, 1 - slot)

    @pl.when(step < n_used)
    def _():
        gather.wait(step, slot)
        rec = _load_records(buf, n_x, 1, ROW_RECORD, lead=(slot,))
        h2 = _rms(_load_records(buf, 0, n_x, ROW_RECORD, lead=(slot,)), gmoe_ref[...]).astype(bf16)

        def expert(wg_ref, wu_ref, wd_ref):
            gate = _dot(h2, wg_ref[0])
            hid = (gate * jax.nn.sigmoid(gate)) * _dot(h2, wu_ref[0])
            return _dot(hid.astype(bf16), wd_ref[0])

        y = rec[:, INFO_WA:INFO_WA + 1] * expert(wga_ref, wua_ref, wda_ref)
        y = y + rec[:, INFO_WB:INFO_WB + 1] * expert(wgb_ref, wub_ref, wdb_ref)
        _store_records(y_ref, [y[:, c * LANES:(c + 1) * LANES] for c in range(n_x)], Y_RECORD)

    @pl.when(step >= n_used)
    def _():
        y_ref[...] = jnp.zeros_like(y_ref)


def _ple_final_kernel(slot_ref, x1_ref, p_ref, y_hbm, gple_ref, wpg_ref, wpp_ref, gfin_ref,
                      out_ref, buf, sems):
    bf16 = jnp.bfloat16
    step, n_steps = pl.program_id(0), pl.num_programs(0)
    n_x = gple_ref.shape[-1] // LANES
    gather = _RowGather(lambda t: slot_ref[t], y_hbm, buf, sems, Y_RECORD, PLE_TILE)

    @pl.when(step == 0)
    def _():
        gather.start(0, 0)

    slot = step % 2

    @pl.when(step + 1 < n_steps)
    def _():
        gather.start(step + 1, 1 - slot)

    gather.wait(step, slot)
    x2 = _load_records(x1_ref, 0, n_x, ROW_RECORD) + _load_records(buf, 0, n_x, Y_RECORD, lead=(slot,))
    gate_p = jax.nn.sigmoid(_dot(_rms(x2, gple_ref[...]).astype(bf16), wpg_ref[...]))
    x3 = x2 + gate_p * _dot(p_ref[...].astype(bf16), wpp_ref[...])
    out_ref[...] = _rms(x3, gfin_ref[...])


def _const_spec(shape):
    return pl.BlockSpec(shape, lambda *_: (0,) * len(shape), pipeline_mode=pl.Buffered(1))


def _mixer(x, g_mix, w_in, w_gla_gate, b_gla_gate, g_gla_out, w_conv, w_out, g_moe,
           w_group, b_group, w_router, b_router, w_exp_gate, w_exp_up, w_exp_down):
    b, s, d = x.shape
    step_tokens = MIX_TILE * MIX_TILES_PER_STEP
    n_steps = s // step_tokens
    n_exp, _, de = w_exp_gate.shape
    assert (n_exp * d) % (BF16_ROWS * n_steps) == 0 and (n_exp * de) % (BF16_ROWS * n_steps) == 0
    up_rows, down_rows = n_exp * d // n_steps, n_exp * de // n_steps
    bf16 = jnp.bfloat16
    n_qkvg = 2 * GLA_QK + 2 * GLA_V
    assert n_qkvg == IN_A
    w_all = jnp.concatenate([w_in[:, :IN_A + GLA_LOWRANK], jnp.zeros((d, IN_CONV - IN_A - GLA_LOWRANK), w_in.dtype),
                             w_in[:, IN_A + GLA_LOWRANK:]], axis=1).astype(bf16)
    cw = (w_all.shape[1] - IN_CONV) // CONV_K
    wrt = jnp.zeros((ROUTE_ROWS, d), jnp.float32)
    wrt = wrt.at[:N_GROUPS].set(w_group.T).at[ROUTE_EXPERT_ROW0:ROUTE_EXPERT_ROW0 + N_EXPERTS].set(w_router.T)
    brt = jnp.zeros((ROUTE_ROWS, 1), jnp.float32)
    brt = brt.at[:N_GROUPS, 0].set(b_group).at[ROUTE_EXPERT_ROW0:ROUTE_EXPERT_ROW0 + N_EXPERTS, 0].set(b_router)
    wrt_split = jnp.concatenate(_split_bf16(wrt), axis=0)
    gate_hi, gate_lo = _split_bf16(w_gla_gate)
    w_gate_split = jnp.concatenate([gate_hi, gate_hi, gate_lo], axis=0)
    args = (x, g_mix[None, :], w_all, w_gate_split, b_gla_gate[None, :], g_gla_out[None, :],
            w_conv, w_out.astype(bf16), g_moe[None, :], wrt_split, brt)
    slabs = (w_exp_gate.reshape(n_exp * d, de), w_exp_up.reshape(n_exp * d, de), w_exp_down.reshape(n_exp * de, d))
    slab_specs = [pl.BlockSpec((rows_, width), lambda j: (j, 0))
                  for rows_, width in ((up_rows, de), (up_rows, de), (down_rows, d))]
    in_specs = [pl.BlockSpec((b, step_tokens, d), lambda j: (0, j, 0))]
    in_specs += [_const_spec(a.shape) for a in args[1:]] + slab_specs
    rows, meta, counts, wg16, wu16, wd16 = pl.pallas_call(
        _mixer_kernel,
        grid=(n_steps,),
        in_specs=in_specs,
        out_specs=[pl.BlockSpec((b, step_tokens * ROW_RECORD, LANES), lambda j: (0, j, 0)),
                   pl.BlockSpec((b, MIX_TILES_PER_STEP, 8, MIX_TILE), lambda j: (0, j, 0, 0)),
                   pl.BlockSpec((LANES, LANES), lambda j: (0, 0))] + slab_specs,
        out_shape=[jax.ShapeDtypeStruct((b, s * ROW_RECORD, LANES), jnp.float32),
                   jax.ShapeDtypeStruct((b, s // MIX_TILE, 8, MIX_TILE), jnp.int32),
                   jax.ShapeDtypeStruct((LANES, LANES), jnp.int32)]
                  + [jax.ShapeDtypeStruct(w.shape, bf16) for w in slabs],
        scratch_shapes=[pltpu.VMEM((b, GLA_DV, GLA_QK), jnp.float32),
                        pltpu.VMEM((b, 8, cw), jnp.float32),
                        pltpu.VMEM((LANES, LANES), jnp.float32),
                        pltpu.VMEM((MIX_TILE, MIX_TILE), jnp.int32),
                        pltpu.VMEM((MIX_TILE, MIX_TILE), bf16)],
        compiler_params=pltpu.CompilerParams(dimension_semantics=("arbitrary",),
                                             vmem_limit_bytes=VMEM_LIMIT),
        name="mixer",
    )(*args, *slabs)
    return (rows.reshape(b * s * ROW_RECORD, LANES), meta.reshape(b * s // MIX_TILE, 8, MIX_TILE), counts,
            wg16.reshape(n_exp, d, de), wu16.reshape(n_exp, d, de), wd16.reshape(n_exp, de, d))


def _sort_plan(meta, counts, n_tok):
    i32 = jnp.int32
    n_tiles = n_tok // MOE_TILE + N_CLASSES
    cls = meta[:, INFO_CLS, :].reshape(n_tok)
    rank = meta[:, INFO_RANK, :].reshape(n_tok)
    cnt = counts[:N_CLASSES, 0]
    tiles_per_cls = (cnt + MOE_TILE - 1) // MOE_TILE
    tile_end = jnp.cumsum(tiles_per_cls)
    n_used = tile_end[-1:]
    tstart = jnp.concatenate([tile_end - tiles_per_cls, n_used])
    tile_id = jnp.minimum(jnp.arange(n_tiles, dtype=i32), n_used - 1)
    tile_cls = jnp.sum((tile_id[:, None] >= tile_end[None, :]).astype(i32), axis=1)
    grp, pair = tile_cls // N_PAIRS, tile_cls % N_PAIRS
    e_a, e_b = grp * EXPERTS_PER_GROUP, grp * EXPERTS_PER_GROUP
    for pp in range(N_PAIRS):
        e_a = e_a + jnp.where(pair == pp, PAIR_A[pp], 0)
        e_b = e_b + jnp.where(pair == pp, PAIR_B[pp], 0)
    of_tile = (tile_cls[:, None] == jnp.arange(N_CLASSES, dtype=i32)[None, :]).astype(i32)
    tile_base = jnp.sum(of_tile * tstart[None, :N_CLASSES], axis=1)
    tile_cnt = jnp.sum(of_tile * cnt[None, :], axis=1)
    n_valid = jnp.clip(tile_cnt - (tile_id - tile_base) * MOE_TILE, 0, MOE_TILE)
    first_tile = functools.reduce(lambda acc, c: jnp.where(cls == c, tstart[c], acc), range(N_CLASSES),
                                  jnp.zeros_like(cls))
    slot = first_tile * MOE_TILE + rank
    return dict(slot=slot, n_valid=n_valid, cnt=cnt, tstart=tstart, e_a=e_a, e_b=e_b, n_used=n_used, n_tiles=n_tiles)


def _experts(plan, rows, g_moe, wg, wu, wd):
    n_tiles = plan["n_tiles"]
    d, de = wg.shape[-2:]
    w_a = lambda shape: pl.BlockSpec(shape, lambda i, ea, eb, *_: (ea[i], 0, 0))
    w_b = lambda shape: pl.BlockSpec(shape, lambda i, ea, eb, *_: (eb[i], 0, 0))
    return pl.pallas_call(
        _expert_kernel,
        grid_spec=pltpu.PrefetchScalarGridSpec(
            num_scalar_prefetch=7, grid=(n_tiles,),
            in_specs=[pl.BlockSpec(memory_space=pl.ANY),
                      pl.BlockSpec((1, d), lambda i, *_: (0, 0)),
                      w_a((1, d, de)), w_a((1, d, de)), w_a((1, de, d)),
                      w_b((1, d, de)), w_b((1, d, de)), w_b((1, de, d))],
            out_specs=pl.BlockSpec((MOE_TILE * Y_RECORD, LANES), lambda i, *_: (i, 0)),
            scratch_shapes=[pltpu.VMEM((2, MOE_TILE * ROW_RECORD, LANES), jnp.float32),
                            pltpu.SemaphoreType.DMA((2,)),
                            pltpu.SMEM((n_tiles * MOE_TILE,), jnp.int32)]),
        out_shape=jax.ShapeDtypeStruct((n_tiles * MOE_TILE * Y_RECORD, LANES), jnp.float32),
        compiler_params=pltpu.CompilerParams(dimension_semantics=("arbitrary",),
                                             vmem_limit_bytes=VMEM_LIMIT),
        name="experts",
    )(plan["e_a"], plan["e_b"], plan["n_used"], plan["n_valid"], plan["tstart"], plan["cnt"], plan["slot"],
      rows, g_moe[None, :], wg, wu, wd, wg, wu, wd)


def _ple_final(plan, rows, p, y_sorted, g_ple, w_ple_gate, w_ple_proj, g_final):
    n_tok, dp = p.shape
    d = w_ple_gate.shape[0]
    bf16 = jnp.bfloat16
    const = lambda shape: pl.BlockSpec(shape, lambda i, *_: (0,) * len(shape))
    tile = lambda width: pl.BlockSpec((PLE_TILE, width), lambda i, *_: (i, 0))
    return pl.pallas_call(
        _ple_final_kernel,
        grid_spec=pltpu.PrefetchScalarGridSpec(
            num_scalar_prefetch=1, grid=(n_tok // PLE_TILE,),
            in_specs=[pl.BlockSpec((PLE_TILE * ROW_RECORD, LANES), lambda i, *_: (i, 0)),
                      tile(dp),
                      pl.BlockSpec(memory_space=pl.ANY),
                      const((1, d)), const((d, d)), const((dp, d)), const((1, d))],
            out_specs=tile(d),
            scratch_shapes=[pltpu.VMEM((2, PLE_TILE * Y_RECORD, LANES), jnp.float32),
                            pltpu.SemaphoreType.DMA((2,))]),
        out_shape=jax.ShapeDtypeStruct((n_tok, d), jnp.float32),
        compiler_params=pltpu.CompilerParams(dimension_semantics=("arbitrary",),
                                             vmem_limit_bytes=VMEM_LIMIT),
        name="ple_final",
    )(plan["slot"], rows, p, y_sorted,
      g_ple[None, :], w_ple_gate.astype(bf16), w_ple_proj.astype(bf16), g_final[None, :])


def kernel(x, p, g_mix, w_in, w_gla_gate, b_gla_gate, g_gla_out, w_conv, w_out, g_moe, w_group, b_group,
           w_router, b_router, w_exp_gate, w_exp_up, w_exp_down, g_ple, w_ple_gate, w_ple_proj, g_final):
    depth = w_in.shape[0]
    assert depth == 1, "the final norm is fused into the last (only) layer"
    b, s, d = x.shape
    n_tok = b * s
    assert s % (MIX_TILE * MIX_TILES_PER_STEP) == 0 and n_tok % MOE_TILE == 0 and n_tok % PLE_TILE == 0 and n_tok % PLACE_UNROLL == 0
    rows, meta, counts, wg16, wu16, wd16 = _mixer(
        x, g_mix[0], w_in[0], w_gla_gate[0], b_gla_gate[0], g_gla_out[0], w_conv[0], w_out[0], g_moe[0],
        w_group[0], b_group[0], w_router[0], b_router[0], w_exp_gate[0], w_exp_up[0], w_exp_down[0])
    plan = _sort_plan(meta, counts, n_tok)
    y_sorted = _experts(plan, rows, g_moe[0], wg16, wu16, wd16)
    out = _ple_final(plan, rows, p[0].reshape(n_tok, -1), y_sorted, g_ple[0], w_ple_gate[0], w_ple_proj[0],
                     g_final)
    return out.reshape(b, s, d)
```

```python
import functools

import jax
import jax.numpy as jnp
from jax import lax
from jax.experimental import pallas as pl
from jax.experimental.pallas import tpu as pltpu

EPS = 1e-6
GLA_HEADS = 4
GLA_DK = 64
GLA_DV = 128
GLA_QK = GLA_HEADS * GLA_DK
GLA_V = GLA_HEADS * GLA_DV
GLA_LOWRANK = 16
GLA_TAU = 16.0
CONV_K = 3
N_GROUPS = 4
EXPERTS_PER_GROUP = 4
N_EXPERTS = N_GROUPS * EXPERTS_PER_GROUP

LANES = 128
MIX_TILE = 256
MIX_LEVELS = 8
SUBLANES = 8
BF16_ROWS = 16
SUB_LEVELS = 3
MIX_TILES_PER_STEP = 1
ROUTE_ROWS = 32
ROUTE_EXPERT_ROW0 = 8
MOE_TILE = 256
VMEM_LIMIT = 56 * 1024 * 1024

PAIR_A = (0, 2, 2, 0, 0, 1)
PAIR_B = (1, 1, 3, 3, 2, 3)
PAIR_OF_KEY = {1: 0, 6: 1, 11: 2, 3: 3, 2: 4, 7: 5}
N_PAIRS = len(PAIR_A)
N_CLASSES = N_GROUPS * N_PAIRS
INFO_CLS, INFO_RANK, INFO_WA, INFO_WB = 0, 1, 2, 3
PLACE_UNROLL = 16
ROW_RECORD = 9
Y_RECORD = 8
GATHER_GROUP = 32
PLE_TILE = 512

_NT = (((1,), (1,)), ((), ()))
_DONE = object()


def _rms(x, g):
    return x * lax.rsqrt(jnp.mean(x * x, axis=-1, keepdims=True) + EPS) * g


def _dot(a, b):
    return jnp.dot(a, b, preferred_element_type=jnp.float32)


def _dot_nt(a, b):
    return lax.dot_general(a, b, _NT, preferred_element_type=jnp.float32)


def _split_bf16(a):
    hi = a.astype(jnp.bfloat16)
    return hi, (a - hi.astype(jnp.float32)).astype(jnp.bfloat16)


def _shift_rows(x, shift):
    return pltpu.roll(x, shift % x.shape[0], axis=0)


def _mixer_kernel(x_ref, gmix_ref, wqkvg_ref, wa_ref, wc3_ref, wgate_ref, bgate_ref, ggla_ref,
                  wconv_ref, wout_ref, gmoe_ref, wrt_ref, brt_ref, wge_ref, wue_ref, wde_ref,
                  rows_ref, meta_ref, counts_ref, wge16_ref, wue16_ref, wde16_ref,
                  st_ref, carry_ref, count_ref, level_ref, tril_ref):
    bf16 = jnp.bfloat16
    T = MIX_TILE

    wge16_ref[...] = wge_ref[...].astype(bf16)
    wue16_ref[...] = wue_ref[...].astype(bf16)
    wde16_ref[...] = wde_ref[...].astype(bf16)

    @pl.when(pl.program_id(0) == 0)
    def _():
        st_ref[...] = jnp.zeros_like(st_ref)
        carry_ref[...] = jnp.zeros_like(carry_ref)
        count_ref[...] = jnp.zeros_like(count_ref)
        tt = lax.broadcasted_iota(jnp.int32, (T, T), 0)
        ss = lax.broadcasted_iota(jnp.int32, (T, T), 1)
        txs = jnp.bitwise_xor(tt, ss)
        level = jnp.zeros((T, T), jnp.int32)
        for j in range(1, MIX_LEVELS):
            level = level + (txs >= (1 << j)).astype(jnp.int32)
        level_ref[...] = jnp.where(tt > ss, level, jnp.where(tt == ss, MIX_LEVELS, -1))
        tril_ref[...] = (ss <= tt).astype(bf16)

    tiles = [_mixer_tile(x_ref.at[pl.ds(b, 1), pl.ds(i * T, T)], gmix_ref, wqkvg_ref, wa_ref, wc3_ref, wgate_ref,
                         bgate_ref, ggla_ref, wconv_ref, wout_ref, gmoe_ref, wrt_ref, brt_ref,
                         rows_ref.at[b, pl.ds(i * T * ROW_RECORD, T * ROW_RECORD)], meta_ref.at[b, pl.ds(i, 1)],
                         counts_ref, st_ref.at[b], carry_ref.at[b], count_ref, level_ref, tril_ref)
             for i in range(MIX_TILES_PER_STEP) for b in range(x_ref.shape[0])]
    while tiles:
        tiles = [t for t in tiles if next(t, _DONE) is not _DONE]


def _mixer_tile(x_ref, gmix_ref, wqkvg_ref, wa_ref, wc3_ref, wgate_ref, bgate_ref, ggla_ref,
                wconv_ref, wout_ref, gmoe_ref, wrt_ref, brt_ref,
                rows_ref, meta_ref, counts_ref,
                st_ref, carry_ref, count_ref, level_ref, tril_ref):
    f32, bf16 = jnp.float32, jnp.bfloat16
    T = MIX_TILE
    D = x_ref.shape[-1]

    x = x_ref[0]
    hb = _rms(x, gmix_ref[...]).astype(bf16)
    qk = _dot(hb, wqkvg_ref[:, :2 * GLA_QK])
    q = qk[:, :GLA_QK] * (GLA_DK ** -0.5)
    k = qk[:, GLA_QK:]
    a_low = _dot(hb, wa_ref[...])
    a_hi, a_lo = _split_bf16(a_low)
    z = _dot(jnp.concatenate([a_hi, a_lo, a_hi], axis=1), wgate_ref[...]) + bgate_ref[...]
    la = (jnp.minimum(z, 0.0) - jnp.log(1.0 + jnp.exp(-jnp.abs(z)))) * (1.0 / GLA_TAU)
    yield

    row = lax.broadcasted_iota(jnp.int32, (T, GLA_QK), 0)

    def next_level(l, q_l, k_l, block):
        upper = ((row >> l) & 1) == 1
        below = _shift_rows(block, 1 << l)
        above = _shift_rows(block, -(1 << l))
        return (q_l * jnp.where(upper, below, 1.0), k_l * jnp.where(upper, 1.0, above),
                block * jnp.where(upper, below, above))

    decay = jnp.exp(la)
    H = T // 2
    assert GLA_DV == H
    half_level = level_ref[0:H, 0:H]
    lane_head_st = lax.broadcasted_iota(jnp.int32, (H, GLA_QK), 1) // GLA_DK

    heads_per_tile = LANES // GLA_DK
    lane_head_tile = lax.broadcasted_iota(jnp.int32, (H, LANES), 1) // GLA_DK

    def head_scores(q_half, k_half):
        out = []
        for tile in range(GLA_QK // LANES):
            q_t = q_half[:, tile * LANES:(tile + 1) * LANES]
            k_t = k_half[:, tile * LANES:(tile + 1) * LANES]
            k_rows = jnp.concatenate([jnp.where(lane_head_tile == j, k_t, jnp.zeros_like(k_t))
                                      for j in range(heads_per_tile)], axis=0)
            p = _dot_nt(q_t, k_rows)
            out += [p[:, j * H:(j + 1) * H] for j in range(heads_per_tile)]
        return out

    diag0 = [jnp.zeros((H, H), f32) for _ in range(GLA_HEADS)]
    diag1 = [jnp.zeros((H, H), f32) for _ in range(GLA_HEADS)]

    def add_level(l, q_l, k_l):
        sel = half_level == l
        ql, kl = q_l.astype(bf16), k_l.astype(bf16)
        p0 = head_scores(ql[:H], kl[:H])
        p1 = head_scores(ql[H:], kl[H:])
        for h in range(GLA_HEADS):
            diag0[h] = jnp.where(sel, p0[h], diag0[h])
            diag1[h] = jnp.where(sel, p1[h], diag1[h])

    def split_groups(a):
        return [a[i * SUBLANES:(i + 1) * SUBLANES] for i in range(T // SUBLANES)]

    def join_groups(groups):
        return jnp.concatenate(groups, axis=0)

    def next_level_groups(l, q_g, k_g, block_g):
        m = 1 << (l - SUB_LEVELS)
        products = {}
        q_n, k_n, block_n = [], [], []
        for gi in range(len(q_g)):
            lo, hi = gi & ~m, gi | m
            key = (id(block_g[lo]), id(block_g[hi]))
            if key not in products:
                products[key] = block_g[lo] * block_g[hi]
            q_n.append(q_g[gi] * block_g[lo] if gi & m else q_g[gi])
            k_n.append(k_g[gi] if gi & m else k_g[gi] * block_g[hi])
            block_n.append(products[key])
        return q_n, k_n, block_n

    add_level(MIX_LEVELS, q, k)
    q_l, k_l, block = q * decay, k, decay
    cw = wc3_ref.shape[1] // 3
    conv_parts = []
    for l in range(SUB_LEVELS):
        add_level(l, q_l, k_l)
        q_l, k_l, block = next_level(l, q_l, k_l, block)
        conv_parts.append(_dot(hb, wc3_ref[:, l * cw:(l + 1) * cw]))
        yield
    q_g, k_g, block_g = split_groups(q_l), split_groups(k_l), split_groups(block)
    half = len(q_g) // 2
    for l in range(SUB_LEVELS, MIX_LEVELS - 1):
        add_level(l, join_groups(q_g), join_groups(k_g))
        q_g, k_g, block_g = next_level_groups(l, q_g, k_g, block_g)
        if l == SUB_LEVELS:
            v = _dot(hb, wqkvg_ref[:, 2 * GLA_QK:2 * GLA_QK + GLA_V])
        yield
    low = head_scores(join_groups(q_g[half:]).astype(bf16), join_groups(k_g[:half]).astype(bf16))
    zero_block = jnp.zeros((H, H), f32)
    scores = [jnp.concatenate([jnp.concatenate([diag0[h], zero_block], axis=1),
                               jnp.concatenate([low[h], diag1[h]], axis=1)], axis=0) for h in range(GLA_HEADS)]

    yield

    q_g, k_g, block_g = next_level_groups(MIX_LEVELS - 1, q_g, k_g, block_g)
    st = st_ref[...]
    o_state = head_scores(join_groups(q_g).astype(bf16), st.astype(bf16))
    upd = _dot(v.T.astype(bf16), join_groups(k_g).astype(bf16))
    new_st = st * block_g[0][0:1]
    for h in range(GLA_HEADS):
        new_st = new_st + jnp.where(lane_head_st == h, upd[h * GLA_DV:(h + 1) * GLA_DV], 0.0)
    st_ref[...] = new_st
    g = _dot(hb, wqkvg_ref[:, 2 * GLA_QK + GLA_V:])
    ggla = ggla_ref[...]
    y_heads = []
    for h in range(GLA_HEADS):
        v_h = v[:, h * GLA_DV:(h + 1) * GLA_DV]
        o = _dot(scores[h].astype(bf16), v_h.astype(bf16)) + o_state[h]
        g_h = g[:, h * GLA_DV:(h + 1) * GLA_DV]
        y_heads.append(_rms(o, ggla) * (g_h * jax.nn.sigmoid(g_h)))
        yield

    yield

    cb, cu = conv_parts[0], conv_parts[1] * conv_parts[2]
    crow = lax.broadcasted_iota(jnp.int32, (T, cw), 0)
    prev2, prev1 = carry_ref[0:1, :], carry_ref[1:2, :]
    m1 = jnp.where(crow == 0, prev1, _shift_rows(cu, 1))
    m2 = jnp.where(crow == 0, prev2, jnp.where(crow == 1, prev1, _shift_rows(cu, 2)))
    wconv = wconv_ref[...]
    y_conv = cb * (wconv[0:1, :] * m2 + wconv[1:2, :] * m1 + wconv[2:3, :] * cu)
    carry_ref[0:2, :] = cu[T - 2:, :]

    y = jnp.concatenate(y_heads + [y_conv], axis=1).astype(bf16)
    x1 = x + _dot(y, wout_ref[...])

    yield

    h2 = _rms(x1, gmoe_ref[...])
    h2_hi, h2_lo = _split_bf16(h2)
    part = _dot_nt(wrt_ref[...], h2_hi)
    logits = (part[:ROUTE_ROWS] + part[ROUTE_ROWS:] + _dot_nt(wrt_ref[:ROUTE_ROWS, :], h2_lo)) + brt_ref[...]
    gl = [logits[i:i + 1, :] for i in range(N_GROUPS)]
    gmax = functools.reduce(jnp.maximum, gl)
    gsum = functools.reduce(lambda a, b: a + b, [jnp.exp(t - gmax) for t in gl])
    p_grp = 1.0 / gsum
    g_sel = jnp.full_like(gmax, N_GROUPS - 1).astype(jnp.int32)
    for i in reversed(range(N_GROUPS - 1)):
        g_sel = jnp.where(gl[i] == gmax, i, g_sel)
    ig = []
    for j in range(EXPERTS_PER_GROUP):
        acc = jnp.zeros_like(gmax)
        for gi in range(N_GROUPS):
            r0 = ROUTE_EXPERT_ROW0 + gi * EXPERTS_PER_GROUP + j
            acc = acc + jnp.where(g_sel == gi, logits[r0:r0 + 1, :], 0.0)
        ig.append(acc)

    def first_argmax(vals):
        m = functools.reduce(jnp.maximum, vals)
        idx = jnp.full_like(m, len(vals) - 1).astype(jnp.int32)
        for i in reversed(range(len(vals) - 1)):
            idx = jnp.where(vals[i] == m, i, idx)
        return m, idx

    m1_, i1 = first_argmax(ig)
    m2_, i2 = first_argmax([jnp.where(i1 == j, -jnp.inf, ig[j]) for j in range(EXPERTS_PER_GROUP)])
    e21 = jnp.exp(m2_ - m1_)
    w1 = p_grp / (1.0 + e21)
    w2 = p_grp * e21 / (1.0 + e21)
    key = jnp.minimum(i1, i2) * EXPERTS_PER_GROUP + jnp.maximum(i1, i2)
    pair = jnp.zeros_like(key)
    a_loc = jnp.zeros_like(key)
    for kk, pp in PAIR_OF_KEY.items():
        pair = jnp.where(key == kk, pp, pair)
        a_loc = jnp.where(key == kk, PAIR_A[pp], a_loc)
    w_a = jnp.where(i1 == a_loc, w1, w2)
    w_b = jnp.where(i1 == a_loc, w2, w1)
    cls = g_sel * N_PAIRS + pair
    rr = lax.broadcasted_iota(jnp.int32, (LANES, T), 0)
    rec_t = (jnp.where(rr == INFO_WA, jnp.broadcast_to(w_a, (LANES, T)), 0.0)
             + jnp.where(rr == INFO_WB, jnp.broadcast_to(w_b, (LANES, T)), 0.0))
    pieces = [x1[:, c * LANES:(c + 1) * LANES] for c in range(D // LANES)] + [rec_t.T]
    assert len(pieces) == ROW_RECORD
    _store_records(rows_ref, pieces, ROW_RECORD)

    onehot = (rr == jnp.broadcast_to(cls, (LANES, T))).astype(f32)
    count = count_ref[:, 0:1]
    before = _dot_nt(onehot.astype(bf16), tril_ref[...]) - onehot + count
    rank = jnp.sum(onehot * before, axis=0, keepdims=True).astype(jnp.int32)
    r8 = lax.broadcasted_iota(jnp.int32, (8, T), 0)
    meta_ref[0] = jnp.where(r8 == INFO_CLS, jnp.broadcast_to(cls, (8, T)),
                            jnp.where(r8 == INFO_RANK, jnp.broadcast_to(rank, (8, T)), 0))
    new_count = jnp.broadcast_to(count + jnp.sum(onehot, axis=1, keepdims=True), count_ref.shape)
    count_ref[...] = new_count
    counts_ref[...] = new_count.astype(jnp.int32)


def _store_records(ref, pieces, record_rows):
    n = ref.shape[0] // record_rows
    for c, piece in enumerate(pieces):
        ref[pl.ds(c, n, stride=record_rows), :] = piece


def _load_records(ref, first, count, record_rows, lead=()):
    n = ref.shape[-2] // record_rows
    return jnp.concatenate([ref[lead + (pl.ds(first + c, n, stride=record_rows), slice(None))]
                            for c in range(count)], axis=1)


class _RowGather:
    def __init__(self, index_of, src_hbm, buf, sems, record_rows, n_records, n_valid_of=None):
        self.index_of, self.src_hbm, self.buf, self.sems = index_of, src_hbm, buf, sems
        self.rr, self.n, self.n_valid_of = record_rows, n_records, n_valid_of

    def _groups(self, tile, body):
        for g0 in range(0, self.n, GATHER_GROUP):
            if self.n_valid_of is None:
                body(g0)
            else:
                pl.when(g0 < self.n_valid_of(tile))(functools.partial(body, g0))

    def _issue(self, tile, slot):
        rr = self.rr

        def group(g0):
            for r in range(g0, g0 + GATHER_GROUP):
                first = self.index_of(tile * self.n + r) * rr
                pltpu.make_async_copy(self.src_hbm.at[pl.ds(first, rr), :],
                                      self.buf.at[slot, pl.ds(r * rr, rr), :], self.sems.at[slot]).start()

        self._groups(tile, group)

    def start(self, tile, slot):
        if isinstance(slot, int):
            self._issue(tile, slot)
        else:
            for static_slot in range(2):
                pl.when(slot == static_slot)(functools.partial(self._issue, tile, static_slot))

    def wait(self, tile, slot):
        rows = GATHER_GROUP * self.rr

        def group(g0):
            pltpu.make_async_copy(self.src_hbm.at[pl.ds(0, rows), :],
                                  self.buf.at[slot, pl.ds(g0 * self.rr, rows), :], self.sems.at[slot]).wait()

        self._groups(tile, group)


def _expert_kernel(ea_ref, eb_ref, nused_ref, nvalid_ref, tstart_ref, cnt_ref, slot_ref,
                   rows_hbm, gmoe_ref, wga_ref, wua_ref, wda_ref, wgb_ref, wub_ref, wdb_ref,
                   y_ref, buf, sems, src_ref):
    bf16 = jnp.bfloat16
    n_x = gmoe_ref.shape[-1] // LANES
    n_tok = slot_ref.shape[0]
    step, n_used = pl.program_id(0), nused_ref[0]
    gather = _RowGather(lambda i: src_ref[i], rows_hbm, buf, sems, ROW_RECORD, MOE_TILE,
                        n_valid_of=lambda tile: nvalid_ref[tile])

    @pl.when(step == 0)
    def _():
        buf[...] = jnp.zeros_like(buf)
        for c in range(N_CLASSES):
            base = tstart_ref[c] * MOE_TILE

            def pad(r, carry, base=base):
                src_ref[base + r] = jnp.minimum(base + r, n_tok - 1)
                return carry

            lax.fori_loop(cnt_ref[c], (tstart_ref[c + 1] - tstart_ref[c]) * MOE_TILE, pad, 0)

        def place(it, carry):
            for u in range(PLACE_UNROLL):
                t = it * PLACE_UNROLL + u
                src_ref[slot_ref[t]] = t
            return carry

        lax.fori_loop(0, n_tok // PLACE_UNROLL, place, 0)
        gather.start(0, 0)

    slot = step % 2

    @pl.when(step + 1 < n_used)
    def _():
        gather.start(step + 1, 1 - slot)

    @pl.when(step < n_used)
    def _():
        gather.wait(step, slot)
        rec = _load_records(buf, n_x, 1, ROW_RECORD, lead=(slot,))
        h2 = _rms(_load_records(buf, 0, n_x, ROW_RECORD, lead=(slot,)), gmoe_ref[...]).astype(bf16)

        gate_a, gate_b = _dot(h2, wga_ref[0]), _dot(h2, wgb_ref[0])
        up_a, up_b = _dot(h2, wua_ref[0]), _dot(h2, wub_ref[0])
        hid_a = ((gate_a * jax.nn.sigmoid(gate_a)) * up_a).astype(bf16)
        hid_b = ((gate_b * jax.nn.sigmoid(gate_b)) * up_b).astype(bf16)
        y = rec[:, INFO_WA:INFO_WA + 1] * _dot(hid_a, wda_ref[0])
        y = y + rec[:, INFO_WB:INFO_WB + 1] * _dot(hid_b, wdb_ref[0])
        _store_records(y_ref, [y[:, c * LANES:(c + 1) * LANES] for c in range(n_x)], Y_RECORD)

    @pl.when(step >= n_used)
    def _():
        y_ref[...] = jnp.zeros_like(y_ref)


def _ple_final_kernel(slot_ref, x1_ref, p_ref, y_hbm, gple_ref, wpg_ref, wpp_ref, gfin_ref,
                      out_ref, buf, sems):
    bf16 = jnp.bfloat16
    step, n_steps = pl.program_id(0), pl.num_programs(0)
    n_x = gple_ref.shape[-1] // LANES
    gather = _RowGather(lambda t: slot_ref[t], y_hbm, buf, sems, Y_RECORD, PLE_TILE)

    @pl.when(step == 0)
    def _():
        gather.start(0, 0)

    slot = step % 2

    @pl.when(step + 1 < n_steps)
    def _():
        gather.start(step + 1, 1 - slot)

    gather.wait(step, slot)
    x2 = _load_records(x1_ref, 0, n_x, ROW_RECORD) + _load_records(buf, 0, n_x, Y_RECORD, lead=(slot,))
    gate_p = jax.nn.sigmoid(_dot(_rms(x2, gple_ref[...]).astype(bf16), wpg_ref[...]))
    x3 = x2 + gate_p * _dot(p_ref[...].astype(bf16), wpp_ref[...])
    out_ref[...] = _rms(x3, gfin_ref[...])


def _const_spec(shape):
    return pl.BlockSpec(shape, lambda *_: (0,) * len(shape), pipeline_mode=pl.Buffered(1))


def _mixer(x, g_mix, w_in, w_gla_gate, b_gla_gate, g_gla_out, w_conv, w_out, g_moe,
           w_group, b_group, w_router, b_router, w_exp_gate, w_exp_up, w_exp_down):
    b, s, d = x.shape
    step_tokens = MIX_TILE * MIX_TILES_PER_STEP
    n_steps = s // step_tokens
    n_exp, _, de = w_exp_gate.shape
    assert (n_exp * d) % (BF16_ROWS * n_steps) == 0 and (n_exp * de) % (BF16_ROWS * n_steps) == 0
    up_rows, down_rows = n_exp * d // n_steps, n_exp * de // n_steps
    bf16 = jnp.bfloat16
    n_qkvg = 2 * GLA_QK + 2 * GLA_V
    w_qkvg = w_in[:, :n_qkvg].astype(bf16)
    w_a = w_in[:, n_qkvg:n_qkvg + GLA_LOWRANK].astype(bf16)
    w_c3 = w_in[:, n_qkvg + GLA_LOWRANK:].astype(bf16)
    cw = w_c3.shape[1] // 3
    wrt = jnp.zeros((ROUTE_ROWS, d), jnp.float32)
    wrt = wrt.at[:N_GROUPS].set(w_group.T).at[ROUTE_EXPERT_ROW0:ROUTE_EXPERT_ROW0 + N_EXPERTS].set(w_router.T)
    brt = jnp.zeros((ROUTE_ROWS, 1), jnp.float32)
    brt = brt.at[:N_GROUPS, 0].set(b_group).at[ROUTE_EXPERT_ROW0:ROUTE_EXPERT_ROW0 + N_EXPERTS, 0].set(b_router)
    wrt_split = jnp.concatenate(_split_bf16(wrt), axis=0)
    gate_hi, gate_lo = _split_bf16(w_gla_gate)
    w_gate_split = jnp.concatenate([gate_hi, gate_hi, gate_lo], axis=0)
    args = (x, g_mix[None, :], w_qkvg, w_a, w_c3, w_gate_split, b_gla_gate[None, :], g_gla_out[None, :],
            w_conv, w_out.astype(bf16), g_moe[None, :], wrt_split, brt)
    slabs = (w_exp_gate.reshape(n_exp * d, de), w_exp_up.reshape(n_exp * d, de), w_exp_down.reshape(n_exp * de, d))
    slab_specs = [pl.BlockSpec((rows_, width), lambda j: (j, 0))
                  for rows_, width in ((up_rows, de), (up_rows, de), (down_rows, d))]
    in_specs = [pl.BlockSpec((b, step_tokens, d), lambda j: (0, j, 0))]
    in_specs += [_const_spec(a.shape) for a in args[1:]] + slab_specs
    rows, meta, counts, wg16, wu16, wd16 = pl.pallas_call(
        _mixer_kernel,
        grid=(n_steps,),
        in_specs=in_specs,
        out_specs=[pl.BlockSpec((b, step_tokens * ROW_RECORD, LANES), lambda j: (0, j, 0)),
                   pl.BlockSpec((b, MIX_TILES_PER_STEP, 8, MIX_TILE), lambda j: (0, j, 0, 0)),
                   pl.BlockSpec((LANES, LANES), lambda j: (0, 0))] + slab_specs,
        out_shape=[jax.ShapeDtypeStruct((b, s * ROW_RECORD, LANES), jnp.float32),
                   jax.ShapeDtypeStruct((b, s // MIX_TILE, 8, MIX_TILE), jnp.int32),
                   jax.ShapeDtypeStruct((LANES, LANES), jnp.int32)]
                  + [jax.ShapeDtypeStruct(w.shape, bf16) for w in slabs],
        scratch_shapes=[pltpu.VMEM((b, GLA_DV, GLA_QK), jnp.float32),
                        pltpu.VMEM((b, 8, cw), jnp.float32),
                        pltpu.VMEM((LANES, LANES), jnp.float32),
                        pltpu.VMEM((MIX_TILE, MIX_TILE), jnp.int32),
                        pltpu.VMEM((MIX_TILE, MIX_TILE), bf16)],
        compiler_params=pltpu.CompilerParams(dimension_semantics=("arbitrary",),
                                             vmem_limit_bytes=VMEM_LIMIT),
        name="mixer",
    )(*args, *slabs)
    return (rows.reshape(b * s * ROW_RECORD, LANES), meta.reshape(b * s // MIX_TILE, 8, MIX_TILE), counts,
            wg16.reshape(n_exp, d, de), wu16.reshape(n_exp, d, de), wd16.reshape(n_exp, de, d))


def _sort_plan(meta, counts, n_tok):
    i32 = jnp.int32
    n_tiles = n_tok // MOE_TILE + N_CLASSES
    cls = meta[:, INFO_CLS, :].reshape(n_tok)
    rank = meta[:, INFO_RANK, :].reshape(n_tok)
    cnt = counts[:N_CLASSES, 0]
    tiles_per_cls = (cnt + MOE_TILE - 1) // MOE_TILE
    tile_end = jnp.cumsum(tiles_per_cls)
    n_used = tile_end[-1:]
    tstart = jnp.concatenate([tile_end - tiles_per_cls, n_used])
    tile_id = jnp.minimum(jnp.arange(n_tiles, dtype=i32), n_used - 1)
    tile_cls = jnp.sum((tile_id[:, None] >= tile_end[None, :]).astype(i32), axis=1)
    grp, pair = tile_cls // N_PAIRS, tile_cls % N_PAIRS
    e_a, e_b = grp * EXPERTS_PER_GROUP, grp * EXPERTS_PER_GROUP
    for pp in range(N_PAIRS):
        e_a = e_a + jnp.where(pair == pp, PAIR_A[pp], 0)
        e_b = e_b + jnp.where(pair == pp, PAIR_B[pp], 0)
    of_tile = (tile_cls[:, None] == jnp.arange(N_CLASSES, dtype=i32)[None, :]).astype(i32)
    tile_base = jnp.sum(of_tile * tstart[None, :N_CLASSES], axis=1)
    tile_cnt = jnp.sum(of_tile * cnt[None, :], axis=1)
    n_valid = jnp.clip(tile_cnt - (tile_id - tile_base) * MOE_TILE, 0, MOE_TILE)
    first_tile = functools.reduce(lambda acc, c: jnp.where(cls == c, tstart[c], acc), range(N_CLASSES),
                                  jnp.zeros_like(cls))
    slot = first_tile * MOE_TILE + rank
    return dict(slot=slot, n_valid=n_valid, cnt=cnt, tstart=tstart, e_a=e_a, e_b=e_b, n_used=n_used, n_tiles=n_tiles)


def _experts(plan, rows, g_moe, wg, wu, wd):
    n_tiles = plan["n_tiles"]
    d, de = wg.shape[-2:]
    w_a = lambda shape: pl.BlockSpec(shape, lambda i, ea, eb, *_: (ea[i], 0, 0))
    w_b = lambda shape: pl.BlockSpec(shape, lambda i, ea, eb, *_: (eb[i], 0, 0))
    return pl.pallas_call(
        _expert_kernel,
        grid_spec=pltpu.PrefetchScalarGridSpec(
            num_scalar_prefetch=7, grid=(n_tiles,),
            in_specs=[pl.BlockSpec(memory_space=pl.ANY),
                      pl.BlockSpec((1, d), lambda i, *_: (0, 0)),
                      w_a((1, d, de)), w_a((1, d, de)), w_a((1, de, d)),
                      w_b((1, d, de)), w_b((1, d, de)), w_b((1, de, d))],
            out_specs=pl.BlockSpec((MOE_TILE * Y_RECORD, LANES), lambda i, *_: (i, 0)),
            scratch_shapes=[pltpu.VMEM((2, MOE_TILE * ROW_RECORD, LANES), jnp.float32),
                            pltpu.SemaphoreType.DMA((2,)),
                            pltpu.SMEM((n_tiles * MOE_TILE,), jnp.int32)]),
        out_shape=jax.ShapeDtypeStruct((n_tiles * MOE_TILE * Y_RECORD, LANES), jnp.float32),
        compiler_params=pltpu.CompilerParams(dimension_semantics=("arbitrary",),
                                             vmem_limit_bytes=VMEM_LIMIT),
        name="experts",
    )(plan["e_a"], plan["e_b"], plan["n_used"], plan["n_valid"], plan["tstart"], plan["cnt"], plan["slot"],
      rows, g_moe[None, :], wg, wu, wd, wg, wu, wd)


def _ple_final(plan, rows, p, y_sorted, g_ple, w_ple_gate, w_ple_proj, g_final):
    n_tok, dp = p.shape
    d = w_ple_gate.shape[0]
    bf16 = jnp.bfloat16
    const = lambda shape: pl.BlockSpec(shape, lambda i, *_: (0,) * len(shape))
    tile = lambda width: pl.BlockSpec((PLE_TILE, width), lambda i, *_: (i, 0))
    return pl.pallas_call(
        _ple_final_kernel,
        grid_spec=pltpu.PrefetchScalarGridSpec(
            num_scalar_prefetch=1, grid=(n_tok // PLE_TILE,),
            in_specs=[pl.BlockSpec((PLE_TILE * ROW_RECORD, LANES), lambda i, *_: (i, 0)),
                      tile(dp),
                      pl.BlockSpec(memory_space=pl.ANY),
                      const((1, d)), const((d, d)), const((dp, d)), const((1, d))],
            out_specs=tile(d),
            scratch_shapes=[pltpu.VMEM((2, PLE_TILE * Y_RECORD, LANES), jnp.float32),
                            pltpu.SemaphoreType.DMA((2,))]),
        out_shape=jax.ShapeDtypeStruct((n_tok, d), jnp.float32),
        compiler_params=pltpu.CompilerParams(dimension_semantics=("arbitrary",),
                                             vmem_limit_bytes=VMEM_LIMIT),
        name="ple_final",
    )(plan["slot"], rows, p, y_sorted,
      g_ple[None, :], w_ple_gate.astype(bf16), w_ple_proj.astype(bf16), g_final[None, :])


def kernel(x, p, g_mix, w_in, w_gla_gate, b_gla_gate, g_gla_out, w_conv, w_out, g_moe, w_group, b_group,
           w_router, b_router, w_exp_gate, w_exp_up, w_exp_down, g_ple, w_ple_gate, w_ple_proj, g_final):
    depth = w_in.shape[0]
    assert depth == 1, "the final norm is fused into the last (only) layer"
    b, s, d = x.shape
    n_tok = b * s
    assert s % (MIX_TILE * MIX_TILES_PER_STEP) == 0 and n_tok % MOE_TILE == 0 and n_tok % PLE_TILE == 0 and n_tok % PLACE_UNROLL == 0
    rows, meta, counts, wg16, wu16, wd16 = _mixer(
        x, g_mix[0], w_in[0], w_gla_gate[0], b_gla_gate[0], g_gla_out[0], w_conv[0], w_out[0], g_moe[0],
        w_group[0], b_group[0], w_router[0], b_router[0], w_exp_gate[0], w_exp_up[0], w_exp_down[0])
    plan = _sort_plan(meta, counts, n_tok)
    y_sorted = _experts(plan, rows, g_moe[0], wg16, wu16, wd16)
    out = _ple_final(plan, rows, p[0].reshape(n_tok, -1), y_sorted, g_ple[0], w_ple_gate[0], w_ple_proj[0],
                     g_final)
    return out.reshape(b, s, d)
```

```python
import functools

import jax
import jax.numpy as jnp
from jax import lax
from jax.experimental import pallas as pl
from jax.experimental.pallas import tpu as pltpu

EPS = 1e-6
GLA_HEADS = 4
GLA_DK = 64
GLA_DV = 128
GLA_QK = GLA_HEADS * GLA_DK
GLA_V = GLA_HEADS * GLA_DV
GLA_LOWRANK = 16
GLA_TAU = 16.0
CONV_K = 3
N_GROUPS = 4
EXPERTS_PER_GROUP = 4
N_EXPERTS = N_GROUPS * EXPERTS_PER_GROUP

LANES = 128
MIX_TILE = 256
MIX_LEVELS = 8
SUBLANES = 8
BF16_ROWS = 16
SUB_LEVELS = 3
MIX_TILES_PER_STEP = 1
ROUTE_ROWS = 32
ROUTE_EXPERT_ROW0 = 8
MOE_TILE = 256
VMEM_LIMIT = 56 * 1024 * 1024

PAIR_A = (0, 2, 2, 0, 0, 1)
PAIR_B = (1, 1, 3, 3, 2, 3)
PAIR_OF_KEY = {1: 0, 6: 1, 11: 2, 3: 3, 2: 4, 7: 5}
N_PAIRS = len(PAIR_A)
N_CLASSES = N_GROUPS * N_PAIRS
INFO_CLS, INFO_RANK, INFO_WA, INFO_WB = 0, 1, 2, 3
PLACE_UNROLL = 16
ROW_RECORD = 9
Y_RECORD = 8
GATHER_GROUP = 32
PLE_TILE = 512

_NT = (((1,), (1,)), ((), ()))
_DONE = object()


def _rms(x, g):
    return x * lax.rsqrt(jnp.mean(x * x, axis=-1, keepdims=True) + EPS) * g


def _dot(a, b):
    return jnp.dot(a, b, preferred_element_type=jnp.float32)


def _dot_nt(a, b):
    return lax.dot_general(a, b, _NT, preferred_element_type=jnp.float32)


def _split_bf16(a):
    hi = a.astype(jnp.bfloat16)
    return hi, (a - hi.astype(jnp.float32)).astype(jnp.bfloat16)


def _shift_rows(x, shift):
    return pltpu.roll(x, shift % x.shape[0], axis=0)


def _mixer_kernel(x_ref, gmix_ref, wqkvg_ref, wa_ref, wc3_ref, wgate_ref, bgate_ref, ggla_ref,
                  wconv_ref, wout_ref, gmoe_ref, wrt_ref, brt_ref, wge_ref, wue_ref, wde_ref,
                  rows_ref, meta_ref, counts_ref, wge16_ref, wue16_ref, wde16_ref,
                  st_ref, carry_ref, count_ref, level_ref, tril_ref):
    bf16 = jnp.bfloat16
    T = MIX_TILE


    @pl.when(pl.program_id(0) == 0)
    def _():
        st_ref[...] = jnp.zeros_like(st_ref)
        carry_ref[...] = jnp.zeros_like(carry_ref)
        count_ref[...] = jnp.zeros_like(count_ref)
        tt = lax.broadcasted_iota(jnp.int32, (T, T), 0)
        ss = lax.broadcasted_iota(jnp.int32, (T, T), 1)
        txs = jnp.bitwise_xor(tt, ss)
        level = jnp.zeros((T, T), jnp.int32)
        for j in range(1, MIX_LEVELS):
            level = level + (txs >= (1 << j)).astype(jnp.int32)
        level_ref[...] = jnp.where(tt > ss, level, jnp.where(tt == ss, MIX_LEVELS, -1))
        tril_ref[...] = (ss <= tt).astype(bf16)

    tiles = [_mixer_tile(x_ref.at[pl.ds(b, 1), pl.ds(i * T, T)], gmix_ref, wqkvg_ref, wa_ref, wc3_ref, wgate_ref,
                         bgate_ref, ggla_ref, wconv_ref, wout_ref, gmoe_ref, wrt_ref, brt_ref,
                         rows_ref.at[b, pl.ds(i * T * ROW_RECORD, T * ROW_RECORD)], meta_ref.at[b, pl.ds(i, 1)],
                         counts_ref, st_ref.at[b], carry_ref.at[b], count_ref, level_ref, tril_ref)
             for i in range(MIX_TILES_PER_STEP) for b in range(x_ref.shape[0])]
    def cast_slabs():
        for src, dst in ((wge_ref, wge16_ref), (wue_ref, wue16_ref), (wde_ref, wde16_ref)):
            rows = src.shape[0] // 2
            for part in range(2):
                dst[part * rows:(part + 1) * rows, :] = src[part * rows:(part + 1) * rows, :].astype(bf16)
                yield

    tiles.append(cast_slabs())
    while tiles:
        tiles = [t for t in tiles if next(t, _DONE) is not _DONE]


def _mixer_tile(x_ref, gmix_ref, wqkvg_ref, wa_ref, wc3_ref, wgate_ref, bgate_ref, ggla_ref,
                wconv_ref, wout_ref, gmoe_ref, wrt_ref, brt_ref,
                rows_ref, meta_ref, counts_ref,
                st_ref, carry_ref, count_ref, level_ref, tril_ref):
    f32, bf16 = jnp.float32, jnp.bfloat16
    T = MIX_TILE
    D = x_ref.shape[-1]

    x = x_ref[0]
    hb = _rms(x, gmix_ref[...]).astype(bf16)
    qk = _dot(hb, wqkvg_ref[:, :2 * GLA_QK])
    q = qk[:, :GLA_QK] * (GLA_DK ** -0.5)
    k = qk[:, GLA_QK:]
    a_low = _dot(hb, wa_ref[...])
    a_hi, a_lo = _split_bf16(a_low)
    z = _dot(jnp.concatenate([a_hi, a_lo, a_hi], axis=1), wgate_ref[...]) + bgate_ref[...]
    la = (jnp.minimum(z, 0.0) - jnp.log(1.0 + jnp.exp(-jnp.abs(z)))) * (1.0 / GLA_TAU)
    yield

    row = lax.broadcasted_iota(jnp.int32, (T, GLA_QK), 0)

    def next_level(l, q_l, k_l, block):
        upper = ((row >> l) & 1) == 1
        below = _shift_rows(block, 1 << l)
        above = _shift_rows(block, -(1 << l))
        return (q_l * jnp.where(upper, below, 1.0), k_l * jnp.where(upper, 1.0, above),
                block * jnp.where(upper, below, above))

    decay = jnp.exp(la)
    H = T // 2
    assert GLA_DV == H
    half_level = level_ref[0:H, 0:H]
    lane_head_st = lax.broadcasted_iota(jnp.int32, (H, GLA_QK), 1) // GLA_DK

    heads_per_tile = LANES // GLA_DK
    lane_head_tile = lax.broadcasted_iota(jnp.int32, (H, LANES), 1) // GLA_DK

    def head_scores(q_half, k_half):
        out = []
        for tile in range(GLA_QK // LANES):
            q_t = q_half[:, tile * LANES:(tile + 1) * LANES]
            k_t = k_half[:, tile * LANES:(tile + 1) * LANES]
            k_rows = jnp.concatenate([jnp.where(lane_head_tile == j, k_t, jnp.zeros_like(k_t))
                                      for j in range(heads_per_tile)], axis=0)
            p = _dot_nt(q_t, k_rows)
            out += [p[:, j * H:(j + 1) * H] for j in range(heads_per_tile)]
        return out

    diag0 = [jnp.zeros((H, H), f32) for _ in range(GLA_HEADS)]
    diag1 = [jnp.zeros((H, H), f32) for _ in range(GLA_HEADS)]

    def add_level(l, q_l, k_l):
        sel = half_level == l
        ql, kl = q_l.astype(bf16), k_l.astype(bf16)
        p0 = head_scores(ql[:H], kl[:H])
        p1 = head_scores(ql[H:], kl[H:])
        for h in range(GLA_HEADS):
            diag0[h] = jnp.where(sel, p0[h], diag0[h])
            diag1[h] = jnp.where(sel, p1[h], diag1[h])

    def split_groups(a):
        return [a[i * SUBLANES:(i + 1) * SUBLANES] for i in range(T // SUBLANES)]

    def join_groups(groups):
        return jnp.concatenate(groups, axis=0)

    def next_level_groups(l, q_g, k_g, block_g):
        m = 1 << (l - SUB_LEVELS)
        products = {}
        q_n, k_n, block_n = [], [], []
        for gi in range(len(q_g)):
            lo, hi = gi & ~m, gi | m
            key = (id(block_g[lo]), id(block_g[hi]))
            if key not in products:
                products[key] = block_g[lo] * block_g[hi]
            q_n.append(q_g[gi] * block_g[lo] if gi & m else q_g[gi])
            k_n.append(k_g[gi] if gi & m else k_g[gi] * block_g[hi])
            block_n.append(products[key])
        return q_n, k_n, block_n

    add_level(MIX_LEVELS, q, k)
    q_l, k_l, block = q * decay, k, decay
    cw = wc3_ref.shape[1] // 3
    conv_parts = []
    for l in range(SUB_LEVELS):
        add_level(l, q_l, k_l)
        q_l, k_l, block = next_level(l, q_l, k_l, block)
        conv_parts.append(_dot(hb, wc3_ref[:, l * cw:(l + 1) * cw]))
        yield
    q_g, k_g, block_g = split_groups(q_l), split_groups(k_l), split_groups(block)
    half = len(q_g) // 2
    for l in range(SUB_LEVELS, MIX_LEVELS - 1):
        add_level(l, join_groups(q_g), join_groups(k_g))
        q_g, k_g, block_g = next_level_groups(l, q_g, k_g, block_g)
        if l == SUB_LEVELS:
            v = _dot(hb, wqkvg_ref[:, 2 * GLA_QK:2 * GLA_QK + GLA_V])
        yield
    low = head_scores(join_groups(q_g[half:]).astype(bf16), join_groups(k_g[:half]).astype(bf16))
    zero_block = jnp.zeros((H, H), f32)
    scores = [jnp.concatenate([jnp.concatenate([diag0[h], zero_block], axis=1),
                               jnp.concatenate([low[h], diag1[h]], axis=1)], axis=0) for h in range(GLA_HEADS)]

    yield

    q_g, k_g, block_g = next_level_groups(MIX_LEVELS - 1, q_g, k_g, block_g)
    st = st_ref[...]
    o_state = head_scores(join_groups(q_g).astype(bf16), st.astype(bf16))
    upd = _dot(v.T.astype(bf16), join_groups(k_g).astype(bf16))
    new_st = st * block_g[0][0:1]
    for h in range(GLA_HEADS):
        new_st = new_st + jnp.where(lane_head_st == h, upd[h * GLA_DV:(h + 1) * GLA_DV], 0.0)
    st_ref[...] = new_st
    g = _dot(hb, wqkvg_ref[:, 2 * GLA_QK + GLA_V:])
    ggla = ggla_ref[...]
    outs = []
    for h in range(GLA_HEADS):
        v_h = v[:, h * GLA_DV:(h + 1) * GLA_DV]
        outs.append(_dot(scores[h].astype(bf16), v_h.astype(bf16)) + o_state[h])
    yield
    y_heads = []
    for h in range(GLA_HEADS):
        g_h = g[:, h * GLA_DV:(h + 1) * GLA_DV]
        y_heads.append(_rms(outs[h], ggla) * (g_h * jax.nn.sigmoid(g_h)))
        yield

    yield

    cb, cu = conv_parts[0], conv_parts[1] * conv_parts[2]
    crow = lax.broadcasted_iota(jnp.int32, (T, cw), 0)
    prev2, prev1 = carry_ref[0:1, :], carry_ref[1:2, :]
    m1 = jnp.where(crow == 0, prev1, _shift_rows(cu, 1))
    m2 = jnp.where(crow == 0, prev2, jnp.where(crow == 1, prev1, _shift_rows(cu, 2)))
    wconv = wconv_ref[...]
    y_conv = cb * (wconv[0:1, :] * m2 + wconv[1:2, :] * m1 + wconv[2:3, :] * cu)
    carry_ref[0:2, :] = cu[T - 2:, :]

    y = jnp.concatenate(y_heads + [y_conv], axis=1).astype(bf16)
    x1 = x + _dot(y, wout_ref[...])

    yield

    h2 = _rms(x1, gmoe_ref[...])
    h2_hi, h2_lo = _split_bf16(h2)
    part = _dot_nt(wrt_ref[...], h2_hi)
    logits = (part[:ROUTE_ROWS] + part[ROUTE_ROWS:] + _dot_nt(wrt_ref[:ROUTE_ROWS, :], h2_lo)) + brt_ref[...]
    gl = [logits[i:i + 1, :] for i in range(N_GROUPS)]
    gmax = functools.reduce(jnp.maximum, gl)
    gsum = functools.reduce(lambda a, b: a + b, [jnp.exp(t - gmax) for t in gl])
    p_grp = 1.0 / gsum
    g_sel = jnp.full_like(gmax, N_GROUPS - 1).astype(jnp.int32)
    for i in reversed(range(N_GROUPS - 1)):
        g_sel = jnp.where(gl[i] == gmax, i, g_sel)
    ig = []
    for j in range(EXPERTS_PER_GROUP):
        acc = jnp.zeros_like(gmax)
        for gi in range(N_GROUPS):
            r0 = ROUTE_EXPERT_ROW0 + gi * EXPERTS_PER_GROUP + j
            acc = acc + jnp.where(g_sel == gi, logits[r0:r0 + 1, :], 0.0)
        ig.append(acc)

    def first_argmax(vals):
        m = functools.reduce(jnp.maximum, vals)
        idx = jnp.full_like(m, len(vals) - 1).astype(jnp.int32)
        for i in reversed(range(len(vals) - 1)):
            idx = jnp.where(vals[i] == m, i, idx)
        return m, idx

    m1_, i1 = first_argmax(ig)
    m2_, i2 = first_argmax([jnp.where(i1 == j, -jnp.inf, ig[j]) for j in range(EXPERTS_PER_GROUP)])
    e21 = jnp.exp(m2_ - m1_)
    w1 = p_grp / (1.0 + e21)
    w2 = p_grp * e21 / (1.0 + e21)
    key = jnp.minimum(i1, i2) * EXPERTS_PER_GROUP + jnp.maximum(i1, i2)
    pair = jnp.zeros_like(key)
    a_loc = jnp.zeros_like(key)
    for kk, pp in PAIR_OF_KEY.items():
        pair = jnp.where(key == kk, pp, pair)
        a_loc = jnp.where(key == kk, PAIR_A[pp], a_loc)
    w_a = jnp.where(i1 == a_loc, w1, w2)
    w_b = jnp.where(i1 == a_loc, w2, w1)
    cls = g_sel * N_PAIRS + pair
    rr = lax.broadcasted_iota(jnp.int32, (LANES, T), 0)
    rec_t = (jnp.where(rr == INFO_WA, jnp.broadcast_to(w_a, (LANES, T)), 0.0)
             + jnp.where(rr == INFO_WB, jnp.broadcast_to(w_b, (LANES, T)), 0.0))
    pieces = [x1[:, c * LANES:(c + 1) * LANES] for c in range(D // LANES)] + [rec_t.T]
    assert len(pieces) == ROW_RECORD
    _store_records(rows_ref, pieces, ROW_RECORD)

    onehot = (rr == jnp.broadcast_to(cls, (LANES, T))).astype(f32)
    count = count_ref[:, 0:1]
    before = _dot_nt(onehot.astype(bf16), tril_ref[...]) - onehot + count
    rank = jnp.sum(onehot * before, axis=0, keepdims=True).astype(jnp.int32)
    r8 = lax.broadcasted_iota(jnp.int32, (8, T), 0)
    meta_ref[0] = jnp.where(r8 == INFO_CLS, jnp.broadcast_to(cls, (8, T)),
                            jnp.where(r8 == INFO_RANK, jnp.broadcast_to(rank, (8, T)), 0))
    new_count = jnp.broadcast_to(count + jnp.sum(onehot, axis=1, keepdims=True), count_ref.shape)
    count_ref[...] = new_count
    counts_ref[...] = new_count.astype(jnp.int32)


def _store_records(ref, pieces, record_rows):
    n = ref.shape[0] // record_rows
    for c, piece in enumerate(pieces):
        ref[pl.ds(c, n, stride=record_rows), :] = piece


def _load_records(ref, first, count, record_rows, lead=()):
    n = ref.shape[-2] // record_rows
    return jnp.concatenate([ref[lead + (pl.ds(first + c, n, stride=record_rows), slice(None))]
                            for c in range(count)], axis=1)


class _RowGather:
    def __init__(self, index_of, src_hbm, buf, sems, record_rows, n_records, n_valid_of=None):
        self.index_of, self.src_hbm, self.buf, self.sems = index_of, src_hbm, buf, sems
        self.rr, self.n, self.n_valid_of = record_rows, n_records, n_valid_of

    def _groups(self, tile, body):
        for g0 in range(0, self.n, GATHER_GROUP):
            if self.n_valid_of is None:
                body(g0)
            else:
                pl.when(g0 < self.n_valid_of(tile))(functools.partial(body, g0))

    def _issue(self, tile, slot):
        rr = self.rr

        def group(g0):
            for r in range(g0, g0 + GATHER_GROUP):
                first = self.index_of(tile * self.n + r) * rr
                pltpu.make_async_copy(self.src_hbm.at[pl.ds(first, rr), :],
                                      self.buf.at[slot, pl.ds(r * rr, rr), :], self.sems.at[slot]).start()

        self._groups(tile, group)

    def start(self, tile, slot):
        if isinstance(slot, int):
            self._issue(tile, slot)
        else:
            for static_slot in range(2):
                pl.when(slot == static_slot)(functools.partial(self._issue, tile, static_slot))

    def wait(self, tile, slot):
        rows = GATHER_GROUP * self.rr

        def group(g0):
            pltpu.make_async_copy(self.src_hbm.at[pl.ds(0, rows), :],
                                  self.buf.at[slot, pl.ds(g0 * self.rr, rows), :], self.sems.at[slot]).wait()

        self._groups(tile, group)


def _expert_kernel(ea_ref, eb_ref, nused_ref, nvalid_ref, tstart_ref, cnt_ref, slot_ref,
                   rows_hbm, gmoe_ref, wga_ref, wua_ref, wda_ref, wgb_ref, wub_ref, wdb_ref,
                   y_ref, buf, sems, src_ref):
    bf16 = jnp.bfloat16
    n_x = gmoe_ref.shape[-1] // LANES
    n_tok = slot_ref.shape[0]
    step, n_used = pl.program_id(0), nused_ref[0]
    gather = _RowGather(lambda i: src_ref[i], rows_hbm, buf, sems, ROW_RECORD, MOE_TILE,
                        n_valid_of=lambda tile: nvalid_ref[tile])

    @pl.when(step == 0)
    def _():
        buf[...] = jnp.zeros_like(buf)
        for c in range(N_CLASSES):
            base = tstart_ref[c] * MOE_TILE

            def pad(r, carry, base=base):
                src_ref[base + r] = jnp.minimum(base + r, n_tok - 1)
                return carry

            lax.fori_loop(cnt_ref[c], (tstart_ref[c + 1] - tstart_ref[c]) * MOE_TILE, pad, 0)

        def place(it, carry):
            for u in range(PLACE_UNROLL):
                t = it * PLACE_UNROLL + u
                src_ref[slot_ref[t]] = t
            return carry

        lax.fori_loop(0, n_tok // PLACE_UNROLL, place, 0)
        gather.start(0, 0)

    slot = step % 2

    @pl.when(step + 1 < n_used)
    def _():
        gather.start(step + 1, 1 - slot)

    @pl.when(step < n_used)
    def _():
        gather.wait(step, slot)
        rec = _load_records(buf, n_x, 1, ROW_RECORD, lead=(slot,))
        h2 = _rms(_load_records(buf, 0, n_x, ROW_RECORD, lead=(slot,)), gmoe_ref[...]).astype(bf16)

        gate_a, gate_b = _dot(h2, wga_ref[0]), _dot(h2, wgb_ref[0])
        up_a, up_b = _dot(h2, wua_ref[0]), _dot(h2, wub_ref[0])
        hid_a = ((gate_a * jax.nn.sigmoid(gate_a)) * up_a).astype(bf16)
        hid_b = ((gate_b * jax.nn.sigmoid(gate_b)) * up_b).astype(bf16)
        y = rec[:, INFO_WA:INFO_WA + 1] * _dot(hid_a, wda_ref[0])
        y = y + rec[:, INFO_WB:INFO_WB + 1] * _dot(hid_b, wdb_ref[0])
        _store_records(y_ref, [y[:, c * LANES:(c + 1) * LANES] for c in range(n_x)], Y_RECORD)

    @pl.when(step >= n_used)
    def _():
        y_ref[...] = jnp.zeros_like(y_ref)


def _ple_final_kernel(slot_ref, x1_ref, p_ref, y_hbm, gple_ref, wpg_ref, wpp_ref, gfin_ref,
                      out_ref, buf, sems):
    bf16 = jnp.bfloat16
    step, n_steps = pl.program_id(0), pl.num_programs(0)
    n_x = gple_ref.shape[-1] // LANES
    gather = _RowGather(lambda t: slot_ref[t], y_hbm, buf, sems, Y_RECORD, PLE_TILE)

    @pl.when(step == 0)
    def _():
        gather.start(0, 0)

    slot = step % 2

    @pl.when(step + 1 < n_steps)
    def _():
        gather.start(step + 1, 1 - slot)

    gather.wait(step, slot)
    x2 = _load_records(x1_ref, 0, n_x, ROW_RECORD) + _load_records(buf, 0, n_x, Y_RECORD, lead=(slot,))
    gate_p = jax.nn.sigmoid(_dot(_rms(x2, gple_ref[...]).astype(bf16), wpg_ref[...]))
    x3 = x2 + gate_p * _dot(p_ref[...].astype(bf16), wpp_ref[...])
    out_ref[...] = _rms(x3, gfin_ref[...])


def _const_spec(shape):
    return pl.BlockSpec(shape, lambda *_: (0,) * len(shape), pipeline_mode=pl.Buffered(1))


def _mixer(x, g_mix, w_in, w_gla_gate, b_gla_gate, g_gla_out, w_conv, w_out, g_moe,
           w_group, b_group, w_router, b_router, w_exp_gate, w_exp_up, w_exp_down):
    b, s, d = x.shape
    step_tokens = MIX_TILE * MIX_TILES_PER_STEP
    n_steps = s // step_tokens
    n_exp, _, de = w_exp_gate.shape
    assert (n_exp * d) % (BF16_ROWS * n_steps) == 0 and (n_exp * de) % (BF16_ROWS * n_steps) == 0
    up_rows, down_rows = n_exp * d // n_steps, n_exp * de // n_steps
    bf16 = jnp.bfloat16
    n_qkvg = 2 * GLA_QK + 2 * GLA_V
    w_qkvg = w_in[:, :n_qkvg].astype(bf16)
    w_a = w_in[:, n_qkvg:n_qkvg + GLA_LOWRANK].astype(bf16)
    w_c3 = w_in[:, n_qkvg + GLA_LOWRANK:].astype(bf16)
    cw = w_c3.shape[1] // 3
    wrt = jnp.zeros((ROUTE_ROWS, d), jnp.float32)
    wrt = wrt.at[:N_GROUPS].set(w_group.T).at[ROUTE_EXPERT_ROW0:ROUTE_EXPERT_ROW0 + N_EXPERTS].set(w_router.T)
    brt = jnp.zeros((ROUTE_ROWS, 1), jnp.float32)
    brt = brt.at[:N_GROUPS, 0].set(b_group).at[ROUTE_EXPERT_ROW0:ROUTE_EXPERT_ROW0 + N_EXPERTS, 0].set(b_router)
    wrt_split = jnp.concatenate(_split_bf16(wrt), axis=0)
    gate_hi, gate_lo = _split_bf16(w_gla_gate)
    w_gate_split = jnp.concatenate([gate_hi, gate_hi, gate_lo], axis=0)
    args = (x, g_mix[None, :], w_qkvg, w_a, w_c3, w_gate_split, b_gla_gate[None, :], g_gla_out[None, :],
            w_conv, w_out.astype(bf16), g_moe[None, :], wrt_split, brt)
    slabs = (w_exp_gate.reshape(n_exp * d, de), w_exp_up.reshape(n_exp * d, de), w_exp_down.reshape(n_exp * de, d))
    slab_specs = [pl.BlockSpec((rows_, width), lambda j: (j, 0))
                  for rows_, width in ((up_rows, de), (up_rows, de), (down_rows, d))]
    in_specs = [pl.BlockSpec((b, step_tokens, d), lambda j: (0, j, 0))]
    in_specs += [_const_spec(a.shape) for a in args[1:]] + slab_specs
    rows, meta, counts, wg16, wu16, wd16 = pl.pallas_call(
        _mixer_kernel,
        grid=(n_steps,),
        in_specs=in_specs,
        out_specs=[pl.BlockSpec((b, step_tokens * ROW_RECORD, LANES), lambda j: (0, j, 0)),
                   pl.BlockSpec((b, MIX_TILES_PER_STEP, 8, MIX_TILE), lambda j: (0, j, 0, 0)),
                   pl.BlockSpec((LANES, LANES), lambda j: (0, 0))] + slab_specs,
        out_shape=[jax.ShapeDtypeStruct((b, s * ROW_RECORD, LANES), jnp.float32),
                   jax.ShapeDtypeStruct((b, s // MIX_TILE, 8, MIX_TILE), jnp.int32),
                   jax.ShapeDtypeStruct((LANES, LANES), jnp.int32)]
                  + [jax.ShapeDtypeStruct(w.shape, bf16) for w in slabs],
        scratch_shapes=[pltpu.VMEM((b, GLA_DV, GLA_QK), jnp.float32),
                        pltpu.VMEM((b, 8, cw), jnp.float32),
                        pltpu.VMEM((LANES, LANES), jnp.float32),
                        pltpu.VMEM((MIX_TILE, MIX_TILE), jnp.int32),
                        pltpu.VMEM((MIX_TILE, MIX_TILE), bf16)],
        compiler_params=pltpu.CompilerParams(dimension_semantics=("arbitrary",),
                                             vmem_limit_bytes=VMEM_LIMIT),
        name="mixer",
    )(*args, *slabs)
    return (rows.reshape(b * s * ROW_RECORD, LANES), meta.reshape(b * s // MIX_TILE, 8, MIX_TILE), counts,
            wg16.reshape(n_exp, d, de), wu16.reshape(n_exp, d, de), wd16.reshape(n_exp, de, d))


def _sort_plan(meta, counts, n_tok):
    i32 = jnp.int32
    n_tiles = n_tok // MOE_TILE + N_CLASSES
    cls = meta[:, INFO_CLS, :].reshape(n_tok)
    rank = meta[:, INFO_RANK, :].reshape(n_tok)
    cnt = counts[:N_CLASSES, 0]
    tiles_per_cls = (cnt + MOE_TILE - 1) // MOE_TILE
    tile_end = jnp.cumsum(tiles_per_cls)
    n_used = tile_end[-1:]
    tstart = jnp.concatenate([tile_end - tiles_per_cls, n_used])
    tile_id = jnp.minimum(jnp.arange(n_tiles, dtype=i32), n_used - 1)
    tile_cls = jnp.sum((tile_id[:, None] >= tile_end[None, :]).astype(i32), axis=1)
    grp, pair = tile_cls // N_PAIRS, tile_cls % N_PAIRS
    e_a, e_b = grp * EXPERTS_PER_GROUP, grp * EXPERTS_PER_GROUP
    for pp in range(N_PAIRS):
        e_a = e_a + jnp.where(pair == pp, PAIR_A[pp], 0)
        e_b = e_b + jnp.where(pair == pp, PAIR_B[pp], 0)
    of_tile = (tile_cls[:, None] == jnp.arange(N_CLASSES, dtype=i32)[None, :]).astype(i32)
    tile_base = jnp.sum(of_tile * tstart[None, :N_CLASSES], axis=1)
    tile_cnt = jnp.sum(of_tile * cnt[None, :], axis=1)
    n_valid = jnp.clip(tile_cnt - (tile_id - tile_base) * MOE_TILE, 0, MOE_TILE)
    first_tile = functools.reduce(lambda acc, c: jnp.where(cls == c, tstart[c], acc), range(N_CLASSES),
                                  jnp.zeros_like(cls))
    slot = first_tile * MOE_TILE + rank
    return dict(slot=slot, n_valid=n_valid, cnt=cnt, tstart=tstart, e_a=e_a, e_b=e_b, n_used=n_used, n_tiles=n_tiles)


def _experts(plan, rows, g_moe, wg, wu, wd):
    n_tiles = plan["n_tiles"]
    d, de = wg.shape[-2:]
    w_a = lambda shape: pl.BlockSpec(shape, lambda i, ea, eb, *_: (ea[i], 0, 0))
    w_b = lambda shape: pl.BlockSpec(shape, lambda i, ea, eb, *_: (eb[i], 0, 0))
    return pl.pallas_call(
        _expert_kernel,
        grid_spec=pltpu.PrefetchScalarGridSpec(
            num_scalar_prefetch=7, grid=(n_tiles,),
            in_specs=[pl.BlockSpec(memory_space=pl.ANY),
                      pl.BlockSpec((1, d), lambda i, *_: (0, 0)),
                      w_a((1, d, de)), w_a((1, d, de)), w_a((1, de, d)),
                      w_b((1, d, de)), w_b((1, d, de)), w_b((1, de, d))],
            out_specs=pl.BlockSpec((MOE_TILE * Y_RECORD, LANES), lambda i, *_: (i, 0)),
            scratch_shapes=[pltpu.VMEM((2, MOE_TILE * ROW_RECORD, LANES), jnp.float32),
                            pltpu.SemaphoreType.DMA((2,)),
                            pltpu.SMEM((n_tiles * MOE_TILE,), jnp.int32)]),
        out_shape=jax.ShapeDtypeStruct((n_tiles * MOE_TILE * Y_RECORD, LANES), jnp.float32),
        compiler_params=pltpu.CompilerParams(dimension_semantics=("arbitrary",),
                                             vmem_limit_bytes=VMEM_LIMIT),
        name="experts",
    )(plan["e_a"], plan["e_b"], plan["n_used"], plan["n_valid"], plan["tstart"], plan["cnt"], plan["slot"],
      rows, g_moe[None, :], wg, wu, wd, wg, wu, wd)


def _ple_final(plan, rows, p, y_sorted, g_ple, w_ple_gate, w_ple_proj, g_final):
    n_tok, dp = p.shape
    d = w_ple_gate.shape[0]
    bf16 = jnp.bfloat16
    const = lambda shape: pl.BlockSpec(shape, lambda i, *_: (0,) * len(shape))
    tile = lambda width: pl.BlockSpec((PLE_TILE, width), lambda i, *_: (i, 0))
    return pl.pallas_call(
        _ple_final_kernel,
        grid_spec=pltpu.PrefetchScalarGridSpec(
            num_scalar_prefetch=1, grid=(n_tok // PLE_TILE,),
            in_specs=[pl.BlockSpec((PLE_TILE * ROW_RECORD, LANES), lambda i, *_: (i, 0)),
                      tile(dp),
                      pl.BlockSpec(memory_space=pl.ANY),
                      const((1, d)), const((d, d)), const((dp, d)), const((1, d))],
            out_specs=tile(d),
            scratch_shapes=[pltpu.VMEM((2, PLE_TILE * Y_RECORD, LANES), jnp.float32),
                            pltpu.SemaphoreType.DMA((2,))]),
        out_shape=jax.ShapeDtypeStruct((n_tok, d), jnp.float32),
        compiler_params=pltpu.CompilerParams(dimension_semantics=("arbitrary",),
                                             vmem_limit_bytes=VMEM_LIMIT),
        name="ple_final",
    )(plan["slot"], rows, p, y_sorted,
      g_ple[None, :], w_ple_gate.astype(bf16), w_ple_proj.astype(bf16), g_final[None, :])


def kernel(x, p, g_mix, w_in, w_gla_gate, b_gla_gate, g_gla_out, w_conv, w_out, g_moe, w_group, b_group,
           w_router, b_router, w_exp_gate, w_exp_up, w_exp_down, g_ple, w_ple_gate, w_ple_proj, g_final):
    depth = w_in.shape[0]
    assert depth == 1, "the final norm is fused into the last (only) layer"
    b, s, d = x.shape
    n_tok = b * s
    assert s % (MIX_TILE * MIX_TILES_PER_STEP) == 0 and n_tok % MOE_TILE == 0 and n_tok % PLE_TILE == 0 and n_tok % PLACE_UNROLL == 0
    rows, meta, counts, wg16, wu16, wd16 = _mixer(
        x, g_mix[0], w_in[0], w_gla_gate[0], b_gla_gate[0], g_gla_out[0], w_conv[0], w_out[0], g_moe[0],
        w_group[0], b_group[0], w_router[0], b_router[0], w_exp_gate[0], w_exp_up[0], w_exp_down[0])
    plan = _sort_plan(meta, counts, n_tok)
    y_sorted = _experts(plan, rows, g_moe[0], wg16, wu16, wd16)
    out = _ple_final(plan, rows, p[0].reshape(n_tok, -1), y_sorted, g_ple[0], w_ple_gate[0], w_ple_proj[0],
                     g_final)
    return out.reshape(b, s, d)
```

```python
import functools

import jax
import jax.numpy as jnp
from jax import lax
from jax.experimental import pallas as pl
from jax.experimental.pallas import tpu as pltpu

EPS = 1e-6
GLA_HEADS = 4
GLA_DK = 64
GLA_DV = 128
GLA_QK = GLA_HEADS * GLA_DK
GLA_V = GLA_HEADS * GLA_DV
GLA_LOWRANK = 16
GLA_TAU = 16.0
CONV_K = 3
N_GROUPS = 4
EXPERTS_PER_GROUP = 4
N_EXPERTS = N_GROUPS * EXPERTS_PER_GROUP

LANES = 128
MIX_TILE = 256
MIX_LEVELS = 8
SUBLANES = 8
BF16_ROWS = 16
SUB_LEVELS = 3
MIX_TILES_PER_STEP = 1
ROUTE_ROWS = 32
ROUTE_EXPERT_ROW0 = 8
MOE_TILE = 256
VMEM_LIMIT = 56 * 1024 * 1024

PAIR_A = (0, 2, 2, 0, 0, 1)
PAIR_B = (1, 1, 3, 3, 2, 3)
PAIR_OF_KEY = {1: 0, 6: 1, 11: 2, 3: 3, 2: 4, 7: 5}
N_PAIRS = len(PAIR_A)
N_CLASSES = N_GROUPS * N_PAIRS
INFO_CLS, INFO_RANK, INFO_WA, INFO_WB = 0, 1, 2, 3
PLACE_UNROLL = 16
ROW_RECORD = 9
Y_RECORD = 8
GATHER_GROUP = 32
PLE_TILE = 512

_NT = (((1,), (1,)), ((), ()))
_DONE = object()


def _rms(x, g):
    return x * lax.rsqrt(jnp.mean(x * x, axis=-1, keepdims=True) + EPS) * g


def _dot(a, b):
    return jnp.dot(a, b, preferred_element_type=jnp.float32)


def _dot_nt(a, b):
    return lax.dot_general(a, b, _NT, preferred_element_type=jnp.float32)


def _split_bf16(a):
    hi = a.astype(jnp.bfloat16)
    return hi, (a - hi.astype(jnp.float32)).astype(jnp.bfloat16)


def _shift_rows(x, shift):
    return pltpu.roll(x, shift % x.shape[0], axis=0)


def _mixer_kernel(x_ref, gmix_ref, wqkvg_ref, wa_ref, wc3_ref, wgate_ref, bgate_ref, ggla_ref,
                  wconv_ref, wout_ref, gmoe_ref, wrt_ref, brt_ref, wge_ref, wue_ref, wde_ref,
                  rows_ref, meta_ref, counts_ref, wge16_ref, wue16_ref, wde16_ref,
                  st_ref, carry_ref, count_ref, level_ref, tril_ref):
    bf16 = jnp.bfloat16
    T = MIX_TILE

    wge16_ref[...] = wge_ref[...].astype(bf16)
    wue16_ref[...] = wue_ref[...].astype(bf16)
    wde16_ref[...] = wde_ref[...].astype(bf16)

    @pl.when(pl.program_id(0) == 0)
    def _():
        st_ref[...] = jnp.zeros_like(st_ref)
        carry_ref[...] = jnp.zeros_like(carry_ref)
        count_ref[...] = jnp.zeros_like(count_ref)
        tt = lax.broadcasted_iota(jnp.int32, (T, T), 0)
        ss = lax.broadcasted_iota(jnp.int32, (T, T), 1)
        txs = jnp.bitwise_xor(tt, ss)
        level = jnp.zeros((T, T), jnp.int32)
        for j in range(1, MIX_LEVELS):
            level = level + (txs >= (1 << j)).astype(jnp.int32)
        level_ref[...] = jnp.where(tt > ss, level, jnp.where(tt == ss, MIX_LEVELS, -1))
        tril_ref[...] = (ss <= tt).astype(bf16)

    tiles = [_mixer_tile(x_ref.at[pl.ds(b, 1), pl.ds(i * T, T)], gmix_ref, wqkvg_ref, wa_ref, wc3_ref, wgate_ref,
                         bgate_ref, ggla_ref, wconv_ref, wout_ref, gmoe_ref, wrt_ref, brt_ref,
                         rows_ref.at[b, pl.ds(i * T * ROW_RECORD, T * ROW_RECORD)], meta_ref.at[b, pl.ds(i, 1)],
                         counts_ref, st_ref.at[b], carry_ref.at[b], count_ref, level_ref, tril_ref)
             for i in range(MIX_TILES_PER_STEP) for b in range(x_ref.shape[0])]
    while tiles:
        tiles = [t for t in tiles if next(t, _DONE) is not _DONE]


def _mixer_tile(x_ref, gmix_ref, wqkvg_ref, wa_ref, wc3_ref, wgate_ref, bgate_ref, ggla_ref,
                wconv_ref, wout_ref, gmoe_ref, wrt_ref, brt_ref,
                rows_ref, meta_ref, counts_ref,
                st_ref, carry_ref, count_ref, level_ref, tril_ref):
    f32, bf16 = jnp.float32, jnp.bfloat16
    T = MIX_TILE
    D = x_ref.shape[-1]

    x = x_ref[0]
    hb = _rms(x, gmix_ref[...]).astype(bf16)
    qk = _dot(hb, wqkvg_ref[:, :2 * GLA_QK])
    q = qk[:, :GLA_QK] * (GLA_DK ** -0.5)
    k = qk[:, GLA_QK:]
    a_low = _dot(hb, wa_ref[...])
    a_hi, a_lo = _split_bf16(a_low)
    z = _dot(jnp.concatenate([a_hi, a_lo, a_hi], axis=1), wgate_ref[...]) + bgate_ref[...]
    la = (jnp.minimum(z, 0.0) - jnp.log(1.0 + jnp.exp(-jnp.abs(z)))) * (1.0 / GLA_TAU)
    yield

    row = lax.broadcasted_iota(jnp.int32, (T, GLA_QK), 0)

    def next_level(l, q_l, k_l, block):
        upper = ((row >> l) & 1) == 1
        below = _shift_rows(block, 1 << l)
        above = _shift_rows(block, -(1 << l))
        return (q_l * jnp.where(upper, below, 1.0), k_l * jnp.where(upper, 1.0, above),
                block * jnp.where(upper, below, above))

    decay = jnp.exp(la)
    H = T // 2
    assert GLA_DV == H
    half_level = level_ref[0:H, 0:H]
    lane_head_st = lax.broadcasted_iota(jnp.int32, (H, GLA_QK), 1) // GLA_DK

    heads_per_tile = LANES // GLA_DK
    lane_head_tile = lax.broadcasted_iota(jnp.int32, (H, LANES), 1) // GLA_DK

    def head_scores(q_half, k_half):
        out = []
        for tile in range(GLA_QK // LANES):
            q_t = q_half[:, tile * LANES:(tile + 1) * LANES]
            k_t = k_half[:, tile * LANES:(tile + 1) * LANES]
            k_rows = jnp.concatenate([jnp.where(lane_head_tile == j, k_t, jnp.zeros_like(k_t))
                                      for j in range(heads_per_tile)], axis=0)
            p = _dot_nt(q_t, k_rows)
            out += [p[:, j * H:(j + 1) * H] for j in range(heads_per_tile)]
        return out

    diag0 = [jnp.zeros((H, H), f32) for _ in range(GLA_HEADS)]
    diag1 = [jnp.zeros((H, H), f32) for _ in range(GLA_HEADS)]

    def add_level(l, q_l, k_l):
        sel = half_level == l
        ql, kl = q_l.astype(bf16), k_l.astype(bf16)
        p0 = head_scores(ql[:H], kl[:H])
        p1 = head_scores(ql[H:], kl[H:])
        for h in range(GLA_HEADS):
            diag0[h] = jnp.where(sel, p0[h], diag0[h])
            diag1[h] = jnp.where(sel, p1[h], diag1[h])

    def split_groups(a):
        return [a[i * SUBLANES:(i + 1) * SUBLANES] for i in range(T // SUBLANES)]

    def join_groups(groups):
        return jnp.concatenate(groups, axis=0)

    def next_level_groups(l, q_g, k_g, block_g):
        m = 1 << (l - SUB_LEVELS)
        products = {}
        q_n, k_n, block_n = [], [], []
        for gi in range(len(q_g)):
            lo, hi = gi & ~m, gi | m
            key = (id(block_g[lo]), id(block_g[hi]))
            if key not in products:
                products[key] = block_g[lo] * block_g[hi]
            q_n.append(q_g[gi] * block_g[lo] if gi & m else q_g[gi])
            k_n.append(k_g[gi] if gi & m else k_g[gi] * block_g[hi])
            block_n.append(products[key])
        return q_n, k_n, block_n

    add_level(MIX_LEVELS, q, k)
    q_l, k_l, block = q * decay, k, decay
    cw = wc3_ref.shape[1] // 3
    conv_parts = []
    for l in range(SUB_LEVELS):
        add_level(l, q_l, k_l)
        q_l, k_l, block = next_level(l, q_l, k_l, block)
        conv_parts.append(_dot(hb, wc3_ref[:, l * cw:(l + 1) * cw]))
        yield
    q_g, k_g, block_g = split_groups(q_l), split_groups(k_l), split_groups(block)
    half = len(q_g) // 2
    for l in range(SUB_LEVELS, MIX_LEVELS - 1):
        add_level(l, join_groups(q_g), join_groups(k_g))
        q_g, k_g, block_g = next_level_groups(l, q_g, k_g, block_g)
        if l == SUB_LEVELS:
            v = _dot(hb, wqkvg_ref[:, 2 * GLA_QK:2 * GLA_QK + GLA_V])
        yield
    low = head_scores(join_groups(q_g[half:]).astype(bf16), join_groups(k_g[:half]).astype(bf16))
    zero_block = jnp.zeros((H, H), f32)
    scores = [jnp.concatenate([jnp.concatenate([diag0[h], zero_block], axis=1),
                               jnp.concatenate([low[h], diag1[h]], axis=1)], axis=0) for h in range(GLA_HEADS)]

    yield

    q_g, k_g, block_g = next_level_groups(MIX_LEVELS - 1, q_g, k_g, block_g)
    st = st_ref[...]
    o_state = head_scores(join_groups(q_g).astype(bf16), st.astype(bf16))
    upd = _dot(v.T.astype(bf16), join_groups(k_g).astype(bf16))
    new_st = st * block_g[0][0:1]
    for h in range(GLA_HEADS):
        new_st = new_st + jnp.where(lane_head_st == h, upd[h * GLA_DV:(h + 1) * GLA_DV], 0.0)
    st_ref[...] = new_st
    g = _dot(hb, wqkvg_ref[:, 2 * GLA_QK + GLA_V:])
    ggla = ggla_ref[...]
    y_heads = []
    for h in range(GLA_HEADS):
        v_h = v[:, h * GLA_DV:(h + 1) * GLA_DV]
        o = _dot(scores[h].astype(bf16), v_h.astype(bf16)) + o_state[h]
        g_h = g[:, h * GLA_DV:(h + 1) * GLA_DV]
        y_heads.append(_rms(o, ggla) * (g_h * jax.nn.sigmoid(g_h)))
        yield

    yield

    cb, cu = conv_parts[0], conv_parts[1] * conv_parts[2]
    crow = lax.broadcasted_iota(jnp.int32, (T, cw), 0)
    prev2, prev1 = carry_ref[0:1, :], carry_ref[1:2, :]
    m1 = jnp.where(crow == 0, prev1, _shift_rows(cu, 1))
    m2 = jnp.where(crow == 0, prev2, jnp.where(crow == 1, prev1, _shift_rows(cu, 2)))
    wconv = wconv_ref[...]
    y_conv = cb * (wconv[0:1, :] * m2 + wconv[1:2, :] * m1 + wconv[2:3, :] * cu)
    carry_ref[0:2, :] = cu[T - 2:, :]

    y = jnp.concatenate(y_heads + [y_conv], axis=1).astype(bf16)
    x1 = x + _dot(y, wout_ref[...])

    yield

    h2 = _rms(x1, gmoe_ref[...])
    h2_hi, h2_lo = _split_bf16(h2)
    part = _dot_nt(wrt_ref[...], h2_hi)
    logits = (part[:ROUTE_ROWS] + part[ROUTE_ROWS:] + _dot_nt(wrt_ref[:ROUTE_ROWS, :], h2_lo)) + brt_ref[...]
    gl = [logits[i:i + 1, :] for i in range(N_GROUPS)]
    gmax = functools.reduce(jnp.maximum, gl)
    gsum = functools.reduce(lambda a, b: a + b, [jnp.exp(t - gmax) for t in gl])
    p_grp = 1.0 / gsum
    g_sel = jnp.full_like(gmax, N_GROUPS - 1).astype(jnp.int32)
    for i in reversed(range(N_GROUPS - 1)):
        g_sel = jnp.where(gl[i] == gmax, i, g_sel)
    ig = []
    for j in range(EXPERTS_PER_GROUP):
        acc = jnp.zeros_like(gmax)
        for gi in range(N_GROUPS):
            r0 = ROUTE_EXPERT_ROW0 + gi * EXPERTS_PER_GROUP + j
            acc = acc + jnp.where(g_sel == gi, logits[r0:r0 + 1, :], 0.0)
        ig.append(acc)

    def first_argmax(vals):
        m = functools.reduce(jnp.maximum, vals)
        idx = jnp.full_like(m, len(vals) - 1).astype(jnp.int32)
        for i in reversed(range(len(vals) - 1)):
            idx = jnp.where(vals[i] == m, i, idx)
        return m, idx

    m1_, i1 = first_argmax(ig)
    m2_, i2 = first_argmax([jnp.where(i1 == j, -jnp.inf, ig[j]) for j in range(EXPERTS_PER_GROUP)])
    e21 = jnp.exp(m2_ - m1_)
    w1 = p_grp / (1.0 + e21)
    w2 = p_grp * e21 / (1.0 + e21)
    key = jnp.minimum(i1, i2) * EXPERTS_PER_GROUP + jnp.maximum(i1, i2)
    pair = jnp.zeros_like(key)
    a_loc = jnp.zeros_like(key)
    for kk, pp in PAIR_OF_KEY.items():
        pair = jnp.where(key == kk, pp, pair)
        a_loc = jnp.where(key == kk, PAIR_A[pp], a_loc)
    w_a = jnp.where(i1 == a_loc, w1, w2)
    w_b = jnp.where(i1 == a_loc, w2, w1)
    cls = g_sel * N_PAIRS + pair
    rr = lax.broadcasted_iota(jnp.int32, (LANES, T), 0)
    rec_t = (jnp.where(rr == INFO_WA, jnp.broadcast_to(w_a, (LANES, T)), 0.0)
             + jnp.where(rr == INFO_WB, jnp.broadcast_to(w_b, (LANES, T)), 0.0))
    pieces = [x1[:, c * LANES:(c + 1) * LANES] for c in range(D // LANES)] + [rec_t.T]
    assert len(pieces) == ROW_RECORD
    _store_records(rows_ref, pieces, ROW_RECORD)

    onehot = (rr == jnp.broadcast_to(cls, (LANES, T))).astype(f32)
    count = count_ref[:, 0:1]
    before = _dot_nt(onehot.astype(bf16), tril_ref[...]) - onehot + count
    rank = jnp.sum(onehot * before, axis=0, keepdims=True).astype(jnp.int32)
    r8 = lax.broadcasted_iota(jnp.int32, (8, T), 0)
    meta_ref[0] = jnp.where(r8 == INFO_CLS, jnp.broadcast_to(cls, (8, T)),
                            jnp.where(r8 == INFO_RANK, jnp.broadcast_to(rank, (8, T)), 0))
    new_count = jnp.broadcast_to(count + jnp.sum(onehot, axis=1, keepdims=True), count_ref.shape)
    count_ref[...] = new_count
    counts_ref[...] = new_count.astype(jnp.int32)


def _store_records(ref, pieces, record_rows):
    n = ref.shape[0] // record_rows
    for c, piece in enumerate(pieces):
        ref[pl.ds(c, n, stride=record_rows), :] = piece


def _load_records(ref, first, count, record_rows, lead=()):
    n = ref.shape[-2] // record_rows
    return jnp.concatenate([ref[lead + (pl.ds(first + c, n, stride=record_rows), slice(None))]
                            for c in range(count)], axis=1)


class _RowGather:
    def __init__(self, index_of, src_hbm, buf, sems, record_rows, n_records, n_valid_of=None):
        self.index_of, self.src_hbm, self.buf, self.sems = index_of, src_hbm, buf, sems
        self.rr, self.n, self.n_valid_of = record_rows, n_records, n_valid_of

    def _groups(self, tile, body):
        for g0 in range(0, self.n, GATHER_GROUP):
            if self.n_valid_of is None:
                body(g0)
            else:
                pl.when(g0 < self.n_valid_of(tile))(functools.partial(body, g0))

    def _issue(self, tile, slot):
        rr = self.rr

        def group(g0):
            for r in range(g0, g0 + GATHER_GROUP):
                first = self.index_of(tile * self.n + r) * rr
                pltpu.make_async_copy(self.src_hbm.at[pl.ds(first, rr), :],
                                      self.buf.at[slot, pl.ds(r * rr, rr), :], self.sems.at[slot]).start()

        self._groups(tile, group)

    def start(self, tile, slot):
        if isinstance(slot, int):
            self._issue(tile, slot)
        else:
            for static_slot in range(2):
                pl.when(slot == static_slot)(functools.partial(self._issue, tile, static_slot))

    def wait(self, tile, slot):
        rows = GATHER_GROUP * self.rr

        def group(g0):
            pltpu.make_async_copy(self.src_hbm.at[pl.ds(0, rows), :],
                                  self.buf.at[slot, pl.ds(g0 * self.rr, rows), :], self.sems.at[slot]).wait()

        self._groups(tile, group)


def _expert_kernel(ea_ref, eb_ref, nused_ref, nvalid_ref, tstart_ref, cnt_ref, slot_ref,
                   rows_hbm, gmoe_ref, wga_ref, wua_ref, wda_ref, wgb_ref, wub_ref, wdb_ref,
                   y_ref, buf, sems, src_ref):
    bf16 = jnp.bfloat16
    n_x = gmoe_ref.shape[-1] // LANES
    n_tok = slot_ref.shape[0]
    step, n_used = pl.program_id(0), nused_ref[0]
    gather = _RowGather(lambda i: src_ref[i], rows_hbm, buf, sems, ROW_RECORD, MOE_TILE,
                        n_valid_of=lambda tile: nvalid_ref[tile])

    @pl.when(step == 0)
    def _():
        buf[...] = jnp.zeros_like(buf)
        for c in range(N_CLASSES):
            base = tstart_ref[c] * MOE_TILE

            def pad(r, carry, base=base):
                src_ref[base + r] = jnp.minimum(base + r, n_tok - 1)
                return carry

            lax.fori_loop(cnt_ref[c], (tstart_ref[c + 1] - tstart_ref[c]) * MOE_TILE, pad, 0)

        def place(it, carry):
            for u in range(PLACE_UNROLL):
                t = it * PLACE_UNROLL + u
                src_ref[slot_ref[t]] = t
            return carry

        lax.fori_loop(0, n_tok // PLACE_UNROLL, place, 0)
        gather.start(0, 0)

    slot = step % 2

    @pl.when(step + 1 < n_used)
    def _():
        gather.start(step + 1, 1 - slot)

    @pl.when(step < n_used)
    def _():
        gather.wait(step, slot)
        rec = _load_records(buf, n_x, 1, ROW_RECORD, lead=(slot,))
        h2 = _rms(_load_records(buf, 0, n_x, ROW_RECORD, lead=(slot,)), gmoe_ref[...]).astype(bf16)

        gate_a, gate_b = _dot(h2, wga_ref[0]), _dot(h2, wgb_ref[0])
        up_a, up_b = _dot(h2, wua_ref[0]), _dot(h2, wub_ref[0])
        hid_a = ((gate_a * jax.nn.sigmoid(gate_a)) * up_a).astype(bf16)
        hid_b = ((gate_b * jax.nn.sigmoid(gate_b)) * up_b).astype(bf16)
        y = rec[:, INFO_WA:INFO_WA + 1] * _dot(hid_a, wda_ref[0])
        y = y + rec[:, INFO_WB:INFO_WB + 1] * _dot(hid_b, wdb_ref[0])
        _store_records(y_ref, [y[:, c * LANES:(c + 1) * LANES] for c in range(n_x)], Y_RECORD)

    @pl.when(step >= n_used)
    def _():
        y_ref[...] = jnp.zeros_like(y_ref)


def _ple_final_kernel(slot_ref, x1_ref, p_ref, y_hbm, gple_ref, wpg_ref, wpp_ref, gfin_ref,
                      out_ref, buf, sems):
    bf16 = jnp.bfloat16
    step, n_steps = pl.program_id(0), pl.num_programs(0)
    n_x = gple_ref.shape[-1] // LANES
    gather = _RowGather(lambda t: slot_ref[t], y_hbm, buf, sems, Y_RECORD, PLE_TILE)

    @pl.when(step == 0)
    def _():
        gather.start(0, 0)

    slot = step % 2

    @pl.when(step + 1 < n_steps)
    def _():
        gather.start(step + 1, 1 - slot)

    gather.wait(step, slot)
    p_proj = _dot(p_ref[...].astype(bf16), wpp_ref[...])
    x2 = _load_records(x1_ref, 0, n_x, ROW_RECORD) + _load_records(buf, 0, n_x, Y_RECORD, lead=(slot,))
    gate_p = jax.nn.sigmoid(_dot(_rms(x2, gple_ref[...]).astype(bf16), wpg_ref[...]))
    x3 = x2 + gate_p * p_proj
    out_ref[...] = _rms(x3, gfin_ref[...])


def _const_spec(shape):
    return pl.BlockSpec(shape, lambda *_: (0,) * len(shape), pipeline_mode=pl.Buffered(1))


def _mixer(x, g_mix, w_in, w_gla_gate, b_gla_gate, g_gla_out, w_conv, w_out, g_moe,
           w_group, b_group, w_router, b_router, w_exp_gate, w_exp_up, w_exp_down):
    b, s, d = x.shape
    step_tokens = MIX_TILE * MIX_TILES_PER_STEP
    n_steps = s // step_tokens
    n_exp, _, de = w_exp_gate.shape
    assert (n_exp * d) % (BF16_ROWS * n_steps) == 0 and (n_exp * de) % (BF16_ROWS * n_steps) == 0
    up_rows, down_rows = n_exp * d // n_steps, n_exp * de // n_steps
    bf16 = jnp.bfloat16
    n_qkvg = 2 * GLA_QK + 2 * GLA_V
    w_qkvg = w_in[:, :n_qkvg].astype(bf16)
    w_a = w_in[:, n_qkvg:n_qkvg + GLA_LOWRANK].astype(bf16)
    w_c3 = w_in[:, n_qkvg + GLA_LOWRANK:].astype(bf16)
    cw = w_c3.shape[1] // 3
    wrt = jnp.zeros((ROUTE_ROWS, d), jnp.float32)
    wrt = wrt.at[:N_GROUPS].set(w_group.T).at[ROUTE_EXPERT_ROW0:ROUTE_EXPERT_ROW0 + N_EXPERTS].set(w_router.T)
    brt = jnp.zeros((ROUTE_ROWS, 1), jnp.float32)
    brt = brt.at[:N_GROUPS, 0].set(b_group).at[ROUTE_EXPERT_ROW0:ROUTE_EXPERT_ROW0 + N_EXPERTS, 0].set(b_router)
    wrt_split = jnp.concatenate(_split_bf16(wrt), axis=0)
    gate_hi, gate_lo = _split_bf16(w_gla_gate)
    w_gate_split = jnp.concatenate([gate_hi, gate_hi, gate_lo], axis=0)
    args = (x, g_mix[None, :], w_qkvg, w_a, w_c3, w_gate_split, b_gla_gate[None, :], g_gla_out[None, :],
            w_conv, w_out.astype(bf16), g_moe[None, :], wrt_split, brt)
    slabs = (w_exp_gate.reshape(n_exp * d, de), w_exp_up.reshape(n_exp * d, de), w_exp_down.reshape(n_exp * de, d))
    slab_specs = [pl.BlockSpec((rows_, width), lambda j: (j, 0))
                  for rows_, width in ((up_rows, de), (up_rows, de), (down_rows, d))]
    in_specs = [pl.BlockSpec((b, step_tokens, d), lambda j: (0, j, 0))]
    in_specs += [_const_spec(a.shape) for a in args[1:]] + slab_specs
    rows, meta, counts, wg16, wu16, wd16 = pl.pallas_call(
        _mixer_kernel,
        grid=(n_steps,),
        in_specs=in_specs,
        out_specs=[pl.BlockSpec((b, step_tokens * ROW_RECORD, LANES), lambda j: (0, j, 0)),
                   pl.BlockSpec((b, MIX_TILES_PER_STEP, 8, MIX_TILE), lambda j: (0, j, 0, 0)),
                   pl.BlockSpec((LANES, LANES), lambda j: (0, 0))] + slab_specs,
        out_shape=[jax.ShapeDtypeStruct((b, s * ROW_RECORD, LANES), jnp.float32),
                   jax.ShapeDtypeStruct((b, s // MIX_TILE, 8, MIX_TILE), jnp.int32),
                   jax.ShapeDtypeStruct((LANES, LANES), jnp.int32)]
                  + [jax.ShapeDtypeStruct(w.shape, bf16) for w in slabs],
        scratch_shapes=[pltpu.VMEM((b, GLA_DV, GLA_QK), jnp.float32),
                        pltpu.VMEM((b, 8, cw), jnp.float32),
                        pltpu.VMEM((LANES, LANES), jnp.float32),
                        pltpu.VMEM((MIX_TILE, MIX_TILE), jnp.int32),
                        pltpu.VMEM((MIX_TILE, MIX_TILE), bf16)],
        compiler_params=pltpu.CompilerParams(dimension_semantics=("arbitrary",),
                                             vmem_limit_bytes=VMEM_LIMIT),
        name="mixer",
    )(*args, *slabs)
    return (rows.reshape(b * s * ROW_RECORD, LANES), meta.reshape(b * s // MIX_TILE, 8, MIX_TILE), counts,
            wg16.reshape(n_exp, d, de), wu16.reshape(n_exp, d, de), wd16.reshape(n_exp, de, d))


def _sort_plan(meta, counts, n_tok):
    i32 = jnp.int32
    n_tiles = n_tok // MOE_TILE + N_CLASSES
    cls = meta[:, INFO_CLS, :].reshape(n_tok)
    rank = meta[:, INFO_RANK, :].reshape(n_tok)
    cnt = counts[:N_CLASSES, 0]
    tiles_per_cls = (cnt + MOE_TILE - 1) // MOE_TILE
    tile_end = jnp.cumsum(tiles_per_cls)
    n_used = tile_end[-1:]
    tstart = jnp.concatenate([tile_end - tiles_per_cls, n_used])
    tile_id = jnp.minimum(jnp.arange(n_tiles, dtype=i32), n_used - 1)
    tile_cls = jnp.sum((tile_id[:, None] >= tile_end[None, :]).astype(i32), axis=1)
    grp, pair = tile_cls // N_PAIRS, tile_cls % N_PAIRS
    e_a, e_b = grp * EXPERTS_PER_GROUP, grp * EXPERTS_PER_GROUP
    for pp in range(N_PAIRS):
        e_a = e_a + jnp.where(pair == pp, PAIR_A[pp], 0)
        e_b = e_b + jnp.where(pair == pp, PAIR_B[pp], 0)
    of_tile = (tile_cls[:, None] == jnp.arange(N_CLASSES, dtype=i32)[None, :]).astype(i32)
    tile_base = jnp.sum(of_tile * tstart[None, :N_CLASSES], axis=1)
    tile_cnt = jnp.sum(of_tile * cnt[None, :], axis=1)
    n_valid = jnp.clip(tile_cnt - (tile_id - tile_base) * MOE_TILE, 0, MOE_TILE)
    first_tile = functools.reduce(lambda acc, c: jnp.where(cls == c, tstart[c], acc), range(N_CLASSES),
                                  jnp.zeros_like(cls))
    slot = first_tile * MOE_TILE + rank
    return dict(slot=slot, n_valid=n_valid, cnt=cnt, tstart=tstart, e_a=e_a, e_b=e_b, n_used=n_used, n_tiles=n_tiles)


def _experts(plan, rows, g_moe, wg, wu, wd):
    n_tiles = plan["n_tiles"]
    d, de = wg.shape[-2:]
    w_a = lambda shape: pl.BlockSpec(shape, lambda i, ea, eb, *_: (ea[i], 0, 0))
    w_b = lambda shape: pl.BlockSpec(shape, lambda i, ea, eb, *_: (eb[i], 0, 0))
    return pl.pallas_call(
        _expert_kernel,
        grid_spec=pltpu.PrefetchScalarGridSpec(
            num_scalar_prefetch=7, grid=(n_tiles,),
            in_specs=[pl.BlockSpec(memory_space=pl.ANY),
                      pl.BlockSpec((1, d), lambda i, *_: (0, 0)),
                      w_a((1, d, de)), w_a((1, d, de)), w_a((1, de, d)),
                      w_b((1, d, de)), w_b((1, d, de)), w_b((1, de, d))],
            out_specs=pl.BlockSpec((MOE_TILE * Y_RECORD, LANES), lambda i, *_: (i, 0)),
            scratch_shapes=[pltpu.VMEM((2, MOE_TILE * ROW_RECORD, LANES), jnp.float32),
                            pltpu.SemaphoreType.DMA((2,)),
                            pltpu.SMEM((n_tiles * MOE_TILE,), jnp.int32)]),
        out_shape=jax.ShapeDtypeStruct((n_tiles * MOE_TILE * Y_RECORD, LANES), jnp.float32),
        compiler_params=pltpu.CompilerParams(dimension_semantics=("arbitrary",),
                                             vmem_limit_bytes=VMEM_LIMIT),
        name="experts",
    )(plan["e_a"], plan["e_b"], plan["n_used"], plan["n_valid"], plan["tstart"], plan["cnt"], plan["slot"],
      rows, g_moe[None, :], wg, wu, wd, wg, wu, wd)


def _ple_final(plan, rows, p, y_sorted, g_ple, w_ple_gate, w_ple_proj, g_final):
    n_tok, dp = p.shape
    d = w_ple_gate.shape[0]
    bf16 = jnp.bfloat16
    const = lambda shape: pl.BlockSpec(shape, lambda i, *_: (0,) * len(shape))
    tile = lambda width: pl.BlockSpec((PLE_TILE, width), lambda i, *_: (i, 0))
    return pl.pallas_call(
        _ple_final_kernel,
        grid_spec=pltpu.PrefetchScalarGridSpec(
            num_scalar_prefetch=1, grid=(n_tok // PLE_TILE,),
            in_specs=[pl.BlockSpec((PLE_TILE * ROW_RECORD, LANES), lambda i, *_: (i, 0)),
                      tile(dp),
                      pl.BlockSpec(memory_space=pl.ANY),
                      const((1, d)), const((d, d)), const((dp, d)), const((1, d))],
            out_specs=tile(d),
            scratch_shapes=[pltpu.VMEM((2, PLE_TILE * Y_RECORD, LANES), jnp.float32),
                            pltpu.SemaphoreType.DMA((2,))]),
        out_shape=jax.ShapeDtypeStruct((n_tok, d), jnp.float32),
        compiler_params=pltpu.CompilerParams(dimension_semantics=("arbitrary",),
                                             vmem_limit_bytes=VMEM_LIMIT),
        name="ple_final",
    )(plan["slot"], rows, p, y_sorted,
      g_ple[None, :], w_ple_gate.astype(bf16), w_ple_proj.astype(bf16), g_final[None, :])


def kernel(x, p, g_mix, w_in, w_gla_gate, b_gla_gate, g_gla_out, w_conv, w_out, g_moe, w_group, b_group,
           w_router, b_router, w_exp_gate, w_exp_up, w_exp_down, g_ple, w_ple_gate, w_ple_proj, g_final):
    depth = w_in.shape[0]
    assert depth == 1, "the final norm is fused into the last (only) layer"
    b, s, d = x.shape
    n_tok = b * s
    assert s % (MIX_TILE * MIX_TILES_PER_STEP) == 0 and n_tok % MOE_TILE == 0 and n_tok % PLE_TILE == 0 and n_tok % PLACE_UNROLL == 0
    rows, meta, counts, wg16, wu16, wd16 = _mixer(
        x, g_mix[0], w_in[0], w_gla_gate[0], b_gla_gate[0], g_gla_out[0], w_conv[0], w_out[0], g_moe[0],
        w_group[0], b_group[0], w_router[0], b_router[0], w_exp_gate[0], w_exp_up[0], w_exp_down[0])
    plan = _sort_plan(meta, counts, n_tok)
    y_sorted = _experts(plan, rows, g_moe[0], wg16, wu16, wd16)
    out = _ple_final(plan, rows, p[0].reshape(n_tok, -1), y_sorted, g_ple[0], w_ple_gate[0], w_ple_proj[0],
                     g_final)
    return out.reshape(b, s, d)
```

```python
import functools

import jax
import jax.numpy as jnp
from jax import lax
from jax.experimental import pallas as pl
from jax.experimental.pallas import tpu as pltpu

EPS = 1e-6
GLA_HEADS = 4
GLA_DK = 64
GLA_DV = 128
GLA_QK = GLA_HEADS * GLA_DK
GLA_V = GLA_HEADS * GLA_DV
GLA_LOWRANK = 16
GLA_TAU = 16.0
CONV_K = 3
N_GROUPS = 4
EXPERTS_PER_GROUP = 4
N_EXPERTS = N_GROUPS * EXPERTS_PER_GROUP

LANES = 128
MIX_TILE = 256
MIX_LEVELS = 8
SUBLANES = 8
BF16_ROWS = 16
SUB_LEVELS = 3
MIX_TILES_PER_STEP = 1
ROUTE_ROWS = 32
ROUTE_EXPERT_ROW0 = 8
MOE_TILE = 256
VMEM_LIMIT = 56 * 1024 * 1024

PAIR_A = (0, 2, 2, 0, 0, 1)
PAIR_B = (1, 1, 3, 3, 2, 3)
PAIR_OF_KEY = {1: 0, 6: 1, 11: 2, 3: 3, 2: 4, 7: 5}
N_PAIRS = len(PAIR_A)
N_CLASSES = N_GROUPS * N_PAIRS
INFO_CLS, INFO_RANK, INFO_WA, INFO_WB = 0, 1, 2, 3
PLACE_UNROLL = 16
ROW_RECORD = 9
Y_RECORD = 8
GATHER_GROUP = 32
PLE_TILE = 512

_NT = (((1,), (1,)), ((), ()))
_DONE = object()


def _rms(x, g):
    return x * lax.rsqrt(jnp.mean(x * x, axis=-1, keepdims=True) + EPS) * g


def _dot(a, b):
    return jnp.dot(a, b, preferred_element_type=jnp.float32)


def _dot_nt(a, b):
    return lax.dot_general(a, b, _NT, preferred_element_type=jnp.float32)


def _split_bf16(a):
    hi = a.astype(jnp.bfloat16)
    return hi, (a - hi.astype(jnp.float32)).astype(jnp.bfloat16)


def _shift_rows(x, shift):
    return pltpu.roll(x, shift % x.shape[0], axis=0)


def _mixer_kernel(x_ref, gmix_ref, wqkvg_ref, wa_ref, wc3_ref, wgate_ref, bgate_ref, ggla_ref,
                  wconv_ref, wout_ref, gmoe_ref, wrt_ref, brt_ref, wge_ref, wue_ref, wde_ref,
                  x1_ref, rows_ref, meta_ref, counts_ref, wge16_ref, wue16_ref, wde16_ref,
                  st_ref, carry_ref, count_ref, level_ref, tril_ref):
    bf16 = jnp.bfloat16
    T = MIX_TILE

    wge16_ref[...] = wge_ref[...].astype(bf16)
    wue16_ref[...] = wue_ref[...].astype(bf16)
    wde16_ref[...] = wde_ref[...].astype(bf16)

    @pl.when(pl.program_id(0) == 0)
    def _():
        st_ref[...] = jnp.zeros_like(st_ref)
        carry_ref[...] = jnp.zeros_like(carry_ref)
        count_ref[...] = jnp.zeros_like(count_ref)
        tt = lax.broadcasted_iota(jnp.int32, (T, T), 0)
        ss = lax.broadcasted_iota(jnp.int32, (T, T), 1)
        txs = jnp.bitwise_xor(tt, ss)
        level = jnp.zeros((T, T), jnp.int32)
        for j in range(1, MIX_LEVELS):
            level = level + (txs >= (1 << j)).astype(jnp.int32)
        level_ref[...] = jnp.where(tt > ss, level, jnp.where(tt == ss, MIX_LEVELS, -1))
        tril_ref[...] = (ss <= tt).astype(bf16)

    tiles = [_mixer_tile(x_ref.at[pl.ds(b, 1), pl.ds(i * T, T)], gmix_ref, wqkvg_ref, wa_ref, wc3_ref, wgate_ref,
                         bgate_ref, ggla_ref, wconv_ref, wout_ref, gmoe_ref, wrt_ref, brt_ref,
                         x1_ref.at[b, pl.ds(i * T, T)],
                         rows_ref.at[b, pl.ds(i * T * ROW_RECORD, T * ROW_RECORD)], meta_ref.at[b, pl.ds(i, 1)],
                         counts_ref, st_ref.at[b], carry_ref.at[b], count_ref, level_ref, tril_ref)
             for i in range(MIX_TILES_PER_STEP) for b in range(x_ref.shape[0])]
    while tiles:
        tiles = [t for t in tiles if next(t, _DONE) is not _DONE]


def _mixer_tile(x_ref, gmix_ref, wqkvg_ref, wa_ref, wc3_ref, wgate_ref, bgate_ref, ggla_ref,
                wconv_ref, wout_ref, gmoe_ref, wrt_ref, brt_ref,
                x1_ref, rows_ref, meta_ref, counts_ref,
                st_ref, carry_ref, count_ref, level_ref, tril_ref):
    f32, bf16 = jnp.float32, jnp.bfloat16
    T = MIX_TILE
    D = x_ref.shape[-1]

    x = x_ref[0]
    hb = _rms(x, gmix_ref[...]).astype(bf16)
    qk = _dot(hb, wqkvg_ref[:, :2 * GLA_QK])
    q = qk[:, :GLA_QK] * (GLA_DK ** -0.5)
    k = qk[:, GLA_QK:]
    a_low = _dot(hb, wa_ref[...])
    a_hi, a_lo = _split_bf16(a_low)
    z = _dot(jnp.concatenate([a_hi, a_lo, a_hi], axis=1), wgate_ref[...]) + bgate_ref[...]
    la = (jnp.minimum(z, 0.0) - jnp.log(1.0 + jnp.exp(-jnp.abs(z)))) * (1.0 / GLA_TAU)
    yield

    row = lax.broadcasted_iota(jnp.int32, (T, GLA_QK), 0)

    def next_level(l, q_l, k_l, block):
        upper = ((row >> l) & 1) == 1
        below = _shift_rows(block, 1 << l)
        above = _shift_rows(block, -(1 << l))
        return (q_l * jnp.where(upper, below, 1.0), k_l * jnp.where(upper, 1.0, above),
                block * jnp.where(upper, below, above))

    decay = jnp.exp(la)
    H = T // 2
    assert GLA_DV == H
    half_level = level_ref[0:H, 0:H]
    lane_head_st = lax.broadcasted_iota(jnp.int32, (H, GLA_QK), 1) // GLA_DK

    heads_per_tile = LANES // GLA_DK
    lane_head_tile = lax.broadcasted_iota(jnp.int32, (H, LANES), 1) // GLA_DK

    def head_scores(q_half, k_half):
        out = []
        for tile in range(GLA_QK // LANES):
            q_t = q_half[:, tile * LANES:(tile + 1) * LANES]
            k_t = k_half[:, tile * LANES:(tile + 1) * LANES]
            k_rows = jnp.concatenate([jnp.where(lane_head_tile == j, k_t, jnp.zeros_like(k_t))
                                      for j in range(heads_per_tile)], axis=0)
            p = _dot_nt(q_t, k_rows)
            out += [p[:, j * H:(j + 1) * H] for j in range(heads_per_tile)]
        return out

    diag0 = [jnp.zeros((H, H), f32) for _ in range(GLA_HEADS)]
    diag1 = [jnp.zeros((H, H), f32) for _ in range(GLA_HEADS)]

    def add_level(l, q_l, k_l):
        sel = half_level == l
        ql, kl = q_l.astype(bf16), k_l.astype(bf16)
        p0 = head_scores(ql[:H], kl[:H])
        p1 = head_scores(ql[H:], kl[H:])
        for h in range(GLA_HEADS):
            diag0[h] = jnp.where(sel, p0[h], diag0[h])
            diag1[h] = jnp.where(sel, p1[h], diag1[h])

    def split_groups(a):
        return [a[i * SUBLANES:(i + 1) * SUBLANES] for i in range(T // SUBLANES)]

    def join_groups(groups):
        return jnp.concatenate(groups, axis=0)

    def next_level_groups(l, q_g, k_g, block_g):
        m = 1 << (l - SUB_LEVELS)
        products = {}
        q_n, k_n, block_n = [], [], []
        for gi in range(len(q_g)):
            lo, hi = gi & ~m, gi | m
            key = (id(block_g[lo]), id(block_g[hi]))
            if key not in products:
                products[key] = block_g[lo] * block_g[hi]
            q_n.append(q_g[gi] * block_g[lo] if gi & m else q_g[gi])
            k_n.append(k_g[gi] if gi & m else k_g[gi] * block_g[hi])
            block_n.append(products[key])
        return q_n, k_n, block_n

    add_level(MIX_LEVELS, q, k)
    q_l, k_l, block = q * decay, k, decay
    cw = wc3_ref.shape[1] // 3
    conv_parts = []
    for l in range(SUB_LEVELS):
        add_level(l, q_l, k_l)
        q_l, k_l, block = next_level(l, q_l, k_l, block)
        conv_parts.append(_dot(hb, wc3_ref[:, l * cw:(l + 1) * cw]))
        yield
    q_g, k_g, block_g = split_groups(q_l), split_groups(k_l), split_groups(block)
    half = len(q_g) // 2
    for l in range(SUB_LEVELS, MIX_LEVELS - 1):
        add_level(l, join_groups(q_g), join_groups(k_g))
        q_g, k_g, block_g = next_level_groups(l, q_g, k_g, block_g)
        if l == SUB_LEVELS:
            v = _dot(hb, wqkvg_ref[:, 2 * GLA_QK:2 * GLA_QK + GLA_V])
        yield
    low = head_scores(join_groups(q_g[half:]).astype(bf16), join_groups(k_g[:half]).astype(bf16))
    zero_block = jnp.zeros((H, H), f32)
    scores = [jnp.concatenate([jnp.concatenate([diag0[h], zero_block], axis=1),
                               jnp.concatenate([low[h], diag1[h]], axis=1)], axis=0) for h in range(GLA_HEADS)]

    yield

    q_g, k_g, block_g = next_level_groups(MIX_LEVELS - 1, q_g, k_g, block_g)
    st = st_ref[...]
    o_state = head_scores(join_groups(q_g).astype(bf16), st.astype(bf16))
    upd = _dot(v.T.astype(bf16), join_groups(k_g).astype(bf16))
    new_st = st * block_g[0][0:1]
    for h in range(GLA_HEADS):
        new_st = new_st + jnp.where(lane_head_st == h, upd[h * GLA_DV:(h + 1) * GLA_DV], 0.0)
    st_ref[...] = new_st
    g = _dot(hb, wqkvg_ref[:, 2 * GLA_QK + GLA_V:])
    ggla = ggla_ref[...]
    y_heads = []
    for h in range(GLA_HEADS):
        v_h = v[:, h * GLA_DV:(h + 1) * GLA_DV]
        o = _dot(scores[h].astype(bf16), v_h.astype(bf16)) + o_state[h]
        g_h = g[:, h * GLA_DV:(h + 1) * GLA_DV]
        y_heads.append(_rms(o, ggla) * (g_h * jax.nn.sigmoid(g_h)))
        yield

    yield

    cb, cu = conv_parts[0], conv_parts[1] * conv_parts[2]
    crow = lax.broadcasted_iota(jnp.int32, (T, cw), 0)
    prev2, prev1 = carry_ref[0:1, :], carry_ref[1:2, :]
    m1 = jnp.where(crow == 0, prev1, _shift_rows(cu, 1))
    m2 = jnp.where(crow == 0, prev2, jnp.where(crow == 1, prev1, _shift_rows(cu, 2)))
    wconv = wconv_ref[...]
    y_conv = cb * (wconv[0:1, :] * m2 + wconv[1:2, :] * m1 + wconv[2:3, :] * cu)
    carry_ref[0:2, :] = cu[T - 2:, :]

    y = jnp.concatenate(y_heads + [y_conv], axis=1).astype(bf16)
    x1 = x + _dot(y, wout_ref[...])
    x1_ref[...] = x1

    yield

    h2 = _rms(x1, gmoe_ref[...])
    h2_hi, h2_lo = _split_bf16(h2)
    part = _dot_nt(wrt_ref[...], h2_hi)
    logits = (part[:ROUTE_ROWS] + part[ROUTE_ROWS:] + _dot_nt(wrt_ref[:ROUTE_ROWS, :], h2_lo)) + brt_ref[...]
    gl = [logits[i:i + 1, :] for i in range(N_GROUPS)]
    gmax = functools.reduce(jnp.maximum, gl)
    gsum = functools.reduce(lambda a, b: a + b, [jnp.exp(t - gmax) for t in gl])
    p_grp = 1.0 / gsum
    g_sel = jnp.full_like(gmax, N_GROUPS - 1).astype(jnp.int32)
    for i in reversed(range(N_GROUPS - 1)):
        g_sel = jnp.where(gl[i] == gmax, i, g_sel)
    ig = []
    for j in range(EXPERTS_PER_GROUP):
        acc = jnp.zeros_like(gmax)
        for gi in range(N_GROUPS):
            r0 = ROUTE_EXPERT_ROW0 + gi * EXPERTS_PER_GROUP + j
            acc = acc + jnp.where(g_sel == gi, logits[r0:r0 + 1, :], 0.0)
        ig.append(acc)

    def first_argmax(vals):
        m = functools.reduce(jnp.maximum, vals)
        idx = jnp.full_like(m, len(vals) - 1).astype(jnp.int32)
        for i in reversed(range(len(vals) - 1)):
            idx = jnp.where(vals[i] == m, i, idx)
        return m, idx

    m1_, i1 = first_argmax(ig)
    m2_, i2 = first_argmax([jnp.where(i1 == j, -jnp.inf, ig[j]) for j in range(EXPERTS_PER_GROUP)])
    e21 = jnp.exp(m2_ - m1_)
    w1 = p_grp / (1.0 + e21)
    w2 = p_grp * e21 / (1.0 + e21)
    key = jnp.minimum(i1, i2) * EXPERTS_PER_GROUP + jnp.maximum(i1, i2)
    pair = jnp.zeros_like(key)
    a_loc = jnp.zeros_like(key)
    for kk, pp in PAIR_OF_KEY.items():
        pair = jnp.where(key == kk, pp, pair)
        a_loc = jnp.where(key == kk, PAIR_A[pp], a_loc)
    w_a = jnp.where(i1 == a_loc, w1, w2)
    w_b = jnp.where(i1 == a_loc, w2, w1)
    cls = g_sel * N_PAIRS + pair
    rr = lax.broadcasted_iota(jnp.int32, (LANES, T), 0)
    rec_t = (jnp.where(rr == INFO_WA, jnp.broadcast_to(w_a, (LANES, T)), 0.0)
             + jnp.where(rr == INFO_WB, jnp.broadcast_to(w_b, (LANES, T)), 0.0))
    pieces = [h2[:, c * LANES:(c + 1) * LANES] for c in range(D // LANES)] + [rec_t.T]
    assert len(pieces) == ROW_RECORD
    _store_records(rows_ref, pieces, ROW_RECORD)

    onehot = (rr == jnp.broadcast_to(cls, (LANES, T))).astype(f32)
    count = count_ref[:, 0:1]
    before = _dot_nt(onehot.astype(bf16), tril_ref[...]) - onehot + count
    rank = jnp.sum(onehot * before, axis=0, keepdims=True).astype(jnp.int32)
    r8 = lax.broadcasted_iota(jnp.int32, (8, T), 0)
    meta_ref[0] = jnp.where(r8 == INFO_CLS, jnp.broadcast_to(cls, (8, T)),
                            jnp.where(r8 == INFO_RANK, jnp.broadcast_to(rank, (8, T)), 0))
    new_count = jnp.broadcast_to(count + jnp.sum(onehot, axis=1, keepdims=True), count_ref.shape)
    count_ref[...] = new_count
    counts_ref[...] = new_count.astype(jnp.int32)


def _store_records(ref, pieces, record_rows):
    n = ref.shape[0] // record_rows
    for c, piece in enumerate(pieces):
        ref[pl.ds(c, n, stride=record_rows), :] = piece


def _load_records(ref, first, count, record_rows, lead=()):
    n = ref.shape[-2] // record_rows
    return jnp.concatenate([ref[lead + (pl.ds(first + c, n, stride=record_rows), slice(None))]
                            for c in range(count)], axis=1)


class _RowGather:
    def __init__(self, index_of, src_hbm, buf, sems, record_rows, n_records, n_valid_of=None):
        self.index_of, self.src_hbm, self.buf, self.sems = index_of, src_hbm, buf, sems
        self.rr, self.n, self.n_valid_of = record_rows, n_records, n_valid_of

    def _groups(self, tile, body):
        for g0 in range(0, self.n, GATHER_GROUP):
            if self.n_valid_of is None:
                body(g0)
            else:
                pl.when(g0 < self.n_valid_of(tile))(functools.partial(body, g0))

    def _issue(self, tile, slot):
        rr = self.rr

        def group(g0):
            for r in range(g0, g0 + GATHER_GROUP):
                first = self.index_of(tile * self.n + r) * rr
                pltpu.make_async_copy(self.src_hbm.at[pl.ds(first, rr), :],
                                      self.buf.at[slot, pl.ds(r * rr, rr), :], self.sems.at[slot]).start()

        self._groups(tile, group)

    def start(self, tile, slot):
        if isinstance(slot, int):
            self._issue(tile, slot)
        else:
            for static_slot in range(2):
                pl.when(slot == static_slot)(functools.partial(self._issue, tile, static_slot))

    def wait(self, tile, slot):
        rows = GATHER_GROUP * self.rr

        def group(g0):
            pltpu.make_async_copy(self.src_hbm.at[pl.ds(0, rows), :],
                                  self.buf.at[slot, pl.ds(g0 * self.rr, rows), :], self.sems.at[slot]).wait()

        self._groups(tile, group)


def _expert_kernel(ea_ref, eb_ref, nused_ref, nvalid_ref, tstart_ref, cnt_ref, slot_ref,
                   rows_hbm, wga_ref, wua_ref, wda_ref, wgb_ref, wub_ref, wdb_ref,
                   y_ref, buf, sems, src_ref):
    bf16 = jnp.bfloat16
    n_x = wga_ref.shape[1] // LANES
    n_tok = slot_ref.shape[0]
    step, n_used = pl.program_id(0), nused_ref[0]
    gather = _RowGather(lambda i: src_ref[i], rows_hbm, buf, sems, ROW_RECORD, MOE_TILE,
                        n_valid_of=lambda tile: nvalid_ref[tile])

    @pl.when(step == 0)
    def _():
        buf[...] = jnp.zeros_like(buf)
        for c in range(N_CLASSES):
            base = tstart_ref[c] * MOE_TILE

            def pad(r, carry, base=base):
                src_ref[base + r] = jnp.minimum(base + r, n_tok - 1)
                return carry

            lax.fori_loop(cnt_ref[c], (tstart_ref[c + 1] - tstart_ref[c]) * MOE_TILE, pad, 0)

        def place(it, carry):
            for u in range(PLACE_UNROLL):
                t = it * PLACE_UNROLL + u
                src_ref[slot_ref[t]] = t
            return carry

        lax.fori_loop(0, n_tok // PLACE_UNROLL, place, 0)
        gather.start(0, 0)

    slot = step % 2

    @pl.when(step + 1 < n_used)
    def _():
        gather.start(step + 1, 1 - slot)

    @pl.when(step < n_used)
    def _():
        gather.wait(step, slot)
        rec = _load_records(buf, n_x, 1, ROW_RECORD, lead=(slot,))
        h2 = _load_records(buf, 0, n_x, ROW_RECORD, lead=(slot,)).astype(bf16)

        gate_a, gate_b = _dot(h2, wga_ref[0]), _dot(h2, wgb_ref[0])
        up_a, up_b = _dot(h2, wua_ref[0]), _dot(h2, wub_ref[0])
        hid_a = ((gate_a * jax.nn.sigmoid(gate_a)) * up_a).astype(bf16)
        hid_b = ((gate_b * jax.nn.sigmoid(gate_b)) * up_b).astype(bf16)
        y = rec[:, INFO_WA:INFO_WA + 1] * _dot(hid_a, wda_ref[0])
        y = y + rec[:, INFO_WB:INFO_WB + 1] * _dot(hid_b, wdb_ref[0])
        _store_records(y_ref, [y[:, c * LANES:(c + 1) * LANES] for c in range(n_x)], Y_RECORD)

    @pl.when(step >= n_used)
    def _():
        y_ref[...] = jnp.zeros_like(y_ref)


def _ple_final_kernel(slot_ref, x1_ref, p_ref, y_hbm, gple_ref, wpg_ref, wpp_ref, gfin_ref,
                      out_ref, buf, sems):
    bf16 = jnp.bfloat16
    step, n_steps = pl.program_id(0), pl.num_programs(0)
    n_x = gple_ref.shape[-1] // LANES
    gather = _RowGather(lambda t: slot_ref[t], y_hbm, buf, sems, Y_RECORD, PLE_TILE)

    @pl.when(step == 0)
    def _():
        gather.start(0, 0)

    slot = step % 2

    @pl.when(step + 1 < n_steps)
    def _():
        gather.start(step + 1, 1 - slot)

    gather.wait(step, slot)
    x2 = x1_ref[...] + _load_records(buf, 0, n_x, Y_RECORD, lead=(slot,))
    gate_p = jax.nn.sigmoid(_dot(_rms(x2, gple_ref[...]).astype(bf16), wpg_ref[...]))
    x3 = x2 + gate_p * _dot(p_ref[...].astype(bf16), wpp_ref[...])
    out_ref[...] = _rms(x3, gfin_ref[...])


def _const_spec(shape):
    return pl.BlockSpec(shape, lambda *_: (0,) * len(shape), pipeline_mode=pl.Buffered(1))


def _mixer(x, g_mix, w_in, w_gla_gate, b_gla_gate, g_gla_out, w_conv, w_out, g_moe,
           w_group, b_group, w_router, b_router, w_exp_gate, w_exp_up, w_exp_down):
    b, s, d = x.shape
    step_tokens = MIX_TILE * MIX_TILES_PER_STEP
    n_steps = s // step_tokens
    n_exp, _, de = w_exp_gate.shape
    assert (n_exp * d) % (BF16_ROWS * n_steps) == 0 and (n_exp * de) % (BF16_ROWS * n_steps) == 0
    up_rows, down_rows = n_exp * d // n_steps, n_exp * de // n_steps
    bf16 = jnp.bfloat16
    n_qkvg = 2 * GLA_QK + 2 * GLA_V
    w_qkvg = w_in[:, :n_qkvg].astype(bf16)
    w_a = w_in[:, n_qkvg:n_qkvg + GLA_LOWRANK].astype(bf16)
    w_c3 = w_in[:, n_qkvg + GLA_LOWRANK:].astype(bf16)
    cw = w_c3.shape[1] // 3
    wrt = jnp.zeros((ROUTE_ROWS, d), jnp.float32)
    wrt = wrt.at[:N_GROUPS].set(w_group.T).at[ROUTE_EXPERT_ROW0:ROUTE_EXPERT_ROW0 + N_EXPERTS].set(w_router.T)
    brt = jnp.zeros((ROUTE_ROWS, 1), jnp.float32)
    brt = brt.at[:N_GROUPS, 0].set(b_group).at[ROUTE_EXPERT_ROW0:ROUTE_EXPERT_ROW0 + N_EXPERTS, 0].set(b_router)
    wrt_split = jnp.concatenate(_split_bf16(wrt), axis=0)
    gate_hi, gate_lo = _split_bf16(w_gla_gate)
    w_gate_split = jnp.concatenate([gate_hi, gate_hi, gate_lo], axis=0)
    args = (x, g_mix[None, :], w_qkvg, w_a, w_c3, w_gate_split, b_gla_gate[None, :], g_gla_out[None, :],
            w_conv, w_out.astype(bf16), g_moe[None, :], wrt_split, brt)
    slabs = (w_exp_gate.reshape(n_exp * d, de), w_exp_up.reshape(n_exp * d, de), w_exp_down.reshape(n_exp * de, d))
    slab_specs = [pl.BlockSpec((rows_, width), lambda j: (j, 0))
                  for rows_, width in ((up_rows, de), (up_rows, de), (down_rows, d))]
    in_specs = [pl.BlockSpec((b, step_tokens, d), lambda j: (0, j, 0))]
    in_specs += [_const_spec(a.shape) for a in args[1:]] + slab_specs
    x1, rows, meta, counts, wg16, wu16, wd16 = pl.pallas_call(
        _mixer_kernel,
        grid=(n_steps,),
        in_specs=in_specs,
        out_specs=[pl.BlockSpec((b, step_tokens, d), lambda j: (0, j, 0)),
                   pl.BlockSpec((b, step_tokens * ROW_RECORD, LANES), lambda j: (0, j, 0)),
                   pl.BlockSpec((b, MIX_TILES_PER_STEP, 8, MIX_TILE), lambda j: (0, j, 0, 0)),
                   pl.BlockSpec((LANES, LANES), lambda j: (0, 0))] + slab_specs,
        out_shape=[jax.ShapeDtypeStruct((b, s, d), jnp.float32),
                   jax.ShapeDtypeStruct((b, s * ROW_RECORD, LANES), jnp.float32),
                   jax.ShapeDtypeStruct((b, s // MIX_TILE, 8, MIX_TILE), jnp.int32),
                   jax.ShapeDtypeStruct((LANES, LANES), jnp.int32)]
                  + [jax.ShapeDtypeStruct(w.shape, bf16) for w in slabs],
        scratch_shapes=[pltpu.VMEM((b, GLA_DV, GLA_QK), jnp.float32),
                        pltpu.VMEM((b, 8, cw), jnp.float32),
                        pltpu.VMEM((LANES, LANES), jnp.float32),
                        pltpu.VMEM((MIX_TILE, MIX_TILE), jnp.int32),
                        pltpu.VMEM((MIX_TILE, MIX_TILE), bf16)],
        compiler_params=pltpu.CompilerParams(dimension_semantics=("arbitrary",),
                                             vmem_limit_bytes=VMEM_LIMIT),
        name="mixer",
    )(*args, *slabs)
    return (x1.reshape(b * s, d), rows.reshape(b * s * ROW_RECORD, LANES), meta.reshape(b * s // MIX_TILE, 8, MIX_TILE), counts,
            wg16.reshape(n_exp, d, de), wu16.reshape(n_exp, d, de), wd16.reshape(n_exp, de, d))


def _sort_plan(meta, counts, n_tok):
    i32 = jnp.int32
    n_tiles = n_tok // MOE_TILE + N_CLASSES
    cls = meta[:, INFO_CLS, :].reshape(n_tok)
    rank = meta[:, INFO_RANK, :].reshape(n_tok)
    cnt = counts[:N_CLASSES, 0]
    tiles_per_cls = (cnt + MOE_TILE - 1) // MOE_TILE
    tile_end = jnp.cumsum(tiles_per_cls)
    n_used = tile_end[-1:]
    tstart = jnp.concatenate([tile_end - tiles_per_cls, n_used])
    tile_id = jnp.minimum(jnp.arange(n_tiles, dtype=i32), n_used - 1)
    tile_cls = jnp.sum((tile_id[:, None] >= tile_end[None, :]).astype(i32), axis=1)
    grp, pair = tile_cls // N_PAIRS, tile_cls % N_PAIRS
    e_a, e_b = grp * EXPERTS_PER_GROUP, grp * EXPERTS_PER_GROUP
    for pp in range(N_PAIRS):
        e_a = e_a + jnp.where(pair == pp, PAIR_A[pp], 0)
        e_b = e_b + jnp.where(pair == pp, PAIR_B[pp], 0)
    of_tile = (tile_cls[:, None] == jnp.arange(N_CLASSES, dtype=i32)[None, :]).astype(i32)
    tile_base = jnp.sum(of_tile * tstart[None, :N_CLASSES], axis=1)
    tile_cnt = jnp.sum(of_tile * cnt[None, :], axis=1)
    n_valid = jnp.clip(tile_cnt - (tile_id - tile_base) * MOE_TILE, 0, MOE_TILE)
    first_tile = functools.reduce(lambda acc, c: jnp.where(cls == c, tstart[c], acc), range(N_CLASSES),
                                  jnp.zeros_like(cls))
    slot = first_tile * MOE_TILE + rank
    return dict(slot=slot, n_valid=n_valid, cnt=cnt, tstart=tstart, e_a=e_a, e_b=e_b, n_used=n_used, n_tiles=n_tiles)


def _experts(plan, rows, wg, wu, wd):
    n_tiles = plan["n_tiles"]
    d, de = wg.shape[-2:]
    w_a = lambda shape: pl.BlockSpec(shape, lambda i, ea, eb, *_: (ea[i], 0, 0))
    w_b = lambda shape: pl.BlockSpec(shape, lambda i, ea, eb, *_: (eb[i], 0, 0))
    return pl.pallas_call(
        _expert_kernel,
        grid_spec=pltpu.PrefetchScalarGridSpec(
            num_scalar_prefetch=7, grid=(n_tiles,),
            in_specs=[pl.BlockSpec(memory_space=pl.ANY),
                      w_a((1, d, de)), w_a((1, d, de)), w_a((1, de, d)),
                      w_b((1, d, de)), w_b((1, d, de)), w_b((1, de, d))],
            out_specs=pl.BlockSpec((MOE_TILE * Y_RECORD, LANES), lambda i, *_: (i, 0)),
            scratch_shapes=[pltpu.VMEM((2, MOE_TILE * ROW_RECORD, LANES), jnp.float32),
                            pltpu.SemaphoreType.DMA((2,)),
                            pltpu.SMEM((n_tiles * MOE_TILE,), jnp.int32)]),
        out_shape=jax.ShapeDtypeStruct((n_tiles * MOE_TILE * Y_RECORD, LANES), jnp.float32),
        compiler_params=pltpu.CompilerParams(dimension_semantics=("arbitrary",),
                                             vmem_limit_bytes=VMEM_LIMIT),
        name="experts",
    )(plan["e_a"], plan["e_b"], plan["n_used"], plan["n_valid"], plan["tstart"], plan["cnt"], plan["slot"],
      rows, wg, wu, wd, wg, wu, wd)


def _ple_final(plan, x1, p, y_sorted, g_ple, w_ple_gate, w_ple_proj, g_final):
    n_tok, dp = p.shape
    d = w_ple_gate.shape[0]
    bf16 = jnp.bfloat16
    const = lambda shape: pl.BlockSpec(shape, lambda i, *_: (0,) * len(shape))
    tile = lambda width: pl.BlockSpec((PLE_TILE, width), lambda i, *_: (i, 0))
    return pl.pallas_call(
        _ple_final_kernel,
        grid_spec=pltpu.PrefetchScalarGridSpec(
            num_scalar_prefetch=1, grid=(n_tok // PLE_TILE,),
            in_specs=[tile(d),
                      tile(dp),
                      pl.BlockSpec(memory_space=pl.ANY),
                      const((1, d)), const((d, d)), const((dp, d)), const((1, d))],
            out_specs=tile(d),
            scratch_shapes=[pltpu.VMEM((2, PLE_TILE * Y_RECORD, LANES), jnp.float32),
                            pltpu.SemaphoreType.DMA((2,))]),
        out_shape=jax.ShapeDtypeStruct((n_tok, d), jnp.float32),
        compiler_params=pltpu.CompilerParams(dimension_semantics=("arbitrary",),
                                             vmem_limit_bytes=VMEM_LIMIT),
        name="ple_final",
    )(plan["slot"], x1, p, y_sorted,
      g_ple[None, :], w_ple_gate.astype(bf16), w_ple_proj.astype(bf16), g_final[None, :])


def kernel(x, p, g_mix, w_in, w_gla_gate, b_gla_gate, g_gla_out, w_conv, w_out, g_moe, w_group, b_group,
           w_router, b_router, w_exp_gate, w_exp_up, w_exp_down, g_ple, w_ple_gate, w_ple_proj, g_final):
    depth = w_in.shape[0]
    assert depth == 1, "the final norm is fused into the last (only) layer"
    b, s, d = x.shape
    n_tok = b * s
    assert s % (MIX_TILE * MIX_TILES_PER_STEP) == 0 and n_tok % MOE_TILE == 0 and n_tok % PLE_TILE == 0 and n_tok % PLACE_UNROLL == 0
    x1, rows, meta, counts, wg16, wu16, wd16 = _mixer(
        x, g_mix[0], w_in[0], w_gla_gate[0], b_gla_gate[0], g_gla_out[0], w_conv[0], w_out[0], g_moe[0],
        w_group[0], b_group[0], w_router[0], b_router[0], w_exp_gate[0], w_exp_up[0], w_exp_down[0])
    plan = _sort_plan(meta, counts, n_tok)
    y_sorted = _experts(plan, rows, wg16, wu16, wd16)
    out = _ple_final(plan, x1, p[0].reshape(n_tok, -1), y_sorted, g_ple[0], w_ple_gate[0], w_ple_proj[0],
                     g_final)
    return out.reshape(b, s, d)
```

```python
import functools

import jax
import jax.numpy as jnp
from jax import lax
from jax.experimental import pallas as pl
from jax.experimental.pallas import tpu as pltpu

EPS = 1e-6
GLA_HEADS = 4
GLA_DK = 64
GLA_DV = 128
GLA_QK = GLA_HEADS * GLA_DK
GLA_V = GLA_HEADS * GLA_DV
GLA_LOWRANK = 16
GLA_TAU = 16.0
CONV_K = 3
N_GROUPS = 4
EXPERTS_PER_GROUP = 4
N_EXPERTS = N_GROUPS * EXPERTS_PER_GROUP

LANES = 128
MIX_TILE = 256
MIX_LEVELS = 8
SUBLANES = 8
BF16_ROWS = 16
SUB_LEVELS = 3
MIX_TILES_PER_STEP = 1
ROUTE_ROWS = 32
ROUTE_EXPERT_ROW0 = 8
MOE_TILE = 256
VMEM_LIMIT = 56 * 1024 * 1024

PAIR_A = (0, 2, 2, 0, 0, 1)
PAIR_B = (1, 1, 3, 3, 2, 3)
PAIR_OF_KEY = {1: 0, 6: 1, 11: 2, 3: 3, 2: 4, 7: 5}
N_PAIRS = len(PAIR_A)
N_CLASSES = N_GROUPS * N_PAIRS
INFO_CLS, INFO_RANK, INFO_WA, INFO_WB = 0, 1, 2, 3
PLACE_UNROLL = 16
ROW_RECORD = 9
Y_RECORD = 8
GATHER_GROUP = 64
PLE_TILE = 512

_NT = (((1,), (1,)), ((), ()))
_DONE = object()


def _rms(x, g):
    return x * lax.rsqrt(jnp.mean(x * x, axis=-1, keepdims=True) + EPS) * g


def _dot(a, b):
    return jnp.dot(a, b, preferred_element_type=jnp.float32)


def _dot_nt(a, b):
    return lax.dot_general(a, b, _NT, preferred_element_type=jnp.float32)


def _split_bf16(a):
    hi = a.astype(jnp.bfloat16)
    return hi, (a - hi.astype(jnp.float32)).astype(jnp.bfloat16)


def _shift_rows(x, shift):
    return pltpu.roll(x, shift % x.shape[0], axis=0)


def _mixer_kernel(x_ref, gmix_ref, wqkvg_ref, wa_ref, wc3_ref, wgate_ref, bgate_ref, ggla_ref,
                  wconv_ref, wout_ref, gmoe_ref, wrt_ref, brt_ref, wge_ref, wue_ref, wde_ref,
                  x1_ref, rows_ref, meta_ref, counts_ref, wge16_ref, wue16_ref, wde16_ref,
                  st_ref, carry_ref, count_ref, level_ref, tril_ref):
    bf16 = jnp.bfloat16
    T = MIX_TILE

    wge16_ref[...] = wge_ref[...].astype(bf16)
    wue16_ref[...] = wue_ref[...].astype(bf16)
    wde16_ref[...] = wde_ref[...].astype(bf16)

    @pl.when(pl.program_id(0) == 0)
    def _():
        st_ref[...] = jnp.zeros_like(st_ref)
        carry_ref[...] = jnp.zeros_like(carry_ref)
        count_ref[...] = jnp.zeros_like(count_ref)
        tt = lax.broadcasted_iota(jnp.int32, (T, T), 0)
        ss = lax.broadcasted_iota(jnp.int32, (T, T), 1)
        txs = jnp.bitwise_xor(tt, ss)
        level = jnp.zeros((T, T), jnp.int32)
        for j in range(1, MIX_LEVELS):
            level = level + (txs >= (1 << j)).astype(jnp.int32)
        level_ref[...] = jnp.where(tt > ss, level, jnp.where(tt == ss, MIX_LEVELS, -1))
        tril_ref[...] = (ss <= tt).astype(bf16)

    tiles = [_mixer_tile(x_ref.at[pl.ds(b, 1), pl.ds(i * T, T)], gmix_ref, wqkvg_ref, wa_ref, wc3_ref, wgate_ref,
                         bgate_ref, ggla_ref, wconv_ref, wout_ref, gmoe_ref, wrt_ref, brt_ref,
                         x1_ref.at[b, pl.ds(i * T, T)],
                         rows_ref.at[b, pl.ds(i * T * ROW_RECORD, T * ROW_RECORD)], meta_ref.at[b, pl.ds(i, 1)],
                         counts_ref, st_ref.at[b], carry_ref.at[b], count_ref, level_ref, tril_ref)
             for i in range(MIX_TILES_PER_STEP) for b in range(x_ref.shape[0])]
    while tiles:
        tiles = [t for t in tiles if next(t, _DONE) is not _DONE]


def _mixer_tile(x_ref, gmix_ref, wqkvg_ref, wa_ref, wc3_ref, wgate_ref, bgate_ref, ggla_ref,
                wconv_ref, wout_ref, gmoe_ref, wrt_ref, brt_ref,
                x1_ref, rows_ref, meta_ref, counts_ref,
                st_ref, carry_ref, count_ref, level_ref, tril_ref):
    f32, bf16 = jnp.float32, jnp.bfloat16
    T = MIX_TILE
    D = x_ref.shape[-1]

    x = x_ref[0]
    hb = _rms(x, gmix_ref[...]).astype(bf16)
    qk = _dot(hb, wqkvg_ref[:, :2 * GLA_QK])
    q = qk[:, :GLA_QK] * (GLA_DK ** -0.5)
    k = qk[:, GLA_QK:]
    a_low = _dot(hb, wa_ref[...])
    a_hi, a_lo = _split_bf16(a_low)
    z = _dot(jnp.concatenate([a_hi, a_lo, a_hi], axis=1), wgate_ref[...]) + bgate_ref[...]
    la = (jnp.minimum(z, 0.0) - jnp.log(1.0 + jnp.exp(-jnp.abs(z)))) * (1.0 / GLA_TAU)
    yield

    row = lax.broadcasted_iota(jnp.int32, (T, GLA_QK), 0)

    def next_level(l, q_l, k_l, block):
        upper = ((row >> l) & 1) == 1
        below = _shift_rows(block, 1 << l)
        above = _shift_rows(block, -(1 << l))
        return (q_l * jnp.where(upper, below, 1.0), k_l * jnp.where(upper, 1.0, above),
                block * jnp.where(upper, below, above))

    decay = jnp.exp(la)
    H = T // 2
    assert GLA_DV == H
    half_level = level_ref[0:H, 0:H]
    lane_head_st = lax.broadcasted_iota(jnp.int32, (H, GLA_QK), 1) // GLA_DK

    heads_per_tile = LANES // GLA_DK
    lane_head_tile = lax.broadcasted_iota(jnp.int32, (H, LANES), 1) // GLA_DK

    def head_scores(q_half, k_half):
        out = []
        for tile in range(GLA_QK // LANES):
            q_t = q_half[:, tile * LANES:(tile + 1) * LANES]
            k_t = k_half[:, tile * LANES:(tile + 1) * LANES]
            k_rows = jnp.concatenate([jnp.where(lane_head_tile == j, k_t, jnp.zeros_like(k_t))
                                      for j in range(heads_per_tile)], axis=0)
            p = _dot_nt(q_t, k_rows)
            out += [p[:, j * H:(j + 1) * H] for j in range(heads_per_tile)]
        return out

    diag0 = [jnp.zeros((H, H), f32) for _ in range(GLA_HEADS)]
    diag1 = [jnp.zeros((H, H), f32) for _ in range(GLA_HEADS)]

    def add_level(l, q_l, k_l):
        sel = half_level == l
        ql, kl = q_l.astype(bf16), k_l.astype(bf16)
        p0 = head_scores(ql[:H], kl[:H])
        p1 = head_scores(ql[H:], kl[H:])
        for h in range(GLA_HEADS):
            diag0[h] = jnp.where(sel, p0[h], diag0[h])
            diag1[h] = jnp.where(sel, p1[h], diag1[h])

    def split_groups(a):
        return [a[i * SUBLANES:(i + 1) * SUBLANES] for i in range(T // SUBLANES)]

    def join_groups(groups):
        return jnp.concatenate(groups, axis=0)

    def next_level_groups(l, q_g, k_g, block_g):
        m = 1 << (l - SUB_LEVELS)
        products = {}
        q_n, k_n, block_n = [], [], []
        for gi in range(len(q_g)):
            lo, hi = gi & ~m, gi | m
            key = (id(block_g[lo]), id(block_g[hi]))
            if key not in products:
                products[key] = block_g[lo] * block_g[hi]
            q_n.append(q_g[gi] * block_g[lo] if gi & m else q_g[gi])
            k_n.append(k_g[gi] if gi & m else k_g[gi] * block_g[hi])
            block_n.append(products[key])
        return q_n, k_n, block_n

    add_level(MIX_LEVELS, q, k)
    q_l, k_l, block = q * decay, k, decay
    cw = wc3_ref.shape[1] // 3
    conv_parts = []
    for l in range(SUB_LEVELS):
        add_level(l, q_l, k_l)
        q_l, k_l, block = next_level(l, q_l, k_l, block)
        conv_parts.append(_dot(hb, wc3_ref[:, l * cw:(l + 1) * cw]))
        yield
    q_g, k_g, block_g = split_groups(q_l), split_groups(k_l), split_groups(block)
    half = len(q_g) // 2
    for l in range(SUB_LEVELS, MIX_LEVELS - 1):
        add_level(l, join_groups(q_g), join_groups(k_g))
        q_g, k_g, block_g = next_level_groups(l, q_g, k_g, block_g)
        if l == SUB_LEVELS:
            v = _dot(hb, wqkvg_ref[:, 2 * GLA_QK:2 * GLA_QK + GLA_V])
        yield
    low = head_scores(join_groups(q_g[half:]).astype(bf16), join_groups(k_g[:half]).astype(bf16))
    zero_block = jnp.zeros((H, H), f32)
    scores = [jnp.concatenate([jnp.concatenate([diag0[h], zero_block], axis=1),
                               jnp.concatenate([low[h], diag1[h]], axis=1)], axis=0) for h in range(GLA_HEADS)]

    yield

    q_g, k_g, block_g = next_level_groups(MIX_LEVELS - 1, q_g, k_g, block_g)
    st = st_ref[...]
    o_state = head_scores(join_groups(q_g).astype(bf16), st.astype(bf16))
    upd = _dot(v.T.astype(bf16), join_groups(k_g).astype(bf16))
    new_st = st * block_g[0][0:1]
    for h in range(GLA_HEADS):
        new_st = new_st + jnp.where(lane_head_st == h, upd[h * GLA_DV:(h + 1) * GLA_DV], 0.0)
    st_ref[...] = new_st
    g = _dot(hb, wqkvg_ref[:, 2 * GLA_QK + GLA_V:])
    ggla = ggla_ref[...]
    y_heads = []
    for h in range(GLA_HEADS):
        v_h = v[:, h * GLA_DV:(h + 1) * GLA_DV]
        o = _dot(scores[h].astype(bf16), v_h.astype(bf16)) + o_state[h]
        g_h = g[:, h * GLA_DV:(h + 1) * GLA_DV]
        y_heads.append(_rms(o, ggla) * (g_h * jax.nn.sigmoid(g_h)))
        yield

    yield

    cb, cu = conv_parts[0], conv_parts[1] * conv_parts[2]
    crow = lax.broadcasted_iota(jnp.int32, (T, cw), 0)
    prev2, prev1 = carry_ref[0:1, :], carry_ref[1:2, :]
    m1 = jnp.where(crow == 0, prev1, _shift_rows(cu, 1))
    m2 = jnp.where(crow == 0, prev2, jnp.where(crow == 1, prev1, _shift_rows(cu, 2)))
    wconv = wconv_ref[...]
    y_conv = cb * (wconv[0:1, :] * m2 + wconv[1:2, :] * m1 + wconv[2:3, :] * cu)
    carry_ref[0:2, :] = cu[T - 2:, :]

    y = jnp.concatenate(y_heads + [y_conv], axis=1).astype(bf16)
    x1 = x + _dot(y, wout_ref[...])
    x1_ref[...] = x1

    yield

    h2 = _rms(x1, gmoe_ref[...])
    h2_hi, h2_lo = _split_bf16(h2)
    part = _dot_nt(wrt_ref[...], h2_hi)
    logits = (part[:ROUTE_ROWS] + part[ROUTE_ROWS:] + _dot_nt(wrt_ref[:ROUTE_ROWS, :], h2_lo)) + brt_ref[...]
    gl = [logits[i:i + 1, :] for i in range(N_GROUPS)]
    gmax = functools.reduce(jnp.maximum, gl)
    gsum = functools.reduce(lambda a, b: a + b, [jnp.exp(t - gmax) for t in gl])
    p_grp = 1.0 / gsum
    g_sel = jnp.full_like(gmax, N_GROUPS - 1).astype(jnp.int32)
    for i in reversed(range(N_GROUPS - 1)):
        g_sel = jnp.where(gl[i] == gmax, i, g_sel)
    ig = []
    for j in range(EXPERTS_PER_GROUP):
        acc = jnp.zeros_like(gmax)
        for gi in range(N_GROUPS):
            r0 = ROUTE_EXPERT_ROW0 + gi * EXPERTS_PER_GROUP + j
            acc = acc + jnp.where(g_sel == gi, logits[r0:r0 + 1, :], 0.0)
        ig.append(acc)

    def first_argmax(vals):
        m = functools.reduce(jnp.maximum, vals)
        idx = jnp.full_like(m, len(vals) - 1).astype(jnp.int32)
        for i in reversed(range(len(vals) - 1)):
            idx = jnp.where(vals[i] == m, i, idx)
        return m, idx

    m1_, i1 = first_argmax(ig)
    m2_, i2 = first_argmax([jnp.where(i1 == j, -jnp.inf, ig[j]) for j in range(EXPERTS_PER_GROUP)])
    e21 = jnp.exp(m2_ - m1_)
    w1 = p_grp / (1.0 + e21)
    w2 = p_grp * e21 / (1.0 + e21)
    key = jnp.minimum(i1, i2) * EXPERTS_PER_GROUP + jnp.maximum(i1, i2)
    pair = jnp.zeros_like(key)
    a_loc = jnp.zeros_like(key)
    for kk, pp in PAIR_OF_KEY.items():
        pair = jnp.where(key == kk, pp, pair)
        a_loc = jnp.where(key == kk, PAIR_A[pp], a_loc)
    w_a = jnp.where(i1 == a_loc, w1, w2)
    w_b = jnp.where(i1 == a_loc, w2, w1)
    cls = g_sel * N_PAIRS + pair
    rr = lax.broadcasted_iota(jnp.int32, (LANES, T), 0)
    rec_t = (jnp.where(rr == INFO_WA, jnp.broadcast_to(w_a, (LANES, T)), 0.0)
             + jnp.where(rr == INFO_WB, jnp.broadcast_to(w_b, (LANES, T)), 0.0))
    pieces = [h2[:, c * LANES:(c + 1) * LANES] for c in range(D // LANES)] + [rec_t.T]
    assert len(pieces) == ROW_RECORD
    _store_records(rows_ref, pieces, ROW_RECORD)

    onehot = (rr == jnp.broadcast_to(cls, (LANES, T))).astype(f32)
    count = count_ref[:, 0:1]
    before = _dot_nt(onehot.astype(bf16), tril_ref[...]) - onehot + count
    rank = jnp.sum(onehot * before, axis=0, keepdims=True).astype(jnp.int32)
    r8 = lax.broadcasted_iota(jnp.int32, (8, T), 0)
    meta_ref[0] = jnp.where(r8 == INFO_CLS, jnp.broadcast_to(cls, (8, T)),
                            jnp.where(r8 == INFO_RANK, jnp.broadcast_to(rank, (8, T)), 0))
    new_count = jnp.broadcast_to(count + jnp.sum(onehot, axis=1, keepdims=True), count_ref.shape)
    count_ref[...] = new_count
    counts_ref[...] = new_count.astype(jnp.int32)


def _store_records(ref, pieces, record_rows):
    n = ref.shape[0] // record_rows
    for c, piece in enumerate(pieces):
        ref[pl.ds(c, n, stride=record_rows), :] = piece


def _load_records(ref, first, count, record_rows, lead=()):
    n = ref.shape[-2] // record_rows
    return jnp.concatenate([ref[lead + (pl.ds(first + c, n, stride=record_rows), slice(None))]
                            for c in range(count)], axis=1)


class _RowGather:
    def __init__(self, index_of, src_hbm, buf, sems, record_rows, n_records, n_valid_of=None):
        self.index_of, self.src_hbm, self.buf, self.sems = index_of, src_hbm, buf, sems
        self.rr, self.n, self.n_valid_of = record_rows, n_records, n_valid_of

    def _groups(self, tile, body):
        for g0 in range(0, self.n, GATHER_GROUP):
            if self.n_valid_of is None:
                body(g0)
            else:
                pl.when(g0 < self.n_valid_of(tile))(functools.partial(body, g0))

    def _issue(self, tile, slot):
        rr = self.rr

        def group(g0):
            for r in range(g0, g0 + GATHER_GROUP):
                first = self.index_of(tile * self.n + r) * rr
                pltpu.make_async_copy(self.src_hbm.at[pl.ds(first, rr), :],
                                      self.buf.at[slot, pl.ds(r * rr, rr), :], self.sems.at[slot]).start()

        self._groups(tile, group)

    def start(self, tile, slot):
        if isinstance(slot, int):
            self._issue(tile, slot)
        else:
            for static_slot in range(2):
                pl.when(slot == static_slot)(functools.partial(self._issue, tile, static_slot))

    def wait(self, tile, slot):
        rows = GATHER_GROUP * self.rr

        def group(g0):
            pltpu.make_async_copy(self.src_hbm.at[pl.ds(0, rows), :],
                                  self.buf.at[slot, pl.ds(g0 * self.rr, rows), :], self.sems.at[slot]).wait()

        self._groups(tile, group)


def _expert_kernel(ea_ref, eb_ref, nused_ref, nvalid_ref, tstart_ref, cnt_ref, slot_ref,
                   rows_hbm, wga_ref, wua_ref, wda_ref, wgb_ref, wub_ref, wdb_ref,
                   y_ref, buf, sems, src_ref):
    bf16 = jnp.bfloat16
    n_x = wga_ref.shape[1] // LANES
    n_tok = slot_ref.shape[0]
    step, n_used = pl.program_id(0), nused_ref[0]
    gather = _RowGather(lambda i: src_ref[i], rows_hbm, buf, sems, ROW_RECORD, MOE_TILE,
                        n_valid_of=lambda tile: nvalid_ref[tile])

    @pl.when(step == 0)
    def _():
        buf[...] = jnp.zeros_like(buf)
        for c in range(N_CLASSES):
            base = tstart_ref[c] * MOE_TILE

            def pad(r, carry, base=base):
                src_ref[base + r] = jnp.minimum(base + r, n_tok - 1)
                return carry

            lax.fori_loop(cnt_ref[c], (tstart_ref[c + 1] - tstart_ref[c]) * MOE_TILE, pad, 0)

        def place(it, carry):
            for u in range(PLACE_UNROLL):
                t = it * PLACE_UNROLL + u
                src_ref[slot_ref[t]] = t
            return carry

        lax.fori_loop(0, n_tok // PLACE_UNROLL, place, 0)
        gather.start(0, 0)

    slot = step % 2

    @pl.when(step + 1 < n_used)
    def _():
        gather.start(step + 1, 1 - slot)

    @pl.when(step < n_used)
    def _():
        gather.wait(step, slot)
        rec = _load_records(buf, n_x, 1, ROW_RECORD, lead=(slot,))
        h2 = _load_records(buf, 0, n_x, ROW_RECORD, lead=(slot,)).astype(bf16)

        gate_a, gate_b = _dot(h2, wga_ref[0]), _dot(h2, wgb_ref[0])
        up_a, up_b = _dot(h2, wua_ref[0]), _dot(h2, wub_ref[0])
        hid_a = ((gate_a * jax.nn.sigmoid(gate_a)) * up_a).astype(bf16)
        hid_b = ((gate_b * jax.nn.sigmoid(gate_b)) * up_b).astype(bf16)
        y = rec[:, INFO_WA:INFO_WA + 1] * _dot(hid_a, wda_ref[0])
        y = y + rec[:, INFO_WB:INFO_WB + 1] * _dot(hid_b, wdb_ref[0])
        _store_records(y_ref, [y[:, c * LANES:(c + 1) * LANES] for c in range(n_x)], Y_RECORD)

    @pl.when(step >= n_used)
    def _():
        y_ref[...] = jnp.zeros_like(y_ref)


def _ple_final_kernel(slot_ref, x1_ref, p_ref, y_hbm, gple_ref, wpg_ref, wpp_ref, gfin_ref,
                      out_ref, buf, sems):
    bf16 = jnp.bfloat16
    step, n_steps = pl.program_id(0), pl.num_programs(0)
    n_x = gple_ref.shape[-1] // LANES
    gather = _RowGather(lambda t: slot_ref[t], y_hbm, buf, sems, Y_RECORD, PLE_TILE)

    @pl.when(step == 0)
    def _():
        gather.start(0, 0)

    slot = step % 2

    @pl.when(step + 1 < n_steps)
    def _():
        gather.start(step + 1, 1 - slot)

    gather.wait(step, slot)
    x2 = x1_ref[...] + _load_records(buf, 0, n_x, Y_RECORD, lead=(slot,))
    gate_p = jax.nn.sigmoid(_dot(_rms(x2, gple_ref[...]).astype(bf16), wpg_ref[...]))
    x3 = x2 + gate_p * _dot(p_ref[...].astype(bf16), wpp_ref[...])
    out_ref[...] = _rms(x3, gfin_ref[...])


def _const_spec(shape):
    return pl.BlockSpec(shape, lambda *_: (0,) * len(shape), pipeline_mode=pl.Buffered(1))


def _mixer(x, g_mix, w_in, w_gla_gate, b_gla_gate, g_gla_out, w_conv, w_out, g_moe,
           w_group, b_group, w_router, b_router, w_exp_gate, w_exp_up, w_exp_down):
    b, s, d = x.shape
    step_tokens = MIX_TILE * MIX_TILES_PER_STEP
    n_steps = s // step_tokens
    n_exp, _, de = w_exp_gate.shape
    assert (n_exp * d) % (BF16_ROWS * n_steps) == 0 and (n_exp * de) % (BF16_ROWS * n_steps) == 0
    up_rows, down_rows = n_exp * d // n_steps, n_exp * de // n_steps
    bf16 = jnp.bfloat16
    n_qkvg = 2 * GLA_QK + 2 * GLA_V
    w_qkvg = w_in[:, :n_qkvg].astype(bf16)
    w_a = w_in[:, n_qkvg:n_qkvg + GLA_LOWRANK].astype(bf16)
    w_c3 = w_in[:, n_qkvg + GLA_LOWRANK:].astype(bf16)
    cw = w_c3.shape[1] // 3
    wrt = jnp.zeros((ROUTE_ROWS, d), jnp.float32)
    wrt = wrt.at[:N_GROUPS].set(w_group.T).at[ROUTE_EXPERT_ROW0:ROUTE_EXPERT_ROW0 + N_EXPERTS].set(w_router.T)
    brt = jnp.zeros((ROUTE_ROWS, 1), jnp.float32)
    brt = brt.at[:N_GROUPS, 0].set(b_group).at[ROUTE_EXPERT_ROW0:ROUTE_EXPERT_ROW0 + N_EXPERTS, 0].set(b_router)
    wrt_split = jnp.concatenate(_split_bf16(wrt), axis=0)
    gate_hi, gate_lo = _split_bf16(w_gla_gate)
    w_gate_split = jnp.concatenate([gate_hi, gate_hi, gate_lo], axis=0)
    args = (x, g_mix[None, :], w_qkvg, w_a, w_c3, w_gate_split, b_gla_gate[None, :], g_gla_out[None, :],
            w_conv, w_out.astype(bf16), g_moe[None, :], wrt_split, brt)
    slabs = (w_exp_gate.reshape(n_exp * d, de), w_exp_up.reshape(n_exp * d, de), w_exp_down.reshape(n_exp * de, d))
    slab_specs = [pl.BlockSpec((rows_, width), lambda j: (j, 0))
                  for rows_, width in ((up_rows, de), (up_rows, de), (down_rows, d))]
    in_specs = [pl.BlockSpec((b, step_tokens, d), lambda j: (0, j, 0))]
    in_specs += [_const_spec(a.shape) for a in args[1:]] + slab_specs
    x1, rows, meta, counts, wg16, wu16, wd16 = pl.pallas_call(
        _mixer_kernel,
        grid=(n_steps,),
        in_specs=in_specs,
        out_specs=[pl.BlockSpec((b, step_tokens, d), lambda j: (0, j, 0)),
                   pl.BlockSpec((b, step_tokens * ROW_RECORD, LANES), lambda j: (0, j, 0)),
                   pl.BlockSpec((b, MIX_TILES_PER_STEP, 8, MIX_TILE), lambda j: (0, j, 0, 0)),
                   pl.BlockSpec((LANES, LANES), lambda j: (0, 0))] + slab_specs,
        out_shape=[jax.ShapeDtypeStruct((b, s, d), jnp.float32),
                   jax.ShapeDtypeStruct((b, s * ROW_RECORD, LANES), jnp.float32),
                   jax.ShapeDtypeStruct((b, s // MIX_TILE, 8, MIX_TILE), jnp.int32),
                   jax.ShapeDtypeStruct((LANES, LANES), jnp.int32)]
                  + [jax.ShapeDtypeStruct(w.shape, bf16) for w in slabs],
        scratch_shapes=[pltpu.VMEM((b, GLA_DV, GLA_QK), jnp.float32),
                        pltpu.VMEM((b, 8, cw), jnp.float32),
                        pltpu.VMEM((LANES, LANES), jnp.float32),
                        pltpu.VMEM((MIX_TILE, MIX_TILE), jnp.int32),
                        pltpu.VMEM((MIX_TILE, MIX_TILE), bf16)],
        compiler_params=pltpu.CompilerParams(dimension_semantics=("arbitrary",),
                                             vmem_limit_bytes=VMEM_LIMIT),
        name="mixer",
    )(*args, *slabs)
    return (x1.reshape(b * s, d), rows.reshape(b * s * ROW_RECORD, LANES), meta.reshape(b * s // MIX_TILE, 8, MIX_TILE), counts,
            wg16.reshape(n_exp, d, de), wu16.reshape(n_exp, d, de), wd16.reshape(n_exp, de, d))


def _sort_plan(meta, counts, n_tok):
    i32 = jnp.int32
    n_tiles = n_tok // MOE_TILE + N_CLASSES
    cls = meta[:, INFO_CLS, :].reshape(n_tok)
    rank = meta[:, INFO_RANK, :].reshape(n_tok)
    cnt = counts[:N_CLASSES, 0]
    tiles_per_cls = (cnt + MOE_TILE - 1) // MOE_TILE
    tile_end = jnp.cumsum(tiles_per_cls)
    n_used = tile_end[-1:]
    tstart = jnp.concatenate([tile_end - tiles_per_cls, n_used])
    tile_id = jnp.minimum(jnp.arange(n_tiles, dtype=i32), n_used - 1)
    tile_cls = jnp.sum((tile_id[:, None] >= tile_end[None, :]).astype(i32), axis=1)
    grp, pair = tile_cls // N_PAIRS, tile_cls % N_PAIRS
    e_a, e_b = grp * EXPERTS_PER_GROUP, grp * EXPERTS_PER_GROUP
    for pp in range(N_PAIRS):
        e_a = e_a + jnp.where(pair == pp, PAIR_A[pp], 0)
        e_b = e_b + jnp.where(pair == pp, PAIR_B[pp], 0)
    of_tile = (tile_cls[:, None] == jnp.arange(N_CLASSES, dtype=i32)[None, :]).astype(i32)
    tile_base = jnp.sum(of_tile * tstart[None, :N_CLASSES], axis=1)
    tile_cnt = jnp.sum(of_tile * cnt[None, :], axis=1)
    n_valid = jnp.clip(tile_cnt - (tile_id - tile_base) * MOE_TILE, 0, MOE_TILE)
    first_tile = functools.reduce(lambda acc, c: jnp.where(cls == c, tstart[c], acc), range(N_CLASSES),
                                  jnp.zeros_like(cls))
    slot = first_tile * MOE_TILE + rank
    return dict(slot=slot, n_valid=n_valid, cnt=cnt, tstart=tstart, e_a=e_a, e_b=e_b, n_used=n_used, n_tiles=n_tiles)


def _experts(plan, rows, wg, wu, wd):
    n_tiles = plan["n_tiles"]
    d, de = wg.shape[-2:]
    w_a = lambda shape: pl.BlockSpec(shape, lambda i, ea, eb, *_: (ea[i], 0, 0))
    w_b = lambda shape: pl.BlockSpec(shape, lambda i, ea, eb, *_: (eb[i], 0, 0))
    return pl.pallas_call(
        _expert_kernel,
        grid_spec=pltpu.PrefetchScalarGridSpec(
            num_scalar_prefetch=7, grid=(n_tiles,),
            in_specs=[pl.BlockSpec(memory_space=pl.ANY),
                      w_a((1, d, de)), w_a((1, d, de)), w_a((1, de, d)),
                      w_b((1, d, de)), w_b((1, d, de)), w_b((1, de, d))],
            out_specs=pl.BlockSpec((MOE_TILE * Y_RECORD, LANES), lambda i, *_: (i, 0)),
            scratch_shapes=[pltpu.VMEM((2, MOE_TILE * ROW_RECORD, LANES), jnp.float32),
                            pltpu.SemaphoreType.DMA((2,)),
                            pltpu.SMEM((n_tiles * MOE_TILE,), jnp.int32)]),
        out_shape=jax.ShapeDtypeStruct((n_tiles * MOE_TILE * Y_RECORD, LANES), jnp.float32),
        compiler_params=pltpu.CompilerParams(dimension_semantics=("arbitrary",),
                                             vmem_limit_bytes=VMEM_LIMIT),
        name="experts",
    )(plan["e_a"], plan["e_b"], plan["n_used"], plan["n_valid"], plan["tstart"], plan["cnt"], plan["slot"],
      rows, wg, wu, wd, wg, wu, wd)


def _ple_final(plan, x1, p, y_sorted, g_ple, w_ple_gate, w_ple_proj, g_final):
    n_tok, dp = p.shape
    d = w_ple_gate.shape[0]
    bf16 = jnp.bfloat16
    const = lambda shape: pl.BlockSpec(shape, lambda i, *_: (0,) * len(shape))
    tile = lambda width: pl.BlockSpec((PLE_TILE, width), lambda i, *_: (i, 0))
    return pl.pallas_call(
        _ple_final_kernel,
        grid_spec=pltpu.PrefetchScalarGridSpec(
            num_scalar_prefetch=1, grid=(n_tok // PLE_TILE,),
            in_specs=[tile(d),
                      tile(dp),
                      pl.BlockSpec(memory_space=pl.ANY),
                      const((1, d)), const((d, d)), const((dp, d)), const((1, d))],
            out_specs=tile(d),
            scratch_shapes=[pltpu.VMEM((2, PLE_TILE * Y_RECORD, LANES), jnp.float32),
                            pltpu.SemaphoreType.DMA((2,))]),
        out_shape=jax.ShapeDtypeStruct((n_tok, d), jnp.float32),
        compiler_params=pltpu.CompilerParams(dimension_semantics=("arbitrary",),
                                             vmem_limit_bytes=VMEM_LIMIT),
        name="ple_final",
    )(plan["slot"], x1, p, y_sorted,
      g_ple[None, :], w_ple_gate.astype(bf16), w_ple_proj.astype(bf16), g_final[None, :])


def kernel(x, p, g_mix, w_in, w_gla_gate, b_gla_gate, g_gla_out, w_conv, w_out, g_moe, w_group, b_group,
           w_router, b_router, w_exp_gate, w_exp_up, w_exp_down, g_ple, w_ple_gate, w_ple_proj, g_final):
    depth = w_in.shape[0]
    assert depth == 1, "the final norm is fused into the last (only) layer"
    b, s, d = x.shape
    n_tok = b * s
    assert s % (MIX_TILE * MIX_TILES_PER_STEP) == 0 and n_tok % MOE_TILE == 0 and n_tok % PLE_TILE == 0 and n_tok % PLACE_UNROLL == 0
    x1, rows, meta, counts, wg16, wu16, wd16 = _mixer(
        x, g_mix[0], w_in[0], w_gla_gate[0], b_gla_gate[0], g_gla_out[0], w_conv[0], w_out[0], g_moe[0],
        w_group[0], b_group[0], w_router[0], b_router[0], w_exp_gate[0], w_exp_up[0], w_exp_down[0])
    plan = _sort_plan(meta, counts, n_tok)
    y_sorted = _experts(plan, rows, wg16, wu16, wd16)
    out = _ple_final(plan, x1, p[0].reshape(n_tok, -1), y_sorted, g_ple[0], w_ple_gate[0], w_ple_proj[0],
                     g_final)
    return out.reshape(b, s, d)
```
